```python
import math
import jax, jax.numpy as jnp
from jax import lax
import numpy as np


D_MODEL = 1024
BATCH = 32
SEQ = 2048
DEPTH = 1
DEC_BATCH = 4
DEC_SEQ = 4096
PAST_LEN = 128

D_MIX = D_MODEL
RET_WIDTH = D_MIX // 2
RET_HEADS = 4
RET_HEAD_DIM = RET_WIDTH // RET_HEADS
RET_CHUNK = 128
ROPE_BASE = 10000.0
RWKV_WIDTH = D_MIX - RET_WIDTH
RWKV_HEAD_DIM = 64
RWKV_HEADS = RWKV_WIDTH // RWKV_HEAD_DIM
DECAY_RANK = 64
ICLR_RANK = 64
GATE_RANK = 128
RET_COLS = 4 * RET_WIDTH
RWKV_SPLITS = (RWKV_WIDTH, RWKV_WIDTH, RWKV_WIDTH, DECAY_RANK, DECAY_RANK, ICLR_RANK, ICLR_RANK, GATE_RANK)
RWKV_COLS = sum(RWKV_SPLITS)
D_IN = RET_COLS + RWKV_COLS
N_EXPERTS = 256
TOP_K = 8
N_GROUPS = 8
TOPK_GROUPS = 4
D_EXPERT = 256
D_SHARED = 256
ROUTE_SCALE = 2.5
MOE_BLOCK = 256
DN_ALPHA = (2 * DEPTH) ** 0.25
DN_BETA = (8 * DEPTH) ** -0.25
LN_EPS = 1e-5
GN_EPS = 1e-5
RWKV_GN_EPS = 64e-5

kernel_name = "hybrid_retention_rwkv7_moe_encoder"


def layer_norm(x, g, b):
    xf = x.astype(jnp.float32)
    mu = xf.mean(-1, keepdims=True)
    var = jnp.square(xf - mu).mean(-1, keepdims=True)
    return ((xf - mu) * lax.rsqrt(var + LN_EPS) * g + b).astype(x.dtype)


def head_norm(y, g, b, eps):
    h, n = y.shape[-2:]
    mu = y.mean(-1, keepdims=True)
    var = jnp.square(y - mu).mean(-1, keepdims=True)
    return (y - mu) * lax.rsqrt(var + eps) * g.reshape(h, n) + b.reshape(h, n)


def rotary(x):
    s, d = x.shape[1], x.shape[-1]
    inv = ROPE_BASE ** (-jnp.arange(0, d, 2, dtype=jnp.float32) / d)
    ang = jnp.arange(s, dtype=jnp.float32)[:, None] * inv[None, :]
    cos = jnp.cos(ang)[None, :, None, :]
    sin = jnp.sin(ang)[None, :, None, :]
    x1, x2 = jnp.split(x, 2, axis=-1)
    return jnp.concatenate([x1 * cos - x2 * sin, x1 * sin + x2 * cos], axis=-1)


def centred_shift(z):
    prev = jnp.pad(z, ((0, 0), (1, 0), (0, 0)))[:, :-1]
    nxt = jnp.pad(z, ((0, 0), (0, 1), (0, 0)))[:, 1:]
    return 0.5 * (prev + nxt)


def retention_chunkwise(q, k, v):
    b, s, h, dk = q.shape
    dv = v.shape[-1]
    C = RET_CHUNK
    n = s // C
    log_g = jnp.log1p(-jnp.exp2(-5.0 - jnp.arange(h, dtype=jnp.float32)))
    pos = jnp.arange(C, dtype=jnp.float32)
    qc = q.reshape(b, n, C, h, dk)
    kc = k.reshape(b, n, C, h, dk)
    vc = v.reshape(b, n, C, h, dv)

    def scale(p):
        return jnp.exp(p[:, None] * log_g[None, :])[:, :, None]

    dist = jnp.abs(pos[:, None] - pos[None, :])
    decay_intra = jnp.exp(dist[None] * log_g[:, None, None])
    scores = jnp.einsum('bnihd,bnjhd->bnhij', qc, kc) * decay_intra
    out = jnp.einsum('bnhij,bnjhv->bnihv', scores, vc)

    kv_fwd = jnp.einsum('bnjhd,bnjhv->nbhdv', kc * scale(C - 1.0 - pos), vc)
    kv_bwd = jnp.einsum('bnjhd,bnjhv->nbhdv', kc * scale(pos), vc)
    g_chunk = jnp.exp(C * log_g)[None, :, None, None]

    def carry_step(S, kv):
        return S * g_chunk + kv, S

    S0 = jnp.zeros((b, h, dk, dv), jnp.float32)
    _, S_prev = lax.scan(carry_step, S0, kv_fwd)
    _, S_next = lax.scan(carry_step, S0, kv_bwd, reverse=True)
    out = out + jnp.einsum('bnihd,nbhdv->bnihv', qc * scale(pos + 1.0), S_prev)
    out = out + jnp.einsum('bnihd,nbhdv->bnihv', qc * scale(C - pos), S_next)
    return out.reshape(b, s, h, dv)


def rwkv7_scan(r, w, k, v, kk, kka, reverse):
    b, s, h, n = r.shape

    def step(S, inp):
        r_t, w_t, k_t, v_t, kk_t, kka_t = inp
        S = (S * w_t[:, :, None, :]
             - jnp.einsum('bhvn,bhn->bhv', S, kk_t)[..., None] * kka_t[:, :, None, :]
             + v_t[..., None] * k_t[:, :, None, :])
        return S, jnp.einsum('bhvn,bhn->bhv', S, r_t)

    xs = tuple(jnp.swapaxes(t, 0, 1) for t in (r, w, k, v, kk, kka))
    _, y = lax.scan(step, jnp.zeros((b, h, n, n), jnp.float32), xs, reverse=reverse)
    return jnp.swapaxes(y, 0, 1)


def token_mixer(x, w_in, ret_gn_g, ret_gn_b, mu, w0, w_up, a0, a_up, g_up,
                k_k, k_a, r_k, gn_g, gn_b, w_out):
    f32 = jnp.float32
    b, s, _ = x.shape
    z = x @ w_in
    z_ret, z_rwkv = z[..., :RET_COLS], z[..., RET_COLS:]

    def heads(t, h, n):
        return t.reshape(b, s, h, n).astype(f32)

    q, k, v, gate = jnp.split(z_ret, 4, axis=-1)
    qh = rotary(heads(q, RET_HEADS, RET_HEAD_DIM)) * RET_HEAD_DIM ** -0.5
    kh = rotary(heads(k, RET_HEADS, RET_HEAD_DIM))
    ret = retention_chunkwise(qh, kh, heads(v, RET_HEADS, RET_HEAD_DIM))
    ret = head_norm(ret, ret_gn_g, ret_gn_b, GN_EPS).reshape(b, s, RET_WIDTH)
    ret_out = jax.nn.silu(gate.astype(f32)) * ret

    z_rwkv = z_rwkv + mu * (centred_shift(z_rwkv) - z_rwkv)
    split_pts = np.cumsum(RWKV_SPLITS)[:-1].tolist()
    r, kx, vx, wd_f, wd_b, ad_f, ad_b, gd = jnp.split(z_rwkv, split_pts, axis=-1)
    H, N = RWKV_HEADS, RWKV_HEAD_DIM
    r = heads(r, H, N)
    kx = heads(kx, H, N)
    vx = heads(vx, H, N)
    kk = kx * k_k.reshape(H, N)
    kk = kk * lax.rsqrt(jnp.maximum(jnp.sum(kk * kk, -1, keepdims=True), 1e-24))

    def direction(d, wd, ad):
        pre = (w0[d] + jnp.tanh(wd) @ w_up[d]).astype(f32)
        decay = jnp.exp(-jnp.exp(-jax.nn.softplus(-pre) - 0.5)).reshape(b, s, H, N)
        a = jax.nn.sigmoid((a0[d] + ad @ a_up[d]).astype(f32)).reshape(b, s, H, N)
        kd = kx * (1.0 + (a - 1.0) * k_a.reshape(H, N))
        y = rwkv7_scan(r, decay, kd, vx, kk, kk * a, reverse=(d == 1))
        return y, kd

    y_f, k_f = direction(0, wd_f, ad_f)
    y_b, k_b = direction(1, wd_b, ad_b)
    y = head_norm(y_f + y_b, gn_g, gn_b, RWKV_GN_EPS)
    bonus = jnp.sum(r * (k_f + k_b) * r_k, -1, keepdims=True) * vx
    g = jax.nn.sigmoid(gd.astype(f32)) @ g_up
    rwkv_out = (y + bonus).reshape(b, s, RWKV_WIDTH) * g

    return jnp.concatenate([ret_out, rwkv_out], axis=-1).astype(x.dtype) @ w_out


def route(x2d, router_w, router_bias):
    T = x2d.shape[0]
    scores = jax.nn.sigmoid((x2d @ router_w).astype(jnp.float32))
    biased = scores + router_bias.astype(jnp.float32)
    group_score = lax.top_k(biased.reshape(T, N_GROUPS, N_EXPERTS // N_GROUPS), 2)[0].sum(-1)
    _, top_groups = lax.top_k(group_score, TOPK_GROUPS)
    group_mask = jax.nn.one_hot(top_groups, N_GROUPS).sum(1) > 0
    expert_mask = jnp.repeat(group_mask, N_EXPERTS // N_GROUPS, axis=1)
    _, idx = lax.top_k(jnp.where(expert_mask, biased, -jnp.inf), TOP_K)
    sel = jnp.take_along_axis(scores, idx, axis=-1)
    gates = sel / jnp.sum(sel, -1, keepdims=True) * ROUTE_SCALE
    return idx, gates


def routed_experts(x2d, idx, gates, w_gate, w_up, w_down):
    T, D = x2d.shape
    A = T * TOP_K
    G = MOE_BLOCK
    n_blocks = -(-(A + N_EXPERTS * (G - 1)) // G)
    P = n_blocks * G
    flat_e = idx.reshape(A)
    flat_tok = jnp.arange(A, dtype=jnp.int32) // TOP_K
    order = jnp.argsort(flat_e)
    e_sorted = flat_e[order]
    counts = jnp.bincount(flat_e, length=N_EXPERTS)
    padded = (counts + G - 1) // G * G
    pad_end = jnp.cumsum(padded)
    pad_start = pad_end - padded
    start = jnp.cumsum(counts) - counts
    dest = pad_start[e_sorted] + jnp.arange(A, dtype=jnp.int32) - start[e_sorted]
    slot_tok = jnp.full((P,), T, jnp.int32).at[dest].set(flat_tok[order])
    slot_gate = jnp.zeros((P,), jnp.float32).at[dest].set(gates.reshape(A)[order])
    block_expert = jnp.minimum(
        jnp.searchsorted(pad_end, jnp.arange(n_blocks, dtype=jnp.int32) * G, side='right'),
        N_EXPERTS - 1)
    x_pad = jnp.concatenate([x2d, jnp.zeros((1, D), x2d.dtype)], axis=0)

    def block_step(y, blk):
        tok, gate, e = blk
        xb = x_pad[tok]
        h = jax.nn.silu(xb @ w_gate[e]) * (xb @ w_up[e])
        out = (h @ w_down[e]).astype(jnp.float32) * gate[:, None]
        return y.at[tok].add(out), None

    y, _ = lax.scan(block_step, jnp.zeros((T + 1, D), jnp.float32),
                    (slot_tok.reshape(n_blocks, G), slot_gate.reshape(n_blocks, G), block_expert))
    return y[:T]


def moe_ffn(x, router_w, router_bias, w_gate, w_up, w_down, sh_gate, sh_up, sh_down):
    b, s, D = x.shape
    x2d = x.reshape(b * s, D)
    idx, gates = route(x2d, router_w, router_bias)
    routed = routed_experts(x2d, idx, gates, w_gate, w_up, w_down)
    shared = (jax.nn.silu(x2d @ sh_gate) * (x2d @ sh_up)) @ sh_down
    return (routed + shared.astype(jnp.float32)).reshape(b, s, D).astype(x.dtype)


def setup_inputs(seed: int = 0) -> dict:
    key = jax.random.key(seed)
    ks = jax.random.split(key, 32)
    nrm = lambda i, shape: jax.random.normal(ks[i], shape, jnp.float32)
    L = DEPTH
    return {
        "x_prompt": nrm(0, (BATCH, SEQ, D_MODEL)),
        "x_sample": nrm(1, (DEC_BATCH, DEC_SEQ, D_MODEL)),
        "w_in": nrm(2, (L, D_MODEL, D_IN)) * D_MODEL ** -0.5,
        "ret_gn_g": 1.0 + 0.02 * nrm(3, (L, RET_WIDTH)),
        "ret_gn_b": 0.02 * nrm(4, (L, RET_WIDTH)),
        "rwkv_mu": jax.random.uniform(ks[5], (L, RWKV_COLS), jnp.float32),
        "rwkv_w0": -1.0 + 0.5 * nrm(6, (L, 2, RWKV_WIDTH)),
        "rwkv_w_up": 0.1 * nrm(7, (L, 2, DECAY_RANK, RWKV_WIDTH)),
        "rwkv_a0": 0.5 * nrm(8, (L, 2, RWKV_WIDTH)),
        "rwkv_a_up": 0.5 * nrm(9, (L, 2, ICLR_RANK, RWKV_WIDTH)) * ICLR_RANK ** -0.5,
        "rwkv_g_up": nrm(10, (L, GATE_RANK, RWKV_WIDTH)) * GATE_RANK ** -0.5,
        "rwkv_k_k": 0.85 + 0.1 * nrm(11, (L, RWKV_WIDTH)),
        "rwkv_k_a": 1.0 + 0.1 * nrm(12, (L, RWKV_WIDTH)),
        "rwkv_r_k": 0.1 * nrm(13, (L, RWKV_HEADS, RWKV_HEAD_DIM)),
        "rwkv_gn_g": 1.0 + 0.02 * nrm(14, (L, RWKV_WIDTH)),
        "rwkv_gn_b": 0.02 * nrm(15, (L, RWKV_WIDTH)),
        "w_out": nrm(16, (L, D_MIX, D_MODEL)) * D_MIX ** -0.5 * DN_BETA,
        "ln1_g": 1.0 + 0.02 * nrm(17, (L, D_MODEL)),
        "ln1_b": 0.02 * nrm(18, (L, D_MODEL)),
        "router_w": nrm(19, (L, D_MODEL, N_EXPERTS)) * D_MODEL ** -0.5,
        "router_bias": 0.01 * nrm(20, (L, N_EXPERTS)),
        "exp_w_gate": nrm(21, (L, N_EXPERTS, D_MODEL, D_EXPERT)) * D_MODEL ** -0.5,
        "exp_w_up": nrm(22, (L, N_EXPERTS, D_MODEL, D_EXPERT)) * D_MODEL ** -0.5,
        "exp_w_down": nrm(23, (L, N_EXPERTS, D_EXPERT, D_MODEL)) * D_EXPERT ** -0.5 * DN_BETA,
        "sh_w_gate": nrm(24, (L, D_MODEL, D_SHARED)) * D_MODEL ** -0.5,
        "sh_w_up": nrm(25, (L, D_MODEL, D_SHARED)) * D_MODEL ** -0.5,
        "sh_w_down": nrm(26, (L, D_SHARED, D_MODEL)) * D_SHARED ** -0.5 * DN_BETA,
        "ln2_g": 1.0 + 0.02 * nrm(27, (L, D_MODEL)),
        "ln2_b": 0.02 * nrm(28, (L, D_MODEL)),
    }


def reference(x_prompt, x_sample, w_in, ret_gn_g, ret_gn_b, rwkv_mu, rwkv_w0, rwkv_w_up,
              rwkv_a0, rwkv_a_up, rwkv_g_up, rwkv_k_k, rwkv_k_a, rwkv_r_k, rwkv_gn_g,
              rwkv_gn_b, w_out, ln1_g, ln1_b, router_w, router_bias, exp_w_gate, exp_w_up,
              exp_w_down, sh_w_gate, sh_w_up, sh_w_down, ln2_g, ln2_b):
    def trunk(x):
        for l in range(DEPTH):
            m = token_mixer(x, w_in[l], ret_gn_g[l], ret_gn_b[l], rwkv_mu[l], rwkv_w0[l],
                            rwkv_w_up[l], rwkv_a0[l], rwkv_a_up[l], rwkv_g_up[l], rwkv_k_k[l],
                            rwkv_k_a[l], rwkv_r_k[l], rwkv_gn_g[l], rwkv_gn_b[l], w_out[l])
            x = layer_norm(DN_ALPHA * x + m, ln1_g[l], ln1_b[l])
            f = moe_ffn(x, router_w[l], router_bias[l], exp_w_gate[l], exp_w_up[l],
                        exp_w_down[l], sh_w_gate[l], sh_w_up[l], sh_w_down[l])
            x = layer_norm(DN_ALPHA * x + f, ln2_g[l], ln2_b[l])
        return x

    y_prompt = trunk(x_prompt)
    y_sample = trunk(x_sample)
    return (y_prompt, y_sample)
```

```python
import functools
import math

import jax
import jax.numpy as jnp
from jax import lax
from jax.experimental import pallas as pl
from jax.experimental.pallas import tpu as pltpu

F32 = jnp.float32
BF16 = jnp.bfloat16

RET_HEADS = 4
RET_CHUNK = 128
ROPE_BASE = 10000.0
TOP_K = 8
N_GROUPS = 8
TOPK_GROUPS = 4
ROUTE_SCALE = 2.5
MOE_BLOCK = 256
LN_EPS = 1e-5
GN_EPS = 1e-5
RWKV_GN_EPS = 64e-5

LANES = 128
SUBLANES = 8
VMEM_LIMIT_BYTES = 56 * 1024 * 1024

SCAN_CHUNK = 64


def _cparams(semantics):
    return pltpu.CompilerParams(dimension_semantics=semantics, vmem_limit_bytes=VMEM_LIMIT_BYTES)


def _dot(a, b):
    return jnp.dot(a.astype(BF16), b.astype(BF16), preferred_element_type=F32)


def _dot_nt(a, b):
    return lax.dot_general(a.astype(BF16), b.astype(BF16), (((1,), (1,)), ((), ())),
                           preferred_element_type=F32)


def _split3(x):
    hi = x.astype(BF16)
    r1 = x - hi.astype(F32)
    mid = r1.astype(BF16)
    lo = (r1 - mid.astype(F32)).astype(BF16)
    return hi, mid, lo


def _dot_exact_rhs(x, w_bf16):
    hi, mid, lo = _split3(x)
    out = jnp.dot(hi, w_bf16, preferred_element_type=F32)
    out += jnp.dot(mid, w_bf16, preferred_element_type=F32)
    out += jnp.dot(lo, w_bf16, preferred_element_type=F32)
    return out


def _dot_exact_lhs(w_bf16, x):
    hi, mid, lo = _split3(x)
    out = jnp.dot(w_bf16, hi, preferred_element_type=F32)
    out += jnp.dot(w_bf16, mid, preferred_element_type=F32)
    out += jnp.dot(w_bf16, lo, preferred_element_type=F32)
    return out


def _sigmoid(x):
    return 1.0 / (1.0 + jnp.exp(-x))


def _silu(x):
    return x * _sigmoid(x)


def _layer_norm(h, g, b):
    mu = jnp.mean(h, axis=-1, keepdims=True)
    d = h - mu
    var = jnp.mean(d * d, axis=-1, keepdims=True)
    return d * lax.rsqrt(var + LN_EPS) * g + b


def _inproj_body(x_ref, wr_ref, ww_ref, zr_ref, zw_ref):
    xb = x_ref[...].astype(BF16)
    zr_ref[...] = jnp.dot(xb, wr_ref[...], preferred_element_type=F32)
    zw_ref[...] = jnp.dot(xb, ww_ref[...], preferred_element_type=F32)


def _in_proj(x2d, w_ret, w_rwkv, tm):
    T, D = x2d.shape
    nr, nw = w_ret.shape[1], w_rwkv.shape[1]
    return pl.pallas_call(
        _inproj_body,
        grid=(T // tm,),
        in_specs=[pl.BlockSpec((tm, D), lambda i: (i, 0)),
                  pl.BlockSpec((D, nr), lambda i: (0, 0)),
                  pl.BlockSpec((D, nw), lambda i: (0, 0))],
        out_specs=[pl.BlockSpec((tm, nr), lambda i: (i, 0)),
                   pl.BlockSpec((tm, nw), lambda i: (i, 0))],
        out_shape=[jax.ShapeDtypeStruct((T, nr), F32), jax.ShapeDtypeStruct((T, nw), F32)],
        compiler_params=_cparams(("parallel",)),
        name="in_proj",
    )(x2d, w_ret, w_rwkv)


def _ret_body(q_ref, k_ref, v_ref, gt_ref, cos_ref, sin_ref, lg_ref, gg_ref, gb_ref, o_ref,
              qs_ref, ks_ref, acc_ref, *, qscale):
    C = RET_CHUNK
    s = q_ref.shape[0]
    n = s // C
    lg = lg_ref[...]
    pos = lax.broadcasted_iota(jnp.int32, (C, C), 0).astype(F32)
    col = lax.broadcasted_iota(jnp.int32, (C, C), 1).astype(F32)
    sc_q_prev = jnp.exp((pos + 1.0) * lg)
    sc_k_fwd = jnp.exp((C - 1.0 - pos) * lg)
    sc_k_bwd = jnp.exp(pos * lg)
    sc_q_next = jnp.exp((C - pos) * lg)
    g_chunk = jnp.exp(float(C) * lg)
    decay = jnp.exp(jnp.abs(pos - col) * lg)

    cos = cos_ref[...]
    sin = sin_ref[...]
    q = q_ref[...]
    qs_ref[...] = (q * cos + pltpu.roll(q, C // 2, 1) * sin) * qscale
    k = k_ref[...]
    ks_ref[...] = k * cos + pltpu.roll(k, C // 2, 1) * sin

    def fwd(c, S):
        r0 = pl.multiple_of(c * C, C)
        qc = qs_ref[pl.ds(r0, C), :]
        kc = ks_ref[pl.ds(r0, C), :]
        vb = v_ref[pl.ds(r0, C), :].astype(BF16)
        sc = _dot_nt(qc, kc) * decay
        out = _dot(sc, vb) + _dot(qc * sc_q_prev, S)
        acc_ref[pl.ds(r0, C), :] = out
        return S * g_chunk + _dot((kc * sc_k_fwd).T, vb)

    lax.fori_loop(0, n, fwd, jnp.zeros((C, C), F32))

    def bwd(i, S):
        r0 = pl.multiple_of((n - 1 - i) * C, C)
        qc = qs_ref[pl.ds(r0, C), :]
        kc = ks_ref[pl.ds(r0, C), :]
        vb = v_ref[pl.ds(r0, C), :].astype(BF16)
        acc_ref[pl.ds(r0, C), :] += _dot(qc * sc_q_next, S)
        return S * g_chunk + _dot((kc * sc_k_bwd).T, vb)

    lax.fori_loop(0, n, bwd, jnp.zeros((C, C), F32))

    y = acc_ref[...]
    mu = jnp.mean(y, axis=-1, keepdims=True)
    d = y - mu
    var = jnp.mean(d * d, axis=-1, keepdims=True)
    yn = d * lax.rsqrt(var + GN_EPS) * gg_ref[...] + gb_ref[...]
    o_ref[...] = _silu(gt_ref[...]) * yn


def _retention(z_ret, cos, sin, logg, gn_g, gn_b):
    b, s, _ = z_ret.shape
    C = RET_CHUNK
    H = RET_HEADS
    assert C == LANES and s % C == 0
    blk = lambda off: pl.BlockSpec((None, s, C), lambda i, h, off=off: (i, 0, off + h))
    return pl.pallas_call(
        functools.partial(_ret_body, qscale=float(C) ** -0.5),
        grid=(b, H),
        in_specs=[blk(0), blk(H), blk(2 * H), blk(3 * H),
                  pl.BlockSpec((s, C), lambda i, h: (0, 0)),
                  pl.BlockSpec((s, C), lambda i, h: (0, 0)),
                  pl.BlockSpec((None, 1, C), lambda i, h: (h, 0, 0)),
                  pl.BlockSpec((1, C), lambda i, h: (0, h)),
                  pl.BlockSpec((1, C), lambda i, h: (0, h))],
        out_specs=pl.BlockSpec((None, s, C), lambda i, h: (i, 0, h)),
        out_shape=jax.ShapeDtypeStruct((b, s, H * C), F32),
        scratch_shapes=[pltpu.VMEM((s, C), F32), pltpu.VMEM((s, C), F32), pltpu.VMEM((s, C), F32)],
        compiler_params=_cparams(("parallel", "parallel")),
        name="retention",
    )(z_ret, z_ret, z_ret, z_ret, cos, sin, logg, gn_g, gn_b)


def _prep_body(z_ref, zp_ref, zn_ref, mu_ref, wup_ref, aup_ref, w0_ref, a0_ref, gup_ref,
               kkp_ref, ka_ref, rk_ref, bd_ref,
               r_o, v_o, kk_o, lw_o, kd_o, kka_o, bonus_o, g_o, *, width):
    t = pl.program_id(1)
    nt = pl.num_programs(1)
    W = width
    z = z_ref[...]
    ts = z.shape[0]
    row = lax.broadcasted_iota(jnp.int32, (ts, 1), 0)
    prev_row = jnp.where(t > 0, zp_ref[SUBLANES - 1:SUBLANES, :], 0.0)
    next_row = jnp.where(t < nt - 1, zn_ref[0:1, :], 0.0)
    prev = jnp.where(row == 0, prev_row, pltpu.roll(z, 1, 0))
    nxt = jnp.where(row == ts - 1, next_row, pltpu.roll(z, ts - 1, 0))
    zs = z + mu_ref[...] * (0.5 * (prev + nxt) - z)

    r = zs[:, 0:W]
    kx = zs[:, W:2 * W]
    vx = zs[:, 2 * W:3 * W]
    wd = jnp.tanh(zs[:, 3 * W:3 * W + LANES])
    ad = zs[:, 3 * W + LANES:3 * W + 2 * LANES]
    gd = _sigmoid(zs[:, 3 * W + 2 * LANES:3 * W + 3 * LANES])
    bd = bd_ref[...]

    kk = kx * kkp_ref[...]
    ssq = _dot_exact_rhs(kk * kk, bd)
    kk = kk * lax.rsqrt(jnp.maximum(ssq, 1e-24))
    ka = ka_ref[...]
    ksum = None
    for d in range(2):
        pre = w0_ref[d:d + 1, :] + _dot(wd, wup_ref[d])
        lw_o[d] = -math.exp(-0.5) * _sigmoid(pre)
        a = _sigmoid(a0_ref[d:d + 1, :] + _dot(ad, aup_ref[d]))
        kd = kx * (1.0 + (a - 1.0) * ka)
        kd_o[d] = kd
        kka_o[d] = kk * a
        ksum = kd if ksum is None else ksum + kd
    r_o[...] = r
    v_o[...] = vx
    kk_o[...] = kk
    bonus_o[...] = _dot_exact_rhs(r * ksum * rk_ref[...], bd) * vx
    g_o[...] = _dot(gd, gup_ref[...])


def _rwkv_prep(z_rwkv, p, ts):
    b, s, ncol = z_rwkv.shape
    W = p["rwkv_width"]
    nt = s // ts
    hb = ts // SUBLANES
    last = s // SUBLANES - 1
    full = lambda a: pl.BlockSpec(a.shape, lambda i, t, nd=a.ndim: (0,) * nd)
    out_tok = pl.BlockSpec((None, ts, W), lambda i, t: (i, t, 0))
    out_dir = pl.BlockSpec((2, None, ts, W), lambda i, t: (0, i, t, 0))
    tok_shape = jax.ShapeDtypeStruct((b, s, W), F32)
    dir_shape = jax.ShapeDtypeStruct((2, b, s, W), F32)
    consts = [p["mu"], p["wup_pad"], p["aup_pad"], p["w0"], p["a0"], p["gup"],
              p["k_k"], p["k_a"], p["r_k"], p["head_ones"]]
    return pl.pallas_call(
        functools.partial(_prep_body, width=W),
        grid=(b, nt),
        in_specs=[pl.BlockSpec((None, ts, ncol), lambda i, t: (i, t, 0)),
                  pl.BlockSpec((None, SUBLANES, ncol), lambda i, t: (i, jnp.maximum(t * hb - 1, 0), 0)),
                  pl.BlockSpec((None, SUBLANES, ncol), lambda i, t: (i, jnp.minimum((t + 1) * hb, last), 0)),
                  ] + [full(a) for a in consts],
        out_specs=[out_tok, out_tok, out_tok, out_dir, out_dir, out_dir, out_tok, out_tok],
        out_shape=[tok_shape, tok_shape, tok_shape, dir_shape, dir_shape, dir_shape, tok_shape, tok_shape],
        compiler_params=_cparams(("parallel", "parallel")),
        name="rwkv_prep",
    )(z_rwkv, z_rwkv, z_rwkv, *consts)


def _scan_body(r_ref, kk_ref, v_ref, lw_ref, kd_ref, kka_ref, y_ref, st_ref, *, n_pairs):
    L = SCAN_CHUNK
    H = 2 * L
    assert H == LANES
    d = pl.program_id(1)
    t = pl.program_id(2)
    tt = r_ref.shape[0]
    nch = tt // L
    fwd = d == 0

    @pl.when(t == 0)
    def _():
        st_ref[...] = jnp.zeros_like(st_ref)

    ii = lax.broadcasted_iota(jnp.int32, (H, H), 0)
    jj = lax.broadcasted_iota(jnp.int32, (H, H), 1)
    sgn = jnp.where(fwd, 1, -1)
    same = (ii < L) == (jj < L)
    strict = jnp.logical_and(same, (ii - jj) * sgn > 0)
    incl = jnp.logical_and(same, (ii - jj) * sgn >= 0)
    eye = ii == jj
    li = lax.broadcasted_iota(jnp.int32, (L, L), 0)
    lj = lax.broadcasted_iota(jnp.int32, (L, L), 1)
    tri = jnp.where((li - lj) * sgn >= 0, 1.0, 0.0).astype(BF16)
    head0 = lax.broadcasted_iota(jnp.int32, (L, H), 1) < L

    def stack(x):
        return jnp.concatenate([jnp.where(head0, x, 0.0), jnp.where(head0, 0.0, x)], axis=0)

    def chunk(j, carry):
        c = jnp.where(fwd, j, nch - 1 - j)
        r0 = pl.multiple_of(c * L, L)
        rows = pl.ds(r0, L)
        for hp in range(n_pairs):
            ls = slice(hp * H, (hp + 1) * H)
            lw = lw_ref[rows, ls]
            cum = _dot_exact_lhs(tri, lw)
            tot = jnp.sum(lw, axis=0, keepdims=True)
            e_incl = jnp.exp(cum)
            e_excl = jnp.exp(cum - lw)
            e_inv = jnp.exp(-cum)
            e_rem = jnp.exp(tot - cum)
            kk = kk_ref[rows, ls]
            kka = kka_ref[rows, ls]
            kd = kd_ref[rows, ls]
            Kk = stack(kk * e_excl)
            R = stack(r_ref[rows, ls] * e_incl)
            B = stack(kka * e_inv)
            Kd = stack(kd * e_inv)
            Bh = stack(kka * e_rem)
            Kh = stack(kd * e_rem)
            V = stack(v_ref[rows, ls]).astype(BF16)

            gram = _dot_nt(jnp.concatenate([Kk, R], axis=0), jnp.concatenate([B, Kd], axis=0))
            N = jnp.where(strict, gram[:H, :H], 0.0)
            Aak = jnp.where(strict, gram[:H, H:], 0.0)
            Cm = jnp.where(incl, gram[H:, :H], 0.0)
            Dm = jnp.where(incl, gram[H:, H:], 0.0)
            AD = _dot(jnp.concatenate([Aak, Dm], axis=0), V)
            X = jnp.concatenate([Kk, AD[:H]], axis=1)
            Np = N
            n_fac = L.bit_length() - 1
            for f in range(n_fac):
                NX = _dot(Np, X)
                X = X - NX if f == 0 else X + NX
                if f < n_fac - 1:
                    Np = _dot(Np, Np)
            QY = jnp.concatenate([R, AD[H:]], axis=1) - _dot(Cm, X)
            BW = _dot(Bh.T, X)
            KV = _dot(Kh.T, V)
            pL = jnp.exp(tot)
            M = jnp.where(eye, pL, 0.0) - BW[:, :H]
            G = KV - BW[:, H:]
            ST = st_ref[hp]
            QM = _dot(jnp.concatenate([QY[:, :H], M], axis=0), ST)
            Ys = QM[:H] + QY[:, H:]
            st_ref[hp] = QM[H:] + G
            y_ref[rows, ls] = Ys[:L] + Ys[L:]
        return carry

    lax.fori_loop(0, nch, chunk, 0)


def _rwkv_scan(r, kk, v, lw, kd, kka, tt):
    b, s, W = r.shape
    nt = s // tt
    n_pairs = W // LANES
    tb = lambda d, t: t + d * (nt - 1 - 2 * t)
    tok = pl.BlockSpec((None, tt, W), lambda i, d, t: (i, tb(d, t), 0))
    dirs = pl.BlockSpec((None, None, tt, W), lambda i, d, t: (d, i, tb(d, t), 0))
    return pl.pallas_call(
        functools.partial(_scan_body, n_pairs=n_pairs),
        grid=(b, 2, nt),
        in_specs=[tok, tok, tok, dirs, dirs, dirs],
        out_specs=dirs,
        out_shape=jax.ShapeDtypeStruct((2, b, s, W), F32),
        scratch_shapes=[pltpu.VMEM((n_pairs, LANES, LANES), F32)],
        compiler_params=_cparams(("parallel", "parallel", "arbitrary")),
        name="rwkv_scan",
    )(r, kk, v, lw, kd, kka)


def _mix_body(y_ref, bonus_ref, g_ref, ret_ref, x_ref, gng_ref, gnb_ref, avg_ref, wo1_ref, wo2_ref,
              l1g_ref, l1b_ref, rwt_ref, rb_ref, x1_ref, idx_ref, gate_ref, *, alpha):
    y = y_ref[0] + y_ref[1]
    avg = avg_ref[...]
    mu = _dot_exact_rhs(y, avg)
    dl = y - mu
    var = _dot_exact_rhs(dl * dl, avg)
    yn = dl * lax.rsqrt(var + RWKV_GN_EPS) * gng_ref[...] + gnb_ref[...]
    rw = (yn + bonus_ref[...]) * g_ref[...]
    m = _dot(ret_ref[...], wo1_ref[...]) + _dot(rw, wo2_ref[...])
    x1 = _layer_norm(alpha * x_ref[...] + m, l1g_ref[...], l1b_ref[...])
    x1_ref[...] = x1

    scores = _sigmoid(_dot_nt(rwt_ref[...], x1))
    E, tm = scores.shape
    GS = E // N_GROUPS
    NEG = -jnp.inf
    biased = scores + rb_ref[...]
    rowi = lax.broadcasted_iota(jnp.int32, (E, tm), 0)
    ri = lax.broadcasted_iota(jnp.int32, (GS, tm), 0)
    gs_rows = []
    for gi in range(N_GROUPS):
        blk = biased[gi * GS:(gi + 1) * GS, :]
        m1 = jnp.max(blk, axis=0, keepdims=True)
        i1 = jnp.min(jnp.where(blk == m1, ri, GS), axis=0, keepdims=True)
        m2 = jnp.max(jnp.where(ri == i1, NEG, blk), axis=0, keepdims=True)
        gs_rows.append(m1 + m2)
    cur = jnp.concatenate(gs_rows, axis=0)
    gidx = lax.broadcasted_iota(jnp.int32, (N_GROUPS, tm), 0)
    row_group = rowi // GS
    emask = jnp.zeros((E, tm), F32)
    for _ in range(TOPK_GROUPS):
        mx = jnp.max(cur, axis=0, keepdims=True)
        ix = jnp.min(jnp.where(cur == mx, gidx, N_GROUPS), axis=0, keepdims=True)
        emask = jnp.where(row_group == ix, 1.0, emask)
        cur = jnp.where(gidx == ix, NEG, cur)
    cur = jnp.where(emask > 0.5, biased, NEG)
    idxs, sels = [], []
    for _ in range(TOP_K):
        mx = jnp.max(cur, axis=0, keepdims=True)
        ix = jnp.min(jnp.where(cur == mx, rowi, E), axis=0, keepdims=True)
        hit = rowi == ix
        sels.append(jnp.sum(jnp.where(hit, scores, 0.0), axis=0, keepdims=True))
        idxs.append(ix)
        cur = jnp.where(hit, NEG, cur)
    sel = jnp.concatenate(sels, axis=0)
    idx_ref[...] = jnp.concatenate(idxs, axis=0)
    gate_ref[...] = sel / jnp.sum(sel, axis=0, keepdims=True) * ROUTE_SCALE


def _mix_out(y2, bonus, g, ret_out, x2d, p, tm):
    T, D = x2d.shape
    W = bonus.shape[1]
    Wr = ret_out.shape[1]
    E = p["router_wt"].shape[0]
    full = lambda a: pl.BlockSpec(a.shape, lambda i, nd=a.ndim: (0,) * nd)
    consts = [p["gn_g"], p["gn_b"], p["head_avg"], p["wo_ret"], p["wo_rwkv"], p["ln1_g"], p["ln1_b"],
              p["router_wt"], p["router_b"]]
    return pl.pallas_call(
        functools.partial(_mix_body, alpha=p["alpha"]),
        grid=(T // tm,),
        in_specs=[pl.BlockSpec((2, tm, W), lambda i: (0, i, 0)),
                  pl.BlockSpec((tm, W), lambda i: (i, 0)),
                  pl.BlockSpec((tm, W), lambda i: (i, 0)),
                  pl.BlockSpec((tm, Wr), lambda i: (i, 0)),
                  pl.BlockSpec((tm, D), lambda i: (i, 0))] + [full(a) for a in consts],
        out_specs=[pl.BlockSpec((tm, D), lambda i: (i, 0)),
                   pl.BlockSpec((TOP_K, tm), lambda i: (0, i)),
                   pl.BlockSpec((TOP_K, tm), lambda i: (0, i))],
        out_shape=[jax.ShapeDtypeStruct((T, D), F32),
                   jax.ShapeDtypeStruct((TOP_K, T), jnp.int32),
                   jax.ShapeDtypeStruct((TOP_K, T), F32)],
        compiler_params=_cparams(("parallel",)),
        name="mix_out",
    )(y2, bonus, g, ret_out, x2d, *consts)


def _moe_body(be_ref, nu_ref, tok_ref, x_hbm, wg_ref, wu_ref, wd_ref, o_ref, xs_ref, sem):
    i = pl.program_id(0)
    G = xs_ref.shape[0]

    def row_copy(t, j):
        return pltpu.make_async_copy(x_hbm.at[pl.ds(t, 1), :], xs_ref.at[pl.ds(j, 1), :], sem)

    @pl.when(i < nu_ref[0])
    def _():
        def issue(j, c):
            row_copy(tok_ref[0, j], j).start()
            return c

        lax.fori_loop(0, G, issue, 0, unroll=8)

        def drain(j, c):
            row_copy(0, j).wait()
            return c

        lax.fori_loop(0, G, drain, 0, unroll=8)
        xb = xs_ref[...].astype(BF16)
        h = _silu(_dot(xb, wg_ref[...])) * _dot(xb, wu_ref[...])
        o_ref[...] = _dot(h, wd_ref[...])

    @pl.when(i >= nu_ref[0])
    def _():
        o_ref[...] = jnp.zeros_like(o_ref)


def _moe_ffn(x1, slot_tok, block_expert, n_used, w_gate, w_up, w_down):
    T, D = x1.shape
    n_blocks, _, G = slot_tok.shape
    E, _, De = w_gate.shape
    grid_spec = pltpu.PrefetchScalarGridSpec(
        num_scalar_prefetch=2,
        grid=(n_blocks,),
        in_specs=[pl.BlockSpec((None, 1, G), lambda i, be, nu: (i, 0, 0), memory_space=pltpu.SMEM),
                  pl.BlockSpec(memory_space=pl.ANY),
                  pl.BlockSpec((None, D, De), lambda i, be, nu: (be[i], 0, 0)),
                  pl.BlockSpec((None, D, De), lambda i, be, nu: (be[i], 0, 0)),
                  pl.BlockSpec((None, De, D), lambda i, be, nu: (be[i], 0, 0))],
        out_specs=pl.BlockSpec((G, D), lambda i, be, nu: (i, 0)),
        scratch_shapes=[pltpu.VMEM((G, D), F32), pltpu.SemaphoreType.DMA(())],
    )
    return pl.pallas_call(
        _moe_body,
        grid_spec=grid_spec,
        out_shape=jax.ShapeDtypeStruct((n_blocks * G, D), F32),
        compiler_params=_cparams(("arbitrary",)),
        name="moe_ffn",
    )(block_expert, n_used, slot_tok, x1, w_gate, w_up, w_down)


def _comb_body(pos_ref, ys_hbm, x1_ref, gt_ref, sg_ref, su_ref, sd_ref, l2g_ref, l2b_ref, o_ref,
               buf_ref, sem, *, alpha):
    tm = x1_ref.shape[0]
    n = tm * TOP_K

    def row_copy(p, j):
        return pltpu.make_async_copy(ys_hbm.at[pl.ds(p, 1), :],
                                     buf_ref.at[j % TOP_K, pl.ds(j // TOP_K, 1), :], sem)

    def issue(j, c):
        row_copy(pos_ref[0, j], j).start()
        return c

    lax.fori_loop(0, n, issue, 0, unroll=8)

    x1 = x1_ref[...]
    xb = x1.astype(BF16)
    shared = _dot(_silu(_dot(xb, sg_ref[...])) * _dot(xb, su_ref[...]), sd_ref[...])

    def drain(j, c):
        row_copy(0, j).wait()
        return c

    lax.fori_loop(0, n, drain, 0, unroll=8)
    gt = gt_ref[...]
    routed = gt[:, 0:1] * buf_ref[0]
    for k in range(1, TOP_K):
        routed += gt[:, k:k + 1] * buf_ref[k]
    o_ref[...] = _layer_norm(alpha * x1 + (routed + shared), l2g_ref[...], l2b_ref[...])


def _combine(pos, ys, x1, gates_t, p, tm):
    T, D = x1.shape
    nt = T // tm
    full = lambda a: pl.BlockSpec(a.shape, lambda i, nd=a.ndim: (0,) * nd)
    consts = [p["sh_gate"], p["sh_up"], p["sh_down"], p["ln2_g"], p["ln2_b"]]
    return pl.pallas_call(
        functools.partial(_comb_body, alpha=p["alpha"]),
        grid=(nt,),
        in_specs=[pl.BlockSpec((None, 1, tm * TOP_K), lambda i: (i, 0, 0), memory_space=pltpu.SMEM),
                  pl.BlockSpec(memory_space=pl.ANY),
                  pl.BlockSpec((tm, D), lambda i: (i, 0)),
                  pl.BlockSpec((tm, TOP_K), lambda i: (i, 0))] + [full(a) for a in consts],
        out_specs=pl.BlockSpec((tm, D), lambda i: (i, 0)),
        out_shape=jax.ShapeDtypeStruct((T, D), F32),
        scratch_shapes=[pltpu.VMEM((TOP_K, tm, D), F32), pltpu.SemaphoreType.DMA(())],
        compiler_params=_cparams(("arbitrary",)),
        name="combine",
    )(pos.reshape(nt, 1, tm * TOP_K), ys, x1, gates_t, *consts)


def _dispatch_plan(idx_t, n_experts):
    G = MOE_BLOCK
    T = idx_t.shape[1]
    A = T * TOP_K
    flat_e = idx_t.T.reshape(A)
    order = jnp.argsort(flat_e)
    e_sorted = flat_e[order]
    counts = jnp.bincount(flat_e, length=n_experts)
    padded = (counts + G - 1) // G * G
    pad_end = jnp.cumsum(padded)
    pad_start = pad_end - padded
    start = jnp.cumsum(counts) - counts
    dest = (pad_start[e_sorted] + jnp.arange(A, dtype=jnp.int32) - start[e_sorted]).astype(jnp.int32)
    n_blocks = -(-(A + n_experts * (G - 1)) // G)
    slot_tok = jnp.zeros((n_blocks * G,), jnp.int32).at[dest].set((order // TOP_K).astype(jnp.int32))
    pos = jnp.zeros((A,), jnp.int32).at[order].set(dest)
    block_expert = jnp.minimum(
        jnp.searchsorted(pad_end, jnp.arange(n_blocks, dtype=jnp.int32) * G, side="right"),
        n_experts - 1).astype(jnp.int32)
    n_used = (pad_end[-1:] // G).astype(jnp.int32)
    return slot_tok.reshape(n_blocks, 1, G), pos, block_expert, n_used


def _rotary_tables(s, d):
    inv = ROPE_BASE ** (-jnp.arange(0, d, 2, dtype=F32) / d)
    ang = jnp.arange(s, dtype=F32)[:, None] * inv[None, :]
    cos = jnp.cos(ang)
    sin = jnp.sin(ang)
    return jnp.concatenate([cos, cos], axis=-1), jnp.concatenate([-sin, sin], axis=-1)


def _layer_params(l, depth, w_in, ret_gn_g, ret_gn_b, rwkv_mu, rwkv_w0, rwkv_w_up, rwkv_a0, rwkv_a_up,
                  rwkv_g_up, rwkv_k_k, rwkv_k_a, rwkv_r_k, rwkv_gn_g, rwkv_gn_b, w_out, ln1_g, ln1_b,
                  router_w, router_bias, exp_w_gate, exp_w_up, exp_w_down, sh_w_gate, sh_w_up,
                  sh_w_down, ln2_g, ln2_b):
    ret_w = ret_gn_g.shape[-1]
    W = rwkv_gn_g.shape[-1]
    n_heads, hd = rwkv_r_k.shape[-2:]
    rank_w = rwkv_w_up.shape[2]
    rank_a = rwkv_a_up.shape[2]
    assert rank_w * 2 == LANES and rank_a * 2 == LANES and rwkv_g_up.shape[1] == LANES
    assert hd * 2 == LANES and SCAN_CHUNK == hd
    row = lambda a: a.reshape(1, -1).astype(F32)
    zw = jnp.zeros((rank_w, W), F32)
    za = jnp.zeros((rank_a, W), F32)
    head_id = jnp.arange(W) // hd
    same_head = (head_id[:, None] == head_id[None, :])
    wi = w_in[l]
    return dict(
        alpha=float((2 * depth) ** 0.25),
        rwkv_width=W,
        w_ret=wi[:, :4 * ret_w].astype(BF16),
        w_rwkv=wi[:, 4 * ret_w:].astype(BF16),
        ret_gn_g=row(ret_gn_g[l]), ret_gn_b=row(ret_gn_b[l]),
        logg=jnp.broadcast_to(
            jnp.log1p(-jnp.exp2(-5.0 - jnp.arange(RET_HEADS, dtype=F32)))[:, None, None],
            (RET_HEADS, 1, LANES)),
        mu=row(rwkv_mu[l]),
        wup_pad=jnp.stack([jnp.concatenate([rwkv_w_up[l, 0], zw], 0),
                           jnp.concatenate([zw, rwkv_w_up[l, 1]], 0)]).astype(BF16),
        aup_pad=jnp.stack([jnp.concatenate([rwkv_a_up[l, 0], za], 0),
                           jnp.concatenate([za, rwkv_a_up[l, 1]], 0)]).astype(BF16),
        w0=rwkv_w0[l].astype(F32), a0=rwkv_a0[l].astype(F32),
        gup=rwkv_g_up[l].astype(BF16),
        k_k=row(rwkv_k_k[l]), k_a=row(rwkv_k_a[l]), r_k=row(rwkv_r_k[l]),
        head_ones=same_head.astype(BF16),
        head_avg=(same_head.astype(F32) / hd).astype(BF16),
        gn_g=row(rwkv_gn_g[l]), gn_b=row(rwkv_gn_b[l]),
        wo_ret=w_out[l, :ret_w].astype(BF16), wo_rwkv=w_out[l, ret_w:].astype(BF16),
        ln1_g=row(ln1_g[l]), ln1_b=row(ln1_b[l]),
        router_wt=router_w[l].T.astype(BF16), router_b=router_bias[l].reshape(-1, 1).astype(F32),
        exp_gate=exp_w_gate[l], exp_up=exp_w_up[l], exp_down=exp_w_down[l],
        sh_gate=sh_w_gate[l].astype(BF16), sh_up=sh_w_up[l].astype(BF16), sh_down=sh_w_down[l].astype(BF16),
        ln2_g=row(ln2_g[l]), ln2_b=row(ln2_b[l]),
    )


def _pick(n, pref):
    t = min(n, pref)
    while n % t:
        t //= 2
    return t


def _layer(x, p):
    b, s, D = x.shape
    T = b * s
    x2d = x.reshape(T, D)
    tm = _pick(T, 256)
    z_ret, z_rwkv = _in_proj(x2d, p["w_ret"], p["w_rwkv"], tm)
    cos, sin = _rotary_tables(s, RET_CHUNK)
    ret_out = _retention(z_ret.reshape(b, s, -1), cos, sin, p["logg"], p["ret_gn_g"], p["ret_gn_b"])
    r, v, kk, lw, kd, kka, bonus, g = _rwkv_prep(z_rwkv.reshape(b, s, -1), p, _pick(s, 256))
    y2 = _rwkv_scan(r, kk, v, lw, kd, kka, _pick(s, 512))
    W = p["rwkv_width"]
    x1, idx_t, gates = _mix_out(y2.reshape(2, T, W), bonus.reshape(T, W), g.reshape(T, W),
                                ret_out.reshape(T, -1), x2d, p, tm)
    slot_tok, pos, block_expert, n_used = _dispatch_plan(idx_t, p["exp_gate"].shape[0])
    ys = _moe_ffn(x1, slot_tok, block_expert, n_used, p["exp_gate"], p["exp_up"], p["exp_down"])
    out = _combine(pos, ys, x1, gates.T, p, _pick(T, 128))
    return out.reshape(b, s, D)


def kernel(x_prompt, x_sample, w_in, ret_gn_g, ret_gn_b, rwkv_mu, rwkv_w0, rwkv_w_up, rwkv_a0, rwkv_a_up,
           rwkv_g_up, rwkv_k_k, rwkv_k_a, rwkv_r_k, rwkv_gn_g, rwkv_gn_b, w_out, ln1_g, ln1_b, router_w,
           router_bias, exp_w_gate, exp_w_up, exp_w_down, sh_w_gate, sh_w_up, sh_w_down, ln2_g, ln2_b):
    weights = (w_in, ret_gn_g, ret_gn_b, rwkv_mu, rwkv_w0, rwkv_w_up, rwkv_a0, rwkv_a_up, rwkv_g_up,
               rwkv_k_k, rwkv_k_a, rwkv_r_k, rwkv_gn_g, rwkv_gn_b, w_out, ln1_g, ln1_b, router_w,
               router_bias, exp_w_gate, exp_w_up, exp_w_down, sh_w_gate, sh_w_up, sh_w_down, ln2_g, ln2_b)
    depth = w_in.shape[0]
    layers = [_layer_params(l, depth, *weights) for l in range(depth)]

    def trunk(x):
        for p in layers:
            x = _layer(x, p)
        return x

    return trunk(x_prompt), trunk(x_sample)
```

```python
import functools
import math

import jax
import jax.numpy as jnp
from jax import lax
from jax.experimental import pallas as pl
from jax.experimental.pallas import tpu as pltpu

F32 = jnp.float32
BF16 = jnp.bfloat16

RET_HEADS = 4
RET_CHUNK = 128
ROPE_BASE = 10000.0
TOP_K = 8
N_GROUPS = 8
TOPK_GROUPS = 4
ROUTE_SCALE = 2.5
MOE_BLOCK = 256
LN_EPS = 1e-5
GN_EPS = 1e-5
RWKV_GN_EPS = 64e-5

LANES = 128
SUBLANES = 8
VMEM_LIMIT_BYTES = 56 * 1024 * 1024

SCAN_CHUNK = 64


def _cparams(semantics):
    return pltpu.CompilerParams(dimension_semantics=semantics, vmem_limit_bytes=VMEM_LIMIT_BYTES)


def _dot(a, b):
    return jnp.dot(a.astype(BF16), b.astype(BF16), preferred_element_type=F32)


def _dot_nt(a, b):
    return lax.dot_general(a.astype(BF16), b.astype(BF16), (((1,), (1,)), ((), ())),
                           preferred_element_type=F32)


def _split3(x):
    hi = x.astype(BF16)
    r1 = x - hi.astype(F32)
    mid = r1.astype(BF16)
    lo = (r1 - mid.astype(F32)).astype(BF16)
    return hi, mid, lo


def _dot_exact_rhs(x, w_bf16):
    hi, mid, lo = _split3(x)
    out = jnp.dot(hi, w_bf16, preferred_element_type=F32)
    out += jnp.dot(mid, w_bf16, preferred_element_type=F32)
    out += jnp.dot(lo, w_bf16, preferred_element_type=F32)
    return out


def _dot_exact_lhs(w_bf16, x):
    hi, mid, lo = _split3(x)
    out = jnp.dot(w_bf16, hi, preferred_element_type=F32)
    out += jnp.dot(w_bf16, mid, preferred_element_type=F32)
    out += jnp.dot(w_bf16, lo, preferred_element_type=F32)
    return out


def _sigmoid(x):
    return 1.0 / (1.0 + jnp.exp(-x))


def _silu(x):
    return x * _sigmoid(x)


def _layer_norm(h, g, b):
    mu = jnp.mean(h, axis=-1, keepdims=True)
    d = h - mu
    var = jnp.mean(d * d, axis=-1, keepdims=True)
    return d * lax.rsqrt(var + LN_EPS) * g + b


def _inproj_body(x_ref, wr_ref, ww_ref, zr_ref, zw_ref):
    xb = x_ref[...].astype(BF16)
    zr_ref[...] = jnp.dot(xb, wr_ref[...], preferred_element_type=F32)
    zw_ref[...] = jnp.dot(xb, ww_ref[...], preferred_element_type=F32)


def _in_proj(x2d, w_ret, w_rwkv, tm):
    T, D = x2d.shape
    nr, nw = w_ret.shape[1], w_rwkv.shape[1]
    return pl.pallas_call(
        _inproj_body,
        grid=(T // tm,),
        in_specs=[pl.BlockSpec((tm, D), lambda i: (i, 0)),
                  pl.BlockSpec((D, nr), lambda i: (0, 0)),
                  pl.BlockSpec((D, nw), lambda i: (0, 0))],
        out_specs=[pl.BlockSpec((tm, nr), lambda i: (i, 0)),
                   pl.BlockSpec((tm, nw), lambda i: (i, 0))],
        out_shape=[jax.ShapeDtypeStruct((T, nr), F32), jax.ShapeDtypeStruct((T, nw), F32)],
        compiler_params=_cparams(("parallel",)),
        name="in_proj",
    )(x2d, w_ret, w_rwkv)


def _ret_body(q_ref, k_ref, v_ref, gt_ref, cos_ref, sin_ref, lg_ref, gg_ref, gb_ref, o_ref,
              qs_ref, ks_ref, acc_ref, *, qscale):
    C = RET_CHUNK
    s = q_ref.shape[0]
    n = s // C
    lg = lg_ref[...]
    pos = lax.broadcasted_iota(jnp.int32, (C, C), 0).astype(F32)
    col = lax.broadcasted_iota(jnp.int32, (C, C), 1).astype(F32)
    sc_q_prev = jnp.exp((pos + 1.0) * lg)
    sc_k_fwd = jnp.exp((C - 1.0 - pos) * lg)
    sc_k_bwd = jnp.exp(pos * lg)
    sc_q_next = jnp.exp((C - pos) * lg)
    g_chunk = jnp.exp(float(C) * lg)
    decay = jnp.exp(jnp.abs(pos - col) * lg)

    cos = cos_ref[...]
    sin = sin_ref[...]
    q = q_ref[...]
    qs_ref[...] = (q * cos + pltpu.roll(q, C // 2, 1) * sin) * qscale
    k = k_ref[...]
    ks_ref[...] = k * cos + pltpu.roll(k, C // 2, 1) * sin

    def fwd(c, S):
        r0 = pl.multiple_of(c * C, C)
        qc = qs_ref[pl.ds(r0, C), :]
        kc = ks_ref[pl.ds(r0, C), :]
        vb = v_ref[pl.ds(r0, C), :].astype(BF16)
        sc = _dot_nt(qc, kc) * decay
        out = _dot(sc, vb) + _dot(qc * sc_q_prev, S)
        acc_ref[pl.ds(r0, C), :] = out
        return S * g_chunk + _dot((kc * sc_k_fwd).T, vb)

    lax.fori_loop(0, n, fwd, jnp.zeros((C, C), F32))

    def bwd(i, S):
        r0 = pl.multiple_of((n - 1 - i) * C, C)
        qc = qs_ref[pl.ds(r0, C), :]
        kc = ks_ref[pl.ds(r0, C), :]
        vb = v_ref[pl.ds(r0, C), :].astype(BF16)
        acc_ref[pl.ds(r0, C), :] += _dot(qc * sc_q_next, S)
        return S * g_chunk + _dot((kc * sc_k_bwd).T, vb)

    lax.fori_loop(0, n, bwd, jnp.zeros((C, C), F32))

    y = acc_ref[...]
    mu = jnp.mean(y, axis=-1, keepdims=True)
    d = y - mu
    var = jnp.mean(d * d, axis=-1, keepdims=True)
    yn = d * lax.rsqrt(var + GN_EPS) * gg_ref[...] + gb_ref[...]
    o_ref[...] = _silu(gt_ref[...]) * yn


def _retention(z_ret, cos, sin, logg, gn_g, gn_b):
    b, s, _ = z_ret.shape
    C = RET_CHUNK
    H = RET_HEADS
    assert C == LANES and s % C == 0
    blk = lambda off: pl.BlockSpec((None, s, C), lambda i, h, off=off: (i, 0, off + h))
    return pl.pallas_call(
        functools.partial(_ret_body, qscale=float(C) ** -0.5),
        grid=(b, H),
        in_specs=[blk(0), blk(H), blk(2 * H), blk(3 * H),
                  pl.BlockSpec((s, C), lambda i, h: (0, 0)),
                  pl.BlockSpec((s, C), lambda i, h: (0, 0)),
                  pl.BlockSpec((None, 1, C), lambda i, h: (h, 0, 0)),
                  pl.BlockSpec((1, C), lambda i, h: (0, h)),
                  pl.BlockSpec((1, C), lambda i, h: (0, h))],
        out_specs=pl.BlockSpec((None, s, C), lambda i, h: (i, 0, h)),
        out_shape=jax.ShapeDtypeStruct((b, s, H * C), F32),
        scratch_shapes=[pltpu.VMEM((s, C), F32), pltpu.VMEM((s, C), F32), pltpu.VMEM((s, C), F32)],
        compiler_params=_cparams(("parallel", "parallel")),
        name="retention",
    )(z_ret, z_ret, z_ret, z_ret, cos, sin, logg, gn_g, gn_b)


def _prep_body(z_ref, zp_ref, zn_ref, mu_ref, wup_ref, aup_ref, w0_ref, a0_ref, gup_ref,
               kkp_ref, ka_ref, rk_ref, bd_ref,
               r_o, v_o, kk_o, lw_o, kd_o, kka_o, bonus_o, g_o, *, width):
    t = pl.program_id(1)
    nt = pl.num_programs(1)
    W = width
    z = z_ref[...]
    ts = z.shape[0]
    row = lax.broadcasted_iota(jnp.int32, (ts, 1), 0)
    prev_row = jnp.where(t > 0, zp_ref[SUBLANES - 1:SUBLANES, :], 0.0)
    next_row = jnp.where(t < nt - 1, zn_ref[0:1, :], 0.0)
    prev = jnp.where(row == 0, prev_row, pltpu.roll(z, 1, 0))
    nxt = jnp.where(row == ts - 1, next_row, pltpu.roll(z, ts - 1, 0))
    zs = z + mu_ref[...] * (0.5 * (prev + nxt) - z)

    r = zs[:, 0:W]
    kx = zs[:, W:2 * W]
    vx = zs[:, 2 * W:3 * W]
    wd = jnp.tanh(zs[:, 3 * W:3 * W + LANES])
    ad = zs[:, 3 * W + LANES:3 * W + 2 * LANES]
    gd = _sigmoid(zs[:, 3 * W + 2 * LANES:3 * W + 3 * LANES])
    bd = bd_ref[...]

    kk = kx * kkp_ref[...]
    ssq = _dot_exact_rhs(kk * kk, bd)
    kk = kk * lax.rsqrt(jnp.maximum(ssq, 1e-24))
    ka = ka_ref[...]
    ksum = None
    for d in range(2):
        pre = w0_ref[d:d + 1, :] + _dot(wd, wup_ref[d])
        lw_o[d] = -math.exp(-0.5) * _sigmoid(pre)
        a = _sigmoid(a0_ref[d:d + 1, :] + _dot(ad, aup_ref[d]))
        kd = kx * (1.0 + (a - 1.0) * ka)
        kd_o[d] = kd
        kka_o[d] = kk * a
        ksum = kd if ksum is None else ksum + kd
    r_o[...] = r
    v_o[...] = vx
    kk_o[...] = kk
    bonus_o[...] = _dot_exact_rhs(r * ksum * rk_ref[...], bd) * vx
    g_o[...] = _dot(gd, gup_ref[...])


def _rwkv_prep(z_rwkv, p, ts):
    b, s, ncol = z_rwkv.shape
    W = p["rwkv_width"]
    nt = s // ts
    hb = ts // SUBLANES
    last = s // SUBLANES - 1
    full = lambda a: pl.BlockSpec(a.shape, lambda i, t, nd=a.ndim: (0,) * nd)
    out_tok = pl.BlockSpec((None, ts, W), lambda i, t: (i, t, 0))
    out_dir = pl.BlockSpec((2, None, ts, W), lambda i, t: (0, i, t, 0))
    tok_shape = jax.ShapeDtypeStruct((b, s, W), F32)
    dir_shape = jax.ShapeDtypeStruct((2, b, s, W), F32)
    consts = [p["mu"], p["wup_pad"], p["aup_pad"], p["w0"], p["a0"], p["gup"],
              p["k_k"], p["k_a"], p["r_k"], p["head_ones"]]
    return pl.pallas_call(
        functools.partial(_prep_body, width=W),
        grid=(b, nt),
        in_specs=[pl.BlockSpec((None, ts, ncol), lambda i, t: (i, t, 0)),
                  pl.BlockSpec((None, SUBLANES, ncol), lambda i, t: (i, jnp.maximum(t * hb - 1, 0), 0)),
                  pl.BlockSpec((None, SUBLANES, ncol), lambda i, t: (i, jnp.minimum((t + 1) * hb, last), 0)),
                  ] + [full(a) for a in consts],
        out_specs=[out_tok, out_tok, out_tok, out_dir, out_dir, out_dir, out_tok, out_tok],
        out_shape=[tok_shape, tok_shape, tok_shape, dir_shape, dir_shape, dir_shape, tok_shape, tok_shape],
        compiler_params=_cparams(("parallel", "parallel")),
        name="rwkv_prep",
    )(z_rwkv, z_rwkv, z_rwkv, *consts)


def _scan_body(r_ref, kk_ref, v_ref, lw_ref, kd_ref, kka_ref, y_ref, st_ref, *, n_pairs):
    L = SCAN_CHUNK
    H = 2 * L
    assert H == LANES
    d = pl.program_id(1)
    t = pl.program_id(2)
    tt = r_ref.shape[0]
    nch = tt // L
    fwd = d == 0

    @pl.when(t == 0)
    def _():
        st_ref[...] = jnp.zeros_like(st_ref)

    ii = lax.broadcasted_iota(jnp.int32, (H, H), 0)
    jj = lax.broadcasted_iota(jnp.int32, (H, H), 1)
    sgn = jnp.where(fwd, 1, -1)
    same = (ii < L) == (jj < L)
    strict = jnp.logical_and(same, (ii - jj) * sgn > 0)
    incl = jnp.logical_and(same, (ii - jj) * sgn >= 0)
    eye = ii == jj
    li = lax.broadcasted_iota(jnp.int32, (L, L), 0)
    lj = lax.broadcasted_iota(jnp.int32, (L, L), 1)
    tri = jnp.where((li - lj) * sgn >= 0, 1.0, 0.0).astype(BF16)
    head0 = lax.broadcasted_iota(jnp.int32, (L, H), 1) < L

    def stack(x):
        return jnp.concatenate([jnp.where(head0, x, 0.0), jnp.where(head0, 0.0, x)], axis=0)

    P = range(n_pairs)
    lanes = [slice(hp * H, (hp + 1) * H) for hp in P]

    def chunk(j, carry):
        c = jnp.where(fwd, j, nch - 1 - j)
        r0 = pl.multiple_of(c * L, L)
        rows = pl.ds(r0, L)
        lw = [lw_ref[rows, ls] for ls in lanes]
        cum = [_dot_exact_lhs(tri, x) for x in lw]
        tot = [jnp.sum(x, axis=0, keepdims=True) for x in lw]
        e_incl = [jnp.exp(a) for a in cum]
        e_excl = [jnp.exp(a - x) for a, x in zip(cum, lw)]
        e_inv = [jnp.exp(-a) for a in cum]
        e_rem = [jnp.exp(t_ - a) for t_, a in zip(tot, cum)]
        kk = [kk_ref[rows, ls] for ls in lanes]
        kka = [kka_ref[rows, ls] for ls in lanes]
        kd = [kd_ref[rows, ls] for ls in lanes]
        Kk = [stack(a * e) for a, e in zip(kk, e_excl)]
        R = [stack(r_ref[rows, ls] * e) for ls, e in zip(lanes, e_incl)]
        B = [stack(a * e) for a, e in zip(kka, e_inv)]
        Kd = [stack(a * e) for a, e in zip(kd, e_inv)]
        Bh = [stack(a * e).T.astype(BF16) for a, e in zip(kka, e_rem)]
        Kh = [stack(a * e).T.astype(BF16) for a, e in zip(kd, e_rem)]
        V = [stack(v_ref[rows, ls]).astype(BF16) for ls in lanes]

        gram = [_dot_nt(jnp.concatenate([Kk[h], R[h]], axis=0), jnp.concatenate([B[h], Kd[h]], axis=0))
                for h in P]
        Np = [jnp.where(strict, g_[:H, :H], 0.0) for g_ in gram]
        AD = [_dot(jnp.concatenate([jnp.where(strict, g_[:H, H:], 0.0), jnp.where(incl, g_[H:, H:], 0.0)],
                                   axis=0), V[h]) for h, g_ in enumerate(gram)]
        Cm = [jnp.where(incl, g_[H:, :H], 0.0).astype(BF16) for g_ in gram]
        X = [jnp.concatenate([Kk[h], AD[h][:H]], axis=1) for h in P]
        n_fac = L.bit_length() - 1
        for f in range(n_fac):
            Nb = [a.astype(BF16) for a in Np]
            NX = [_dot(Nb[h], X[h]) for h in P]
            if f < n_fac - 1:
                Np = [_dot(Nb[h], Nb[h]) for h in P]
            X = [x - nx if f == 0 else x + nx for x, nx in zip(X, NX)]
        Xb = [x.astype(BF16) for x in X]
        CX = [_dot(Cm[h], Xb[h]) for h in P]
        BW = [_dot(Bh[h], Xb[h]) for h in P]
        KV = [_dot(Kh[h], V[h]) for h in P]
        QM = []
        for h in P:
            Qh = R[h] - CX[h][:, :H]
            M = jnp.where(eye, jnp.exp(tot[h]), 0.0) - BW[h][:, :H]
            QM.append(_dot(jnp.concatenate([Qh, M], axis=0), st_ref[h]))
        for h in P:
            Ys = QM[h][:H] + (AD[h][H:] - CX[h][:, H:])
            st_ref[h] = QM[h][H:] + (KV[h] - BW[h][:, H:])
            y_ref[rows, lanes[h]] = Ys[:L] + Ys[L:]
        return carry

    lax.fori_loop(0, nch, chunk, 0)


def _rwkv_scan(r, kk, v, lw, kd, kka, tt):
    b, s, W = r.shape
    nt = s // tt
    n_pairs = W // LANES
    tb = lambda d, t: t + d * (nt - 1 - 2 * t)
    tok = pl.BlockSpec((None, tt, W), lambda i, d, t: (i, tb(d, t), 0))
    dirs = pl.BlockSpec((None, None, tt, W), lambda i, d, t: (d, i, tb(d, t), 0))
    return pl.pallas_call(
        functools.partial(_scan_body, n_pairs=n_pairs),
        grid=(b, 2, nt),
        in_specs=[tok, tok, tok, dirs, dirs, dirs],
        out_specs=dirs,
        out_shape=jax.ShapeDtypeStruct((2, b, s, W), F32),
        scratch_shapes=[pltpu.VMEM((n_pairs, LANES, LANES), F32)],
        compiler_params=_cparams(("parallel", "parallel", "arbitrary")),
        name="rwkv_scan",
    )(r, kk, v, lw, kd, kka)


def _mix_body(y_ref, bonus_ref, g_ref, ret_ref, x_ref, gng_ref, gnb_ref, avg_ref, wo1_ref, wo2_ref,
              l1g_ref, l1b_ref, rwt_ref, rb_ref, x1_ref, idx_ref, gate_ref, cnt_ref, *, alpha):
    y = y_ref[0] + y_ref[1]
    avg = avg_ref[...]
    mu = _dot_exact_rhs(y, avg)
    dl = y - mu
    var = _dot_exact_rhs(dl * dl, avg)
    yn = dl * lax.rsqrt(var + RWKV_GN_EPS) * gng_ref[...] + gnb_ref[...]
    rw = (yn + bonus_ref[...]) * g_ref[...]
    m = _dot(ret_ref[...], wo1_ref[...]) + _dot(rw, wo2_ref[...])
    x1 = _layer_norm(alpha * x_ref[...] + m, l1g_ref[...], l1b_ref[...])
    x1_ref[...] = x1

    scores = _sigmoid(_dot_nt(rwt_ref[...], x1))
    E, tm = scores.shape
    GS = E // N_GROUPS
    NEG = -jnp.inf
    biased = scores + rb_ref[...]
    rowi = lax.broadcasted_iota(jnp.int32, (E, tm), 0)
    ri = lax.broadcasted_iota(jnp.int32, (GS, tm), 0)
    gs_rows = []
    for gi in range(N_GROUPS):
        blk = biased[gi * GS:(gi + 1) * GS, :]
        m1 = jnp.max(blk, axis=0, keepdims=True)
        i1 = jnp.min(jnp.where(blk == m1, ri, GS), axis=0, keepdims=True)
        m2 = jnp.max(jnp.where(ri == i1, NEG, blk), axis=0, keepdims=True)
        gs_rows.append(m1 + m2)
    cur = jnp.concatenate(gs_rows, axis=0)
    gidx = lax.broadcasted_iota(jnp.int32, (N_GROUPS, tm), 0)
    row_group = rowi // GS
    emask = jnp.zeros((E, tm), F32)
    for _ in range(TOPK_GROUPS):
        mx = jnp.max(cur, axis=0, keepdims=True)
        ix = jnp.min(jnp.where(cur == mx, gidx, N_GROUPS), axis=0, keepdims=True)
        emask = jnp.where(row_group == ix, 1.0, emask)
        cur = jnp.where(gidx == ix, NEG, cur)
    cur = jnp.where(emask > 0.5, biased, NEG)
    idxs, sels = [], []
    chosen = jnp.zeros((E, tm), F32)
    for _ in range(TOP_K):
        mx = jnp.max(cur, axis=0, keepdims=True)
        ix = jnp.min(jnp.where(cur == mx, rowi, E), axis=0, keepdims=True)
        hit = rowi == ix
        sels.append(jnp.sum(jnp.where(hit, scores, 0.0), axis=0, keepdims=True))
        idxs.append(ix)
        cur = jnp.where(hit, NEG, cur)
        chosen = jnp.where(hit, 1.0, chosen)
    sel = jnp.concatenate(sels, axis=0)
    idx_ref[...] = jnp.concatenate(idxs, axis=0)
    gate_ref[...] = sel / jnp.sum(sel, axis=0, keepdims=True) * ROUTE_SCALE

    @pl.when(pl.program_id(0) == 0)
    def _():
        cnt_ref[...] = jnp.zeros_like(cnt_ref)

    cnt_ref[...] += jnp.sum(chosen, axis=1, keepdims=True)


def _mix_out(y2, bonus, g, ret_out, x2d, p, tm):
    T, D = x2d.shape
    W = bonus.shape[1]
    Wr = ret_out.shape[1]
    E = p["router_wt"].shape[0]
    full = lambda a: pl.BlockSpec(a.shape, lambda i, nd=a.ndim: (0,) * nd)
    consts = [p["gn_g"], p["gn_b"], p["head_avg"], p["wo_ret"], p["wo_rwkv"], p["ln1_g"], p["ln1_b"],
              p["router_wt"], p["router_b"]]
    return pl.pallas_call(
        functools.partial(_mix_body, alpha=p["alpha"]),
        grid=(T // tm,),
        in_specs=[pl.BlockSpec((2, tm, W), lambda i: (0, i, 0)),
                  pl.BlockSpec((tm, W), lambda i: (i, 0)),
                  pl.BlockSpec((tm, W), lambda i: (i, 0)),
                  pl.BlockSpec((tm, Wr), lambda i: (i, 0)),
                  pl.BlockSpec((tm, D), lambda i: (i, 0))] + [full(a) for a in consts],
        out_specs=[pl.BlockSpec((tm, D), lambda i: (i, 0)),
                   pl.BlockSpec((TOP_K, tm), lambda i: (0, i)),
                   pl.BlockSpec((TOP_K, tm), lambda i: (0, i)),
                   pl.BlockSpec((E, LANES), lambda i: (0, 0))],
        out_shape=[jax.ShapeDtypeStruct((T, D), F32),
                   jax.ShapeDtypeStruct((TOP_K, T), jnp.int32),
                   jax.ShapeDtypeStruct((TOP_K, T), F32),
                   jax.ShapeDtypeStruct((E, LANES), F32)],
        compiler_params=_cparams(("arbitrary",)),
        name="mix_out",
    )(y2, bonus, g, ret_out, x2d, *consts)


def _plan_body(idx_ref, pstart_ref, upper_ref, dest_ref, run_ref):
    @pl.when(pl.program_id(0) == 0)
    def _():
        run_ref[...] = jnp.zeros_like(run_ref)

    E = pstart_ref.shape[0]
    tm = idx_ref.shape[1]
    rowi = lax.broadcasted_iota(jnp.int32, (E, tm), 0)
    hits = [rowi == idx_ref[k:k + 1, :] for k in range(TOP_K)]
    member = jnp.zeros((E, tm), F32)
    for hit in hits:
        member = jnp.where(hit, 1.0, member)
    before = jnp.dot(member.astype(BF16), upper_ref[...], preferred_element_type=F32)
    slot = pstart_ref[...] + run_ref[:, 0:1] + before
    dest_ref[...] = jnp.concatenate(
        [jnp.sum(jnp.where(hit, slot, 0.0), axis=0, keepdims=True) for hit in hits], axis=0).astype(jnp.int32)
    run_ref[...] += jnp.sum(member, axis=1, keepdims=True)


def _slot_plan(idx_t, pad_start, tm):
    T = idx_t.shape[1]
    E = pad_start.shape[0]
    upper = (jnp.arange(tm)[:, None] < jnp.arange(tm)[None, :]).astype(BF16)
    return pl.pallas_call(
        _plan_body,
        grid=(T // tm,),
        in_specs=[pl.BlockSpec((TOP_K, tm), lambda i: (0, i)),
                  pl.BlockSpec((E, 1), lambda i: (0, 0)),
                  pl.BlockSpec((tm, tm), lambda i: (0, 0))],
        out_specs=pl.BlockSpec((TOP_K, tm), lambda i: (0, i)),
        out_shape=jax.ShapeDtypeStruct((TOP_K, T), jnp.int32),
        scratch_shapes=[pltpu.VMEM((E, LANES), F32)],
        compiler_params=_cparams(("arbitrary",)),
        name="slot_plan",
    )(idx_t, pad_start.reshape(E, 1).astype(F32), upper)


def _dispatch_body(plo_ref, pn_ref, dest_ref, x_ref, xs_hbm, zero_ref, sem, zsem):
    i = pl.program_id(0)
    tm = x_ref.shape[0]
    E = plo_ref.shape[0]

    def row_copy(t, slot):
        return pltpu.make_async_copy(x_ref.at[pl.ds(t, 1), :], xs_hbm.at[pl.ds(slot, 1), :], sem)

    def issue(t, c):
        for k in range(TOP_K):
            row_copy(t, dest_ref[k, t]).start()
        return c

    lax.fori_loop(0, tm, issue, 0, unroll=2)

    def zero_copy(slot):
        return pltpu.make_async_copy(zero_ref, xs_hbm.at[pl.ds(slot, 1), :], zsem)

    @pl.when(i == 0)
    def _():
        zero_ref[...] = jnp.zeros_like(zero_ref)

        def per_expert(e, c):
            lo = plo_ref[e]

            def one(j, c2):
                zero_copy(lo + j).start()
                return c2

            lax.fori_loop(0, pn_ref[e], one, 0)

            def one_done(j, c2):
                zero_copy(0).wait()
                return c2

            lax.fori_loop(0, pn_ref[e], one_done, 0)
            return c

        lax.fori_loop(0, E, per_expert, 0)

    def drain(t, c):
        for k in range(TOP_K):
            row_copy(0, 0).wait()
        return c

    lax.fori_loop(0, tm, drain, 0, unroll=2)


def _dispatch(x1, dest, pad_lo, pad_n, n_slots, tm):
    T, D = x1.shape
    grid_spec = pltpu.PrefetchScalarGridSpec(
        num_scalar_prefetch=2,
        grid=(T // tm,),
        in_specs=[pl.BlockSpec((TOP_K, tm), lambda i, plo, pn: (0, i), memory_space=pltpu.SMEM),
                  pl.BlockSpec((tm, D), lambda i, plo, pn: (i, 0))],
        out_specs=pl.BlockSpec(memory_space=pl.ANY),
        scratch_shapes=[pltpu.VMEM((1, D), F32), pltpu.SemaphoreType.DMA(()), pltpu.SemaphoreType.DMA(())],
    )
    return pl.pallas_call(
        _dispatch_body,
        grid_spec=grid_spec,
        out_shape=jax.ShapeDtypeStruct((n_slots, D), F32),
        compiler_params=_cparams(("arbitrary",)),
        name="dispatch",
    )(pad_lo, pad_n, dest, x1)


def _moe_body(be_ref, nu_ref, xs_ref, wg_ref, wu_ref, wd_ref, o_ref):
    i = pl.program_id(0)

    @pl.when(i < nu_ref[0])
    def _():
        xb = xs_ref[...].astype(BF16)
        h = _silu(_dot(xb, wg_ref[...])) * _dot(xb, wu_ref[...])
        o_ref[...] = _dot(h, wd_ref[...])

    @pl.when(i >= nu_ref[0])
    def _():
        o_ref[...] = jnp.zeros_like(o_ref)


def _moe_ffn(xs, block_expert, n_used, w_gate, w_up, w_down):
    P, D = xs.shape
    G = MOE_BLOCK
    n_blocks = P // G
    E, _, De = w_gate.shape
    blk = lambda i, nu: jnp.minimum(i, nu[0] - 1)
    grid_spec = pltpu.PrefetchScalarGridSpec(
        num_scalar_prefetch=2,
        grid=(n_blocks,),
        in_specs=[pl.BlockSpec((G, D), lambda i, be, nu: (blk(i, nu), 0)),
                  pl.BlockSpec((None, D, De), lambda i, be, nu: (be[blk(i, nu)], 0, 0)),
                  pl.BlockSpec((None, D, De), lambda i, be, nu: (be[blk(i, nu)], 0, 0)),
                  pl.BlockSpec((None, De, D), lambda i, be, nu: (be[blk(i, nu)], 0, 0))],
        out_specs=pl.BlockSpec((G, D), lambda i, be, nu: (i, 0)),
    )
    return pl.pallas_call(
        _moe_body,
        grid_spec=grid_spec,
        out_shape=jax.ShapeDtypeStruct((P, D), F32),
        compiler_params=_cparams(("arbitrary",)),
        name="moe_ffn",
    )(block_expert, n_used, xs, w_gate, w_up, w_down)


def _comb_body(pos_ref, ys_hbm, x1_ref, gt_ref, sg_ref, su_ref, sd_ref, l2g_ref, l2b_ref, o_ref,
               buf_ref, sem, *, alpha):
    tm = x1_ref.shape[0]

    def row_copy(slot, k, t):
        return pltpu.make_async_copy(ys_hbm.at[pl.ds(slot, 1), :], buf_ref.at[k, pl.ds(t, 1), :], sem)

    def issue(t, c):
        for k in range(TOP_K):
            row_copy(pos_ref[k, t], k, t).start()
        return c

    lax.fori_loop(0, tm, issue, 0, unroll=2)

    x1 = x1_ref[...]
    xb = x1.astype(BF16)
    shared = _dot(_silu(_dot(xb, sg_ref[...])) * _dot(xb, su_ref[...]), sd_ref[...])

    def drain(t, c):
        for k in range(TOP_K):
            row_copy(0, k, t).wait()
        return c

    lax.fori_loop(0, tm, drain, 0, unroll=2)
    gt = gt_ref[...]
    routed = gt[:, 0:1] * buf_ref[0]
    for k in range(1, TOP_K):
        routed += gt[:, k:k + 1] * buf_ref[k]
    o_ref[...] = _layer_norm(alpha * x1 + (routed + shared), l2g_ref[...], l2b_ref[...])


def _combine(pos, ys, x1, gates_t, p, tm):
    T, D = x1.shape
    nt = T // tm
    full = lambda a: pl.BlockSpec(a.shape, lambda i, nd=a.ndim: (0,) * nd)
    consts = [p["sh_gate"], p["sh_up"], p["sh_down"], p["ln2_g"], p["ln2_b"]]
    return pl.pallas_call(
        functools.partial(_comb_body, alpha=p["alpha"]),
        grid=(nt,),
        in_specs=[pl.BlockSpec((TOP_K, tm), lambda i: (0, i), memory_space=pltpu.SMEM),
                  pl.BlockSpec(memory_space=pl.ANY),
                  pl.BlockSpec((tm, D), lambda i: (i, 0)),
                  pl.BlockSpec((tm, TOP_K), lambda i: (i, 0))] + [full(a) for a in consts],
        out_specs=pl.BlockSpec((tm, D), lambda i: (i, 0)),
        out_shape=jax.ShapeDtypeStruct((T, D), F32),
        scratch_shapes=[pltpu.VMEM((TOP_K, tm, D), F32), pltpu.SemaphoreType.DMA(())],
        compiler_params=_cparams(("arbitrary",)),
        name="combine",
    )(pos, ys, x1, gates_t, *consts)


def _segment_layout(counts, n_tokens):
    G = MOE_BLOCK
    E = counts.shape[0]
    n_blocks = -(-(n_tokens * TOP_K + E * (G - 1)) // G)
    padded = (counts + G - 1) // G * G
    pad_end = jnp.cumsum(padded)
    pad_start = pad_end - padded
    first_slot = jnp.arange(n_blocks, dtype=jnp.int32) * G
    block_expert = jnp.minimum(jnp.sum(first_slot[:, None] >= pad_end[None, :], axis=1), E - 1)
    n_used = pad_end[-1:] // G
    return (pad_start, (pad_start + counts).astype(jnp.int32), (padded - counts).astype(jnp.int32),
            block_expert.astype(jnp.int32), n_used.astype(jnp.int32), n_blocks)


def _rotary_tables(s, d):
    inv = ROPE_BASE ** (-jnp.arange(0, d, 2, dtype=F32) / d)
    ang = jnp.arange(s, dtype=F32)[:, None] * inv[None, :]
    cos = jnp.cos(ang)
    sin = jnp.sin(ang)
    return jnp.concatenate([cos, cos], axis=-1), jnp.concatenate([-sin, sin], axis=-1)


def _layer_params(l, depth, w_in, ret_gn_g, ret_gn_b, rwkv_mu, rwkv_w0, rwkv_w_up, rwkv_a0, rwkv_a_up,
                  rwkv_g_up, rwkv_k_k, rwkv_k_a, rwkv_r_k, rwkv_gn_g, rwkv_gn_b, w_out, ln1_g, ln1_b,
                  router_w, router_bias, exp_w_gate, exp_w_up, exp_w_down, sh_w_gate, sh_w_up,
                  sh_w_down, ln2_g, ln2_b):
    ret_w = ret_gn_g.shape[-1]
    W = rwkv_gn_g.shape[-1]
    n_heads, hd = rwkv_r_k.shape[-2:]
    rank_w = rwkv_w_up.shape[2]
    rank_a = rwkv_a_up.shape[2]
    assert rank_w * 2 == LANES and rank_a * 2 == LANES and rwkv_g_up.shape[1] == LANES
    assert hd * 2 == LANES and SCAN_CHUNK == hd
    row = lambda a: a.reshape(1, -1).astype(F32)
    zw = jnp.zeros((rank_w, W), F32)
    za = jnp.zeros((rank_a, W), F32)
    head_id = jnp.arange(W) // hd
    same_head = (head_id[:, None] == head_id[None, :])
    wi = w_in[l]
    return dict(
        alpha=float((2 * depth) ** 0.25),
        rwkv_width=W,
        w_ret=wi[:, :4 * ret_w].astype(BF16),
        w_rwkv=wi[:, 4 * ret_w:].astype(BF16),
        ret_gn_g=row(ret_gn_g[l]), ret_gn_b=row(ret_gn_b[l]),
        logg=jnp.broadcast_to(
            jnp.log1p(-jnp.exp2(-5.0 - jnp.arange(RET_HEADS, dtype=F32)))[:, None, None],
            (RET_HEADS, 1, LANES)),
        mu=row(rwkv_mu[l]),
        wup_pad=jnp.stack([jnp.concatenate([rwkv_w_up[l, 0], zw], 0),
                           jnp.concatenate([zw, rwkv_w_up[l, 1]], 0)]).astype(BF16),
        aup_pad=jnp.stack([jnp.concatenate([rwkv_a_up[l, 0], za], 0),
                           jnp.concatenate([za, rwkv_a_up[l, 1]], 0)]).astype(BF16),
        w0=rwkv_w0[l].astype(F32), a0=rwkv_a0[l].astype(F32),
        gup=rwkv_g_up[l].astype(BF16),
        k_k=row(rwkv_k_k[l]), k_a=row(rwkv_k_a[l]), r_k=row(rwkv_r_k[l]),
        head_ones=same_head.astype(BF16),
        head_avg=(same_head.astype(F32) / hd).astype(BF16),
        gn_g=row(rwkv_gn_g[l]), gn_b=row(rwkv_gn_b[l]),
        wo_ret=w_out[l, :ret_w].astype(BF16), wo_rwkv=w_out[l, ret_w:].astype(BF16),
        ln1_g=row(ln1_g[l]), ln1_b=row(ln1_b[l]),
        router_wt=router_w[l].T.astype(BF16), router_b=router_bias[l].reshape(-1, 1).astype(F32),
        exp_gate=exp_w_gate[l], exp_up=exp_w_up[l], exp_down=exp_w_down[l],
        sh_gate=sh_w_gate[l].astype(BF16), sh_up=sh_w_up[l].astype(BF16), sh_down=sh_w_down[l].astype(BF16),
        ln2_g=row(ln2_g[l]), ln2_b=row(ln2_b[l]),
    )


def _pick(n, pref):
    t = min(n, pref)
    while n % t:
        t //= 2
    return t


def _layer(x, p):
    b, s, D = x.shape
    T = b * s
    x2d = x.reshape(T, D)
    tm = _pick(T, 256)
    z_ret, z_rwkv = _in_proj(x2d, p["w_ret"], p["w_rwkv"], tm)
    cos, sin = _rotary_tables(s, RET_CHUNK)
    ret_out = _retention(z_ret.reshape(b, s, -1), cos, sin, p["logg"], p["ret_gn_g"], p["ret_gn_b"])
    r, v, kk, lw, kd, kka, bonus, g = _rwkv_prep(z_rwkv.reshape(b, s, -1), p, _pick(s, 256))
    y2 = _rwkv_scan(r, kk, v, lw, kd, kka, _pick(s, 512))
    W = p["rwkv_width"]
    x1, idx_t, gates, cnt = _mix_out(y2.reshape(2, T, W), bonus.reshape(T, W), g.reshape(T, W),
                                     ret_out.reshape(T, -1), x2d, p, tm)
    counts = cnt[:, 0].astype(jnp.int32)
    pad_start, pad_lo, pad_n, block_expert, n_used, n_blocks = _segment_layout(counts, T)
    dest = _slot_plan(idx_t, pad_start, _pick(T, 512))
    xs = _dispatch(x1, dest, pad_lo, pad_n, n_blocks * MOE_BLOCK, tm)
    ys = _moe_ffn(xs, block_expert, n_used, p["exp_gate"], p["exp_up"], p["exp_down"])
    out = _combine(dest, ys, x1, gates.T, p, _pick(T, 128))
    return out.reshape(b, s, D)


def kernel(x_prompt, x_sample, w_in, ret_gn_g, ret_gn_b, rwkv_mu, rwkv_w0, rwkv_w_up, rwkv_a0, rwkv_a_up,
           rwkv_g_up, rwkv_k_k, rwkv_k_a, rwkv_r_k, rwkv_gn_g, rwkv_gn_b, w_out, ln1_g, ln1_b, router_w,
           router_bias, exp_w_gate, exp_w_up, exp_w_down, sh_w_gate, sh_w_up, sh_w_down, ln2_g, ln2_b):
    weights = (w_in, ret_gn_g, ret_gn_b, rwkv_mu, rwkv_w0, rwkv_w_up, rwkv_a0, rwkv_a_up, rwkv_g_up,
               rwkv_k_k, rwkv_k_a, rwkv_r_k, rwkv_gn_g, rwkv_gn_b, w_out, ln1_g, ln1_b, router_w,
               router_bias, exp_w_gate, exp_w_up, exp_w_down, sh_w_gate, sh_w_up, sh_w_down, ln2_g, ln2_b)
    depth = w_in.shape[0]
    layers = [_layer_params(l, depth, *weights) for l in range(depth)]

    def trunk(x):
        for p in layers:
            x = _layer(x, p)
        return x

    return trunk(x_prompt), trunk(x_sample)
```

```python
import functools
import math

import jax
import jax.numpy as jnp
from jax import lax
from jax.experimental import pallas as pl
from jax.experimental.pallas import tpu as pltpu

F32 = jnp.float32
BF16 = jnp.bfloat16

RET_HEADS = 4
RET_CHUNK = 128
ROPE_BASE = 10000.0
TOP_K = 8
N_GROUPS = 8
TOPK_GROUPS = 4
ROUTE_SCALE = 2.5
MOE_BLOCK = 256
LN_EPS = 1e-5
GN_EPS = 1e-5
RWKV_GN_EPS = 64e-5

LANES = 128
SUBLANES = 8
VMEM_LIMIT_BYTES = 56 * 1024 * 1024

SCAN_CHUNK = 64


def _cparams(semantics):
    return pltpu.CompilerParams(dimension_semantics=semantics, vmem_limit_bytes=VMEM_LIMIT_BYTES)


def _dot(a, b):
    return jnp.dot(a.astype(BF16), b.astype(BF16), preferred_element_type=F32)


def _dot_nt(a, b):
    return lax.dot_general(a.astype(BF16), b.astype(BF16), (((1,), (1,)), ((), ())),
                           preferred_element_type=F32)


def _split3(x):
    hi = x.astype(BF16)
    r1 = x - hi.astype(F32)
    mid = r1.astype(BF16)
    lo = (r1 - mid.astype(F32)).astype(BF16)
    return hi, mid, lo


def _dot_exact_rhs(x, w_bf16):
    hi, mid, lo = _split3(x)
    out = jnp.dot(hi, w_bf16, preferred_element_type=F32)
    out += jnp.dot(mid, w_bf16, preferred_element_type=F32)
    out += jnp.dot(lo, w_bf16, preferred_element_type=F32)
    return out


def _dot_exact_lhs(w_bf16, x):
    hi, mid, lo = _split3(x)
    out = jnp.dot(w_bf16, hi, preferred_element_type=F32)
    out += jnp.dot(w_bf16, mid, preferred_element_type=F32)
    out += jnp.dot(w_bf16, lo, preferred_element_type=F32)
    return out


def _sigmoid(x):
    return 1.0 / (1.0 + jnp.exp(-x))


def _silu(x):
    return x * _sigmoid(x)


def _layer_norm(h, g, b):
    mu = jnp.mean(h, axis=-1, keepdims=True)
    d = h - mu
    var = jnp.mean(d * d, axis=-1, keepdims=True)
    return d * lax.rsqrt(var + LN_EPS) * g + b


def _inproj_body(x_ref, wr_ref, ww_ref, zr_ref, zw_ref):
    xb = x_ref[...].astype(BF16)
    zr_ref[...] = jnp.dot(xb, wr_ref[...], preferred_element_type=F32)
    zw_ref[...] = jnp.dot(xb, ww_ref[...], preferred_element_type=F32)


def _in_proj(x2d, w_ret, w_rwkv, tm):
    T, D = x2d.shape
    nr, nw = w_ret.shape[1], w_rwkv.shape[1]
    return pl.pallas_call(
        _inproj_body,
        grid=(T // tm,),
        in_specs=[pl.BlockSpec((tm, D), lambda i: (i, 0)),
                  pl.BlockSpec((D, nr), lambda i: (0, 0)),
                  pl.BlockSpec((D, nw), lambda i: (0, 0))],
        out_specs=[pl.BlockSpec((tm, nr), lambda i: (i, 0)),
                   pl.BlockSpec((tm, nw), lambda i: (i, 0))],
        out_shape=[jax.ShapeDtypeStruct((T, nr), F32), jax.ShapeDtypeStruct((T, nw), F32)],
        compiler_params=_cparams(("parallel",)),
        name="in_proj",
    )(x2d, w_ret, w_rwkv)


def _ret_body(q_ref, k_ref, v_ref, gt_ref, cos_ref, sin_ref, lg_ref, gg_ref, gb_ref, o_ref,
              qs_ref, ks_ref, acc_ref, *, qscale):
    C = RET_CHUNK
    s = q_ref.shape[0]
    n = s // C
    lg = lg_ref[...]
    pos = lax.broadcasted_iota(jnp.int32, (C, C), 0).astype(F32)
    col = lax.broadcasted_iota(jnp.int32, (C, C), 1).astype(F32)
    sc_q_prev = jnp.exp((pos + 1.0) * lg)
    sc_k_fwd = jnp.exp((C - 1.0 - pos) * lg)
    sc_k_bwd = jnp.exp(pos * lg)
    sc_q_next = jnp.exp((C - pos) * lg)
    g_chunk = jnp.exp(float(C) * lg)
    decay = jnp.exp(jnp.abs(pos - col) * lg)

    cos = cos_ref[...]
    sin = sin_ref[...]
    q = q_ref[...]
    qs_ref[...] = (q * cos + pltpu.roll(q, C // 2, 1) * sin) * qscale
    k = k_ref[...]
    ks_ref[...] = k * cos + pltpu.roll(k, C // 2, 1) * sin

    def fwd(c, S):
        r0 = pl.multiple_of(c * C, C)
        qc = qs_ref[pl.ds(r0, C), :]
        kc = ks_ref[pl.ds(r0, C), :]
        vb = v_ref[pl.ds(r0, C), :].astype(BF16)
        sc = _dot_nt(qc, kc) * decay
        out = _dot(sc, vb) + _dot(qc * sc_q_prev, S)
        acc_ref[pl.ds(r0, C), :] = out
        return S * g_chunk + _dot((kc * sc_k_fwd).T, vb)

    lax.fori_loop(0, n, fwd, jnp.zeros((C, C), F32))

    def bwd(i, S):
        r0 = pl.multiple_of((n - 1 - i) * C, C)
        qc = qs_ref[pl.ds(r0, C), :]
        kc = ks_ref[pl.ds(r0, C), :]
        vb = v_ref[pl.ds(r0, C), :].astype(BF16)
        acc_ref[pl.ds(r0, C), :] += _dot(qc * sc_q_next, S)
        return S * g_chunk + _dot((kc * sc_k_bwd).T, vb)

    lax.fori_loop(0, n, bwd, jnp.zeros((C, C), F32))

    y = acc_ref[...]
    mu = jnp.mean(y, axis=-1, keepdims=True)
    d = y - mu
    var = jnp.mean(d * d, axis=-1, keepdims=True)
    yn = d * lax.rsqrt(var + GN_EPS) * gg_ref[...] + gb_ref[...]
    o_ref[...] = _silu(gt_ref[...]) * yn


def _retention(z_ret, cos, sin, logg, gn_g, gn_b):
    b, s, _ = z_ret.shape
    C = RET_CHUNK
    H = RET_HEADS
    assert C == LANES and s % C == 0
    blk = lambda off: pl.BlockSpec((None, s, C), lambda i, h, off=off: (i, 0, off + h))
    return pl.pallas_call(
        functools.partial(_ret_body, qscale=float(C) ** -0.5),
        grid=(b, H),
        in_specs=[blk(0), blk(H), blk(2 * H), blk(3 * H),
                  pl.BlockSpec((s, C), lambda i, h: (0, 0)),
                  pl.BlockSpec((s, C), lambda i, h: (0, 0)),
                  pl.BlockSpec((None, 1, C), lambda i, h: (h, 0, 0)),
                  pl.BlockSpec((1, C), lambda i, h: (0, h)),
                  pl.BlockSpec((1, C), lambda i, h: (0, h))],
        out_specs=pl.BlockSpec((None, s, C), lambda i, h: (i, 0, h)),
        out_shape=jax.ShapeDtypeStruct((b, s, H * C), F32),
        scratch_shapes=[pltpu.VMEM((s, C), F32), pltpu.VMEM((s, C), F32), pltpu.VMEM((s, C), F32)],
        compiler_params=_cparams(("parallel", "parallel")),
        name="retention",
    )(z_ret, z_ret, z_ret, z_ret, cos, sin, logg, gn_g, gn_b)


def _prep_body(z_ref, zp_ref, zn_ref, mu_ref, wup_ref, aup_ref, w0_ref, a0_ref, gup_ref,
               kkp_ref, ka_ref, rk_ref, bd_ref,
               r_o, v_o, kk_o, lw_o, kd_o, kka_o, bonus_o, g_o, *, width):
    t = pl.program_id(1)
    nt = pl.num_programs(1)
    W = width
    z = z_ref[...]
    ts = z.shape[0]
    row = lax.broadcasted_iota(jnp.int32, (ts, 1), 0)
    prev_row = jnp.where(t > 0, zp_ref[SUBLANES - 1:SUBLANES, :], 0.0)
    next_row = jnp.where(t < nt - 1, zn_ref[0:1, :], 0.0)
    prev = jnp.where(row == 0, prev_row, pltpu.roll(z, 1, 0))
    nxt = jnp.where(row == ts - 1, next_row, pltpu.roll(z, ts - 1, 0))
    zs = z + mu_ref[...] * (0.5 * (prev + nxt) - z)

    r = zs[:, 0:W]
    kx = zs[:, W:2 * W]
    vx = zs[:, 2 * W:3 * W]
    wd = jnp.tanh(zs[:, 3 * W:3 * W + LANES])
    ad = zs[:, 3 * W + LANES:3 * W + 2 * LANES]
    gd = _sigmoid(zs[:, 3 * W + 2 * LANES:3 * W + 3 * LANES])
    bd = bd_ref[...]

    kk = kx * kkp_ref[...]
    ssq = _dot_exact_rhs(kk * kk, bd)
    kk = kk * lax.rsqrt(jnp.maximum(ssq, 1e-24))
    ka = ka_ref[...]
    ksum = None
    for d in range(2):
        pre = w0_ref[d:d + 1, :] + _dot(wd, wup_ref[d])
        lw_o[d] = -math.exp(-0.5) * _sigmoid(pre)
        a = _sigmoid(a0_ref[d:d + 1, :] + _dot(ad, aup_ref[d]))
        kd = kx * (1.0 + (a - 1.0) * ka)
        kd_o[d] = kd
        kka_o[d] = kk * a
        ksum = kd if ksum is None else ksum + kd
    r_o[...] = r
    v_o[...] = vx
    kk_o[...] = kk
    bonus_o[...] = _dot_exact_rhs(r * ksum * rk_ref[...], bd) * vx
    g_o[...] = _dot(gd, gup_ref[...])


def _rwkv_prep(z_rwkv, p, ts):
    b, s, ncol = z_rwkv.shape
    W = p["rwkv_width"]
    nt = s // ts
    hb = ts // SUBLANES
    last = s // SUBLANES - 1
    full = lambda a: pl.BlockSpec(a.shape, lambda i, t, nd=a.ndim: (0,) * nd)
    out_tok = pl.BlockSpec((None, ts, W), lambda i, t: (i, t, 0))
    out_dir = pl.BlockSpec((2, None, ts, W), lambda i, t: (0, i, t, 0))
    tok_shape = jax.ShapeDtypeStruct((b, s, W), F32)
    dir_shape = jax.ShapeDtypeStruct((2, b, s, W), F32)
    consts = [p["mu"], p["wup_pad"], p["aup_pad"], p["w0"], p["a0"], p["gup"],
              p["k_k"], p["k_a"], p["r_k"], p["head_ones"]]
    return pl.pallas_call(
        functools.partial(_prep_body, width=W),
        grid=(b, nt),
        in_specs=[pl.BlockSpec((None, ts, ncol), lambda i, t: (i, t, 0)),
                  pl.BlockSpec((None, SUBLANES, ncol), lambda i, t: (i, jnp.maximum(t * hb - 1, 0), 0)),
                  pl.BlockSpec((None, SUBLANES, ncol), lambda i, t: (i, jnp.minimum((t + 1) * hb, last), 0)),
                  ] + [full(a) for a in consts],
        out_specs=[out_tok, out_tok, out_tok, out_dir, out_dir, out_dir, out_tok, out_tok],
        out_shape=[tok_shape, tok_shape, tok_shape, dir_shape, dir_shape, dir_shape, tok_shape, tok_shape],
        compiler_params=_cparams(("parallel", "parallel")),
        name="rwkv_prep",
    )(z_rwkv, z_rwkv, z_rwkv, *consts)


def _scan_body(rf_ref, kkf_ref, vf_ref, rb_ref, kkb_ref, vb_ref, lwf_ref, kdf_ref, kkaf_ref,
               lwb_ref, kdb_ref, kkab_ref, yf_ref, yb_ref, st_ref, *, n_pairs):
    L = SCAN_CHUNK
    H = 2 * L
    assert H == LANES
    tt = rf_ref.shape[0]
    nch = tt // L

    @pl.when(pl.program_id(1) == 0)
    def _():
        st_ref[...] = jnp.zeros_like(st_ref)

    r_refs, kk_refs, v_refs = (rf_ref, rb_ref), (kkf_ref, kkb_ref), (vf_ref, vb_ref)
    lw_refs, kd_refs, kka_refs = (lwf_ref, lwb_ref), (kdf_ref, kdb_ref), (kkaf_ref, kkab_ref)
    y_refs = (yf_ref, yb_ref)
    ii = lax.broadcasted_iota(jnp.int32, (H, H), 0)
    jj = lax.broadcasted_iota(jnp.int32, (H, H), 1)
    same = (ii < L) == (jj < L)
    strict = (jnp.logical_and(same, ii > jj), jnp.logical_and(same, ii < jj))
    incl = (jnp.logical_and(same, ii >= jj), jnp.logical_and(same, ii <= jj))
    eye = ii == jj
    li = lax.broadcasted_iota(jnp.int32, (L, L), 0)
    lj = lax.broadcasted_iota(jnp.int32, (L, L), 1)
    tri = (jnp.where(li >= lj, 1.0, 0.0).astype(BF16), jnp.where(li <= lj, 1.0, 0.0).astype(BF16))
    head0 = lax.broadcasted_iota(jnp.int32, (L, H), 1) < L

    def stack(x):
        return jnp.concatenate([jnp.where(head0, x, 0.0), jnp.where(head0, 0.0, x)], axis=0)

    chains = [(d, hp) for d in range(2) for hp in range(n_pairs)]
    P = range(len(chains))
    lanes = [slice(hp * H, (hp + 1) * H) for _, hp in chains]
    dirs = [d for d, _ in chains]

    def chunk(j, carry):
        rows = (pl.ds(pl.multiple_of(j * L, L), L), pl.ds(pl.multiple_of((nch - 1 - j) * L, L), L))
        ld = lambda refs, h: refs[dirs[h]][rows[dirs[h]], lanes[h]]
        lw = [ld(lw_refs, h) for h in P]
        cum = [_dot_exact_lhs(tri[dirs[h]], lw[h]) for h in P]
        tot = [jnp.sum(x, axis=0, keepdims=True) for x in lw]
        e_incl = [jnp.exp(a) for a in cum]
        e_excl = [jnp.exp(a - x) for a, x in zip(cum, lw)]
        e_inv = [jnp.exp(-a) for a in cum]
        e_rem = [jnp.exp(t_ - a) for t_, a in zip(tot, cum)]
        kk = [ld(kk_refs, h) for h in P]
        kka = [ld(kka_refs, h) for h in P]
        kd = [ld(kd_refs, h) for h in P]
        Kk = [stack(a * e) for a, e in zip(kk, e_excl)]
        R = [stack(ld(r_refs, h) * e_incl[h]) for h in P]
        B = [stack(a * e) for a, e in zip(kka, e_inv)]
        Kd = [stack(a * e) for a, e in zip(kd, e_inv)]
        Bh = [stack(a * e).T.astype(BF16) for a, e in zip(kka, e_rem)]
        Kh = [stack(a * e).T.astype(BF16) for a, e in zip(kd, e_rem)]
        V = [stack(ld(v_refs, h)).astype(BF16) for h in P]

        gram = [_dot_nt(jnp.concatenate([Kk[h], R[h]], axis=0), jnp.concatenate([B[h], Kd[h]], axis=0))
                for h in P]
        Np = [jnp.where(strict[dirs[h]], gram[h][:H, :H], 0.0) for h in P]
        AD = [_dot(jnp.concatenate([jnp.where(strict[dirs[h]], gram[h][:H, H:], 0.0),
                                    jnp.where(incl[dirs[h]], gram[h][H:, H:], 0.0)], axis=0), V[h]) for h in P]
        Cm = [jnp.where(incl[dirs[h]], gram[h][H:, :H], 0.0).astype(BF16) for h in P]
        X = [jnp.concatenate([Kk[h], AD[h][:H]], axis=1) for h in P]
        n_fac = L.bit_length() - 1
        for f in range(n_fac):
            Nb = [a.astype(BF16) for a in Np]
            NX = [_dot(Nb[h], X[h]) for h in P]
            if f < n_fac - 1:
                Np = [_dot(Nb[h], Nb[h]) for h in P]
            X = [x - nx if f == 0 else x + nx for x, nx in zip(X, NX)]
        Xb = [x.astype(BF16) for x in X]
        CX = [_dot(Cm[h], Xb[h]) for h in P]
        BW = [_dot(Bh[h], Xb[h]) for h in P]
        KV = [_dot(Kh[h], V[h]) for h in P]
        QM = []
        for h in P:
            Qh = R[h] - CX[h][:, :H]
            M = jnp.where(eye, jnp.exp(tot[h]), 0.0) - BW[h][:, :H]
            QM.append(_dot(jnp.concatenate([Qh, M], axis=0), st_ref[h]))
        for h in P:
            Ys = QM[h][:H] + (AD[h][H:] - CX[h][:, H:])
            st_ref[h] = QM[h][H:] + (KV[h] - BW[h][:, H:])
            y_refs[dirs[h]][rows[dirs[h]], lanes[h]] = Ys[:L] + Ys[L:]
        return carry

    lax.fori_loop(0, nch, chunk, 0)


def _rwkv_scan(r, kk, v, lw, kd, kka, tt):
    b, s, W = r.shape
    nt = s // tt
    n_pairs = W // LANES
    tok_f = pl.BlockSpec((None, tt, W), lambda i, t: (i, t, 0))
    tok_b = pl.BlockSpec((None, tt, W), lambda i, t: (i, nt - 1 - t, 0))
    dir_f = pl.BlockSpec((None, None, tt, W), lambda i, t: (0, i, t, 0))
    dir_b = pl.BlockSpec((None, None, tt, W), lambda i, t: (1, i, nt - 1 - t, 0))
    out = jax.ShapeDtypeStruct((b, s, W), F32)
    return pl.pallas_call(
        functools.partial(_scan_body, n_pairs=n_pairs),
        grid=(b, nt),
        in_specs=[tok_f, tok_f, tok_f, tok_b, tok_b, tok_b, dir_f, dir_f, dir_f, dir_b, dir_b, dir_b],
        out_specs=[tok_f, tok_b],
        out_shape=[out, out],
        scratch_shapes=[pltpu.VMEM((2 * n_pairs, LANES, LANES), F32)],
        compiler_params=_cparams(("parallel", "arbitrary")),
        name="rwkv_scan",
    )(r, kk, v, r, kk, v, lw, kd, kka, lw, kd, kka)


def _mix_body(yf_ref, yb_ref, bonus_ref, g_ref, ret_ref, x_ref, gng_ref, gnb_ref, avg_ref, wo1_ref, wo2_ref,
              l1g_ref, l1b_ref, rwt_ref, rb_ref, x1_ref, idx_ref, gate_ref, cnt_ref, *, alpha):
    y = yf_ref[...] + yb_ref[...]
    avg = avg_ref[...]
    mu = _dot_exact_rhs(y, avg)
    dl = y - mu
    var = _dot_exact_rhs(dl * dl, avg)
    yn = dl * lax.rsqrt(var + RWKV_GN_EPS) * gng_ref[...] + gnb_ref[...]
    rw = (yn + bonus_ref[...]) * g_ref[...]
    m = _dot(ret_ref[...], wo1_ref[...]) + _dot(rw, wo2_ref[...])
    x1 = _layer_norm(alpha * x_ref[...] + m, l1g_ref[...], l1b_ref[...])
    x1_ref[...] = x1

    scores = _sigmoid(_dot_nt(rwt_ref[...], x1))
    E, tm = scores.shape
    GS = E // N_GROUPS
    NEG = -jnp.inf
    biased = scores + rb_ref[...]
    rowi = lax.broadcasted_iota(jnp.int32, (E, tm), 0)
    ri = lax.broadcasted_iota(jnp.int32, (GS, tm), 0)
    gs_rows = []
    for gi in range(N_GROUPS):
        blk = biased[gi * GS:(gi + 1) * GS, :]
        m1 = jnp.max(blk, axis=0, keepdims=True)
        i1 = jnp.min(jnp.where(blk == m1, ri, GS), axis=0, keepdims=True)
        m2 = jnp.max(jnp.where(ri == i1, NEG, blk), axis=0, keepdims=True)
        gs_rows.append(m1 + m2)
    cur = jnp.concatenate(gs_rows, axis=0)
    gidx = lax.broadcasted_iota(jnp.int32, (N_GROUPS, tm), 0)
    row_group = rowi // GS
    emask = jnp.zeros((E, tm), F32)
    for _ in range(TOPK_GROUPS):
        mx = jnp.max(cur, axis=0, keepdims=True)
        ix = jnp.min(jnp.where(cur == mx, gidx, N_GROUPS), axis=0, keepdims=True)
        emask = jnp.where(row_group == ix, 1.0, emask)
        cur = jnp.where(gidx == ix, NEG, cur)
    cur = jnp.where(emask > 0.5, biased, NEG)
    idxs, sels = [], []
    chosen = jnp.zeros((E, tm), F32)
    for _ in range(TOP_K):
        mx = jnp.max(cur, axis=0, keepdims=True)
        ix = jnp.min(jnp.where(cur == mx, rowi, E), axis=0, keepdims=True)
        hit = rowi == ix
        sels.append(jnp.sum(jnp.where(hit, scores, 0.0), axis=0, keepdims=True))
        idxs.append(ix)
        cur = jnp.where(hit, NEG, cur)
        chosen = jnp.where(hit, 1.0, chosen)
    sel = jnp.concatenate(sels, axis=0)
    idx_ref[...] = jnp.concatenate(idxs, axis=0)
    gate_ref[...] = sel / jnp.sum(sel, axis=0, keepdims=True) * ROUTE_SCALE

    @pl.when(pl.program_id(0) == 0)
    def _():
        cnt_ref[...] = jnp.zeros_like(cnt_ref)

    cnt_ref[...] += jnp.sum(chosen, axis=1, keepdims=True)


def _mix_out(y_f, y_b, bonus, g, ret_out, x2d, p, tm):
    T, D = x2d.shape
    W = bonus.shape[1]
    Wr = ret_out.shape[1]
    E = p["router_wt"].shape[0]
    full = lambda a: pl.BlockSpec(a.shape, lambda i, nd=a.ndim: (0,) * nd)
    consts = [p["gn_g"], p["gn_b"], p["head_avg"], p["wo_ret"], p["wo_rwkv"], p["ln1_g"], p["ln1_b"],
              p["router_wt"], p["router_b"]]
    return pl.pallas_call(
        functools.partial(_mix_body, alpha=p["alpha"]),
        grid=(T // tm,),
        in_specs=[pl.BlockSpec((tm, W), lambda i: (i, 0)),
                  pl.BlockSpec((tm, W), lambda i: (i, 0)),
                  pl.BlockSpec((tm, W), lambda i: (i, 0)),
                  pl.BlockSpec((tm, W), lambda i: (i, 0)),
                  pl.BlockSpec((tm, Wr), lambda i: (i, 0)),
                  pl.BlockSpec((tm, D), lambda i: (i, 0))] + [full(a) for a in consts],
        out_specs=[pl.BlockSpec((tm, D), lambda i: (i, 0)),
                   pl.BlockSpec((TOP_K, tm), lambda i: (0, i)),
                   pl.BlockSpec((TOP_K, tm), lambda i: (0, i)),
                   pl.BlockSpec((E, LANES), lambda i: (0, 0))],
        out_shape=[jax.ShapeDtypeStruct((T, D), F32),
                   jax.ShapeDtypeStruct((TOP_K, T), jnp.int32),
                   jax.ShapeDtypeStruct((TOP_K, T), F32),
                   jax.ShapeDtypeStruct((E, LANES), F32)],
        compiler_params=_cparams(("arbitrary",)),
        name="mix_out",
    )(y_f, y_b, bonus, g, ret_out, x2d, *consts)


def _plan_body(idx_ref, pstart_ref, upper_ref, dest_ref, run_ref):
    @pl.when(pl.program_id(0) == 0)
    def _():
        run_ref[...] = jnp.zeros_like(run_ref)

    E = pstart_ref.shape[0]
    tm = idx_ref.shape[1]
    rowi = lax.broadcasted_iota(jnp.int32, (E, tm), 0)
    hits = [rowi == idx_ref[k:k + 1, :] for k in range(TOP_K)]
    member = jnp.zeros((E, tm), F32)
    for hit in hits:
        member = jnp.where(hit, 1.0, member)
    before = jnp.dot(member.astype(BF16), upper_ref[...], preferred_element_type=F32)
    slot = pstart_ref[...] + run_ref[:, 0:1] + before
    dest_ref[...] = jnp.concatenate(
        [jnp.sum(jnp.where(hit, slot, 0.0), axis=0, keepdims=True) for hit in hits], axis=0).astype(jnp.int32)
    run_ref[...] += jnp.sum(member, axis=1, keepdims=True)


def _slot_plan(idx_t, pad_start, tm):
    T = idx_t.shape[1]
    E = pad_start.shape[0]
    upper = (jnp.arange(tm)[:, None] < jnp.arange(tm)[None, :]).astype(BF16)
    return pl.pallas_call(
        _plan_body,
        grid=(T // tm,),
        in_specs=[pl.BlockSpec((TOP_K, tm), lambda i: (0, i)),
                  pl.BlockSpec((E, 1), lambda i: (0, 0)),
                  pl.BlockSpec((tm, tm), lambda i: (0, 0))],
        out_specs=pl.BlockSpec((TOP_K, tm), lambda i: (0, i)),
        out_shape=jax.ShapeDtypeStruct((TOP_K, T), jnp.int32),
        scratch_shapes=[pltpu.VMEM((E, LANES), F32)],
        compiler_params=_cparams(("arbitrary",)),
        name="slot_plan",
    )(idx_t, pad_start.reshape(E, 1).astype(F32), upper)


_UPPER_HALF = -65536


def _bf16_bits(x):
    b = lax.bitcast_convert_type(x, jnp.int32)
    return (b + 0x7FFF + (jnp.right_shift(b, 16) & 1)) & _UPPER_HALF


def _pack_halves(x):
    c = x.shape[1] // 2
    return _bf16_bits(x[:, :c]) | (jnp.right_shift(_bf16_bits(x[:, c:]), 16) & 0xFFFF)


def _unpack_halves(u):
    hi = lax.bitcast_convert_type(u & _UPPER_HALF, F32)
    lo = lax.bitcast_convert_type(jnp.left_shift(u, 16), F32)
    return hi, lo


def _dispatch_body(plo_ref, pn_ref, dest_ref, x_ref, xs_hbm, xp_ref, zero_ref, sem, zsem):
    i = pl.program_id(0)
    tm = x_ref.shape[0]
    E = plo_ref.shape[0]
    xp_ref[...] = _pack_halves(x_ref[...])

    def row_copy(t, slot):
        return pltpu.make_async_copy(xp_ref.at[pl.ds(t, 1), :], xs_hbm.at[pl.ds(slot, 1), :], sem)

    def issue(t, c):
        for k in range(TOP_K):
            row_copy(t, dest_ref[k, t]).start(priority=k % 2)
        return c

    lax.fori_loop(0, tm, issue, 0, unroll=2)

    def zero_copy(slot):
        return pltpu.make_async_copy(zero_ref, xs_hbm.at[pl.ds(slot, 1), :], zsem)

    @pl.when(i == 0)
    def _():
        zero_ref[...] = jnp.zeros_like(zero_ref)

        def per_expert(e, c):
            lo = plo_ref[e]

            def one(j, c2):
                zero_copy(lo + j).start()
                return c2

            lax.fori_loop(0, pn_ref[e], one, 0)

            def one_done(j, c2):
                zero_copy(0).wait()
                return c2

            lax.fori_loop(0, pn_ref[e], one_done, 0)
            return c

        lax.fori_loop(0, E, per_expert, 0)

    def drain(t, c):
        for k in range(TOP_K):
            row_copy(0, 0).wait()
        return c

    lax.fori_loop(0, tm, drain, 0, unroll=2)


def _dispatch(x1, dest, pad_lo, pad_n, n_slots, tm):
    T, D = x1.shape
    grid_spec = pltpu.PrefetchScalarGridSpec(
        num_scalar_prefetch=2,
        grid=(T // tm,),
        in_specs=[pl.BlockSpec((TOP_K, tm), lambda i, plo, pn: (0, i), memory_space=pltpu.SMEM),
                  pl.BlockSpec((tm, D), lambda i, plo, pn: (i, 0))],
        out_specs=pl.BlockSpec(memory_space=pl.ANY),
        scratch_shapes=[pltpu.VMEM((tm, D // 2), jnp.int32), pltpu.VMEM((1, D // 2), jnp.int32),
                        pltpu.SemaphoreType.DMA(()), pltpu.SemaphoreType.DMA(())],
    )
    return pl.pallas_call(
        _dispatch_body,
        grid_spec=grid_spec,
        out_shape=jax.ShapeDtypeStruct((n_slots, D // 2), jnp.int32),
        compiler_params=_cparams(("arbitrary",)),
        name="dispatch",
    )(pad_lo, pad_n, dest, x1)


def _moe_body(be_ref, nu_ref, xs_ref, wg_ref, wu_ref, wd_ref, o_ref):
    i = pl.program_id(0)

    @pl.when(i < nu_ref[0])
    def _():
        x_hi, x_lo = _unpack_halves(xs_ref[...])
        c = x_hi.shape[1]
        gate = _dot(x_hi, wg_ref[:c, :]) + _dot(x_lo, wg_ref[c:, :])
        up = _dot(x_hi, wu_ref[:c, :]) + _dot(x_lo, wu_ref[c:, :])
        o_ref[...] = _pack_halves(_dot(_silu(gate) * up, wd_ref[...]))

    @pl.when(i >= nu_ref[0])
    def _():
        o_ref[...] = jnp.zeros_like(o_ref)


def _moe_ffn(xs, block_expert, n_used, w_gate, w_up, w_down):
    P = xs.shape[0]
    G = MOE_BLOCK
    n_blocks = P // G
    E, D, De = w_gate.shape
    blk = lambda i, nu: jnp.minimum(i, nu[0] - 1)
    grid_spec = pltpu.PrefetchScalarGridSpec(
        num_scalar_prefetch=2,
        grid=(n_blocks,),
        in_specs=[pl.BlockSpec((G, D // 2), lambda i, be, nu: (blk(i, nu), 0)),
                  pl.BlockSpec((None, D, De), lambda i, be, nu: (be[blk(i, nu)], 0, 0)),
                  pl.BlockSpec((None, D, De), lambda i, be, nu: (be[blk(i, nu)], 0, 0)),
                  pl.BlockSpec((None, De, D), lambda i, be, nu: (be[blk(i, nu)], 0, 0))],
        out_specs=pl.BlockSpec((G, D // 2), lambda i, be, nu: (i, 0)),
    )
    return pl.pallas_call(
        _moe_body,
        grid_spec=grid_spec,
        out_shape=jax.ShapeDtypeStruct((P, D // 2), jnp.int32),
        compiler_params=_cparams(("arbitrary",)),
        name="moe_ffn",
    )(block_expert, n_used, xs, w_gate, w_up, w_down)


def _comb_body(pos_ref, ys_hbm, x1_ref, gt_ref, sg_ref, su_ref, sd_ref, l2g_ref, l2b_ref, o_ref,
               buf_ref, sem, *, alpha):
    tm = x1_ref.shape[0]

    def row_copy(slot, k, t):
        return pltpu.make_async_copy(ys_hbm.at[pl.ds(slot, 1), :], buf_ref.at[k, pl.ds(t, 1), :], sem)

    def issue(t, c):
        for k in range(TOP_K):
            row_copy(pos_ref[k, t], k, t).start(priority=k % 2)
        return c

    lax.fori_loop(0, tm, issue, 0, unroll=2)

    x1 = x1_ref[...]
    xb = x1.astype(BF16)
    shared = _dot(_silu(_dot(xb, sg_ref[...])) * _dot(xb, su_ref[...]), sd_ref[...])

    def drain(t, c):
        for k in range(TOP_K):
            row_copy(0, k, t).wait()
        return c

    lax.fori_loop(0, tm, drain, 0, unroll=2)
    gt = gt_ref[...]
    acc_hi = acc_lo = None
    for k in range(TOP_K):
        hi, lo = _unpack_halves(buf_ref[k])
        gk = gt[:, k:k + 1]
        acc_hi = gk * hi if acc_hi is None else acc_hi + gk * hi
        acc_lo = gk * lo if acc_lo is None else acc_lo + gk * lo
    routed = jnp.concatenate([acc_hi, acc_lo], axis=1)
    o_ref[...] = _layer_norm(alpha * x1 + (routed + shared), l2g_ref[...], l2b_ref[...])


def _combine(pos, ys, x1, gates_t, p, tm):
    T, D = x1.shape
    nt = T // tm
    full = lambda a: pl.BlockSpec(a.shape, lambda i, nd=a.ndim: (0,) * nd)
    consts = [p["sh_gate"], p["sh_up"], p["sh_down"], p["ln2_g"], p["ln2_b"]]
    return pl.pallas_call(
        functools.partial(_comb_body, alpha=p["alpha"]),
        grid=(nt,),
        in_specs=[pl.BlockSpec((TOP_K, tm), lambda i: (0, i), memory_space=pltpu.SMEM),
                  pl.BlockSpec(memory_space=pl.ANY),
                  pl.BlockSpec((tm, D), lambda i: (i, 0)),
                  pl.BlockSpec((tm, TOP_K), lambda i: (i, 0))] + [full(a) for a in consts],
        out_specs=pl.BlockSpec((tm, D), lambda i: (i, 0)),
        out_shape=jax.ShapeDtypeStruct((T, D), F32),
        scratch_shapes=[pltpu.VMEM((TOP_K, tm, D // 2), jnp.int32), pltpu.SemaphoreType.DMA(())],
        compiler_params=_cparams(("arbitrary",)),
        name="combine",
    )(pos, ys, x1, gates_t, *consts)


def _segment_layout(counts, n_tokens):
    G = MOE_BLOCK
    E = counts.shape[0]
    n_blocks = -(-(n_tokens * TOP_K + E * (G - 1)) // G)
    padded = (counts + G - 1) // G * G
    pad_end = jnp.cumsum(padded)
    pad_start = pad_end - padded
    first_slot = jnp.arange(n_blocks, dtype=jnp.int32) * G
    block_expert = jnp.minimum(jnp.sum(first_slot[:, None] >= pad_end[None, :], axis=1), E - 1)
    n_used = pad_end[-1:] // G
    return (pad_start, (pad_start + counts).astype(jnp.int32), (padded - counts).astype(jnp.int32),
            block_expert.astype(jnp.int32), n_used.astype(jnp.int32), n_blocks)


def _rotary_tables(s, d):
    inv = ROPE_BASE ** (-jnp.arange(0, d, 2, dtype=F32) / d)
    ang = jnp.arange(s, dtype=F32)[:, None] * inv[None, :]
    cos = jnp.cos(ang)
    sin = jnp.sin(ang)
    return jnp.concatenate([cos, cos], axis=-1), jnp.concatenate([-sin, sin], axis=-1)


def _layer_params(l, depth, w_in, ret_gn_g, ret_gn_b, rwkv_mu, rwkv_w0, rwkv_w_up, rwkv_a0, rwkv_a_up,
                  rwkv_g_up, rwkv_k_k, rwkv_k_a, rwkv_r_k, rwkv_gn_g, rwkv_gn_b, w_out, ln1_g, ln1_b,
                  router_w, router_bias, exp_w_gate, exp_w_up, exp_w_down, sh_w_gate, sh_w_up,
                  sh_w_down, ln2_g, ln2_b):
    ret_w = ret_gn_g.shape[-1]
    W = rwkv_gn_g.shape[-1]
    n_heads, hd = rwkv_r_k.shape[-2:]
    rank_w = rwkv_w_up.shape[2]
    rank_a = rwkv_a_up.shape[2]
    assert rank_w * 2 == LANES and rank_a * 2 == LANES and rwkv_g_up.shape[1] == LANES
    assert hd * 2 == LANES and SCAN_CHUNK == hd
    row = lambda a: a.reshape(1, -1).astype(F32)
    zw = jnp.zeros((rank_w, W), F32)
    za = jnp.zeros((rank_a, W), F32)
    head_id = jnp.arange(W) // hd
    same_head = (head_id[:, None] == head_id[None, :])
    wi = w_in[l]
    return dict(
        alpha=float((2 * depth) ** 0.25),
        rwkv_width=W,
        w_ret=wi[:, :4 * ret_w].astype(BF16),
        w_rwkv=wi[:, 4 * ret_w:].astype(BF16),
        ret_gn_g=row(ret_gn_g[l]), ret_gn_b=row(ret_gn_b[l]),
        logg=jnp.broadcast_to(
            jnp.log1p(-jnp.exp2(-5.0 - jnp.arange(RET_HEADS, dtype=F32)))[:, None, None],
            (RET_HEADS, 1, LANES)),
        mu=row(rwkv_mu[l]),
        wup_pad=jnp.stack([jnp.concatenate([rwkv_w_up[l, 0], zw], 0),
                           jnp.concatenate([zw, rwkv_w_up[l, 1]], 0)]).astype(BF16),
        aup_pad=jnp.stack([jnp.concatenate([rwkv_a_up[l, 0], za], 0),
                           jnp.concatenate([za, rwkv_a_up[l, 1]], 0)]).astype(BF16),
        w0=rwkv_w0[l].astype(F32), a0=rwkv_a0[l].astype(F32),
        gup=rwkv_g_up[l].astype(BF16),
        k_k=row(rwkv_k_k[l]), k_a=row(rwkv_k_a[l]), r_k=row(rwkv_r_k[l]),
        head_ones=same_head.astype(BF16),
        head_avg=(same_head.astype(F32) / hd).astype(BF16),
        gn_g=row(rwkv_gn_g[l]), gn_b=row(rwkv_gn_b[l]),
        wo_ret=w_out[l, :ret_w].astype(BF16), wo_rwkv=w_out[l, ret_w:].astype(BF16),
        ln1_g=row(ln1_g[l]), ln1_b=row(ln1_b[l]),
        router_wt=router_w[l].T.astype(BF16), router_b=router_bias[l].reshape(-1, 1).astype(F32),
        exp_gate=exp_w_gate[l], exp_up=exp_w_up[l], exp_down=exp_w_down[l],
        sh_gate=sh_w_gate[l].astype(BF16), sh_up=sh_w_up[l].astype(BF16), sh_down=sh_w_down[l].astype(BF16),
        ln2_g=row(ln2_g[l]), ln2_b=row(ln2_b[l]),
    )


def _pick(n, pref):
    t = min(n, pref)
    while n % t:
        t //= 2
    return t


def _layer(x, p):
    b, s, D = x.shape
    T = b * s
    x2d = x.reshape(T, D)
    tm = _pick(T, 256)
    z_ret, z_rwkv = _in_proj(x2d, p["w_ret"], p["w_rwkv"], tm)
    cos, sin = _rotary_tables(s, RET_CHUNK)
    ret_out = _retention(z_ret.reshape(b, s, -1), cos, sin, p["logg"], p["ret_gn_g"], p["ret_gn_b"])
    r, v, kk, lw, kd, kka, bonus, g = _rwkv_prep(z_rwkv.reshape(b, s, -1), p, _pick(s, 256))
    y_f, y_b = _rwkv_scan(r, kk, v, lw, kd, kka, _pick(s, 512))
    W = p["rwkv_width"]
    x1, idx_t, gates, cnt = _mix_out(y_f.reshape(T, W), y_b.reshape(T, W), bonus.reshape(T, W), g.reshape(T, W),
                                     ret_out.reshape(T, -1), x2d, p, tm)
    counts = cnt[:, 0].astype(jnp.int32)
    pad_start, pad_lo, pad_n, block_expert, n_used, n_blocks = _segment_layout(counts, T)
    dest = _slot_plan(idx_t, pad_start, _pick(T, 512))
    xs = _dispatch(x1, dest, pad_lo, pad_n, n_blocks * MOE_BLOCK, tm)
    ys = _moe_ffn(xs, block_expert, n_used, p["exp_gate"], p["exp_up"], p["exp_down"])
    out = _combine(dest, ys, x1, gates.T, p, _pick(T, 128))
    return out.reshape(b, s, D)


def kernel(x_prompt, x_sample, w_in, ret_gn_g, ret_gn_b, rwkv_mu, rwkv_w0, rwkv_w_up, rwkv_a0, rwkv_a_up,
           rwkv_g_up, rwkv_k_k, rwkv_k_a, rwkv_r_k, rwkv_gn_g, rwkv_gn_b, w_out, ln1_g, ln1_b, router_w,
           router_bias, exp_w_gate, exp_w_up, exp_w_down, sh_w_gate, sh_w_up, sh_w_down, ln2_g, ln2_b):
    weights = (w_in, ret_gn_g, ret_gn_b, rwkv_mu, rwkv_w0, rwkv_w_up, rwkv_a0, rwkv_a_up, rwkv_g_up,
               rwkv_k_k, rwkv_k_a, rwkv_r_k, rwkv_gn_g, rwkv_gn_b, w_out, ln1_g, ln1_b, router_w,
               router_bias, exp_w_gate, exp_w_up, exp_w_down, sh_w_gate, sh_w_up, sh_w_down, ln2_g, ln2_b)
    depth = w_in.shape[0]
    layers = [_layer_params(l, depth, *weights) for l in range(depth)]

    def trunk(x):
        for p in layers:
            x = _layer(x, p)
        return x

    return trunk(x_prompt), trunk(x_sample)
```

```python
import functools
import math

import jax
import jax.numpy as jnp
from jax import lax
from jax.experimental import pallas as pl
from jax.experimental.pallas import tpu as pltpu
from jax.experimental.pallas import tpu_sc as plsc

F32 = jnp.float32
BF16 = jnp.bfloat16

RET_HEADS = 4
RET_CHUNK = 128
ROPE_BASE = 10000.0
TOP_K = 8
N_GROUPS = 8
TOPK_GROUPS = 4
ROUTE_SCALE = 2.5
MOE_BLOCK = 256
LN_EPS = 1e-5
GN_EPS = 1e-5
RWKV_GN_EPS = 64e-5

LANES = 128
SUBLANES = 8
VMEM_LIMIT_BYTES = 56 * 1024 * 1024

SCAN_CHUNK = 64


def _cparams(semantics):
    return pltpu.CompilerParams(dimension_semantics=semantics, vmem_limit_bytes=VMEM_LIMIT_BYTES)


def _dot(a, b):
    return jnp.dot(a.astype(BF16), b.astype(BF16), preferred_element_type=F32)


def _dot_nt(a, b):
    return lax.dot_general(a.astype(BF16), b.astype(BF16), (((1,), (1,)), ((), ())),
                           preferred_element_type=F32)


def _split3(x):
    hi = x.astype(BF16)
    r1 = x - hi.astype(F32)
    mid = r1.astype(BF16)
    lo = (r1 - mid.astype(F32)).astype(BF16)
    return hi, mid, lo


def _dot_exact_rhs(x, w_bf16):
    hi, mid, lo = _split3(x)
    out = jnp.dot(hi, w_bf16, preferred_element_type=F32)
    out += jnp.dot(mid, w_bf16, preferred_element_type=F32)
    out += jnp.dot(lo, w_bf16, preferred_element_type=F32)
    return out


def _dot_exact_lhs(w_bf16, x):
    hi, mid, lo = _split3(x)
    out = jnp.dot(w_bf16, hi, preferred_element_type=F32)
    out += jnp.dot(w_bf16, mid, preferred_element_type=F32)
    out += jnp.dot(w_bf16, lo, preferred_element_type=F32)
    return out


def _sigmoid(x):
    return 1.0 / (1.0 + jnp.exp(-x))


def _silu(x):
    return x * _sigmoid(x)


def _layer_norm(h, g, b):
    mu = jnp.mean(h, axis=-1, keepdims=True)
    d = h - mu
    var = jnp.mean(d * d, axis=-1, keepdims=True)
    return d * lax.rsqrt(var + LN_EPS) * g + b


def _inproj_body(x_ref, wr_ref, ww_ref, zr_ref, zw_ref):
    xb = x_ref[...].astype(BF16)
    zr_ref[...] = jnp.dot(xb, wr_ref[...], preferred_element_type=F32)
    zw_ref[...] = jnp.dot(xb, ww_ref[...], preferred_element_type=F32)


def _in_proj(x2d, w_ret, w_rwkv, tm):
    T, D = x2d.shape
    nr, nw = w_ret.shape[1], w_rwkv.shape[1]
    return pl.pallas_call(
        _inproj_body,
        grid=(T // tm,),
        in_specs=[pl.BlockSpec((tm, D), lambda i: (i, 0)),
                  pl.BlockSpec((D, nr), lambda i: (0, 0)),
                  pl.BlockSpec((D, nw), lambda i: (0, 0))],
        out_specs=[pl.BlockSpec((tm, nr), lambda i: (i, 0)),
                   pl.BlockSpec((tm, nw), lambda i: (i, 0))],
        out_shape=[jax.ShapeDtypeStruct((T, nr), F32), jax.ShapeDtypeStruct((T, nw), F32)],
        compiler_params=_cparams(("parallel",)),
        name="in_proj",
    )(x2d, w_ret, w_rwkv)


def _ret_body(q_ref, k_ref, v_ref, gt_ref, cos_ref, sin_ref, lg_ref, gg_ref, gb_ref, o_ref,
              qs_ref, ks_ref, acc_ref, *, qscale):
    C = RET_CHUNK
    s = q_ref.shape[0]
    n = s // C
    lg = lg_ref[...]
    pos = lax.broadcasted_iota(jnp.int32, (C, C), 0).astype(F32)
    col = lax.broadcasted_iota(jnp.int32, (C, C), 1).astype(F32)
    sc_q_prev = jnp.exp((pos + 1.0) * lg)
    sc_k_fwd = jnp.exp((C - 1.0 - pos) * lg)
    sc_k_bwd = jnp.exp(pos * lg)
    sc_q_next = jnp.exp((C - pos) * lg)
    g_chunk = jnp.exp(float(C) * lg)
    decay = jnp.exp(jnp.abs(pos - col) * lg)

    cos = cos_ref[...]
    sin = sin_ref[...]
    q = q_ref[...]
    qs_ref[...] = (q * cos + pltpu.roll(q, C // 2, 1) * sin) * qscale
    k = k_ref[...]
    ks_ref[...] = k * cos + pltpu.roll(k, C // 2, 1) * sin

    def fwd(c, S):
        r0 = pl.multiple_of(c * C, C)
        qc = qs_ref[pl.ds(r0, C), :]
        kc = ks_ref[pl.ds(r0, C), :]
        vb = v_ref[pl.ds(r0, C), :].astype(BF16)
        sc = _dot_nt(qc, kc) * decay
        out = _dot(sc, vb) + _dot(qc * sc_q_prev, S)
        acc_ref[pl.ds(r0, C), :] = out
        return S * g_chunk + _dot((kc * sc_k_fwd).T, vb)

    lax.fori_loop(0, n, fwd, jnp.zeros((C, C), F32))

    def bwd(i, S):
        r0 = pl.multiple_of((n - 1 - i) * C, C)
        qc = qs_ref[pl.ds(r0, C), :]
        kc = ks_ref[pl.ds(r0, C), :]
        vb = v_ref[pl.ds(r0, C), :].astype(BF16)
        acc_ref[pl.ds(r0, C), :] += _dot(qc * sc_q_next, S)
        return S * g_chunk + _dot((kc * sc_k_bwd).T, vb)

    lax.fori_loop(0, n, bwd, jnp.zeros((C, C), F32))

    y = acc_ref[...]
    mu = jnp.mean(y, axis=-1, keepdims=True)
    d = y - mu
    var = jnp.mean(d * d, axis=-1, keepdims=True)
    yn = d * lax.rsqrt(var + GN_EPS) * gg_ref[...] + gb_ref[...]
    o_ref[...] = _silu(gt_ref[...]) * yn


def _retention(z_ret, cos, sin, logg, gn_g, gn_b):
    b, s, _ = z_ret.shape
    C = RET_CHUNK
    H = RET_HEADS
    assert C == LANES and s % C == 0
    blk = lambda off: pl.BlockSpec((None, s, C), lambda i, h, off=off: (i, 0, off + h))
    return pl.pallas_call(
        functools.partial(_ret_body, qscale=float(C) ** -0.5),
        grid=(b, H),
        in_specs=[blk(0), blk(H), blk(2 * H), blk(3 * H),
                  pl.BlockSpec((s, C), lambda i, h: (0, 0)),
                  pl.BlockSpec((s, C), lambda i, h: (0, 0)),
                  pl.BlockSpec((None, 1, C), lambda i, h: (h, 0, 0)),
                  pl.BlockSpec((1, C), lambda i, h: (0, h)),
                  pl.BlockSpec((1, C), lambda i, h: (0, h))],
        out_specs=pl.BlockSpec((None, s, C), lambda i, h: (i, 0, h)),
        out_shape=jax.ShapeDtypeStruct((b, s, H * C), F32),
        scratch_shapes=[pltpu.VMEM((s, C), F32), pltpu.VMEM((s, C), F32), pltpu.VMEM((s, C), F32)],
        compiler_params=_cparams(("parallel", "parallel")),
        name="retention",
    )(z_ret, z_ret, z_ret, z_ret, cos, sin, logg, gn_g, gn_b)


def _prep_body(z_ref, zp_ref, zn_ref, mu_ref, wup_ref, aup_ref, w0_ref, a0_ref, gup_ref,
               kkp_ref, ka_ref, rk_ref, bd_ref,
               r_o, v_o, kk_o, lw_o, kd_o, kka_o, bonus_o, g_o, *, width):
    t = pl.program_id(1)
    nt = pl.num_programs(1)
    W = width
    z = z_ref[...]
    ts = z.shape[0]
    row = lax.broadcasted_iota(jnp.int32, (ts, 1), 0)
    prev_row = jnp.where(t > 0, zp_ref[SUBLANES - 1:SUBLANES, :], 0.0)
    next_row = jnp.where(t < nt - 1, zn_ref[0:1, :], 0.0)
    prev = jnp.where(row == 0, prev_row, pltpu.roll(z, 1, 0))
    nxt = jnp.where(row == ts - 1, next_row, pltpu.roll(z, ts - 1, 0))
    zs = z + mu_ref[...] * (0.5 * (prev + nxt) - z)

    r = zs[:, 0:W]
    kx = zs[:, W:2 * W]
    vx = zs[:, 2 * W:3 * W]
    wd = jnp.tanh(zs[:, 3 * W:3 * W + LANES])
    ad = zs[:, 3 * W + LANES:3 * W + 2 * LANES]
    gd = _sigmoid(zs[:, 3 * W + 2 * LANES:3 * W + 3 * LANES])
    bd = bd_ref[...]

    kk = kx * kkp_ref[...]
    ssq = _dot_exact_rhs(kk * kk, bd)
    kk = kk * lax.rsqrt(jnp.maximum(ssq, 1e-24))
    ka = ka_ref[...]
    ksum = None
    for d in range(2):
        pre = w0_ref[d:d + 1, :] + _dot(wd, wup_ref[d])
        lw_o[d] = -math.exp(-0.5) * _sigmoid(pre)
        a = _sigmoid(a0_ref[d:d + 1, :] + _dot(ad, aup_ref[d]))
        kd = kx * (1.0 + (a - 1.0) * ka)
        kd_o[d] = kd
        kka_o[d] = kk * a
        ksum = kd if ksum is None else ksum + kd
    r_o[...] = r
    v_o[...] = vx
    kk_o[...] = kk
    bonus_o[...] = _dot_exact_rhs(r * ksum * rk_ref[...], bd) * vx
    g_o[...] = _dot(gd, gup_ref[...])


def _rwkv_prep(z_rwkv, p, ts):
    b, s, ncol = z_rwkv.shape
    W = p["rwkv_width"]
    nt = s // ts
    hb = ts // SUBLANES
    last = s // SUBLANES - 1
    full = lambda a: pl.BlockSpec(a.shape, lambda i, t, nd=a.ndim: (0,) * nd)
    out_tok = pl.BlockSpec((None, ts, W), lambda i, t: (i, t, 0))
    out_dir = pl.BlockSpec((2, None, ts, W), lambda i, t: (0, i, t, 0))
    tok_shape = jax.ShapeDtypeStruct((b, s, W), F32)
    dir_shape = jax.ShapeDtypeStruct((2, b, s, W), F32)
    consts = [p["mu"], p["wup_pad"], p["aup_pad"], p["w0"], p["a0"], p["gup"],
              p["k_k"], p["k_a"], p["r_k"], p["head_ones"]]
    return pl.pallas_call(
        functools.partial(_prep_body, width=W),
        grid=(b, nt),
        in_specs=[pl.BlockSpec((None, ts, ncol), lambda i, t: (i, t, 0)),
                  pl.BlockSpec((None, SUBLANES, ncol), lambda i, t: (i, jnp.maximum(t * hb - 1, 0), 0)),
                  pl.BlockSpec((None, SUBLANES, ncol), lambda i, t: (i, jnp.minimum((t + 1) * hb, last), 0)),
                  ] + [full(a) for a in consts],
        out_specs=[out_tok, out_tok, out_tok, out_dir, out_dir, out_dir, out_tok, out_tok],
        out_shape=[tok_shape, tok_shape, tok_shape, dir_shape, dir_shape, dir_shape, tok_shape, tok_shape],
        compiler_params=_cparams(("parallel", "parallel")),
        name="rwkv_prep",
    )(z_rwkv, z_rwkv, z_rwkv, *consts)


def _scan_body(rf_ref, kkf_ref, vf_ref, rb_ref, kkb_ref, vb_ref, lwf_ref, kdf_ref, kkaf_ref,
               lwb_ref, kdb_ref, kkab_ref, yf_ref, yb_ref, st_ref, *, n_pairs):
    L = SCAN_CHUNK
    H = 2 * L
    assert H == LANES
    tt = rf_ref.shape[0]
    nch = tt // L

    @pl.when(pl.program_id(1) == 0)
    def _():
        st_ref[...] = jnp.zeros_like(st_ref)

    r_refs, kk_refs, v_refs = (rf_ref, rb_ref), (kkf_ref, kkb_ref), (vf_ref, vb_ref)
    lw_refs, kd_refs, kka_refs = (lwf_ref, lwb_ref), (kdf_ref, kdb_ref), (kkaf_ref, kkab_ref)
    y_refs = (yf_ref, yb_ref)
    ii = lax.broadcasted_iota(jnp.int32, (H, H), 0)
    jj = lax.broadcasted_iota(jnp.int32, (H, H), 1)
    same = (ii < L) == (jj < L)
    strict = (jnp.logical_and(same, ii > jj), jnp.logical_and(same, ii < jj))
    incl = (jnp.logical_and(same, ii >= jj), jnp.logical_and(same, ii <= jj))
    eye = ii == jj
    li = lax.broadcasted_iota(jnp.int32, (L, L), 0)
    lj = lax.broadcasted_iota(jnp.int32, (L, L), 1)
    tri = (jnp.where(li >= lj, 1.0, 0.0).astype(BF16), jnp.where(li <= lj, 1.0, 0.0).astype(BF16))
    head0 = lax.broadcasted_iota(jnp.int32, (L, H), 1) < L

    def stack(x):
        return jnp.concatenate([jnp.where(head0, x, 0.0), jnp.where(head0, 0.0, x)], axis=0)

    chains = [(d, hp) for d in range(2) for hp in range(n_pairs)]
    P = range(len(chains))
    lanes = [slice(hp * H, (hp + 1) * H) for _, hp in chains]
    dirs = [d for d, _ in chains]

    def chunk(j, carry):
        rows = (pl.ds(pl.multiple_of(j * L, L), L), pl.ds(pl.multiple_of((nch - 1 - j) * L, L), L))
        ld = lambda refs, h: refs[dirs[h]][rows[dirs[h]], lanes[h]]
        lw = [ld(lw_refs, h) for h in P]
        cum = [_dot_exact_lhs(tri[dirs[h]], lw[h]) for h in P]
        tot = [jnp.sum(x, axis=0, keepdims=True) for x in lw]
        e_incl = [jnp.exp(a) for a in cum]
        e_excl = [jnp.exp(a - x) for a, x in zip(cum, lw)]
        e_inv = [jnp.exp(-a) for a in cum]
        e_rem = [jnp.exp(t_ - a) for t_, a in zip(tot, cum)]
        kk = [ld(kk_refs, h) for h in P]
        kka = [ld(kka_refs, h) for h in P]
        kd = [ld(kd_refs, h) for h in P]
        Kk = [stack(a * e) for a, e in zip(kk, e_excl)]
        R = [stack(ld(r_refs, h) * e_incl[h]) for h in P]
        B = [stack(a * e) for a, e in zip(kka, e_inv)]
        Kd = [stack(a * e) for a, e in zip(kd, e_inv)]
        Bh = [stack(a * e).T.astype(BF16) for a, e in zip(kka, e_rem)]
        Kh = [stack(a * e).T.astype(BF16) for a, e in zip(kd, e_rem)]
        V = [stack(ld(v_refs, h)).astype(BF16) for h in P]

        gram = [_dot_nt(jnp.concatenate([Kk[h], R[h]], axis=0), jnp.concatenate([B[h], Kd[h]], axis=0))
                for h in P]
        Np = [jnp.where(strict[dirs[h]], gram[h][:H, :H], 0.0) for h in P]
        AD = [_dot(jnp.concatenate([jnp.where(strict[dirs[h]], gram[h][:H, H:], 0.0),
                                    jnp.where(incl[dirs[h]], gram[h][H:, H:], 0.0)], axis=0), V[h]) for h in P]
        Cm = [jnp.where(incl[dirs[h]], gram[h][H:, :H], 0.0).astype(BF16) for h in P]
        X = [jnp.concatenate([Kk[h], AD[h][:H]], axis=1) for h in P]
        n_fac = L.bit_length() - 1
        for f in range(n_fac):
            Nb = [a.astype(BF16) for a in Np]
            NX = [_dot(Nb[h], X[h]) for h in P]
            if f < n_fac - 1:
                Np = [_dot(Nb[h], Nb[h]) for h in P]
            X = [x - nx if f == 0 else x + nx for x, nx in zip(X, NX)]
        Xb = [x.astype(BF16) for x in X]
        CX = [_dot(Cm[h], Xb[h]) for h in P]
        BW = [_dot(Bh[h], Xb[h]) for h in P]
        KV = [_dot(Kh[h], V[h]) for h in P]
        QM = []
        for h in P:
            Qh = R[h] - CX[h][:, :H]
            M = jnp.where(eye, jnp.exp(tot[h]), 0.0) - BW[h][:, :H]
            QM.append(_dot(jnp.concatenate([Qh, M], axis=0), st_ref[h]))
        for h in P:
            Ys = QM[h][:H] + (AD[h][H:] - CX[h][:, H:])
            st_ref[h] = QM[h][H:] + (KV[h] - BW[h][:, H:])
            y_refs[dirs[h]][rows[dirs[h]], lanes[h]] = Ys[:L] + Ys[L:]
        return carry

    lax.fori_loop(0, nch, chunk, 0)


def _rwkv_scan(r, kk, v, lw, kd, kka, tt):
    b, s, W = r.shape
    nt = s // tt
    n_pairs = W // LANES
    tok_f = pl.BlockSpec((None, tt, W), lambda i, t: (i, t, 0))
    tok_b = pl.BlockSpec((None, tt, W), lambda i, t: (i, nt - 1 - t, 0))
    dir_f = pl.BlockSpec((None, None, tt, W), lambda i, t: (0, i, t, 0))
    dir_b = pl.BlockSpec((None, None, tt, W), lambda i, t: (1, i, nt - 1 - t, 0))
    out = jax.ShapeDtypeStruct((b, s, W), F32)
    return pl.pallas_call(
        functools.partial(_scan_body, n_pairs=n_pairs),
        grid=(b, nt),
        in_specs=[tok_f, tok_f, tok_f, tok_b, tok_b, tok_b, dir_f, dir_f, dir_f, dir_b, dir_b, dir_b],
        out_specs=[tok_f, tok_b],
        out_shape=[out, out],
        scratch_shapes=[pltpu.VMEM((2 * n_pairs, LANES, LANES), F32)],
        compiler_params=_cparams(("parallel", "arbitrary")),
        name="rwkv_scan",
    )(r, kk, v, r, kk, v, lw, kd, kka, lw, kd, kka)


def _mix_body(yf_ref, yb_ref, bonus_ref, g_ref, ret_ref, x_ref, gng_ref, gnb_ref, avg_ref, wo1_ref, wo2_ref,
              l1g_ref, l1b_ref, rwt_ref, rb_ref, x1_ref, xp_ref, idx_ref, gate_ref, cnt_ref, *, alpha):
    y = yf_ref[...] + yb_ref[...]
    avg = avg_ref[...]
    mu = _dot_exact_rhs(y, avg)
    dl = y - mu
    var = _dot_exact_rhs(dl * dl, avg)
    yn = dl * lax.rsqrt(var + RWKV_GN_EPS) * gng_ref[...] + gnb_ref[...]
    rw = (yn + bonus_ref[...]) * g_ref[...]
    m = _dot(ret_ref[...], wo1_ref[...]) + _dot(rw, wo2_ref[...])
    x1 = _layer_norm(alpha * x_ref[...] + m, l1g_ref[...], l1b_ref[...])
    x1_ref[...] = x1
    xp_ref[...] = _pack_halves(x1)

    scores = _sigmoid(_dot_nt(rwt_ref[...], x1))
    E, tm = scores.shape
    GS = E // N_GROUPS
    NEG = -jnp.inf
    biased = scores + rb_ref[...]
    rowi = lax.broadcasted_iota(jnp.int32, (E, tm), 0)
    ri = lax.broadcasted_iota(jnp.int32, (GS, tm), 0)
    gs_rows = []
    for gi in range(N_GROUPS):
        blk = biased[gi * GS:(gi + 1) * GS, :]
        m1 = jnp.max(blk, axis=0, keepdims=True)
        i1 = jnp.min(jnp.where(blk == m1, ri, GS), axis=0, keepdims=True)
        m2 = jnp.max(jnp.where(ri == i1, NEG, blk), axis=0, keepdims=True)
        gs_rows.append(m1 + m2)
    cur = jnp.concatenate(gs_rows, axis=0)
    gidx = lax.broadcasted_iota(jnp.int32, (N_GROUPS, tm), 0)
    row_group = rowi // GS
    emask = jnp.zeros((E, tm), F32)
    for _ in range(TOPK_GROUPS):
        mx = jnp.max(cur, axis=0, keepdims=True)
        ix = jnp.min(jnp.where(cur == mx, gidx, N_GROUPS), axis=0, keepdims=True)
        emask = jnp.where(row_group == ix, 1.0, emask)
        cur = jnp.where(gidx == ix, NEG, cur)
    cur = jnp.where(emask > 0.5, biased, NEG)
    idxs, sels = [], []
    chosen = jnp.zeros((E, tm), F32)
    for _ in range(TOP_K):
        mx = jnp.max(cur, axis=0, keepdims=True)
        ix = jnp.min(jnp.where(cur == mx, rowi, E), axis=0, keepdims=True)
        hit = rowi == ix
        sels.append(jnp.sum(jnp.where(hit, scores, 0.0), axis=0, keepdims=True))
        idxs.append(ix)
        cur = jnp.where(hit, NEG, cur)
        chosen = jnp.where(hit, 1.0, chosen)
    sel = jnp.concatenate(sels, axis=0)
    idx_ref[...] = jnp.concatenate(idxs, axis=0)
    gate_ref[...] = sel / jnp.sum(sel, axis=0, keepdims=True) * ROUTE_SCALE

    @pl.when(pl.program_id(0) == 0)
    def _():
        cnt_ref[...] = jnp.zeros_like(cnt_ref)

    cnt_ref[...] += jnp.sum(chosen, axis=1, keepdims=True)


def _mix_out(y_f, y_b, bonus, g, ret_out, x2d, p, tm):
    T, D = x2d.shape
    W = bonus.shape[1]
    Wr = ret_out.shape[1]
    E = p["router_wt"].shape[0]
    full = lambda a: pl.BlockSpec(a.shape, lambda i, nd=a.ndim: (0,) * nd)
    consts = [p["gn_g"], p["gn_b"], p["head_avg"], p["wo_ret"], p["wo_rwkv"], p["ln1_g"], p["ln1_b"],
              p["router_wt"], p["router_b"]]
    return pl.pallas_call(
        functools.partial(_mix_body, alpha=p["alpha"]),
        grid=(T // tm,),
        in_specs=[pl.BlockSpec((tm, W), lambda i: (i, 0)),
                  pl.BlockSpec((tm, W), lambda i: (i, 0)),
                  pl.BlockSpec((tm, W), lambda i: (i, 0)),
                  pl.BlockSpec((tm, W), lambda i: (i, 0)),
                  pl.BlockSpec((tm, Wr), lambda i: (i, 0)),
                  pl.BlockSpec((tm, D), lambda i: (i, 0))] + [full(a) for a in consts],
        out_specs=[pl.BlockSpec((tm, D), lambda i: (i, 0)),
                   pl.BlockSpec((tm, D // 2), lambda i: (i, 0)),
                   pl.BlockSpec((TOP_K, tm), lambda i: (0, i)),
                   pl.BlockSpec((TOP_K, tm), lambda i: (0, i)),
                   pl.BlockSpec((E, LANES), lambda i: (0, 0))],
        out_shape=[jax.ShapeDtypeStruct((T, D), F32),
                   jax.ShapeDtypeStruct((T, D // 2), jnp.int32),
                   jax.ShapeDtypeStruct((TOP_K, T), jnp.int32),
                   jax.ShapeDtypeStruct((TOP_K, T), F32),
                   jax.ShapeDtypeStruct((E, LANES), F32)],
        compiler_params=_cparams(("arbitrary",)),
        name="mix_out",
    )(y_f, y_b, bonus, g, ret_out, x2d, *consts)


def _plan_body(idx_ref, pstart_ref, upper_ref, dest_ref, run_ref):
    @pl.when(pl.program_id(0) == 0)
    def _():
        run_ref[...] = jnp.zeros_like(run_ref)

    E = pstart_ref.shape[0]
    tm = idx_ref.shape[1]
    rowi = lax.broadcasted_iota(jnp.int32, (E, tm), 0)
    hits = [rowi == idx_ref[k:k + 1, :] for k in range(TOP_K)]
    member = jnp.zeros((E, tm), F32)
    for hit in hits:
        member = jnp.where(hit, 1.0, member)
    before = jnp.dot(member.astype(BF16), upper_ref[...], preferred_element_type=F32)
    slot = pstart_ref[...] + run_ref[:, 0:1] + before
    dest_ref[...] = jnp.concatenate(
        [jnp.sum(jnp.where(hit, slot, 0.0), axis=0, keepdims=True) for hit in hits], axis=0).astype(jnp.int32)
    run_ref[...] += jnp.sum(member, axis=1, keepdims=True)


def _slot_plan(idx_t, pad_start, tm):
    T = idx_t.shape[1]
    E = pad_start.shape[0]
    upper = (jnp.arange(tm)[:, None] < jnp.arange(tm)[None, :]).astype(BF16)
    return pl.pallas_call(
        _plan_body,
        grid=(T // tm,),
        in_specs=[pl.BlockSpec((TOP_K, tm), lambda i: (0, i)),
                  pl.BlockSpec((E, 1), lambda i: (0, 0)),
                  pl.BlockSpec((tm, tm), lambda i: (0, 0))],
        out_specs=pl.BlockSpec((TOP_K, tm), lambda i: (0, i)),
        out_shape=jax.ShapeDtypeStruct((TOP_K, T), jnp.int32),
        scratch_shapes=[pltpu.VMEM((E, LANES), F32)],
        compiler_params=_cparams(("arbitrary",)),
        name="slot_plan",
    )(idx_t, pad_start.reshape(E, 1).astype(F32), upper)


_UPPER_HALF = -65536


def _bf16_bits(x):
    b = lax.bitcast_convert_type(x, jnp.int32)
    return (b + 0x7FFF + (jnp.right_shift(b, 16) & 1)) & _UPPER_HALF


def _pack_halves(x):
    c = x.shape[1] // 2
    return _bf16_bits(x[:, :c]) | (jnp.right_shift(_bf16_bits(x[:, c:]), 16) & 0xFFFF)


def _unpack_halves(u):
    hi = lax.bitcast_convert_type(u & _UPPER_HALF, F32)
    lo = lax.bitcast_convert_type(jnp.left_shift(u, 16), F32)
    return hi, lo


SC_WINDOW = 64


def _sc_dispatch(xp, dest, n_slots):
    T, C = xp.shape
    n_win = T // SC_WINDOW
    mesh = plsc.VectorSubcoreMesh(core_axis_name="core", subcore_axis_name="subcore")

    @pl.kernel(out_type=jax.ShapeDtypeStruct((n_slots, C), xp.dtype), mesh=mesh, scratch_types=[])
    def scatter_rows(x_hbm, i_hbm, o_hbm):
        def body(x_vmem, i_vmem):
            pltpu.sync_copy(x_vmem, o_hbm.at[i_vmem.at[0]])

        pltpu.emit_pipeline(
            body,
            grid=(T // SC_WINDOW, TOP_K),
            in_specs=[pl.BlockSpec((SC_WINDOW, C), lambda i, k: (i, 0)),
                      pl.BlockSpec((1, SC_WINDOW), lambda i, k: (k * n_win + i, 0))],
            out_specs=[],
            core_axis_name=("core", "subcore"),
            dimension_semantics=(pltpu.PARALLEL, pltpu.ARBITRARY),
        )(x_hbm, i_hbm)

    return scatter_rows(xp, dest.reshape(TOP_K * n_win, SC_WINDOW))


def _sc_gather(ys, dest):
    n_idx = dest.shape[0] * dest.shape[1]
    C = ys.shape[1]
    mesh = plsc.VectorSubcoreMesh(core_axis_name="core", subcore_axis_name="subcore")

    @pl.kernel(out_type=jax.ShapeDtypeStruct((n_idx, C), ys.dtype), mesh=mesh, scratch_types=[])
    def gather_rows(y_hbm, i_hbm, o_hbm):
        def body(i_vmem, o_vmem):
            pltpu.sync_copy(y_hbm.at[i_vmem.at[0]], o_vmem)

        pltpu.emit_pipeline(
            body,
            grid=(n_idx // SC_WINDOW,),
            in_specs=[pl.BlockSpec((1, SC_WINDOW), lambda i: (i, 0))],
            out_specs=[pl.BlockSpec((SC_WINDOW, C), lambda i: (i, 0))],
            core_axis_name=("core", "subcore"),
            dimension_semantics=(pltpu.PARALLEL,),
        )(i_hbm, o_hbm)

    return gather_rows(ys, dest.reshape(n_idx // SC_WINDOW, SC_WINDOW))


def _moe_body(be_ref, nu_ref, nv_ref, xs_ref, wg_ref, wu_ref, wd_ref, o_ref):
    i = pl.program_id(0)

    @pl.when(i < nu_ref[0])
    def _():
        row = lax.broadcasted_iota(jnp.int32, xs_ref.shape, 0)
        x_hi, x_lo = _unpack_halves(jnp.where(row < nv_ref[i], xs_ref[...], 0))
        c = x_hi.shape[1]
        gate = _dot(x_hi, wg_ref[:c, :]) + _dot(x_lo, wg_ref[c:, :])
        up = _dot(x_hi, wu_ref[:c, :]) + _dot(x_lo, wu_ref[c:, :])
        o_ref[...] = _pack_halves(_dot(_silu(gate) * up, wd_ref[...]))

    @pl.when(i >= nu_ref[0])
    def _():
        o_ref[...] = jnp.zeros_like(o_ref)


def _moe_ffn(xs, block_expert, n_used, n_valid, w_gate, w_up, w_down):
    P = xs.shape[0]
    G = MOE_BLOCK
    n_blocks = P // G
    E, D, De = w_gate.shape
    blk = lambda i, nu: jnp.minimum(i, nu[0] - 1)
    grid_spec = pltpu.PrefetchScalarGridSpec(
        num_scalar_prefetch=3,
        grid=(n_blocks,),
        in_specs=[pl.BlockSpec((G, D // 2), lambda i, be, nu, nv: (blk(i, nu), 0)),
                  pl.BlockSpec((None, D, De), lambda i, be, nu, nv: (be[blk(i, nu)], 0, 0)),
                  pl.BlockSpec((None, D, De), lambda i, be, nu, nv: (be[blk(i, nu)], 0, 0)),
                  pl.BlockSpec((None, De, D), lambda i, be, nu, nv: (be[blk(i, nu)], 0, 0))],
        out_specs=pl.BlockSpec((G, D // 2), lambda i, be, nu, nv: (i, 0)),
    )
    return pl.pallas_call(
        _moe_body,
        grid_spec=grid_spec,
        out_shape=jax.ShapeDtypeStruct((P, D // 2), jnp.int32),
        compiler_params=_cparams(("arbitrary",)),
        name="moe_ffn",
    )(block_expert, n_used, n_valid, xs, w_gate, w_up, w_down)


def _comb_body(yg_ref, x1_ref, gt_ref, sg_ref, su_ref, sd_ref, l2g_ref, l2b_ref, o_ref, *, alpha):
    x1 = x1_ref[...]
    xb = x1.astype(BF16)
    shared = _dot(_silu(_dot(xb, sg_ref[...])) * _dot(xb, su_ref[...]), sd_ref[...])
    gt = gt_ref[...]
    acc_hi = acc_lo = None
    for k in range(TOP_K):
        hi, lo = _unpack_halves(yg_ref[k])
        gk = gt[:, k:k + 1]
        acc_hi = gk * hi if acc_hi is None else acc_hi + gk * hi
        acc_lo = gk * lo if acc_lo is None else acc_lo + gk * lo
    routed = jnp.concatenate([acc_hi, acc_lo], axis=1)
    o_ref[...] = _layer_norm(alpha * x1 + (routed + shared), l2g_ref[...], l2b_ref[...])


def _combine(yg, x1, gates_t, p, tm):
    T, D = x1.shape
    full = lambda a: pl.BlockSpec(a.shape, lambda i, nd=a.ndim: (0,) * nd)
    consts = [p["sh_gate"], p["sh_up"], p["sh_down"], p["ln2_g"], p["ln2_b"]]
    return pl.pallas_call(
        functools.partial(_comb_body, alpha=p["alpha"]),
        grid=(T // tm,),
        in_specs=[pl.BlockSpec((TOP_K, tm, D // 2), lambda i: (0, i, 0)),
                  pl.BlockSpec((tm, D), lambda i: (i, 0)),
                  pl.BlockSpec((tm, TOP_K), lambda i: (i, 0))] + [full(a) for a in consts],
        out_specs=pl.BlockSpec((tm, D), lambda i: (i, 0)),
        out_shape=jax.ShapeDtypeStruct((T, D), F32),
        compiler_params=_cparams(("parallel",)),
        name="combine",
    )(yg, x1, gates_t, *consts)


def _segment_layout(counts, n_tokens):
    G = MOE_BLOCK
    E = counts.shape[0]
    n_blocks = -(-(n_tokens * TOP_K + E * (G - 1)) // G)
    padded = (counts + G - 1) // G * G
    pad_end = jnp.cumsum(padded)
    pad_start = pad_end - padded
    first_slot = jnp.arange(n_blocks, dtype=jnp.int32) * G
    block_expert = jnp.minimum(jnp.sum(first_slot[:, None] >= pad_end[None, :], axis=1), E - 1)
    n_used = pad_end[-1:] // G
    n_valid = jnp.clip((pad_start + counts)[block_expert] - first_slot, 0, G)
    return (pad_start, block_expert.astype(jnp.int32), n_used.astype(jnp.int32), n_valid.astype(jnp.int32),
            n_blocks)


def _rotary_tables(s, d):
    inv = ROPE_BASE ** (-jnp.arange(0, d, 2, dtype=F32) / d)
    ang = jnp.arange(s, dtype=F32)[:, None] * inv[None, :]
    cos = jnp.cos(ang)
    sin = jnp.sin(ang)
    return jnp.concatenate([cos, cos], axis=-1), jnp.concatenate([-sin, sin], axis=-1)


def _layer_params(l, depth, w_in, ret_gn_g, ret_gn_b, rwkv_mu, rwkv_w0, rwkv_w_up, rwkv_a0, rwkv_a_up,
                  rwkv_g_up, rwkv_k_k, rwkv_k_a, rwkv_r_k, rwkv_gn_g, rwkv_gn_b, w_out, ln1_g, ln1_b,
                  router_w, router_bias, exp_w_gate, exp_w_up, exp_w_down, sh_w_gate, sh_w_up,
                  sh_w_down, ln2_g, ln2_b):
    ret_w = ret_gn_g.shape[-1]
    W = rwkv_gn_g.shape[-1]
    n_heads, hd = rwkv_r_k.shape[-2:]
    rank_w = rwkv_w_up.shape[2]
    rank_a = rwkv_a_up.shape[2]
    assert rank_w * 2 == LANES and rank_a * 2 == LANES and rwkv_g_up.shape[1] == LANES
    assert hd * 2 == LANES and SCAN_CHUNK == hd
    row = lambda a: a.reshape(1, -1).astype(F32)
    zw = jnp.zeros((rank_w, W), F32)
    za = jnp.zeros((rank_a, W), F32)
    head_id = jnp.arange(W) // hd
    same_head = (head_id[:, None] == head_id[None, :])
    wi = w_in[l]
    return dict(
        alpha=float((2 * depth) ** 0.25),
        rwkv_width=W,
        w_ret=wi[:, :4 * ret_w].astype(BF16),
        w_rwkv=wi[:, 4 * ret_w:].astype(BF16),
        ret_gn_g=row(ret_gn_g[l]), ret_gn_b=row(ret_gn_b[l]),
        logg=jnp.broadcast_to(
            jnp.log1p(-jnp.exp2(-5.0 - jnp.arange(RET_HEADS, dtype=F32)))[:, None, None],
            (RET_HEADS, 1, LANES)),
        mu=row(rwkv_mu[l]),
        wup_pad=jnp.stack([jnp.concatenate([rwkv_w_up[l, 0], zw], 0),
                           jnp.concatenate([zw, rwkv_w_up[l, 1]], 0)]).astype(BF16),
        aup_pad=jnp.stack([jnp.concatenate([rwkv_a_up[l, 0], za], 0),
                           jnp.concatenate([za, rwkv_a_up[l, 1]], 0)]).astype(BF16),
        w0=rwkv_w0[l].astype(F32), a0=rwkv_a0[l].astype(F32),
        gup=rwkv_g_up[l].astype(BF16),
        k_k=row(rwkv_k_k[l]), k_a=row(rwkv_k_a[l]), r_k=row(rwkv_r_k[l]),
        head_ones=same_head.astype(BF16),
        head_avg=(same_head.astype(F32) / hd).astype(BF16),
        gn_g=row(rwkv_gn_g[l]), gn_b=row(rwkv_gn_b[l]),
        wo_ret=w_out[l, :ret_w].astype(BF16), wo_rwkv=w_out[l, ret_w:].astype(BF16),
        ln1_g=row(ln1_g[l]), ln1_b=row(ln1_b[l]),
        router_wt=router_w[l].T.astype(BF16), router_b=router_bias[l].reshape(-1, 1).astype(F32),
        exp_gate=exp_w_gate[l], exp_up=exp_w_up[l], exp_down=exp_w_down[l],
        sh_gate=sh_w_gate[l].astype(BF16), sh_up=sh_w_up[l].astype(BF16), sh_down=sh_w_down[l].astype(BF16),
        ln2_g=row(ln2_g[l]), ln2_b=row(ln2_b[l]),
    )


def _pick(n, pref):
    t = min(n, pref)
    while n % t:
        t //= 2
    return t


def _layer(x, p):
    b, s, D = x.shape
    T = b * s
    x2d = x.reshape(T, D)
    tm = _pick(T, 256)
    z_ret, z_rwkv = _in_proj(x2d, p["w_ret"], p["w_rwkv"], tm)
    cos, sin = _rotary_tables(s, RET_CHUNK)
    ret_out = _retention(z_ret.reshape(b, s, -1), cos, sin, p["logg"], p["ret_gn_g"], p["ret_gn_b"])
    r, v, kk, lw, kd, kka, bonus, g = _rwkv_prep(z_rwkv.reshape(b, s, -1), p, _pick(s, 256))
    y_f, y_b = _rwkv_scan(r, kk, v, lw, kd, kka, _pick(s, 512))
    W = p["rwkv_width"]
    x1, xp, idx_t, gates, cnt = _mix_out(y_f.reshape(T, W), y_b.reshape(T, W), bonus.reshape(T, W),
                                         g.reshape(T, W), ret_out.reshape(T, -1), x2d, p, tm)
    counts = cnt[:, 0].astype(jnp.int32)
    pad_start, block_expert, n_used, n_valid, n_blocks = _segment_layout(counts, T)
    dest = _slot_plan(idx_t, pad_start, _pick(T, 512))
    xs = _sc_dispatch(xp, dest, n_blocks * MOE_BLOCK)
    ys = _moe_ffn(xs, block_expert, n_used, n_valid, p["exp_gate"], p["exp_up"], p["exp_down"])
    yg = _sc_gather(ys, dest).reshape(TOP_K, T, D // 2)
    out = _combine(yg, x1, gates.T, p, _pick(T, 256))
    return out.reshape(b, s, D)


def kernel(x_prompt, x_sample, w_in, ret_gn_g, ret_gn_b, rwkv_mu, rwkv_w0, rwkv_w_up, rwkv_a0, rwkv_a_up,
           rwkv_g_up, rwkv_k_k, rwkv_k_a, rwkv_r_k, rwkv_gn_g, rwkv_gn_b, w_out, ln1_g, ln1_b, router_w,
           router_bias, exp_w_gate, exp_w_up, exp_w_down, sh_w_gate, sh_w_up, sh_w_down, ln2_g, ln2_b):
    weights = (w_in, ret_gn_g, ret_gn_b, rwkv_mu, rwkv_w0, rwkv_w_up, rwkv_a0, rwkv_a_up, rwkv_g_up,
               rwkv_k_k, rwkv_k_a, rwkv_r_k, rwkv_gn_g, rwkv_gn_b, w_out, ln1_g, ln1_b, router_w,
               router_bias, exp_w_gate, exp_w_up, exp_w_down, sh_w_gate, sh_w_up, sh_w_down, ln2_g, ln2_b)
    depth = w_in.shape[0]
    layers = [_layer_params(l, depth, *weights) for l in range(depth)]

    def trunk(x):
        for p in layers:
            x = _layer(x, p)
        return x

    return trunk(x_prompt), trunk(x_sample)
```

```python
import functools
import math

import jax
import jax.numpy as jnp
from jax import lax
from jax.experimental import pallas as pl
from jax.experimental.pallas import tpu as pltpu
from jax.experimental.pallas import tpu_sc as plsc

F32 = jnp.float32
BF16 = jnp.bfloat16

RET_HEADS = 4
RET_CHUNK = 128
ROPE_BASE = 10000.0
TOP_K = 8
N_GROUPS = 8
TOPK_GROUPS = 4
ROUTE_SCALE = 2.5
MOE_BLOCK = 256
LN_EPS = 1e-5
GN_EPS = 1e-5
RWKV_GN_EPS = 64e-5

LANES = 128
SUBLANES = 8
VMEM_LIMIT_BYTES = 56 * 1024 * 1024

SCAN_CHUNK = 64


def _cparams(semantics):
    return pltpu.CompilerParams(dimension_semantics=semantics, vmem_limit_bytes=VMEM_LIMIT_BYTES)


def _dot(a, b):
    return jnp.dot(a.astype(BF16), b.astype(BF16), preferred_element_type=F32)


def _dot_nt(a, b):
    return lax.dot_general(a.astype(BF16), b.astype(BF16), (((1,), (1,)), ((), ())),
                           preferred_element_type=F32)


def _split3(x):
    hi = x.astype(BF16)
    r1 = x - hi.astype(F32)
    mid = r1.astype(BF16)
    lo = (r1 - mid.astype(F32)).astype(BF16)
    return hi, mid, lo


def _dot_exact_rhs(x, w_bf16):
    hi, mid, lo = _split3(x)
    out = jnp.dot(hi, w_bf16, preferred_element_type=F32)
    out += jnp.dot(mid, w_bf16, preferred_element_type=F32)
    out += jnp.dot(lo, w_bf16, preferred_element_type=F32)
    return out


def _dot_exact_lhs(w_bf16, x):
    hi, mid, lo = _split3(x)
    out = jnp.dot(w_bf16, hi, preferred_element_type=F32)
    out += jnp.dot(w_bf16, mid, preferred_element_type=F32)
    out += jnp.dot(w_bf16, lo, preferred_element_type=F32)
    return out


def _sigmoid(x):
    return 1.0 / (1.0 + jnp.exp(-x))


def _silu(x):
    return x * _sigmoid(x)


def _layer_norm(h, g, b):
    mu = jnp.mean(h, axis=-1, keepdims=True)
    d = h - mu
    var = jnp.mean(d * d, axis=-1, keepdims=True)
    return d * lax.rsqrt(var + LN_EPS) * g + b


def _inproj_body(x_ref, wr_ref, ww_ref, zr_ref, zw_ref):
    xb = x_ref[...].astype(BF16)
    zr_ref[...] = jnp.dot(xb, wr_ref[...], preferred_element_type=F32)
    zw_ref[...] = jnp.dot(xb, ww_ref[...], preferred_element_type=F32)


def _in_proj(x2d, w_ret, w_rwkv, tm):
    T, D = x2d.shape
    nr, nw = w_ret.shape[1], w_rwkv.shape[1]
    return pl.pallas_call(
        _inproj_body,
        grid=(T // tm,),
        in_specs=[pl.BlockSpec((tm, D), lambda i: (i, 0)),
                  pl.BlockSpec((D, nr), lambda i: (0, 0)),
                  pl.BlockSpec((D, nw), lambda i: (0, 0))],
        out_specs=[pl.BlockSpec((tm, nr), lambda i: (i, 0)),
                   pl.BlockSpec((tm, nw), lambda i: (i, 0))],
        out_shape=[jax.ShapeDtypeStruct((T, nr), F32), jax.ShapeDtypeStruct((T, nw), F32)],
        compiler_params=_cparams(("parallel",)),
        name="in_proj",
    )(x2d, w_ret, w_rwkv)


RET_GROUP = 4


def _ret_body(q_ref, k_ref, v_ref, gt_ref, cos_ref, sin_ref, lg_ref, gg_ref, gb_ref, o_ref,
              qs_ref, sf_ref, sb_ref, acc_ref, *, qscale):
    C = RET_CHUNK
    U = RET_GROUP
    s = q_ref.shape[0]
    n = s // C
    lg = lg_ref[...]
    pos = lax.broadcasted_iota(jnp.int32, (C, C), 0).astype(F32)
    col = lax.broadcasted_iota(jnp.int32, (C, C), 1).astype(F32)
    sc_q_prev = jnp.exp((pos + 1.0) * lg)
    sc_k_fwd = jnp.exp((C - 1.0 - pos) * lg)
    sc_k_bwd = jnp.exp(pos * lg)
    sc_q_next = jnp.exp((C - pos) * lg)
    g_chunk = jnp.exp(float(C) * lg)
    decay = jnp.exp(jnp.abs(pos - col) * lg)
    gg = gg_ref[...]
    gb = gb_ref[...]

    def rows_of(grp):
        return [pl.ds(pl.multiple_of((grp * U + u) * C, C), C) for u in range(U)]

    def rot(x, r):
        return x * cos_ref[r, :] + pltpu.roll(x, C // 2, 1) * sin_ref[r, :]

    def local(grp, carry):
        rows = rows_of(grp)
        q = [rot(q_ref[r, :], r) * qscale for r in rows]
        k = [rot(k_ref[r, :], r) for r in rows]
        vb = [v_ref[r, :].astype(BF16) for r in rows]
        sc = [_dot_nt(q[u], k[u]) * decay for u in range(U)]
        out = [_dot(sc[u], vb[u]) for u in range(U)]
        kf = [_dot((k[u] * sc_k_fwd).T, vb[u]) for u in range(U)]
        kb = [_dot((k[u] * sc_k_bwd).T, vb[u]) for u in range(U)]
        for u in range(U):
            qs_ref[rows[u], :] = q[u]
            acc_ref[rows[u], :] = out[u]
            sf_ref[grp * U + u] = kf[u]
            sb_ref[grp * U + u] = kb[u]
        return carry

    lax.fori_loop(0, n // U, local, 0)

    def fwd(c, S):
        kv = sf_ref[c]
        sf_ref[c] = S
        return S * g_chunk + kv

    lax.fori_loop(0, n, fwd, jnp.zeros((C, C), F32))

    def bwd(i, S):
        c = n - 1 - i
        kv = sb_ref[c]
        sb_ref[c] = S
        return S * g_chunk + kv

    lax.fori_loop(0, n, bwd, jnp.zeros((C, C), F32))

    def cross(grp, carry):
        rows = rows_of(grp)
        q = [qs_ref[r, :] for r in rows]
        y = [acc_ref[rows[u], :] + _dot(jnp.concatenate([q[u] * sc_q_prev, q[u] * sc_q_next], axis=1),
                                        jnp.concatenate([sf_ref[grp * U + u], sb_ref[grp * U + u]], axis=0))
             for u in range(U)]
        for u in range(U):
            mu = jnp.mean(y[u], axis=-1, keepdims=True)
            d = y[u] - mu
            var = jnp.mean(d * d, axis=-1, keepdims=True)
            yn = d * lax.rsqrt(var + GN_EPS) * gg + gb
            o_ref[rows[u], :] = _silu(gt_ref[rows[u], :]) * yn
        return carry

    lax.fori_loop(0, n // U, cross, 0)


def _retention(z_ret, cos, sin, logg, gn_g, gn_b):
    b, s, _ = z_ret.shape
    C = RET_CHUNK
    H = RET_HEADS
    assert C == LANES and s % (C * RET_GROUP) == 0
    n = s // C
    blk = lambda off: pl.BlockSpec((None, s, C), lambda i, h, off=off: (i, 0, off + h))
    return pl.pallas_call(
        functools.partial(_ret_body, qscale=float(C) ** -0.5),
        grid=(b, H),
        in_specs=[blk(0), blk(H), blk(2 * H), blk(3 * H),
                  pl.BlockSpec((s, C), lambda i, h: (0, 0)),
                  pl.BlockSpec((s, C), lambda i, h: (0, 0)),
                  pl.BlockSpec((None, 1, C), lambda i, h: (h, 0, 0)),
                  pl.BlockSpec((1, C), lambda i, h: (0, h)),
                  pl.BlockSpec((1, C), lambda i, h: (0, h))],
        out_specs=pl.BlockSpec((None, s, C), lambda i, h: (i, 0, h)),
        out_shape=jax.ShapeDtypeStruct((b, s, H * C), F32),
        scratch_shapes=[pltpu.VMEM((s, C), F32), pltpu.VMEM((n, C, C), F32), pltpu.VMEM((n, C, C), F32),
                        pltpu.VMEM((s, C), F32)],
        compiler_params=_cparams(("parallel", "parallel")),
        name="retention",
    )(z_ret, z_ret, z_ret, z_ret, cos, sin, logg, gn_g, gn_b)


def _prep_body(z_ref, zp_ref, zn_ref, mu_ref, wup_ref, aup_ref, w0_ref, a0_ref, gup_ref,
               kkp_ref, ka_ref, rk_ref, bd_ref,
               r_o, v_o, kk_o, lw_o, kd_o, kka_o, bonus_o, g_o, *, width):
    t = pl.program_id(1)
    nt = pl.num_programs(1)
    W = width
    z = z_ref[...]
    ts = z.shape[0]
    row = lax.broadcasted_iota(jnp.int32, (ts, 1), 0)
    prev_row = jnp.where(t > 0, zp_ref[SUBLANES - 1:SUBLANES, :], 0.0)
    next_row = jnp.where(t < nt - 1, zn_ref[0:1, :], 0.0)
    prev = jnp.where(row == 0, prev_row, pltpu.roll(z, 1, 0))
    nxt = jnp.where(row == ts - 1, next_row, pltpu.roll(z, ts - 1, 0))
    zs = z + mu_ref[...] * (0.5 * (prev + nxt) - z)

    r = zs[:, 0:W]
    kx = zs[:, W:2 * W]
    vx = zs[:, 2 * W:3 * W]
    wd = jnp.tanh(zs[:, 3 * W:3 * W + LANES])
    ad = zs[:, 3 * W + LANES:3 * W + 2 * LANES]
    gd = _sigmoid(zs[:, 3 * W + 2 * LANES:3 * W + 3 * LANES])
    bd = bd_ref[...]

    kk = kx * kkp_ref[...]
    ssq = _dot_exact_rhs(kk * kk, bd)
    kk = kk * lax.rsqrt(jnp.maximum(ssq, 1e-24))
    ka = ka_ref[...]
    ksum = None
    for d in range(2):
        pre = w0_ref[d:d + 1, :] + _dot(wd, wup_ref[d])
        lw_o[d] = -math.exp(-0.5) * _sigmoid(pre)
        a = _sigmoid(a0_ref[d:d + 1, :] + _dot(ad, aup_ref[d]))
        kd = kx * (1.0 + (a - 1.0) * ka)
        kd_o[d] = kd.astype(kd_o.dtype)
        kka_o[d] = (kk * a).astype(kka_o.dtype)
        ksum = kd if ksum is None else ksum + kd
    r_o[...] = r.astype(r_o.dtype)
    v_o[...] = vx.astype(v_o.dtype)
    kk_o[...] = kk.astype(kk_o.dtype)
    bonus_o[...] = (_dot_exact_rhs(r * ksum * rk_ref[...], bd) * vx).astype(bonus_o.dtype)
    g_o[...] = _dot(gd, gup_ref[...]).astype(g_o.dtype)


def _rwkv_prep(z_rwkv, p, ts):
    b, s, ncol = z_rwkv.shape
    W = p["rwkv_width"]
    nt = s // ts
    hb = ts // SUBLANES
    last = s // SUBLANES - 1
    full = lambda a: pl.BlockSpec(a.shape, lambda i, t, nd=a.ndim: (0,) * nd)
    out_tok = pl.BlockSpec((None, ts, W), lambda i, t: (i, t, 0))
    out_dir = pl.BlockSpec((2, None, ts, W), lambda i, t: (0, i, t, 0))
    tok_shape = jax.ShapeDtypeStruct((b, s, W), BF16)
    dir_shape = jax.ShapeDtypeStruct((2, b, s, W), BF16)
    lw_shape = jax.ShapeDtypeStruct((2, b, s, W), F32)
    consts = [p["mu"], p["wup_pad"], p["aup_pad"], p["w0"], p["a0"], p["gup"],
              p["k_k"], p["k_a"], p["r_k"], p["head_ones"]]
    return pl.pallas_call(
        functools.partial(_prep_body, width=W),
        grid=(b, nt),
        in_specs=[pl.BlockSpec((None, ts, ncol), lambda i, t: (i, t, 0)),
                  pl.BlockSpec((None, SUBLANES, ncol), lambda i, t: (i, jnp.maximum(t * hb - 1, 0), 0)),
                  pl.BlockSpec((None, SUBLANES, ncol), lambda i, t: (i, jnp.minimum((t + 1) * hb, last), 0)),
                  ] + [full(a) for a in consts],
        out_specs=[out_tok, out_tok, out_tok, out_dir, out_dir, out_dir, out_tok, out_tok],
        out_shape=[tok_shape, tok_shape, tok_shape, lw_shape, dir_shape, dir_shape, tok_shape, tok_shape],
        compiler_params=_cparams(("parallel", "parallel")),
        name="rwkv_prep",
    )(z_rwkv, z_rwkv, z_rwkv, *consts)


def _scan_body(rf_ref, kkf_ref, vf_ref, rb_ref, kkb_ref, vb_ref, lwf_ref, kdf_ref, kkaf_ref,
               lwb_ref, kdb_ref, kkab_ref, yf_ref, yb_ref, st_ref, *, n_pairs):
    L = SCAN_CHUNK
    H = 2 * L
    assert H == LANES
    tt = rf_ref.shape[0]
    nch = tt // L

    @pl.when(pl.program_id(1) == 0)
    def _():
        st_ref[...] = jnp.zeros_like(st_ref)

    r_refs, kk_refs, v_refs = (rf_ref, rb_ref), (kkf_ref, kkb_ref), (vf_ref, vb_ref)
    lw_refs, kd_refs, kka_refs = (lwf_ref, lwb_ref), (kdf_ref, kdb_ref), (kkaf_ref, kkab_ref)
    y_refs = (yf_ref, yb_ref)
    ii = lax.broadcasted_iota(jnp.int32, (H, H), 0)
    jj = lax.broadcasted_iota(jnp.int32, (H, H), 1)
    same = (ii < L) == (jj < L)
    strict = (jnp.logical_and(same, ii > jj), jnp.logical_and(same, ii < jj))
    incl = (jnp.logical_and(same, ii >= jj), jnp.logical_and(same, ii <= jj))
    eye = ii == jj
    li = lax.broadcasted_iota(jnp.int32, (L, L), 0)
    lj = lax.broadcasted_iota(jnp.int32, (L, L), 1)
    tri = (jnp.where(li >= lj, 1.0, 0.0).astype(BF16), jnp.where(li <= lj, 1.0, 0.0).astype(BF16))
    head0 = lax.broadcasted_iota(jnp.int32, (L, H), 1) < L

    def stack(x):
        return jnp.concatenate([jnp.where(head0, x, 0.0), jnp.where(head0, 0.0, x)], axis=0)

    chains = [(d, hp) for d in range(2) for hp in range(n_pairs)]
    P = range(len(chains))
    lanes = [slice(hp * H, (hp + 1) * H) for _, hp in chains]
    dirs = [d for d, _ in chains]

    def chunk(j, carry):
        rows = (pl.ds(pl.multiple_of(j * L, L), L), pl.ds(pl.multiple_of((nch - 1 - j) * L, L), L))
        ld = lambda refs, h: refs[dirs[h]][rows[dirs[h]], lanes[h]].astype(F32)
        lw = [ld(lw_refs, h) for h in P]
        cum = [_dot_exact_lhs(tri[dirs[h]], lw[h]) for h in P]
        tot = [jnp.sum(x, axis=0, keepdims=True) for x in lw]
        e_incl = [jnp.exp(a) for a in cum]
        e_excl = [jnp.exp(a - x) for a, x in zip(cum, lw)]
        e_inv = [jnp.exp(-a) for a in cum]
        e_rem = [jnp.exp(t_ - a) for t_, a in zip(tot, cum)]
        kk = [ld(kk_refs, h) for h in P]
        kka = [ld(kka_refs, h) for h in P]
        kd = [ld(kd_refs, h) for h in P]
        Kk = [stack(a * e) for a, e in zip(kk, e_excl)]
        R = [stack(ld(r_refs, h) * e_incl[h]) for h in P]
        B = [stack(a * e) for a, e in zip(kka, e_inv)]
        Kd = [stack(a * e) for a, e in zip(kd, e_inv)]
        Bh = [stack(a * e).T.astype(BF16) for a, e in zip(kka, e_rem)]
        Kh = [stack(a * e).T.astype(BF16) for a, e in zip(kd, e_rem)]
        V = [stack(ld(v_refs, h)).astype(BF16) for h in P]

        gram = [_dot_nt(jnp.concatenate([Kk[h], R[h]], axis=0), jnp.concatenate([B[h], Kd[h]], axis=0))
                for h in P]
        Np = [jnp.where(strict[dirs[h]], gram[h][:H, :H], 0.0) for h in P]
        AD = [_dot(jnp.concatenate([jnp.where(strict[dirs[h]], gram[h][:H, H:], 0.0),
                                    jnp.where(incl[dirs[h]], gram[h][H:, H:], 0.0)], axis=0), V[h]) for h in P]
        Cm = [jnp.where(incl[dirs[h]], gram[h][H:, :H], 0.0).astype(BF16) for h in P]
        X = [jnp.concatenate([Kk[h], AD[h][:H]], axis=1) for h in P]
        n_fac = L.bit_length() - 1
        for f in range(n_fac):
            Nb = [a.astype(BF16) for a in Np]
            NX = [_dot(Nb[h], X[h]) for h in P]
            if f < n_fac - 1:
                Np = [_dot(Nb[h], Nb[h]) for h in P]
            X = [x - nx if f == 0 else x + nx for x, nx in zip(X, NX)]
        Xb = [x.astype(BF16) for x in X]
        CX = [_dot(Cm[h], Xb[h]) for h in P]
        BW = [_dot(Bh[h], Xb[h]) for h in P]
        KV = [_dot(Kh[h], V[h]) for h in P]
        QM = []
        for h in P:
            Qh = R[h] - CX[h][:, :H]
            M = jnp.where(eye, jnp.exp(tot[h]), 0.0) - BW[h][:, :H]
            QM.append(_dot(jnp.concatenate([Qh, M], axis=0), st_ref[h]))
        for h in P:
            Ys = QM[h][:H] + (AD[h][H:] - CX[h][:, H:])
            st_ref[h] = QM[h][H:] + (KV[h] - BW[h][:, H:])
            y_refs[dirs[h]][rows[dirs[h]], lanes[h]] = Ys[:L] + Ys[L:]
        return carry

    lax.fori_loop(0, nch, chunk, 0)


def _rwkv_scan(r, kk, v, lw, kd, kka, tt):
    b, s, W = r.shape
    nt = s // tt
    n_pairs = W // LANES
    tok_f = pl.BlockSpec((None, tt, W), lambda i, t: (i, t, 0))
    tok_b = pl.BlockSpec((None, tt, W), lambda i, t: (i, nt - 1 - t, 0))
    dir_f = pl.BlockSpec((None, None, tt, W), lambda i, t: (0, i, t, 0))
    dir_b = pl.BlockSpec((None, None, tt, W), lambda i, t: (1, i, nt - 1 - t, 0))
    out = jax.ShapeDtypeStruct((b, s, W), F32)
    return pl.pallas_call(
        functools.partial(_scan_body, n_pairs=n_pairs),
        grid=(b, nt),
        in_specs=[tok_f, tok_f, tok_f, tok_b, tok_b, tok_b, dir_f, dir_f, dir_f, dir_b, dir_b, dir_b],
        out_specs=[tok_f, tok_b],
        out_shape=[out, out],
        scratch_shapes=[pltpu.VMEM((2 * n_pairs, LANES, LANES), F32)],
        compiler_params=_cparams(("parallel", "arbitrary")),
        name="rwkv_scan",
    )(r, kk, v, r, kk, v, lw, kd, kka, lw, kd, kka)


def _mix_body(yf_ref, yb_ref, bonus_ref, g_ref, ret_ref, x_ref, gng_ref, gnb_ref, avg_ref, wo1_ref, wo2_ref,
              l1g_ref, l1b_ref, rwt_ref, rb_ref, x1_ref, xp_ref, idx_ref, gate_ref, cnt_ref, *, alpha):
    y = yf_ref[...] + yb_ref[...]
    avg = avg_ref[...]
    mu = _dot_exact_rhs(y, avg)
    dl = y - mu
    var = _dot_exact_rhs(dl * dl, avg)
    yn = dl * lax.rsqrt(var + RWKV_GN_EPS) * gng_ref[...] + gnb_ref[...]
    rw = (yn + bonus_ref[...]) * g_ref[...]
    m = _dot(ret_ref[...], wo1_ref[...]) + _dot(rw, wo2_ref[...])
    x1 = _layer_norm(alpha * x_ref[...] + m, l1g_ref[...], l1b_ref[...])
    x1_ref[...] = x1
    xp_ref[...] = _pack_halves(x1)

    scores = _sigmoid(_dot_nt(rwt_ref[...], x1))
    E, tm = scores.shape
    GS = E // N_GROUPS
    NEG = -jnp.inf
    biased = scores + rb_ref[...]
    rowi = lax.broadcasted_iota(jnp.int32, (E, tm), 0)
    ri = lax.broadcasted_iota(jnp.int32, (GS, tm), 0)
    gs_rows = []
    for gi in range(N_GROUPS):
        blk = biased[gi * GS:(gi + 1) * GS, :]
        m1 = jnp.max(blk, axis=0, keepdims=True)
        i1 = jnp.min(jnp.where(blk == m1, ri, GS), axis=0, keepdims=True)
        m2 = jnp.max(jnp.where(ri == i1, NEG, blk), axis=0, keepdims=True)
        gs_rows.append(m1 + m2)
    cur = jnp.concatenate(gs_rows, axis=0)
    gidx = lax.broadcasted_iota(jnp.int32, (N_GROUPS, tm), 0)
    row_group = rowi // GS
    emask = jnp.zeros((E, tm), F32)
    for _ in range(TOPK_GROUPS):
        mx = jnp.max(cur, axis=0, keepdims=True)
        ix = jnp.min(jnp.where(cur == mx, gidx, N_GROUPS), axis=0, keepdims=True)
        emask = jnp.where(row_group == ix, 1.0, emask)
        cur = jnp.where(gidx == ix, NEG, cur)
    cur = jnp.where(emask > 0.5, biased, NEG)
    idxs, sels = [], []
    chosen = jnp.zeros((E, tm), F32)
    for _ in range(TOP_K):
        mx = jnp.max(cur, axis=0, keepdims=True)
        ix = jnp.min(jnp.where(cur == mx, rowi, E), axis=0, keepdims=True)
        hit = rowi == ix
        sels.append(jnp.sum(jnp.where(hit, scores, 0.0), axis=0, keepdims=True))
        idxs.append(ix)
        cur = jnp.where(hit, NEG, cur)
        chosen = jnp.where(hit, 1.0, chosen)
    sel = jnp.concatenate(sels, axis=0)
    idx_ref[...] = jnp.concatenate(idxs, axis=0)
    gate_ref[...] = sel / jnp.sum(sel, axis=0, keepdims=True) * ROUTE_SCALE

    @pl.when(pl.program_id(0) == 0)
    def _():
        cnt_ref[...] = jnp.zeros_like(cnt_ref)

    cnt_ref[...] += jnp.sum(chosen, axis=1, keepdims=True)


def _mix_out(y_f, y_b, bonus, g, ret_out, x2d, p, tm):
    T, D = x2d.shape
    W = bonus.shape[1]
    Wr = ret_out.shape[1]
    E = p["router_wt"].shape[0]
    full = lambda a: pl.BlockSpec(a.shape, lambda i, nd=a.ndim: (0,) * nd)
    consts = [p["gn_g"], p["gn_b"], p["head_avg"], p["wo_ret"], p["wo_rwkv"], p["ln1_g"], p["ln1_b"],
              p["router_wt"], p["router_b"]]
    return pl.pallas_call(
        functools.partial(_mix_body, alpha=p["alpha"]),
        grid=(T // tm,),
        in_specs=[pl.BlockSpec((tm, W), lambda i: (i, 0)),
                  pl.BlockSpec((tm, W), lambda i: (i, 0)),
                  pl.BlockSpec((tm, W), lambda i: (i, 0)),
                  pl.BlockSpec((tm, W), lambda i: (i, 0)),
                  pl.BlockSpec((tm, Wr), lambda i: (i, 0)),
                  pl.BlockSpec((tm, D), lambda i: (i, 0))] + [full(a) for a in consts],
        out_specs=[pl.BlockSpec((tm, D), lambda i: (i, 0)),
                   pl.BlockSpec((tm, D // 2), lambda i: (i, 0)),
                   pl.BlockSpec((TOP_K, tm), lambda i: (0, i)),
                   pl.BlockSpec((TOP_K, tm), lambda i: (0, i)),
                   pl.BlockSpec((E, LANES), lambda i: (0, 0))],
        out_shape=[jax.ShapeDtypeStruct((T, D), F32),
                   jax.ShapeDtypeStruct((T, D // 2), jnp.int32),
                   jax.ShapeDtypeStruct((TOP_K, T), jnp.int32),
                   jax.ShapeDtypeStruct((TOP_K, T), F32),
                   jax.ShapeDtypeStruct((E, LANES), F32)],
        compiler_params=_cparams(("arbitrary",)),
        name="mix_out",
    )(y_f, y_b, bonus, g, ret_out, x2d, *consts)


def _plan_body(idx_ref, pstart_ref, upper_ref, dest_ref, run_ref):
    @pl.when(pl.program_id(0) == 0)
    def _():
        run_ref[...] = jnp.zeros_like(run_ref)

    E = pstart_ref.shape[0]
    tm = idx_ref.shape[1]
    rowi = lax.broadcasted_iota(jnp.int32, (E, tm), 0)
    hits = [rowi == idx_ref[k:k + 1, :] for k in range(TOP_K)]
    member = jnp.zeros((E, tm), F32)
    for hit in hits:
        member = jnp.where(hit, 1.0, member)
    before = jnp.dot(member.astype(BF16), upper_ref[...], preferred_element_type=F32)
    slot = pstart_ref[...] + run_ref[:, 0:1] + before
    dest_ref[...] = jnp.concatenate(
        [jnp.sum(jnp.where(hit, slot, 0.0), axis=0, keepdims=True) for hit in hits], axis=0).astype(jnp.int32)
    run_ref[...] += jnp.sum(member, axis=1, keepdims=True)


def _slot_plan(idx_t, pad_start, tm):
    T = idx_t.shape[1]
    E = pad_start.shape[0]
    upper = (jnp.arange(tm)[:, None] < jnp.arange(tm)[None, :]).astype(BF16)
    return pl.pallas_call(
        _plan_body,
        grid=(T // tm,),
        in_specs=[pl.BlockSpec((TOP_K, tm), lambda i: (0, i)),
                  pl.BlockSpec((E, 1), lambda i: (0, 0)),
                  pl.BlockSpec((tm, tm), lambda i: (0, 0))],
        out_specs=pl.BlockSpec((TOP_K, tm), lambda i: (0, i)),
        out_shape=jax.ShapeDtypeStruct((TOP_K, T), jnp.int32),
        scratch_shapes=[pltpu.VMEM((E, LANES), F32)],
        compiler_params=_cparams(("arbitrary",)),
        name="slot_plan",
    )(idx_t, pad_start.reshape(E, 1).astype(F32), upper)


_UPPER_HALF = -65536


def _bf16_bits(x):
    return lax.bitcast_convert_type(x.astype(BF16).astype(F32), jnp.int32)


def _pack_halves(x):
    c = x.shape[1] // 2
    return _bf16_bits(x[:, :c]) | (jnp.right_shift(_bf16_bits(x[:, c:]), 16) & 0xFFFF)


def _unpack_halves(u):
    hi = lax.bitcast_convert_type(u & _UPPER_HALF, F32)
    lo = lax.bitcast_convert_type(jnp.left_shift(u, 16), F32)
    return hi, lo


SC_WINDOW = 64


def _sc_dispatch(xp, dest, n_slots):
    T, C = xp.shape
    n_win = T // SC_WINDOW
    mesh = plsc.VectorSubcoreMesh(core_axis_name="core", subcore_axis_name="subcore")

    @pl.kernel(out_type=jax.ShapeDtypeStruct((n_slots, C), xp.dtype), mesh=mesh, scratch_types=[])
    def scatter_rows(x_hbm, i_hbm, o_hbm):
        def body(x_vmem, i_vmem):
            pltpu.sync_copy(x_vmem, o_hbm.at[i_vmem.at[0]])

        pltpu.emit_pipeline(
            body,
            grid=(T // SC_WINDOW, TOP_K),
            in_specs=[pl.BlockSpec((SC_WINDOW, C), lambda i, k: (i, 0)),
                      pl.BlockSpec((1, SC_WINDOW), lambda i, k: (k * n_win + i, 0))],
            out_specs=[],
            core_axis_name=("core", "subcore"),
            dimension_semantics=(pltpu.PARALLEL, pltpu.ARBITRARY),
        )(x_hbm, i_hbm)

    return scatter_rows(xp, dest.reshape(TOP_K * n_win, SC_WINDOW))


def _sc_gather(ys, dest):
    n_idx = dest.shape[0] * dest.shape[1]
    C = ys.shape[1]
    mesh = plsc.VectorSubcoreMesh(core_axis_name="core", subcore_axis_name="subcore")

    @pl.kernel(out_type=jax.ShapeDtypeStruct((n_idx, C), ys.dtype), mesh=mesh, scratch_types=[])
    def gather_rows(y_hbm, i_hbm, o_hbm):
        def body(i_vmem, o_vmem):
            pltpu.sync_copy(y_hbm.at[i_vmem.at[0]], o_vmem)

        pltpu.emit_pipeline(
            body,
            grid=(n_idx // SC_WINDOW,),
            in_specs=[pl.BlockSpec((1, SC_WINDOW), lambda i: (i, 0))],
            out_specs=[pl.BlockSpec((SC_WINDOW, C), lambda i: (i, 0))],
            core_axis_name=("core", "subcore"),
            dimension_semantics=(pltpu.PARALLEL,),
        )(i_hbm, o_hbm)

    return gather_rows(ys, dest.reshape(n_idx // SC_WINDOW, SC_WINDOW))


def _moe_body(be_ref, nu_ref, nv_ref, xs_ref, wg_ref, wu_ref, wd_ref, o_ref, wgb_ref, wub_ref, wdb_ref):
    i = pl.program_id(0)
    used = i < nu_ref[0]

    @pl.when(jnp.logical_and(used, jnp.logical_or(i == 0, be_ref[i] != be_ref[jnp.maximum(i - 1, 0)])))
    def _():
        wgb_ref[...] = wg_ref[...].astype(BF16)
        wub_ref[...] = wu_ref[...].astype(BF16)
        wdb_ref[...] = wd_ref[...].astype(BF16)

    @pl.when(used)
    def _():
        row = lax.broadcasted_iota(jnp.int32, xs_ref.shape, 0)
        x_hi, x_lo = _unpack_halves(jnp.where(row < nv_ref[i], xs_ref[...], 0))
        c = x_hi.shape[1]
        gate = _dot(x_hi, wgb_ref[:c, :]) + _dot(x_lo, wgb_ref[c:, :])
        up = _dot(x_hi, wub_ref[:c, :]) + _dot(x_lo, wub_ref[c:, :])
        o_ref[...] = _pack_halves(_dot(_silu(gate) * up, wdb_ref[...]))

    @pl.when(i >= nu_ref[0])
    def _():
        o_ref[...] = jnp.zeros_like(o_ref)


def _moe_ffn(xs, block_expert, n_used, n_valid, w_gate, w_up, w_down):
    P = xs.shape[0]
    G = MOE_BLOCK
    n_blocks = P // G
    E, D, De = w_gate.shape
    blk = lambda i, nu: jnp.minimum(i, nu[0] - 1)
    grid_spec = pltpu.PrefetchScalarGridSpec(
        num_scalar_prefetch=3,
        grid=(n_blocks,),
        in_specs=[pl.BlockSpec((G, D // 2), lambda i, be, nu, nv: (blk(i, nu), 0)),
                  pl.BlockSpec((None, D, De), lambda i, be, nu, nv: (be[blk(i, nu)], 0, 0)),
                  pl.BlockSpec((None, D, De), lambda i, be, nu, nv: (be[blk(i, nu)], 0, 0)),
                  pl.BlockSpec((None, De, D), lambda i, be, nu, nv: (be[blk(i, nu)], 0, 0))],
        out_specs=pl.BlockSpec((G, D // 2), lambda i, be, nu, nv: (i, 0)),
        scratch_shapes=[pltpu.VMEM((D, De), BF16), pltpu.VMEM((D, De), BF16), pltpu.VMEM((De, D), BF16)],
    )
    return pl.pallas_call(
        _moe_body,
        grid_spec=grid_spec,
        out_shape=jax.ShapeDtypeStruct((P, D // 2), jnp.int32),
        compiler_params=_cparams(("arbitrary",)),
        name="moe_ffn",
    )(block_expert, n_used, n_valid, xs, w_gate, w_up, w_down)


def _comb_body(yg_ref, x1_ref, gt_ref, sg_ref, su_ref, sd_ref, l2g_ref, l2b_ref, o_ref, *, alpha):
    x1 = x1_ref[...]
    xb = x1.astype(BF16)
    shared = _dot(_silu(_dot(xb, sg_ref[...])) * _dot(xb, su_ref[...]), sd_ref[...])
    gt = gt_ref[...]
    acc_hi = acc_lo = None
    for k in range(TOP_K):
        hi, lo = _unpack_halves(yg_ref[k])
        gk = gt[:, k:k + 1]
        acc_hi = gk * hi if acc_hi is None else acc_hi + gk * hi
        acc_lo = gk * lo if acc_lo is None else acc_lo + gk * lo
    routed = jnp.concatenate([acc_hi, acc_lo], axis=1)
    o_ref[...] = _layer_norm(alpha * x1 + (routed + shared), l2g_ref[...], l2b_ref[...])


def _combine(yg, x1, gates_t, p, tm):
    T, D = x1.shape
    full = lambda a: pl.BlockSpec(a.shape, lambda i, nd=a.ndim: (0,) * nd)
    consts = [p["sh_gate"], p["sh_up"], p["sh_down"], p["ln2_g"], p["ln2_b"]]
    return pl.pallas_call(
        functools.partial(_comb_body, alpha=p["alpha"]),
        grid=(T // tm,),
        in_specs=[pl.BlockSpec((TOP_K, tm, D // 2), lambda i: (0, i, 0)),
                  pl.BlockSpec((tm, D), lambda i: (i, 0)),
                  pl.BlockSpec((tm, TOP_K), lambda i: (i, 0))] + [full(a) for a in consts],
        out_specs=pl.BlockSpec((tm, D), lambda i: (i, 0)),
        out_shape=jax.ShapeDtypeStruct((T, D), F32),
        compiler_params=_cparams(("parallel",)),
        name="combine",
    )(yg, x1, gates_t, *consts)


def _segment_layout(counts, n_tokens):
    G = MOE_BLOCK
    E = counts.shape[0]
    n_blocks = -(-(n_tokens * TOP_K + E * (G - 1)) // G)
    padded = (counts + G - 1) // G * G
    pad_end = jnp.cumsum(padded)
    pad_start = pad_end - padded
    first_slot = jnp.arange(n_blocks, dtype=jnp.int32) * G
    block_expert = jnp.minimum(jnp.sum(first_slot[:, None] >= pad_end[None, :], axis=1), E - 1)
    n_used = pad_end[-1:] // G
    n_valid = jnp.clip((pad_start + counts)[block_expert] - first_slot, 0, G)
    return (pad_start, block_expert.astype(jnp.int32), n_used.astype(jnp.int32), n_valid.astype(jnp.int32),
            n_blocks)


def _rotary_tables(s, d):
    inv = ROPE_BASE ** (-jnp.arange(0, d, 2, dtype=F32) / d)
    ang = jnp.arange(s, dtype=F32)[:, None] * inv[None, :]
    cos = jnp.cos(ang)
    sin = jnp.sin(ang)
    return jnp.concatenate([cos, cos], axis=-1), jnp.concatenate([-sin, sin], axis=-1)


def _layer_params(l, depth, w_in, ret_gn_g, ret_gn_b, rwkv_mu, rwkv_w0, rwkv_w_up, rwkv_a0, rwkv_a_up,
                  rwkv_g_up, rwkv_k_k, rwkv_k_a, rwkv_r_k, rwkv_gn_g, rwkv_gn_b, w_out, ln1_g, ln1_b,
                  router_w, router_bias, exp_w_gate, exp_w_up, exp_w_down, sh_w_gate, sh_w_up,
                  sh_w_down, ln2_g, ln2_b):
    ret_w = ret_gn_g.shape[-1]
    W = rwkv_gn_g.shape[-1]
    n_heads, hd = rwkv_r_k.shape[-2:]
    rank_w = rwkv_w_up.shape[2]
    rank_a = rwkv_a_up.shape[2]
    assert rank_w * 2 == LANES and rank_a * 2 == LANES and rwkv_g_up.shape[1] == LANES
    assert hd * 2 == LANES and SCAN_CHUNK == hd
    row = lambda a: a.reshape(1, -1).astype(F32)
    zw = jnp.zeros((rank_w, W), F32)
    za = jnp.zeros((rank_a, W), F32)
    head_id = jnp.arange(W) // hd
    same_head = (head_id[:, None] == head_id[None, :])
    wi = w_in[l]
    return dict(
        alpha=float((2 * depth) ** 0.25),
        rwkv_width=W,
        w_ret=wi[:, :4 * ret_w].astype(BF16),
        w_rwkv=wi[:, 4 * ret_w:].astype(BF16),
        ret_gn_g=row(ret_gn_g[l]), ret_gn_b=row(ret_gn_b[l]),
        logg=jnp.broadcast_to(
            jnp.log1p(-jnp.exp2(-5.0 - jnp.arange(RET_HEADS, dtype=F32)))[:, None, None],
            (RET_HEADS, 1, LANES)),
        mu=row(rwkv_mu[l]),
        wup_pad=jnp.stack([jnp.concatenate([rwkv_w_up[l, 0], zw], 0),
                           jnp.concatenate([zw, rwkv_w_up[l, 1]], 0)]).astype(BF16),
        aup_pad=jnp.stack([jnp.concatenate([rwkv_a_up[l, 0], za], 0),
                           jnp.concatenate([za, rwkv_a_up[l, 1]], 0)]).astype(BF16),
        w0=rwkv_w0[l].astype(F32), a0=rwkv_a0[l].astype(F32),
        gup=rwkv_g_up[l].astype(BF16),
        k_k=row(rwkv_k_k[l]), k_a=row(rwkv_k_a[l]), r_k=row(rwkv_r_k[l]),
        head_ones=same_head.astype(BF16),
        head_avg=(same_head.astype(F32) / hd).astype(BF16),
        gn_g=row(rwkv_gn_g[l]), gn_b=row(rwkv_gn_b[l]),
        wo_ret=w_out[l, :ret_w].astype(BF16), wo_rwkv=w_out[l, ret_w:].astype(BF16),
        ln1_g=row(ln1_g[l]), ln1_b=row(ln1_b[l]),
        router_wt=router_w[l].T.astype(BF16), router_b=router_bias[l].reshape(-1, 1).astype(F32),
        exp_gate=exp_w_gate[l], exp_up=exp_w_up[l], exp_down=exp_w_down[l],
        sh_gate=sh_w_gate[l].astype(BF16), sh_up=sh_w_up[l].astype(BF16), sh_down=sh_w_down[l].astype(BF16),
        ln2_g=row(ln2_g[l]), ln2_b=row(ln2_b[l]),
    )


def _pick(n, pref):
    t = min(n, pref)
    while n % t:
        t //= 2
    return t


def _layer(x, p):
    b, s, D = x.shape
    T = b * s
    x2d = x.reshape(T, D)
    tm = _pick(T, 256)
    z_ret, z_rwkv = _in_proj(x2d, p["w_ret"], p["w_rwkv"], tm)
    cos, sin = _rotary_tables(s, RET_CHUNK)
    ret_out = _retention(z_ret.reshape(b, s, -1), cos, sin, p["logg"], p["ret_gn_g"], p["ret_gn_b"])
    r, v, kk, lw, kd, kka, bonus, g = _rwkv_prep(z_rwkv.reshape(b, s, -1), p, _pick(s, 256))
    y_f, y_b = _rwkv_scan(r, kk, v, lw, kd, kka, _pick(s, 512))
    W = p["rwkv_width"]
    x1, xp, idx_t, gates, cnt = _mix_out(y_f.reshape(T, W), y_b.reshape(T, W), bonus.reshape(T, W),
                                         g.reshape(T, W), ret_out.reshape(T, -1), x2d, p, tm)
    counts = cnt[:, 0].astype(jnp.int32)
    pad_start, block_expert, n_used, n_valid, n_blocks = _segment_layout(counts, T)
    dest = _slot_plan(idx_t, pad_start, _pick(T, 512))
    xs = _sc_dispatch(xp, dest, n_blocks * MOE_BLOCK)
    ys = _moe_ffn(xs, block_expert, n_used, n_valid, p["exp_gate"], p["exp_up"], p["exp_down"])
    yg = _sc_gather(ys, dest).reshape(TOP_K, T, D // 2)
    out = _combine(yg, x1, gates.T, p, _pick(T, 256))
    return out.reshape(b, s, D)


def kernel(x_prompt, x_sample, w_in, ret_gn_g, ret_gn_b, rwkv_mu, rwkv_w0, rwkv_w_up, rwkv_a0, rwkv_a_up,
           rwkv_g_up, rwkv_k_k, rwkv_k_a, rwkv_r_k, rwkv_gn_g, rwkv_gn_b, w_out, ln1_g, ln1_b, router_w,
           router_bias, exp_w_gate, exp_w_up, exp_w_down, sh_w_gate, sh_w_up, sh_w_down, ln2_g, ln2_b):
    weights = (w_in, ret_gn_g, ret_gn_b, rwkv_mu, rwkv_w0, rwkv_w_up, rwkv_a0, rwkv_a_up, rwkv_g_up,
               rwkv_k_k, rwkv_k_a, rwkv_r_k, rwkv_gn_g, rwkv_gn_b, w_out, ln1_g, ln1_b, router_w,
               router_bias, exp_w_gate, exp_w_up, exp_w_down, sh_w_gate, sh_w_up, sh_w_down, ln2_g, ln2_b)
    depth = w_in.shape[0]
    layers = [_layer_params(l, depth, *weights) for l in range(depth)]

    def trunk(x):
        for p in layers:
            x = _layer(x, p)
        return x

    return trunk(x_prompt), trunk(x_sample)
```

```python
import functools
import math

import jax
import jax.numpy as jnp
from jax import lax
from jax.experimental import pallas as pl
from jax.experimental.pallas import tpu as pltpu
from jax.experimental.pallas import tpu_sc as plsc

F32 = jnp.float32
BF16 = jnp.bfloat16

RET_HEADS = 4
RET_CHUNK = 128
ROPE_BASE = 10000.0
TOP_K = 8
N_GROUPS = 8
TOPK_GROUPS = 4
ROUTE_SCALE = 2.5
MOE_BLOCK = 256
LN_EPS = 1e-5
GN_EPS = 1e-5
RWKV_GN_EPS = 64e-5

LANES = 128
SUBLANES = 8
VMEM_LIMIT_BYTES = 56 * 1024 * 1024

SCAN_CHUNK = 64


def _cparams(semantics):
    return pltpu.CompilerParams(dimension_semantics=semantics, vmem_limit_bytes=VMEM_LIMIT_BYTES)


def _dot(a, b):
    return jnp.dot(a.astype(BF16), b.astype(BF16), preferred_element_type=F32)


def _dot_nt(a, b):
    return lax.dot_general(a.astype(BF16), b.astype(BF16), (((1,), (1,)), ((), ())),
                           preferred_element_type=F32)


def _split3(x):
    hi = x.astype(BF16)
    r1 = x - hi.astype(F32)
    mid = r1.astype(BF16)
    lo = (r1 - mid.astype(F32)).astype(BF16)
    return hi, mid, lo


def _dot_exact_rhs(x, w_bf16):
    hi, mid, lo = _split3(x)
    out = jnp.dot(hi, w_bf16, preferred_element_type=F32)
    out += jnp.dot(mid, w_bf16, preferred_element_type=F32)
    out += jnp.dot(lo, w_bf16, preferred_element_type=F32)
    return out


def _dot_exact_lhs(w_bf16, x):
    hi, mid, lo = _split3(x)
    out = jnp.dot(w_bf16, hi, preferred_element_type=F32)
    out += jnp.dot(w_bf16, mid, preferred_element_type=F32)
    out += jnp.dot(w_bf16, lo, preferred_element_type=F32)
    return out


def _sigmoid(x):
    return 1.0 / (1.0 + jnp.exp(-x))


def _silu(x):
    return x * _sigmoid(x)


def _layer_norm(h, g, b):
    mu = jnp.mean(h, axis=-1, keepdims=True)
    d = h - mu
    var = jnp.mean(d * d, axis=-1, keepdims=True)
    return d * lax.rsqrt(var + LN_EPS) * g + b


def _inproj_body(x_ref, wr_ref, ww_ref, zr_ref, zw_ref):
    xb = x_ref[...].astype(BF16)
    zr_ref[...] = jnp.dot(xb, wr_ref[...], preferred_element_type=F32)
    zw_ref[...] = jnp.dot(xb, ww_ref[...], preferred_element_type=F32)


def _in_proj(x2d, w_ret, w_rwkv, tm):
    T, D = x2d.shape
    nr, nw = w_ret.shape[1], w_rwkv.shape[1]
    return pl.pallas_call(
        _inproj_body,
        grid=(T // tm,),
        in_specs=[pl.BlockSpec((tm, D), lambda i: (i, 0)),
                  pl.BlockSpec((D, nr), lambda i: (0, 0)),
                  pl.BlockSpec((D, nw), lambda i: (0, 0))],
        out_specs=[pl.BlockSpec((tm, nr), lambda i: (i, 0)),
                   pl.BlockSpec((tm, nw), lambda i: (i, 0))],
        out_shape=[jax.ShapeDtypeStruct((T, nr), F32), jax.ShapeDtypeStruct((T, nw), F32)],
        compiler_params=_cparams(("parallel",)),
        name="in_proj",
    )(x2d, w_ret, w_rwkv)


RET_GROUP = 4


def _ret_body(q_ref, k_ref, v_ref, gt_ref, cos_ref, sin_ref, lg_ref, gg_ref, gb_ref, o_ref,
              qs_ref, sf_ref, sb_ref, acc_ref, *, qscale):
    C = RET_CHUNK
    U = RET_GROUP
    s = q_ref.shape[0]
    n = s // C
    lg = lg_ref[...]
    pos = lax.broadcasted_iota(jnp.int32, (C, C), 0).astype(F32)
    col = lax.broadcasted_iota(jnp.int32, (C, C), 1).astype(F32)
    sc_q_prev = jnp.exp((pos + 1.0) * lg)
    sc_k_fwd = jnp.exp((C - 1.0 - pos) * lg)
    sc_k_bwd = jnp.exp(pos * lg)
    sc_q_next = jnp.exp((C - pos) * lg)
    g_chunk = jnp.exp(float(C) * lg)
    decay = jnp.exp(jnp.abs(pos - col) * lg)
    gg = gg_ref[...]
    gb = gb_ref[...]

    def rows_of(grp):
        return [pl.ds(pl.multiple_of((grp * U + u) * C, C), C) for u in range(U)]

    def rot(x, r):
        return x * cos_ref[r, :] + pltpu.roll(x, C // 2, 1) * sin_ref[r, :]

    def local(grp, carry):
        rows = rows_of(grp)
        q = [rot(q_ref[r, :], r) * qscale for r in rows]
        k = [rot(k_ref[r, :], r) for r in rows]
        vb = [v_ref[r, :].astype(BF16) for r in rows]
        sc = [_dot_nt(q[u], k[u]) * decay for u in range(U)]
        out = [_dot(sc[u], vb[u]) for u in range(U)]
        kf = [_dot((k[u] * sc_k_fwd).T, vb[u]) for u in range(U)]
        kb = [_dot((k[u] * sc_k_bwd).T, vb[u]) for u in range(U)]
        for u in range(U):
            qs_ref[rows[u], :] = q[u]
            acc_ref[rows[u], :] = out[u]
            sf_ref[grp * U + u] = kf[u]
            sb_ref[grp * U + u] = kb[u]
        return carry

    lax.fori_loop(0, n // U, local, 0)

    def fwd(c, S):
        kv = sf_ref[c]
        sf_ref[c] = S
        return S * g_chunk + kv

    lax.fori_loop(0, n, fwd, jnp.zeros((C, C), F32))

    def bwd(i, S):
        c = n - 1 - i
        kv = sb_ref[c]
        sb_ref[c] = S
        return S * g_chunk + kv

    lax.fori_loop(0, n, bwd, jnp.zeros((C, C), F32))

    def cross(grp, carry):
        rows = rows_of(grp)
        q = [qs_ref[r, :] for r in rows]
        y = [acc_ref[rows[u], :] + _dot(jnp.concatenate([q[u] * sc_q_prev, q[u] * sc_q_next], axis=1),
                                        jnp.concatenate([sf_ref[grp * U + u], sb_ref[grp * U + u]], axis=0))
             for u in range(U)]
        for u in range(U):
            mu = jnp.mean(y[u], axis=-1, keepdims=True)
            d = y[u] - mu
            var = jnp.mean(d * d, axis=-1, keepdims=True)
            yn = d * lax.rsqrt(var + GN_EPS) * gg + gb
            o_ref[rows[u], :] = _silu(gt_ref[rows[u], :]) * yn
        return carry

    lax.fori_loop(0, n // U, cross, 0)


def _retention(z_ret, cos, sin, logg, gn_g, gn_b):
    b, s, _ = z_ret.shape
    C = RET_CHUNK
    H = RET_HEADS
    assert C == LANES and s % (C * RET_GROUP) == 0
    n = s // C
    blk = lambda off: pl.BlockSpec((None, s, C), lambda i, h, off=off: (i, 0, off + h))
    return pl.pallas_call(
        functools.partial(_ret_body, qscale=float(C) ** -0.5),
        grid=(b, H),
        in_specs=[blk(0), blk(H), blk(2 * H), blk(3 * H),
                  pl.BlockSpec((s, C), lambda i, h: (0, 0)),
                  pl.BlockSpec((s, C), lambda i, h: (0, 0)),
                  pl.BlockSpec((None, 1, C), lambda i, h: (h, 0, 0)),
                  pl.BlockSpec((1, C), lambda i, h: (0, h)),
                  pl.BlockSpec((1, C), lambda i, h: (0, h))],
        out_specs=pl.BlockSpec((None, s, C), lambda i, h: (i, 0, h)),
        out_shape=jax.ShapeDtypeStruct((b, s, H * C), F32),
        scratch_shapes=[pltpu.VMEM((s, C), F32), pltpu.VMEM((n, C, C), F32), pltpu.VMEM((n, C, C), F32),
                        pltpu.VMEM((s, C), F32)],
        compiler_params=_cparams(("parallel", "parallel")),
        name="retention",
    )(z_ret, z_ret, z_ret, z_ret, cos, sin, logg, gn_g, gn_b)


def _prep_body(z_ref, zp_ref, zn_ref, mu_ref, wup_ref, aup_ref, w0_ref, a0_ref, gup_ref,
               kkp_ref, ka_ref, rk_ref, bd_ref,
               r_o, v_o, kk_o, lw_o, kd_o, kka_o, bonus_o, g_o, *, width):
    t = pl.program_id(1)
    nt = pl.num_programs(1)
    W = width
    z = z_ref[...]
    ts = z.shape[0]
    row = lax.broadcasted_iota(jnp.int32, (ts, 1), 0)
    prev_row = jnp.where(t > 0, zp_ref[SUBLANES - 1:SUBLANES, :], 0.0)
    next_row = jnp.where(t < nt - 1, zn_ref[0:1, :], 0.0)
    prev = jnp.where(row == 0, prev_row, pltpu.roll(z, 1, 0))
    nxt = jnp.where(row == ts - 1, next_row, pltpu.roll(z, ts - 1, 0))
    zs = z + mu_ref[...] * (0.5 * (prev + nxt) - z)

    r = zs[:, 0:W]
    kx = zs[:, W:2 * W]
    vx = zs[:, 2 * W:3 * W]
    wd = jnp.tanh(zs[:, 3 * W:3 * W + LANES])
    ad = zs[:, 3 * W + LANES:3 * W + 2 * LANES]
    gd = _sigmoid(zs[:, 3 * W + 2 * LANES:3 * W + 3 * LANES])
    bd = bd_ref[...]

    kk = kx * kkp_ref[...]
    ssq = _dot_exact_rhs(kk * kk, bd)
    kk = kk * lax.rsqrt(jnp.maximum(ssq, 1e-24))
    ka = ka_ref[...]
    ksum = None
    for d in range(2):
        pre = w0_ref[d:d + 1, :] + _dot(wd, wup_ref[d])
        lw_o[d] = -math.exp(-0.5) * _sigmoid(pre)
        a = _sigmoid(a0_ref[d:d + 1, :] + _dot(ad, aup_ref[d]))
        kd = kx * (1.0 + (a - 1.0) * ka)
        kd_o[d] = kd.astype(kd_o.dtype)
        kka_o[d] = (kk * a).astype(kka_o.dtype)
        ksum = kd if ksum is None else ksum + kd
    r_o[...] = r.astype(r_o.dtype)
    v_o[...] = vx.astype(v_o.dtype)
    kk_o[...] = kk.astype(kk_o.dtype)
    bonus_o[...] = (_dot_exact_rhs(r * ksum * rk_ref[...], bd) * vx).astype(bonus_o.dtype)
    g_o[...] = _dot(gd, gup_ref[...]).astype(g_o.dtype)


def _rwkv_prep(z_rwkv, p, ts):
    b, s, ncol = z_rwkv.shape
    W = p["rwkv_width"]
    nt = s // ts
    hb = ts // SUBLANES
    last = s // SUBLANES - 1
    full = lambda a: pl.BlockSpec(a.shape, lambda i, t, nd=a.ndim: (0,) * nd)
    out_tok = pl.BlockSpec((None, ts, W), lambda i, t: (i, t, 0))
    out_dir = pl.BlockSpec((2, None, ts, W), lambda i, t: (0, i, t, 0))
    tok_shape = jax.ShapeDtypeStruct((b, s, W), BF16)
    dir_shape = jax.ShapeDtypeStruct((2, b, s, W), BF16)
    lw_shape = jax.ShapeDtypeStruct((2, b, s, W), F32)
    consts = [p["mu"], p["wup_pad"], p["aup_pad"], p["w0"], p["a0"], p["gup"],
              p["k_k"], p["k_a"], p["r_k"], p["head_ones"]]
    return pl.pallas_call(
        functools.partial(_prep_body, width=W),
        grid=(b, nt),
        in_specs=[pl.BlockSpec((None, ts, ncol), lambda i, t: (i, t, 0)),
                  pl.BlockSpec((None, SUBLANES, ncol), lambda i, t: (i, jnp.maximum(t * hb - 1, 0), 0)),
                  pl.BlockSpec((None, SUBLANES, ncol), lambda i, t: (i, jnp.minimum((t + 1) * hb, last), 0)),
                  ] + [full(a) for a in consts],
        out_specs=[out_tok, out_tok, out_tok, out_dir, out_dir, out_dir, out_tok, out_tok],
        out_shape=[tok_shape, tok_shape, tok_shape, lw_shape, dir_shape, dir_shape, tok_shape, tok_shape],
        compiler_params=_cparams(("parallel", "parallel")),
        name="rwkv_prep",
    )(z_rwkv, z_rwkv, z_rwkv, *consts)


def _scan_body(rf_ref, kkf_ref, vf_ref, rb_ref, kkb_ref, vb_ref, lwf_ref, kdf_ref, kkaf_ref,
               lwb_ref, kdb_ref, kkab_ref, yf_ref, yb_ref, st_ref, *, n_pairs):
    L = SCAN_CHUNK
    H = 2 * L
    assert H == LANES
    tt = rf_ref.shape[0]
    nch = tt // L

    @pl.when(pl.program_id(1) == 0)
    def _():
        st_ref[...] = jnp.zeros_like(st_ref)

    r_refs, kk_refs, v_refs = (rf_ref, rb_ref), (kkf_ref, kkb_ref), (vf_ref, vb_ref)
    lw_refs, kd_refs, kka_refs = (lwf_ref, lwb_ref), (kdf_ref, kdb_ref), (kkaf_ref, kkab_ref)
    y_refs = (yf_ref, yb_ref)
    ii = lax.broadcasted_iota(jnp.int32, (H, H), 0)
    jj = lax.broadcasted_iota(jnp.int32, (H, H), 1)
    same = (ii < L) == (jj < L)
    strict = (jnp.logical_and(same, ii > jj), jnp.logical_and(same, ii < jj))
    incl = (jnp.logical_and(same, ii >= jj), jnp.logical_and(same, ii <= jj))
    eye = ii == jj
    li = lax.broadcasted_iota(jnp.int32, (L, L), 0)
    lj = lax.broadcasted_iota(jnp.int32, (L, L), 1)
    tri = (jnp.where(li >= lj, 1.0, 0.0).astype(BF16), jnp.where(li <= lj, 1.0, 0.0).astype(BF16))
    head0 = lax.broadcasted_iota(jnp.int32, (L, H), 1) < L

    def stack(x):
        return jnp.concatenate([jnp.where(head0, x, 0.0), jnp.where(head0, 0.0, x)], axis=0)

    chains = [(d, hp) for d in range(2) for hp in range(n_pairs)]
    P = range(len(chains))
    lanes = [slice(hp * H, (hp + 1) * H) for _, hp in chains]
    dirs = [d for d, _ in chains]

    def chunk(j, carry):
        rows = (pl.ds(pl.multiple_of(j * L, L), L), pl.ds(pl.multiple_of((nch - 1 - j) * L, L), L))
        ld = lambda refs, h: refs[dirs[h]][rows[dirs[h]], lanes[h]].astype(F32)
        lw = [ld(lw_refs, h) for h in P]
        cum = [_dot_exact_lhs(tri[dirs[h]], lw[h]) for h in P]
        tot = [jnp.sum(x, axis=0, keepdims=True) for x in lw]
        e_incl = [jnp.exp(a) for a in cum]
        e_excl = [jnp.exp(a - x) for a, x in zip(cum, lw)]
        e_inv = [jnp.exp(-a) for a in cum]
        e_rem = [jnp.exp(t_ - a) for t_, a in zip(tot, cum)]
        kk = [ld(kk_refs, h) for h in P]
        kka = [ld(kka_refs, h) for h in P]
        kd = [ld(kd_refs, h) for h in P]
        Kk = [stack(a * e) for a, e in zip(kk, e_excl)]
        R = [stack(ld(r_refs, h) * e_incl[h]) for h in P]
        B = [stack(a * e) for a, e in zip(kka, e_inv)]
        Kd = [stack(a * e) for a, e in zip(kd, e_inv)]
        Bh = [stack(a * e).T.astype(BF16) for a, e in zip(kka, e_rem)]
        Kh = [stack(a * e).T.astype(BF16) for a, e in zip(kd, e_rem)]
        V = [stack(ld(v_refs, h)).astype(BF16) for h in P]

        gram = [_dot_nt(jnp.concatenate([Kk[h], R[h]], axis=0), jnp.concatenate([B[h], Kd[h]], axis=0))
                for h in P]
        Np = [jnp.where(strict[dirs[h]], gram[h][:H, :H], 0.0) for h in P]
        AD = [_dot(jnp.concatenate([jnp.where(strict[dirs[h]], gram[h][:H, H:], 0.0),
                                    jnp.where(incl[dirs[h]], gram[h][H:, H:], 0.0)], axis=0), V[h]) for h in P]
        Cm = [jnp.where(incl[dirs[h]], gram[h][H:, :H], 0.0).astype(BF16) for h in P]
        X = [jnp.concatenate([Kk[h], AD[h][:H]], axis=1) for h in P]
        n_fac = L.bit_length() - 1
        for f in range(n_fac):
            Nb = [a.astype(BF16) for a in Np]
            NX = [_dot(Nb[h], X[h]) for h in P]
            if f < n_fac - 1:
                Np = [_dot(Nb[h], Nb[h]) for h in P]
            X = [x - nx if f == 0 else x + nx for x, nx in zip(X, NX)]
        Xb = [x.astype(BF16) for x in X]
        CX = [_dot(Cm[h], Xb[h]) for h in P]
        BW = [_dot(Bh[h], Xb[h]) for h in P]
        KV = [_dot(Kh[h], V[h]) for h in P]
        QM = []
        for h in P:
            Qh = R[h] - CX[h][:, :H]
            M = jnp.where(eye, jnp.exp(tot[h]), 0.0) - BW[h][:, :H]
            QM.append(_dot(jnp.concatenate([Qh, M], axis=0), st_ref[h]))
        for h in P:
            Ys = QM[h][:H] + (AD[h][H:] - CX[h][:, H:])
            st_ref[h] = QM[h][H:] + (KV[h] - BW[h][:, H:])
            y_refs[dirs[h]][rows[dirs[h]], lanes[h]] = Ys[:L] + Ys[L:]
        return carry

    lax.fori_loop(0, nch, chunk, 0)


def _rwkv_scan(r, kk, v, lw, kd, kka, tt):
    b, s, W = r.shape
    nt = s // tt
    n_pairs = W // LANES
    tok_f = pl.BlockSpec((None, tt, W), lambda i, t: (i, t, 0))
    tok_b = pl.BlockSpec((None, tt, W), lambda i, t: (i, nt - 1 - t, 0))
    dir_f = pl.BlockSpec((None, None, tt, W), lambda i, t: (0, i, t, 0))
    dir_b = pl.BlockSpec((None, None, tt, W), lambda i, t: (1, i, nt - 1 - t, 0))
    out = jax.ShapeDtypeStruct((b, s, W), F32)
    return pl.pallas_call(
        functools.partial(_scan_body, n_pairs=n_pairs),
        grid=(b, nt),
        in_specs=[tok_f, tok_f, tok_f, tok_b, tok_b, tok_b, dir_f, dir_f, dir_f, dir_b, dir_b, dir_b],
        out_specs=[tok_f, tok_b],
        out_shape=[out, out],
        scratch_shapes=[pltpu.VMEM((2 * n_pairs, LANES, LANES), F32)],
        compiler_params=_cparams(("parallel", "arbitrary")),
        name="rwkv_scan",
    )(r, kk, v, r, kk, v, lw, kd, kka, lw, kd, kka)


def _mix_body(yf_ref, yb_ref, bonus_ref, g_ref, ret_ref, x_ref, gng_ref, gnb_ref, avg_ref, wo1_ref, wo2_ref,
              l1g_ref, l1b_ref, rwt_ref, rb_ref, x1_ref, xp_ref, idx_ref, gate_ref, cnt_ref, *, alpha):
    y = yf_ref[...] + yb_ref[...]
    avg = avg_ref[...]
    mu = _dot_exact_rhs(y, avg)
    dl = y - mu
    var = _dot_exact_rhs(dl * dl, avg)
    yn = dl * lax.rsqrt(var + RWKV_GN_EPS) * gng_ref[...] + gnb_ref[...]
    rw = (yn + bonus_ref[...]) * g_ref[...]
    m = _dot(ret_ref[...], wo1_ref[...]) + _dot(rw, wo2_ref[...])
    x1 = _layer_norm(alpha * x_ref[...] + m, l1g_ref[...], l1b_ref[...])
    x1_ref[...] = x1
    xp_ref[...] = _pack_halves(x1)

    scores = _sigmoid(_dot_nt(rwt_ref[...], x1))
    E, tm = scores.shape
    GS = E // N_GROUPS
    NEG = -jnp.inf
    biased = scores + rb_ref[...]
    rowi = lax.broadcasted_iota(jnp.int32, (E, tm), 0)
    ri = lax.broadcasted_iota(jnp.int32, (GS, tm), 0)
    gs_rows = []
    for gi in range(N_GROUPS):
        blk = biased[gi * GS:(gi + 1) * GS, :]
        m1 = jnp.max(blk, axis=0, keepdims=True)
        i1 = jnp.min(jnp.where(blk == m1, ri, GS), axis=0, keepdims=True)
        m2 = jnp.max(jnp.where(ri == i1, NEG, blk), axis=0, keepdims=True)
        gs_rows.append(m1 + m2)
    cur = jnp.concatenate(gs_rows, axis=0)
    gidx = lax.broadcasted_iota(jnp.int32, (N_GROUPS, tm), 0)
    row_group = rowi // GS
    emask = jnp.zeros((E, tm), F32)
    for _ in range(TOPK_GROUPS):
        mx = jnp.max(cur, axis=0, keepdims=True)
        ix = jnp.min(jnp.where(cur == mx, gidx, N_GROUPS), axis=0, keepdims=True)
        emask = jnp.where(row_group == ix, 1.0, emask)
        cur = jnp.where(gidx == ix, NEG, cur)
    cur = jnp.where(emask > 0.5, biased, NEG)
    idxs, sels = [], []
    chosen = jnp.zeros((E, tm), F32)
    for _ in range(TOP_K):
        mx = jnp.max(cur, axis=0, keepdims=True)
        ix = jnp.min(jnp.where(cur == mx, rowi, E), axis=0, keepdims=True)
        hit = rowi == ix
        sels.append(jnp.sum(jnp.where(hit, scores, 0.0), axis=0, keepdims=True))
        idxs.append(ix)
        cur = jnp.where(hit, NEG, cur)
        chosen = jnp.where(hit, 1.0, chosen)
    sel = jnp.concatenate(sels, axis=0)
    idx_ref[...] = jnp.concatenate(idxs, axis=0)
    gate_ref[...] = sel / jnp.sum(sel, axis=0, keepdims=True) * ROUTE_SCALE

    @pl.when(pl.program_id(0) == 0)
    def _():
        cnt_ref[...] = jnp.zeros_like(cnt_ref)

    cnt_ref[...] += jnp.sum(chosen, axis=1, keepdims=True)


def _mix_out(y_f, y_b, bonus, g, ret_out, x2d, p, tm):
    T, D = x2d.shape
    W = bonus.shape[1]
    Wr = ret_out.shape[1]
    E = p["router_wt"].shape[0]
    full = lambda a: pl.BlockSpec(a.shape, lambda i, nd=a.ndim: (0,) * nd)
    consts = [p["gn_g"], p["gn_b"], p["head_avg"], p["wo_ret"], p["wo_rwkv"], p["ln1_g"], p["ln1_b"],
              p["router_wt"], p["router_b"]]
    return pl.pallas_call(
        functools.partial(_mix_body, alpha=p["alpha"]),
        grid=(T // tm,),
        in_specs=[pl.BlockSpec((tm, W), lambda i: (i, 0)),
                  pl.BlockSpec((tm, W), lambda i: (i, 0)),
                  pl.BlockSpec((tm, W), lambda i: (i, 0)),
                  pl.BlockSpec((tm, W), lambda i: (i, 0)),
                  pl.BlockSpec((tm, Wr), lambda i: (i, 0)),
                  pl.BlockSpec((tm, D), lambda i: (i, 0))] + [full(a) for a in consts],
        out_specs=[pl.BlockSpec((tm, D), lambda i: (i, 0)),
                   pl.BlockSpec((tm, D // 2), lambda i: (i, 0)),
                   pl.BlockSpec((TOP_K, tm), lambda i: (0, i)),
                   pl.BlockSpec((TOP_K, tm), lambda i: (0, i)),
                   pl.BlockSpec((E, LANES), lambda i: (0, 0))],
        out_shape=[jax.ShapeDtypeStruct((T, D), F32),
                   jax.ShapeDtypeStruct((T, D // 2), jnp.int32),
                   jax.ShapeDtypeStruct((TOP_K, T), jnp.int32),
                   jax.ShapeDtypeStruct((TOP_K, T), F32),
                   jax.ShapeDtypeStruct((E, LANES), F32)],
        compiler_params=_cparams(("arbitrary",)),
        name="mix_out",
    )(y_f, y_b, bonus, g, ret_out, x2d, *consts)


def _plan_body(idx_ref, pstart_ref, upper_ref, dest_ref, run_ref):
    @pl.when(pl.program_id(0) == 0)
    def _():
        run_ref[...] = jnp.zeros_like(run_ref)

    E = pstart_ref.shape[0]
    tm = idx_ref.shape[1]
    rowi = lax.broadcasted_iota(jnp.int32, (E, tm), 0)
    hits = [rowi == idx_ref[k:k + 1, :] for k in range(TOP_K)]
    member = jnp.zeros((E, tm), F32)
    for hit in hits:
        member = jnp.where(hit, 1.0, member)
    before = jnp.dot(member.astype(BF16), upper_ref[...], preferred_element_type=F32)
    slot = pstart_ref[...] + run_ref[:, 0:1] + before
    dest_ref[...] = jnp.concatenate(
        [jnp.sum(jnp.where(hit, slot, 0.0), axis=0, keepdims=True) for hit in hits], axis=0).astype(jnp.int32)
    run_ref[...] += jnp.sum(member, axis=1, keepdims=True)


def _slot_plan(idx_t, pad_start, tm):
    T = idx_t.shape[1]
    E = pad_start.shape[0]
    upper = (jnp.arange(tm)[:, None] < jnp.arange(tm)[None, :]).astype(BF16)
    return pl.pallas_call(
        _plan_body,
        grid=(T // tm,),
        in_specs=[pl.BlockSpec((TOP_K, tm), lambda i: (0, i)),
                  pl.BlockSpec((E, 1), lambda i: (0, 0)),
                  pl.BlockSpec((tm, tm), lambda i: (0, 0))],
        out_specs=pl.BlockSpec((TOP_K, tm), lambda i: (0, i)),
        out_shape=jax.ShapeDtypeStruct((TOP_K, T), jnp.int32),
        scratch_shapes=[pltpu.VMEM((E, LANES), F32)],
        compiler_params=_cparams(("arbitrary",)),
        name="slot_plan",
    )(idx_t, pad_start.reshape(E, 1).astype(F32), upper)


_UPPER_HALF = -65536


def _bf16_bits(x):
    return lax.bitcast_convert_type(x.astype(BF16).astype(F32), jnp.int32)


def _pack_halves(x):
    c = x.shape[1] // 2
    return _bf16_bits(x[:, :c]) | (jnp.right_shift(_bf16_bits(x[:, c:]), 16) & 0xFFFF)


def _unpack_halves(u):
    hi = lax.bitcast_convert_type(u & _UPPER_HALF, F32)
    lo = lax.bitcast_convert_type(jnp.left_shift(u, 16), F32)
    return hi, lo


SC_WINDOW = 64


def _sc_dispatch(xp, dest, n_slots):
    T, C = xp.shape
    n_win = T // SC_WINDOW
    mesh = plsc.VectorSubcoreMesh(core_axis_name="core", subcore_axis_name="subcore")

    @pl.kernel(out_type=jax.ShapeDtypeStruct((n_slots, C), xp.dtype), mesh=mesh, scratch_types=[])
    def scatter_rows(x_hbm, i_hbm, o_hbm):
        def body(x_vmem, i_vmem):
            pltpu.sync_copy(x_vmem, o_hbm.at[i_vmem.at[0]])

        pltpu.emit_pipeline(
            body,
            grid=(T // SC_WINDOW, TOP_K),
            in_specs=[pl.BlockSpec((SC_WINDOW, C), lambda i, k: (i, 0)),
                      pl.BlockSpec((1, SC_WINDOW), lambda i, k: (k * n_win + i, 0))],
            out_specs=[],
            core_axis_name=("core", "subcore"),
            dimension_semantics=(pltpu.PARALLEL, pltpu.ARBITRARY),
        )(x_hbm, i_hbm)

    return scatter_rows(xp, dest.reshape(TOP_K * n_win, SC_WINDOW))


def _sc_gather(ys, dest):
    n_idx = dest.shape[0] * dest.shape[1]
    C = ys.shape[1]
    mesh = plsc.VectorSubcoreMesh(core_axis_name="core", subcore_axis_name="subcore")

    @pl.kernel(out_type=jax.ShapeDtypeStruct((n_idx, C), ys.dtype), mesh=mesh, scratch_types=[])
    def gather_rows(y_hbm, i_hbm, o_hbm):
        def body(i_vmem, o_vmem):
            pltpu.sync_copy(y_hbm.at[i_vmem.at[0]], o_vmem)

        pltpu.emit_pipeline(
            body,
            grid=(n_idx // SC_WINDOW,),
            in_specs=[pl.BlockSpec((1, SC_WINDOW), lambda i: (i, 0))],
            out_specs=[pl.BlockSpec((SC_WINDOW, C), lambda i: (i, 0))],
            core_axis_name=("core", "subcore"),
            dimension_semantics=(pltpu.PARALLEL,),
        )(i_hbm, o_hbm)

    return gather_rows(ys, dest.reshape(n_idx // SC_WINDOW, SC_WINDOW))


MOE_IN_SLOTS = 3
MOE_OUT_SLOTS = 2


def _moe_body(nb_ref, b0_ref, cnt_ref, nu_ref, xs_hbm, wg_ref, wu_ref, wd_ref, ys_hbm,
              xbuf, obuf, wgb_ref, wub_ref, wdb_ref, in_sem, out_sem):
    G = MOE_BLOCK
    e = pl.program_id(0)
    nb = nb_ref[e]
    b0 = b0_ref[e]
    n_used = nu_ref[0]

    def in_copy(g):
        slot = lax.rem(g, MOE_IN_SLOTS)
        return pltpu.make_async_copy(xs_hbm.at[pl.ds(pl.multiple_of(g * G, G), G), :], xbuf.at[slot],
                                     in_sem.at[slot])

    def out_copy(g):
        slot = lax.rem(g, MOE_OUT_SLOTS)
        return pltpu.make_async_copy(obuf.at[slot], ys_hbm.at[pl.ds(pl.multiple_of(g * G, G), G), :],
                                     out_sem.at[slot])

    @pl.when(e == 0)
    def _():
        for g in range(MOE_IN_SLOTS - 1):
            @pl.when(g < n_used)
            def _():
                in_copy(g).start()

    @pl.when(nb > 0)
    def _():
        wgb_ref[...] = wg_ref[...].astype(BF16)
        wub_ref[...] = wu_ref[...].astype(BF16)
        wdb_ref[...] = wd_ref[...].astype(BF16)
        row = lax.broadcasted_iota(jnp.int32, (G, xbuf.shape[2]), 0)
        c = xbuf.shape[2]

        def step(j, carry):
            g = b0 + j
            in_copy(g).wait()

            @pl.when(g + (MOE_IN_SLOTS - 1) < n_used)
            def _():
                in_copy(g + (MOE_IN_SLOTS - 1)).start()

            @pl.when(g >= MOE_OUT_SLOTS)
            def _():
                out_copy(g - MOE_OUT_SLOTS).wait()

            u = jnp.where(row < cnt_ref[e] - j * G, xbuf[lax.rem(g, MOE_IN_SLOTS)], 0)
            x_hi, x_lo = _unpack_halves(u)
            gate = _dot(x_hi, wgb_ref[:c, :]) + _dot(x_lo, wgb_ref[c:, :])
            up = _dot(x_hi, wub_ref[:c, :]) + _dot(x_lo, wub_ref[c:, :])
            obuf[lax.rem(g, MOE_OUT_SLOTS)] = _pack_halves(_dot(_silu(gate) * up, wdb_ref[...]))
            out_copy(g).start()
            return carry

        lax.fori_loop(0, nb, step, 0)

    @pl.when(e == pl.num_programs(0) - 1)
    def _():
        for back in range(MOE_OUT_SLOTS, 0, -1):
            @pl.when(n_used >= back)
            def _():
                out_copy(n_used - back).wait()


def _moe_ffn(xs, blocks_per_expert, first_block, counts, n_used, w_gate, w_up, w_down):
    P, C = xs.shape
    G = MOE_BLOCK
    E, D, De = w_gate.shape
    wspec = lambda shape: pl.BlockSpec((None,) + shape, lambda e, nb, b0, cnt, nu: (e, 0, 0))
    grid_spec = pltpu.PrefetchScalarGridSpec(
        num_scalar_prefetch=4,
        grid=(E,),
        in_specs=[pl.BlockSpec(memory_space=pl.ANY), wspec((D, De)), wspec((D, De)), wspec((De, D))],
        out_specs=pl.BlockSpec(memory_space=pl.ANY),
        scratch_shapes=[pltpu.VMEM((MOE_IN_SLOTS, G, C), jnp.int32), pltpu.VMEM((MOE_OUT_SLOTS, G, C), jnp.int32),
                        pltpu.VMEM((D, De), BF16), pltpu.VMEM((D, De), BF16), pltpu.VMEM((De, D), BF16),
                        pltpu.SemaphoreType.DMA((MOE_IN_SLOTS,)), pltpu.SemaphoreType.DMA((MOE_OUT_SLOTS,))],
    )
    return pl.pallas_call(
        _moe_body,
        grid_spec=grid_spec,
        out_shape=jax.ShapeDtypeStruct((P, C), jnp.int32),
        compiler_params=_cparams(("arbitrary",)),
        name="moe_ffn",
    )(blocks_per_expert, first_block, counts, n_used, xs, w_gate, w_up, w_down)


def _comb_body(yg_ref, x1_ref, gt_ref, sg_ref, su_ref, sd_ref, l2g_ref, l2b_ref, o_ref, *, alpha):
    x1 = x1_ref[...]
    xb = x1.astype(BF16)
    shared = _dot(_silu(_dot(xb, sg_ref[...])) * _dot(xb, su_ref[...]), sd_ref[...])
    gt = gt_ref[...]
    acc_hi = acc_lo = None
    for k in range(TOP_K):
        hi, lo = _unpack_halves(yg_ref[k])
        gk = gt[:, k:k + 1]
        acc_hi = gk * hi if acc_hi is None else acc_hi + gk * hi
        acc_lo = gk * lo if acc_lo is None else acc_lo + gk * lo
    routed = jnp.concatenate([acc_hi, acc_lo], axis=1)
    o_ref[...] = _layer_norm(alpha * x1 + (routed + shared), l2g_ref[...], l2b_ref[...])


def _combine(yg, x1, gates_t, p, tm):
    T, D = x1.shape
    full = lambda a: pl.BlockSpec(a.shape, lambda i, nd=a.ndim: (0,) * nd)
    consts = [p["sh_gate"], p["sh_up"], p["sh_down"], p["ln2_g"], p["ln2_b"]]
    return pl.pallas_call(
        functools.partial(_comb_body, alpha=p["alpha"]),
        grid=(T // tm,),
        in_specs=[pl.BlockSpec((TOP_K, tm, D // 2), lambda i: (0, i, 0)),
                  pl.BlockSpec((tm, D), lambda i: (i, 0)),
                  pl.BlockSpec((tm, TOP_K), lambda i: (i, 0))] + [full(a) for a in consts],
        out_specs=pl.BlockSpec((tm, D), lambda i: (i, 0)),
        out_shape=jax.ShapeDtypeStruct((T, D), F32),
        compiler_params=_cparams(("parallel",)),
        name="combine",
    )(yg, x1, gates_t, *consts)


def _segment_layout(counts, n_tokens):
    G = MOE_BLOCK
    E = counts.shape[0]
    n_blocks = -(-(n_tokens * TOP_K + E * (G - 1)) // G)
    padded = (counts + G - 1) // G * G
    pad_end = jnp.cumsum(padded)
    pad_start = pad_end - padded
    n_used = pad_end[-1:] // G
    return (pad_start, (padded // G).astype(jnp.int32), (pad_start // G).astype(jnp.int32),
            n_used.astype(jnp.int32), n_blocks)


def _rotary_tables(s, d):
    inv = ROPE_BASE ** (-jnp.arange(0, d, 2, dtype=F32) / d)
    ang = jnp.arange(s, dtype=F32)[:, None] * inv[None, :]
    cos = jnp.cos(ang)
    sin = jnp.sin(ang)
    return jnp.concatenate([cos, cos], axis=-1), jnp.concatenate([-sin, sin], axis=-1)


def _layer_params(l, depth, w_in, ret_gn_g, ret_gn_b, rwkv_mu, rwkv_w0, rwkv_w_up, rwkv_a0, rwkv_a_up,
                  rwkv_g_up, rwkv_k_k, rwkv_k_a, rwkv_r_k, rwkv_gn_g, rwkv_gn_b, w_out, ln1_g, ln1_b,
                  router_w, router_bias, exp_w_gate, exp_w_up, exp_w_down, sh_w_gate, sh_w_up,
                  sh_w_down, ln2_g, ln2_b):
    ret_w = ret_gn_g.shape[-1]
    W = rwkv_gn_g.shape[-1]
    n_heads, hd = rwkv_r_k.shape[-2:]
    rank_w = rwkv_w_up.shape[2]
    rank_a = rwkv_a_up.shape[2]
    assert rank_w * 2 == LANES and rank_a * 2 == LANES and rwkv_g_up.shape[1] == LANES
    assert hd * 2 == LANES and SCAN_CHUNK == hd
    row = lambda a: a.reshape(1, -1).astype(F32)
    zw = jnp.zeros((rank_w, W), F32)
    za = jnp.zeros((rank_a, W), F32)
    head_id = jnp.arange(W) // hd
    same_head = (head_id[:, None] == head_id[None, :])
    wi = w_in[l]
    return dict(
        alpha=float((2 * depth) ** 0.25),
        rwkv_width=W,
        w_ret=wi[:, :4 * ret_w].astype(BF16),
        w_rwkv=wi[:, 4 * ret_w:].astype(BF16),
        ret_gn_g=row(ret_gn_g[l]), ret_gn_b=row(ret_gn_b[l]),
        logg=jnp.broadcast_to(
            jnp.log1p(-jnp.exp2(-5.0 - jnp.arange(RET_HEADS, dtype=F32)))[:, None, None],
            (RET_HEADS, 1, LANES)),
        mu=row(rwkv_mu[l]),
        wup_pad=jnp.stack([jnp.concatenate([rwkv_w_up[l, 0], zw], 0),
                           jnp.concatenate([zw, rwkv_w_up[l, 1]], 0)]).astype(BF16),
        aup_pad=jnp.stack([jnp.concatenate([rwkv_a_up[l, 0], za], 0),
                           jnp.concatenate([za, rwkv_a_up[l, 1]], 0)]).astype(BF16),
        w0=rwkv_w0[l].astype(F32), a0=rwkv_a0[l].astype(F32),
        gup=rwkv_g_up[l].astype(BF16),
        k_k=row(rwkv_k_k[l]), k_a=row(rwkv_k_a[l]), r_k=row(rwkv_r_k[l]),
        head_ones=same_head.astype(BF16),
        head_avg=(same_head.astype(F32) / hd).astype(BF16),
        gn_g=row(rwkv_gn_g[l]), gn_b=row(rwkv_gn_b[l]),
        wo_ret=w_out[l, :ret_w].astype(BF16), wo_rwkv=w_out[l, ret_w:].astype(BF16),
        ln1_g=row(ln1_g[l]), ln1_b=row(ln1_b[l]),
        router_wt=router_w[l].T.astype(BF16), router_b=router_bias[l].reshape(-1, 1).astype(F32),
        exp_gate=exp_w_gate[l], exp_up=exp_w_up[l], exp_down=exp_w_down[l],
        sh_gate=sh_w_gate[l].astype(BF16), sh_up=sh_w_up[l].astype(BF16), sh_down=sh_w_down[l].astype(BF16),
        ln2_g=row(ln2_g[l]), ln2_b=row(ln2_b[l]),
    )


def _pick(n, pref):
    t = min(n, pref)
    while n % t:
        t //= 2
    return t


def _layer(x, p):
    b, s, D = x.shape
    T = b * s
    x2d = x.reshape(T, D)
    tm = _pick(T, 256)
    z_ret, z_rwkv = _in_proj(x2d, p["w_ret"], p["w_rwkv"], tm)
    cos, sin = _rotary_tables(s, RET_CHUNK)
    ret_out = _retention(z_ret.reshape(b, s, -1), cos, sin, p["logg"], p["ret_gn_g"], p["ret_gn_b"])
    r, v, kk, lw, kd, kka, bonus, g = _rwkv_prep(z_rwkv.reshape(b, s, -1), p, _pick(s, 256))
    y_f, y_b = _rwkv_scan(r, kk, v, lw, kd, kka, _pick(s, 512))
    W = p["rwkv_width"]
    x1, xp, idx_t, gates, cnt = _mix_out(y_f.reshape(T, W), y_b.reshape(T, W), bonus.reshape(T, W),
                                         g.reshape(T, W), ret_out.reshape(T, -1), x2d, p, tm)
    counts = cnt[:, 0].astype(jnp.int32)
    pad_start, blocks_per_expert, first_block, n_used, n_blocks = _segment_layout(counts, T)
    dest = _slot_plan(idx_t, pad_start, _pick(T, 512))
    xs = _sc_dispatch(xp, dest, n_blocks * MOE_BLOCK)
    ys = _moe_ffn(xs, blocks_per_expert, first_block, counts, n_used, p["exp_gate"], p["exp_up"], p["exp_down"])
    yg = _sc_gather(ys, dest).reshape(TOP_K, T, D // 2)
    out = _combine(yg, x1, gates.T, p, _pick(T, 256))
    return out.reshape(b, s, D)


def kernel(x_prompt, x_sample, w_in, ret_gn_g, ret_gn_b, rwkv_mu, rwkv_w0, rwkv_w_up, rwkv_a0, rwkv_a_up,
           rwkv_g_up, rwkv_k_k, rwkv_k_a, rwkv_r_k, rwkv_gn_g, rwkv_gn_b, w_out, ln1_g, ln1_b, router_w,
           router_bias, exp_w_gate, exp_w_up, exp_w_down, sh_w_gate, sh_w_up, sh_w_down, ln2_g, ln2_b):
    weights = (w_in, ret_gn_g, ret_gn_b, rwkv_mu, rwkv_w0, rwkv_w_up, rwkv_a0, rwkv_a_up, rwkv_g_up,
               rwkv_k_k, rwkv_k_a, rwkv_r_k, rwkv_gn_g, rwkv_gn_b, w_out, ln1_g, ln1_b, router_w,
               router_bias, exp_w_gate, exp_w_up, exp_w_down, sh_w_gate, sh_w_up, sh_w_down, ln2_g, ln2_b)
    depth = w_in.shape[0]
    layers = [_layer_params(l, depth, *weights) for l in range(depth)]

    def trunk(x):
        for p in layers:
            x = _layer(x, p)
        return x

    return trunk(x_prompt), trunk(x_sample)
```

```python
import functools
import math

import jax
import jax.numpy as jnp
from jax import lax
from jax.experimental import pallas as pl
from jax.experimental.pallas import tpu as pltpu
from jax.experimental.pallas import tpu_sc as plsc

F32 = jnp.float32
BF16 = jnp.bfloat16

RET_HEADS = 4
RET_CHUNK = 128
ROPE_BASE = 10000.0
TOP_K = 8
N_GROUPS = 8
TOPK_GROUPS = 4
ROUTE_SCALE = 2.5
MOE_BLOCK = 256
LN_EPS = 1e-5
GN_EPS = 1e-5
RWKV_GN_EPS = 64e-5

LANES = 128
SUBLANES = 8
VMEM_LIMIT_BYTES = 56 * 1024 * 1024

SCAN_CHUNK = 64


def _cparams(semantics):
    return pltpu.CompilerParams(dimension_semantics=semantics, vmem_limit_bytes=VMEM_LIMIT_BYTES)


def _dot(a, b):
    return jnp.dot(a.astype(BF16), b.astype(BF16), preferred_element_type=F32)


def _dot_nt(a, b):
    return lax.dot_general(a.astype(BF16), b.astype(BF16), (((1,), (1,)), ((), ())),
                           preferred_element_type=F32)


def _split3(x):
    hi = x.astype(BF16)
    r1 = x - hi.astype(F32)
    mid = r1.astype(BF16)
    lo = (r1 - mid.astype(F32)).astype(BF16)
    return hi, mid, lo


def _dot_exact_rhs(x, w_bf16):
    hi, mid, lo = _split3(x)
    out = jnp.dot(hi, w_bf16, preferred_element_type=F32)
    out += jnp.dot(mid, w_bf16, preferred_element_type=F32)
    out += jnp.dot(lo, w_bf16, preferred_element_type=F32)
    return out


def _dot_exact_lhs(w_bf16, x):
    hi, mid, lo = _split3(x)
    out = jnp.dot(w_bf16, hi, preferred_element_type=F32)
    out += jnp.dot(w_bf16, mid, preferred_element_type=F32)
    out += jnp.dot(w_bf16, lo, preferred_element_type=F32)
    return out


def _sigmoid(x):
    return 1.0 / (1.0 + jnp.exp(-x))


def _silu(x):
    return x * _sigmoid(x)


def _layer_norm(h, g, b):
    mu = jnp.mean(h, axis=-1, keepdims=True)
    d = h - mu
    var = jnp.mean(d * d, axis=-1, keepdims=True)
    return d * lax.rsqrt(var + LN_EPS) * g + b


def _inproj_body(x_ref, wr_ref, ww_ref, zr_ref, zw_ref):
    xb = x_ref[...].astype(BF16)
    zr_ref[...] = jnp.dot(xb, wr_ref[...], preferred_element_type=F32).astype(zr_ref.dtype)
    zw_ref[...] = jnp.dot(xb, ww_ref[...], preferred_element_type=F32)


def _in_proj(x2d, w_ret, w_rwkv, tm):
    T, D = x2d.shape
    nr, nw = w_ret.shape[1], w_rwkv.shape[1]
    return pl.pallas_call(
        _inproj_body,
        grid=(T // tm,),
        in_specs=[pl.BlockSpec((tm, D), lambda i: (i, 0)),
                  pl.BlockSpec((D, nr), lambda i: (0, 0)),
                  pl.BlockSpec((D, nw), lambda i: (0, 0))],
        out_specs=[pl.BlockSpec((tm, nr), lambda i: (i, 0)),
                   pl.BlockSpec((tm, nw), lambda i: (i, 0))],
        out_shape=[jax.ShapeDtypeStruct((T, nr), BF16), jax.ShapeDtypeStruct((T, nw), F32)],
        compiler_params=_cparams(("parallel",)),
        name="in_proj",
    )(x2d, w_ret, w_rwkv)


RET_GROUP = 4


def _ret_body(q_ref, k_ref, v_ref, gt_ref, cos_ref, sin_ref, lg_ref, gg_ref, gb_ref, o_ref,
              qs_ref, sf_ref, sb_ref, acc_ref, *, qscale):
    C = RET_CHUNK
    U = RET_GROUP
    s = q_ref.shape[0]
    n = s // C
    lg = lg_ref[...]
    pos = lax.broadcasted_iota(jnp.int32, (C, C), 0).astype(F32)
    col = lax.broadcasted_iota(jnp.int32, (C, C), 1).astype(F32)
    sc_q_prev = jnp.exp((pos + 1.0) * lg)
    sc_k_fwd = jnp.exp((C - 1.0 - pos) * lg)
    sc_k_bwd = jnp.exp(pos * lg)
    sc_q_next = jnp.exp((C - pos) * lg)
    g_chunk = jnp.exp(float(C) * lg)
    decay = jnp.exp(jnp.abs(pos - col) * lg)
    gg = gg_ref[...]
    gb = gb_ref[...]

    def rows_of(grp):
        return [pl.ds(pl.multiple_of((grp * U + u) * C, C), C) for u in range(U)]

    def rot(x, r):
        x = x.astype(F32)
        return x * cos_ref[r, :] + pltpu.roll(x, C // 2, 1) * sin_ref[r, :]

    def local(grp, carry):
        rows = rows_of(grp)
        q = [rot(q_ref[r, :], r) * qscale for r in rows]
        k = [rot(k_ref[r, :], r) for r in rows]
        vb = [v_ref[r, :].astype(BF16) for r in rows]
        sc = [_dot_nt(q[u], k[u]) * decay for u in range(U)]
        out = [_dot(sc[u], vb[u]) for u in range(U)]
        kf = [_dot((k[u] * sc_k_fwd).T, vb[u]) for u in range(U)]
        kb = [_dot((k[u] * sc_k_bwd).T, vb[u]) for u in range(U)]
        for u in range(U):
            qs_ref[rows[u], :] = q[u]
            acc_ref[rows[u], :] = out[u]
            sf_ref[grp * U + u] = kf[u]
            sb_ref[grp * U + u] = kb[u]
        return carry

    lax.fori_loop(0, n // U, local, 0)

    def fwd(c, S):
        kv = sf_ref[c]
        sf_ref[c] = S
        return S * g_chunk + kv

    lax.fori_loop(0, n, fwd, jnp.zeros((C, C), F32))

    def bwd(i, S):
        c = n - 1 - i
        kv = sb_ref[c]
        sb_ref[c] = S
        return S * g_chunk + kv

    lax.fori_loop(0, n, bwd, jnp.zeros((C, C), F32))

    def cross(grp, carry):
        rows = rows_of(grp)
        q = [qs_ref[r, :] for r in rows]
        y = [acc_ref[rows[u], :] + _dot(jnp.concatenate([q[u] * sc_q_prev, q[u] * sc_q_next], axis=1),
                                        jnp.concatenate([sf_ref[grp * U + u], sb_ref[grp * U + u]], axis=0))
             for u in range(U)]
        for u in range(U):
            mu = jnp.mean(y[u], axis=-1, keepdims=True)
            d = y[u] - mu
            var = jnp.mean(d * d, axis=-1, keepdims=True)
            yn = d * lax.rsqrt(var + GN_EPS) * gg + gb
            o_ref[rows[u], :] = _silu(gt_ref[rows[u], :].astype(F32)) * yn
        return carry

    lax.fori_loop(0, n // U, cross, 0)


def _retention(z_ret, cos, sin, logg, gn_g, gn_b):
    b, s, _ = z_ret.shape
    C = RET_CHUNK
    H = RET_HEADS
    assert C == LANES and s % (C * RET_GROUP) == 0
    n = s // C
    blk = lambda off: pl.BlockSpec((None, s, C), lambda i, h, off=off: (i, 0, off + h))
    return pl.pallas_call(
        functools.partial(_ret_body, qscale=float(C) ** -0.5),
        grid=(b, H),
        in_specs=[blk(0), blk(H), blk(2 * H), blk(3 * H),
                  pl.BlockSpec((s, C), lambda i, h: (0, 0)),
                  pl.BlockSpec((s, C), lambda i, h: (0, 0)),
                  pl.BlockSpec((None, 1, C), lambda i, h: (h, 0, 0)),
                  pl.BlockSpec((1, C), lambda i, h: (0, h)),
                  pl.BlockSpec((1, C), lambda i, h: (0, h))],
        out_specs=pl.BlockSpec((None, s, C), lambda i, h: (i, 0, h)),
        out_shape=jax.ShapeDtypeStruct((b, s, H * C), F32),
        scratch_shapes=[pltpu.VMEM((s, C), F32), pltpu.VMEM((n, C, C), F32), pltpu.VMEM((n, C, C), F32),
                        pltpu.VMEM((s, C), F32)],
        compiler_params=_cparams(("parallel", "parallel")),
        name="retention",
    )(z_ret, z_ret, z_ret, z_ret, cos, sin, logg, gn_g, gn_b)


def _prep_body(z_ref, zp_ref, zn_ref, mu_ref, wup_ref, aup_ref, w0_ref, a0_ref, gup_ref,
               kkp_ref, ka_ref, rk_ref, bd_ref,
               r_o, v_o, kk_o, lw_o, kd_o, kka_o, bonus_o, g_o, *, width):
    t = pl.program_id(1)
    nt = pl.num_programs(1)
    W = width
    z = z_ref[...]
    ts = z.shape[0]
    row = lax.broadcasted_iota(jnp.int32, (ts, 1), 0)
    prev_row = jnp.where(t > 0, zp_ref[SUBLANES - 1:SUBLANES, :], 0.0)
    next_row = jnp.where(t < nt - 1, zn_ref[0:1, :], 0.0)
    prev = jnp.where(row == 0, prev_row, pltpu.roll(z, 1, 0))
    nxt = jnp.where(row == ts - 1, next_row, pltpu.roll(z, ts - 1, 0))
    zs = z + mu_ref[...] * (0.5 * (prev + nxt) - z)

    r = zs[:, 0:W]
    kx = zs[:, W:2 * W]
    vx = zs[:, 2 * W:3 * W]
    wd = jnp.tanh(zs[:, 3 * W:3 * W + LANES])
    ad = zs[:, 3 * W + LANES:3 * W + 2 * LANES]
    gd = _sigmoid(zs[:, 3 * W + 2 * LANES:3 * W + 3 * LANES])
    bd = bd_ref[...]

    kk = kx * kkp_ref[...]
    ssq = _dot_exact_rhs(kk * kk, bd)
    kk = kk * lax.rsqrt(jnp.maximum(ssq, 1e-24))
    ka = ka_ref[...]
    ksum = None
    for d in range(2):
        pre = w0_ref[d:d + 1, :] + _dot(wd, wup_ref[d])
        lw_o[d] = -math.exp(-0.5) * _sigmoid(pre)
        a = _sigmoid(a0_ref[d:d + 1, :] + _dot(ad, aup_ref[d]))
        kd = kx * (1.0 + (a - 1.0) * ka)
        kd_o[d] = kd.astype(kd_o.dtype)
        kka_o[d] = (kk * a).astype(kka_o.dtype)
        ksum = kd if ksum is None else ksum + kd
    r_o[...] = r.astype(r_o.dtype)
    v_o[...] = vx.astype(v_o.dtype)
    kk_o[...] = kk.astype(kk_o.dtype)
    bonus_o[...] = (_dot_exact_rhs(r * ksum * rk_ref[...], bd) * vx).astype(bonus_o.dtype)
    g_o[...] = _dot(gd, gup_ref[...]).astype(g_o.dtype)


def _rwkv_prep(z_rwkv, p, ts):
    b, s, ncol = z_rwkv.shape
    W = p["rwkv_width"]
    nt = s // ts
    hb = ts // SUBLANES
    last = s // SUBLANES - 1
    full = lambda a: pl.BlockSpec(a.shape, lambda i, t, nd=a.ndim: (0,) * nd)
    out_tok = pl.BlockSpec((None, ts, W), lambda i, t: (i, t, 0))
    out_dir = pl.BlockSpec((2, None, ts, W), lambda i, t: (0, i, t, 0))
    tok_shape = jax.ShapeDtypeStruct((b, s, W), BF16)
    dir_shape = jax.ShapeDtypeStruct((2, b, s, W), BF16)
    lw_shape = jax.ShapeDtypeStruct((2, b, s, W), F32)
    consts = [p["mu"], p["wup_pad"], p["aup_pad"], p["w0"], p["a0"], p["gup"],
              p["k_k"], p["k_a"], p["r_k"], p["head_ones"]]
    return pl.pallas_call(
        functools.partial(_prep_body, width=W),
        grid=(b, nt),
        in_specs=[pl.BlockSpec((None, ts, ncol), lambda i, t: (i, t, 0)),
                  pl.BlockSpec((None, SUBLANES, ncol), lambda i, t: (i, jnp.maximum(t * hb - 1, 0), 0)),
                  pl.BlockSpec((None, SUBLANES, ncol), lambda i, t: (i, jnp.minimum((t + 1) * hb, last), 0)),
                  ] + [full(a) for a in consts],
        out_specs=[out_tok, out_tok, out_tok, out_dir, out_dir, out_dir, out_tok, out_tok],
        out_shape=[tok_shape, tok_shape, tok_shape, lw_shape, dir_shape, dir_shape, tok_shape, tok_shape],
        compiler_params=_cparams(("parallel", "parallel")),
        name="rwkv_prep",
    )(z_rwkv, z_rwkv, z_rwkv, *consts)


def _scan_body(rf_ref, kkf_ref, vf_ref, rb_ref, kkb_ref, vb_ref, lwf_ref, kdf_ref, kkaf_ref,
               lwb_ref, kdb_ref, kkab_ref, yf_ref, yb_ref, st_ref, *, n_pairs):
    L = SCAN_CHUNK
    H = 2 * L
    assert H == LANES
    tt = rf_ref.shape[0]
    nch = tt // L

    @pl.when(pl.program_id(1) == 0)
    def _():
        st_ref[...] = jnp.zeros_like(st_ref)

    r_refs, kk_refs, v_refs = (rf_ref, rb_ref), (kkf_ref, kkb_ref), (vf_ref, vb_ref)
    lw_refs, kd_refs, kka_refs = (lwf_ref, lwb_ref), (kdf_ref, kdb_ref), (kkaf_ref, kkab_ref)
    y_refs = (yf_ref, yb_ref)
    ii = lax.broadcasted_iota(jnp.int32, (H, H), 0)
    jj = lax.broadcasted_iota(jnp.int32, (H, H), 1)
    same = (ii < L) == (jj < L)
    strict = (jnp.logical_and(same, ii > jj), jnp.logical_and(same, ii < jj))
    incl = (jnp.logical_and(same, ii >= jj), jnp.logical_and(same, ii <= jj))
    eye = ii == jj
    li = lax.broadcasted_iota(jnp.int32, (L, L), 0)
    lj = lax.broadcasted_iota(jnp.int32, (L, L), 1)
    tri = (jnp.where(li >= lj, 1.0, 0.0).astype(BF16), jnp.where(li <= lj, 1.0, 0.0).astype(BF16))
    head0 = lax.broadcasted_iota(jnp.int32, (L, H), 1) < L

    def stack(x):
        return jnp.concatenate([jnp.where(head0, x, 0.0), jnp.where(head0, 0.0, x)], axis=0)

    chains = [(d, hp) for d in range(2) for hp in range(n_pairs)]
    P = range(len(chains))
    lanes = [slice(hp * H, (hp + 1) * H) for _, hp in chains]
    dirs = [d for d, _ in chains]

    def chunk(j, carry):
        rows = (pl.ds(pl.multiple_of(j * L, L), L), pl.ds(pl.multiple_of((nch - 1 - j) * L, L), L))
        ld = lambda refs, h: refs[dirs[h]][rows[dirs[h]], lanes[h]].astype(F32)
        lw = [ld(lw_refs, h) for h in P]
        cum = [_dot_exact_lhs(tri[dirs[h]], lw[h]) for h in P]
        tot = [jnp.sum(x, axis=0, keepdims=True) for x in lw]
        e_incl = [jnp.exp(a) for a in cum]
        e_excl = [jnp.exp(a - x) for a, x in zip(cum, lw)]
        e_inv = [jnp.exp(-a) for a in cum]
        e_rem = [jnp.exp(t_ - a) for t_, a in zip(tot, cum)]
        kk = [ld(kk_refs, h) for h in P]
        kka = [ld(kka_refs, h) for h in P]
        kd = [ld(kd_refs, h) for h in P]
        Kk = [stack(a * e) for a, e in zip(kk, e_excl)]
        R = [stack(ld(r_refs, h) * e_incl[h]) for h in P]
        B = [stack(a * e) for a, e in zip(kka, e_inv)]
        Kd = [stack(a * e) for a, e in zip(kd, e_inv)]
        Bh = [stack(a * e).T.astype(BF16) for a, e in zip(kka, e_rem)]
        Kh = [stack(a * e).T.astype(BF16) for a, e in zip(kd, e_rem)]
        V = [stack(ld(v_refs, h)).astype(BF16) for h in P]

        gram = [_dot_nt(jnp.concatenate([Kk[h], R[h]], axis=0), jnp.concatenate([B[h], Kd[h]], axis=0))
                for h in P]
        Np = [jnp.where(strict[dirs[h]], gram[h][:H, :H], 0.0) for h in P]
        AD = [_dot(jnp.concatenate([jnp.where(strict[dirs[h]], gram[h][:H, H:], 0.0).astype(BF16),
                                    jnp.where(incl[dirs[h]], gram[h][H:, H:], 0.0).astype(BF16),
                                    Kh[h]], axis=0), V[h]) for h in P]
        CB = [jnp.concatenate([jnp.where(incl[dirs[h]], gram[h][H:, :H], 0.0).astype(BF16), Bh[h]], axis=0)
              for h in P]
        X = [jnp.concatenate([Kk[h], AD[h][:H]], axis=1) for h in P]
        n_fac = L.bit_length() - 1
        for f in range(n_fac):
            Nb = [a.astype(BF16) for a in Np]
            NX = [_dot(Nb[h], X[h]) for h in P]
            if f < n_fac - 1:
                Np = [_dot(Nb[h], Nb[h]) for h in P]
            X = [x - nx if f == 0 else x + nx for x, nx in zip(X, NX)]
        CBX = [_dot(CB[h], X[h]) for h in P]
        QM = []
        for h in P:
            Qh = R[h] - CBX[h][:H, :H]
            M = jnp.where(eye, jnp.exp(tot[h]), 0.0) - CBX[h][H:, :H]
            QM.append(_dot(jnp.concatenate([Qh, M], axis=0), st_ref[h]))
        for h in P:
            Ys = QM[h][:H] + (AD[h][H:2 * H] - CBX[h][:H, H:])
            st_ref[h] = QM[h][H:] + (AD[h][2 * H:] - CBX[h][H:, H:])
            y_refs[dirs[h]][rows[dirs[h]], lanes[h]] = Ys[:L] + Ys[L:]
        return carry

    lax.fori_loop(0, nch, chunk, 0)


def _rwkv_scan(r, kk, v, lw, kd, kka, tt):
    b, s, W = r.shape
    nt = s // tt
    n_pairs = W // LANES
    tok_f = pl.BlockSpec((None, tt, W), lambda i, t: (i, t, 0))
    tok_b = pl.BlockSpec((None, tt, W), lambda i, t: (i, nt - 1 - t, 0))
    dir_f = pl.BlockSpec((None, None, tt, W), lambda i, t: (0, i, t, 0))
    dir_b = pl.BlockSpec((None, None, tt, W), lambda i, t: (1, i, nt - 1 - t, 0))
    out = jax.ShapeDtypeStruct((b, s, W), F32)
    return pl.pallas_call(
        functools.partial(_scan_body, n_pairs=n_pairs),
        grid=(b, nt),
        in_specs=[tok_f, tok_f, tok_f, tok_b, tok_b, tok_b, dir_f, dir_f, dir_f, dir_b, dir_b, dir_b],
        out_specs=[tok_f, tok_b],
        out_shape=[out, out],
        scratch_shapes=[pltpu.VMEM((2 * n_pairs, LANES, LANES), F32)],
        compiler_params=_cparams(("parallel", "arbitrary")),
        name="rwkv_scan",
    )(r, kk, v, r, kk, v, lw, kd, kka, lw, kd, kka)


def _mix_body(yf_ref, yb_ref, bonus_ref, g_ref, ret_ref, x_ref, gng_ref, gnb_ref, avg_ref, wo1_ref, wo2_ref,
              l1g_ref, l1b_ref, rwt_ref, rb_ref, x1_ref, xp_ref, idx_ref, gate_ref, cnt_ref, *, alpha):
    y = yf_ref[...] + yb_ref[...]
    avg = avg_ref[...]
    mu = _dot_exact_rhs(y, avg)
    dl = y - mu
    var = jnp.dot((dl * dl).astype(BF16), avg, preferred_element_type=F32)
    yn = dl * lax.rsqrt(var + RWKV_GN_EPS) * gng_ref[...] + gnb_ref[...]
    rw = (yn + bonus_ref[...]) * g_ref[...]
    m = _dot(ret_ref[...], wo1_ref[...]) + _dot(rw, wo2_ref[...])
    x1 = _layer_norm(alpha * x_ref[...] + m, l1g_ref[...], l1b_ref[...])
    x1_ref[...] = x1
    xp_ref[...] = _pack_halves(x1)

    scores = _sigmoid(_dot_nt(rwt_ref[...], x1))
    E, tm = scores.shape
    GS = E // N_GROUPS
    NEG = -jnp.inf
    biased = scores + rb_ref[...]
    rowi = lax.broadcasted_iota(jnp.int32, (E, tm), 0)
    ri = lax.broadcasted_iota(jnp.int32, (GS, tm), 0)
    gs_rows = []
    for gi in range(N_GROUPS):
        blk = biased[gi * GS:(gi + 1) * GS, :]
        m1 = jnp.max(blk, axis=0, keepdims=True)
        i1 = jnp.min(jnp.where(blk == m1, ri, GS), axis=0, keepdims=True)
        m2 = jnp.max(jnp.where(ri == i1, NEG, blk), axis=0, keepdims=True)
        gs_rows.append(m1 + m2)
    cur = jnp.concatenate(gs_rows, axis=0)
    gidx = lax.broadcasted_iota(jnp.int32, (N_GROUPS, tm), 0)
    row_group = rowi // GS
    emask = jnp.zeros((E, tm), F32)
    for _ in range(TOPK_GROUPS):
        mx = jnp.max(cur, axis=0, keepdims=True)
        ix = jnp.min(jnp.where(cur == mx, gidx, N_GROUPS), axis=0, keepdims=True)
        emask = jnp.where(row_group == ix, 1.0, emask)
        cur = jnp.where(gidx == ix, NEG, cur)
    cur = jnp.where(emask > 0.5, biased, NEG)
    idxs, sels = [], []
    chosen = jnp.zeros((E, tm), F32)
    for _ in range(TOP_K):
        mx = jnp.max(cur, axis=0, keepdims=True)
        ix = jnp.min(jnp.where(cur == mx, rowi, E), axis=0, keepdims=True)
        hit = rowi == ix
        sels.append(jnp.sum(jnp.where(hit, scores, 0.0), axis=0, keepdims=True))
        idxs.append(ix)
        cur = jnp.where(hit, NEG, cur)
        chosen = jnp.where(hit, 1.0, chosen)
    sel = jnp.concatenate(sels, axis=0)
    idx_ref[...] = jnp.concatenate(idxs, axis=0)
    gate_ref[...] = sel / jnp.sum(sel, axis=0, keepdims=True) * ROUTE_SCALE

    @pl.when(pl.program_id(0) == 0)
    def _():
        cnt_ref[...] = jnp.zeros_like(cnt_ref)

    cnt_ref[...] += jnp.sum(chosen, axis=1, keepdims=True)


def _mix_out(y_f, y_b, bonus, g, ret_out, x2d, p, tm):
    T, D = x2d.shape
    W = bonus.shape[1]
    Wr = ret_out.shape[1]
    E = p["router_wt"].shape[0]
    full = lambda a: pl.BlockSpec(a.shape, lambda i, nd=a.ndim: (0,) * nd)
    consts = [p["gn_g"], p["gn_b"], p["head_avg"], p["wo_ret"], p["wo_rwkv"], p["ln1_g"], p["ln1_b"],
              p["router_wt"], p["router_b"]]
    return pl.pallas_call(
        functools.partial(_mix_body, alpha=p["alpha"]),
        grid=(T // tm,),
        in_specs=[pl.BlockSpec((tm, W), lambda i: (i, 0)),
                  pl.BlockSpec((tm, W), lambda i: (i, 0)),
                  pl.BlockSpec((tm, W), lambda i: (i, 0)),
                  pl.BlockSpec((tm, W), lambda i: (i, 0)),
                  pl.BlockSpec((tm, Wr), lambda i: (i, 0)),
                  pl.BlockSpec((tm, D), lambda i: (i, 0))] + [full(a) for a in consts],
        out_specs=[pl.BlockSpec((tm, D), lambda i: (i, 0)),
                   pl.BlockSpec((tm, D // 2), lambda i: (i, 0)),
                   pl.BlockSpec((TOP_K, tm), lambda i: (0, i)),
                   pl.BlockSpec((TOP_K, tm), lambda i: (0, i)),
                   pl.BlockSpec((E, LANES), lambda i: (0, 0))],
        out_shape=[jax.ShapeDtypeStruct((T, D), F32),
                   jax.ShapeDtypeStruct((T, D // 2), jnp.int32),
                   jax.ShapeDtypeStruct((TOP_K, T), jnp.int32),
                   jax.ShapeDtypeStruct((TOP_K, T), F32),
                   jax.ShapeDtypeStruct((E, LANES), F32)],
        compiler_params=_cparams(("arbitrary",)),
        name="mix_out",
    )(y_f, y_b, bonus, g, ret_out, x2d, *consts)


def _plan_body(idx_ref, pstart_ref, upper_ref, dest_ref, run_ref):
    @pl.when(pl.program_id(0) == 0)
    def _():
        run_ref[...] = jnp.zeros_like(run_ref)

    E = pstart_ref.shape[0]
    tm = idx_ref.shape[1]
    rowi = lax.broadcasted_iota(jnp.int32, (E, tm), 0)
    hits = [rowi == idx_ref[k:k + 1, :] for k in range(TOP_K)]
    member = jnp.zeros((E, tm), F32)
    for hit in hits:
        member = jnp.where(hit, 1.0, member)
    before = jnp.dot(member.astype(BF16), upper_ref[...], preferred_element_type=F32)
    slot = pstart_ref[...] + run_ref[:, 0:1] + before
    dest_ref[...] = jnp.concatenate(
        [jnp.sum(jnp.where(hit, slot, 0.0), axis=0, keepdims=True) for hit in hits], axis=0).astype(jnp.int32)
    run_ref[...] += jnp.sum(member, axis=1, keepdims=True)


def _slot_plan(idx_t, pad_start, tm):
    T = idx_t.shape[1]
    E = pad_start.shape[0]
    upper = (jnp.arange(tm)[:, None] < jnp.arange(tm)[None, :]).astype(BF16)
    return pl.pallas_call(
        _plan_body,
        grid=(T // tm,),
        in_specs=[pl.BlockSpec((TOP_K, tm), lambda i: (0, i)),
                  pl.BlockSpec((E, 1), lambda i: (0, 0)),
                  pl.BlockSpec((tm, tm), lambda i: (0, 0))],
        out_specs=pl.BlockSpec((TOP_K, tm), lambda i: (0, i)),
        out_shape=jax.ShapeDtypeStruct((TOP_K, T), jnp.int32),
        scratch_shapes=[pltpu.VMEM((E, LANES), F32)],
        compiler_params=_cparams(("arbitrary",)),
        name="slot_plan",
    )(idx_t, pad_start.reshape(E, 1).astype(F32), upper)


_UPPER_HALF = -65536


def _bf16_bits(x):
    return lax.bitcast_convert_type(x.astype(BF16).astype(F32), jnp.int32)


def _pack_halves(x):
    c = x.shape[1] // 2
    return _bf16_bits(x[:, :c]) | (jnp.right_shift(_bf16_bits(x[:, c:]), 16) & 0xFFFF)


def _unpack_halves(u):
    hi = lax.bitcast_convert_type(u & _UPPER_HALF, F32)
    lo = lax.bitcast_convert_type(jnp.left_shift(u, 16), F32)
    return hi, lo


SC_WINDOW = 64


def _sc_dispatch(xp, dest, n_slots):
    T, C = xp.shape
    n_win = T // SC_WINDOW
    mesh = plsc.VectorSubcoreMesh(core_axis_name="core", subcore_axis_name="subcore")

    @pl.kernel(out_type=jax.ShapeDtypeStruct((n_slots, C), xp.dtype), mesh=mesh, scratch_types=[])
    def scatter_rows(x_hbm, i_hbm, o_hbm):
        def body(x_vmem, i_vmem):
            for k in range(TOP_K):
                pltpu.sync_copy(x_vmem, o_hbm.at[i_vmem.at[k]])

        pltpu.emit_pipeline(
            body,
            grid=(n_win,),
            in_specs=[pl.BlockSpec((SC_WINDOW, C), lambda i: (i, 0)),
                      pl.BlockSpec((None, TOP_K, SC_WINDOW), lambda i: (i, 0, 0))],
            out_specs=[],
            core_axis_name=("core", "subcore"),
            dimension_semantics=(pltpu.PARALLEL,),
        )(x_hbm, i_hbm)

    idx = dest.reshape(TOP_K, n_win, SC_WINDOW).transpose(1, 0, 2)
    return scatter_rows(xp, idx)


def _sc_gather(ys, dest):
    n_idx = dest.shape[0] * dest.shape[1]
    C = ys.shape[1]
    mesh = plsc.VectorSubcoreMesh(core_axis_name="core", subcore_axis_name="subcore")

    @pl.kernel(out_type=jax.ShapeDtypeStruct((n_idx, C), ys.dtype), mesh=mesh, scratch_types=[])
    def gather_rows(y_hbm, i_hbm, o_hbm):
        def body(i_vmem, o_vmem):
            pltpu.sync_copy(y_hbm.at[i_vmem.at[0]], o_vmem)

        pltpu.emit_pipeline(
            body,
            grid=(n_idx // SC_WINDOW,),
            in_specs=[pl.BlockSpec((1, SC_WINDOW), lambda i: (i, 0))],
            out_specs=[pl.BlockSpec((SC_WINDOW, C), lambda i: (i, 0))],
            core_axis_name=("core", "subcore"),
            dimension_semantics=(pltpu.PARALLEL,),
        )(i_hbm, o_hbm)

    return gather_rows(ys, dest.reshape(n_idx // SC_WINDOW, SC_WINDOW))


MOE_IN_SLOTS = 3
MOE_OUT_SLOTS = 2


def _moe_body(nb_ref, b0_ref, cnt_ref, nu_ref, xs_hbm, wg_ref, wu_ref, wd_ref, ys_hbm,
              xbuf, obuf, wgb_ref, wub_ref, wdb_ref, in_sem, out_sem):
    G = MOE_BLOCK
    e = pl.program_id(0)
    nb = nb_ref[e]
    b0 = b0_ref[e]
    n_used = nu_ref[0]

    def in_copy(g):
        slot = lax.rem(g, MOE_IN_SLOTS)
        return pltpu.make_async_copy(xs_hbm.at[pl.ds(pl.multiple_of(g * G, G), G), :], xbuf.at[slot],
                                     in_sem.at[slot])

    def out_copy(g):
        slot = lax.rem(g, MOE_OUT_SLOTS)
        return pltpu.make_async_copy(obuf.at[slot], ys_hbm.at[pl.ds(pl.multiple_of(g * G, G), G), :],
                                     out_sem.at[slot])

    @pl.when(e == 0)
    def _():
        for g in range(MOE_IN_SLOTS - 1):
            @pl.when(g < n_used)
            def _():
                in_copy(g).start()

    @pl.when(nb > 0)
    def _():
        wgb_ref[...] = wg_ref[...].astype(BF16)
        wub_ref[...] = wu_ref[...].astype(BF16)
        wdb_ref[...] = wd_ref[...].astype(BF16)
        row = lax.broadcasted_iota(jnp.int32, (G, xbuf.shape[2]), 0)
        c = xbuf.shape[2]

        def step(j, carry):
            g = b0 + j
            in_copy(g).wait()

            @pl.when(g + (MOE_IN_SLOTS - 1) < n_used)
            def _():
                in_copy(g + (MOE_IN_SLOTS - 1)).start()

            @pl.when(g >= MOE_OUT_SLOTS)
            def _():
                out_copy(g - MOE_OUT_SLOTS).wait()

            u = jnp.where(row < cnt_ref[e] - j * G, xbuf[lax.rem(g, MOE_IN_SLOTS)], 0)
            x_hi, x_lo = _unpack_halves(u)
            gate = _dot(x_hi, wgb_ref[:c, :]) + _dot(x_lo, wgb_ref[c:, :])
            up = _dot(x_hi, wub_ref[:c, :]) + _dot(x_lo, wub_ref[c:, :])
            obuf[lax.rem(g, MOE_OUT_SLOTS)] = _pack_halves(_dot(_silu(gate) * up, wdb_ref[...]))
            out_copy(g).start()
            return carry

        lax.fori_loop(0, nb, step, 0)

    @pl.when(e == pl.num_programs(0) - 1)
    def _():
        for back in range(MOE_OUT_SLOTS, 0, -1):
            @pl.when(n_used >= back)
            def _():
                out_copy(n_used - back).wait()


def _moe_ffn(xs, blocks_per_expert, first_block, counts, n_used, w_gate, w_up, w_down):
    P, C = xs.shape
    G = MOE_BLOCK
    E, D, De = w_gate.shape
    wspec = lambda shape: pl.BlockSpec((None,) + shape, lambda e, nb, b0, cnt, nu: (e, 0, 0))
    grid_spec = pltpu.PrefetchScalarGridSpec(
        num_scalar_prefetch=4,
        grid=(E,),
        in_specs=[pl.BlockSpec(memory_space=pl.ANY), wspec((D, De)), wspec((D, De)), wspec((De, D))],
        out_specs=pl.BlockSpec(memory_space=pl.ANY),
        scratch_shapes=[pltpu.VMEM((MOE_IN_SLOTS, G, C), jnp.int32), pltpu.VMEM((MOE_OUT_SLOTS, G, C), jnp.int32),
                        pltpu.VMEM((D, De), BF16), pltpu.VMEM((D, De), BF16), pltpu.VMEM((De, D), BF16),
                        pltpu.SemaphoreType.DMA((MOE_IN_SLOTS,)), pltpu.SemaphoreType.DMA((MOE_OUT_SLOTS,))],
    )
    return pl.pallas_call(
        _moe_body,
        grid_spec=grid_spec,
        out_shape=jax.ShapeDtypeStruct((P, C), jnp.int32),
        compiler_params=_cparams(("arbitrary",)),
        name="moe_ffn",
    )(blocks_per_expert, first_block, counts, n_used, xs, w_gate, w_up, w_down)


def _comb_body(yg_ref, x1_ref, gt_ref, sg_ref, su_ref, sd_ref, l2g_ref, l2b_ref, o_ref, *, alpha):
    x1 = x1_ref[...]
    xb = x1.astype(BF16)
    shared = _dot(_silu(_dot(xb, sg_ref[...])) * _dot(xb, su_ref[...]), sd_ref[...])
    gt = gt_ref[...]
    acc_hi = acc_lo = None
    for k in range(TOP_K):
        hi, lo = _unpack_halves(yg_ref[k])
        gk = gt[:, k:k + 1]
        acc_hi = gk * hi if acc_hi is None else acc_hi + gk * hi
        acc_lo = gk * lo if acc_lo is None else acc_lo + gk * lo
    routed = jnp.concatenate([acc_hi, acc_lo], axis=1)
    o_ref[...] = _layer_norm(alpha * x1 + (routed + shared), l2g_ref[...], l2b_ref[...])


def _combine(yg, x1, gates_t, p, tm):
    T, D = x1.shape
    full = lambda a: pl.BlockSpec(a.shape, lambda i, nd=a.ndim: (0,) * nd)
    consts = [p["sh_gate"], p["sh_up"], p["sh_down"], p["ln2_g"], p["ln2_b"]]
    return pl.pallas_call(
        functools.partial(_comb_body, alpha=p["alpha"]),
        grid=(T // tm,),
        in_specs=[pl.BlockSpec((TOP_K, tm, D // 2), lambda i: (0, i, 0)),
                  pl.BlockSpec((tm, D), lambda i: (i, 0)),
                  pl.BlockSpec((tm, TOP_K), lambda i: (i, 0))] + [full(a) for a in consts],
        out_specs=pl.BlockSpec((tm, D), lambda i: (i, 0)),
        out_shape=jax.ShapeDtypeStruct((T, D), F32),
        compiler_params=_cparams(("parallel",)),
        name="combine",
    )(yg, x1, gates_t, *consts)


def _segment_layout(counts, n_tokens):
    G = MOE_BLOCK
    E = counts.shape[0]
    n_blocks = -(-(n_tokens * TOP_K + E * (G - 1)) // G)
    padded = (counts + G - 1) // G * G
    pad_end = jnp.cumsum(padded)
    pad_start = pad_end - padded
    n_used = pad_end[-1:] // G
    return (pad_start, (padded // G).astype(jnp.int32), (pad_start // G).astype(jnp.int32),
            n_used.astype(jnp.int32), n_blocks)


def _rotary_tables(s, d):
    inv = ROPE_BASE ** (-jnp.arange(0, d, 2, dtype=F32) / d)
    ang = jnp.arange(s, dtype=F32)[:, None] * inv[None, :]
    cos = jnp.cos(ang)
    sin = jnp.sin(ang)
    return jnp.concatenate([cos, cos], axis=-1), jnp.concatenate([-sin, sin], axis=-1)


def _layer_params(l, depth, w_in, ret_gn_g, ret_gn_b, rwkv_mu, rwkv_w0, rwkv_w_up, rwkv_a0, rwkv_a_up,
                  rwkv_g_up, rwkv_k_k, rwkv_k_a, rwkv_r_k, rwkv_gn_g, rwkv_gn_b, w_out, ln1_g, ln1_b,
                  router_w, router_bias, exp_w_gate, exp_w_up, exp_w_down, sh_w_gate, sh_w_up,
                  sh_w_down, ln2_g, ln2_b):
    ret_w = ret_gn_g.shape[-1]
    W = rwkv_gn_g.shape[-1]
    n_heads, hd = rwkv_r_k.shape[-2:]
    rank_w = rwkv_w_up.shape[2]
    rank_a = rwkv_a_up.shape[2]
    assert rank_w * 2 == LANES and rank_a * 2 == LANES and rwkv_g_up.shape[1] == LANES
    assert hd * 2 == LANES and SCAN_CHUNK == hd
    row = lambda a: a.reshape(1, -1).astype(F32)
    zw = jnp.zeros((rank_w, W), F32)
    za = jnp.zeros((rank_a, W), F32)
    head_id = jnp.arange(W) // hd
    same_head = (head_id[:, None] == head_id[None, :])
    wi = w_in[l]
    return dict(
        alpha=float((2 * depth) ** 0.25),
        rwkv_width=W,
        w_ret=wi[:, :4 * ret_w].astype(BF16),
        w_rwkv=wi[:, 4 * ret_w:].astype(BF16),
        ret_gn_g=row(ret_gn_g[l]), ret_gn_b=row(ret_gn_b[l]),
        logg=jnp.broadcast_to(
            jnp.log1p(-jnp.exp2(-5.0 - jnp.arange(RET_HEADS, dtype=F32)))[:, None, None],
            (RET_HEADS, 1, LANES)),
        mu=row(rwkv_mu[l]),
        wup_pad=jnp.stack([jnp.concatenate([rwkv_w_up[l, 0], zw], 0),
                           jnp.concatenate([zw, rwkv_w_up[l, 1]], 0)]).astype(BF16),
        aup_pad=jnp.stack([jnp.concatenate([rwkv_a_up[l, 0], za], 0),
                           jnp.concatenate([za, rwkv_a_up[l, 1]], 0)]).astype(BF16),
        w0=rwkv_w0[l].astype(F32), a0=rwkv_a0[l].astype(F32),
        gup=rwkv_g_up[l].astype(BF16),
        k_k=row(rwkv_k_k[l]), k_a=row(rwkv_k_a[l]), r_k=row(rwkv_r_k[l]),
        head_ones=same_head.astype(BF16),
        head_avg=(same_head.astype(F32) / hd).astype(BF16),
        gn_g=row(rwkv_gn_g[l]), gn_b=row(rwkv_gn_b[l]),
        wo_ret=w_out[l, :ret_w].astype(BF16), wo_rwkv=w_out[l, ret_w:].astype(BF16),
        ln1_g=row(ln1_g[l]), ln1_b=row(ln1_b[l]),
        router_wt=router_w[l].T.astype(BF16), router_b=router_bias[l].reshape(-1, 1).astype(F32),
        exp_gate=exp_w_gate[l], exp_up=exp_w_up[l], exp_down=exp_w_down[l],
        sh_gate=sh_w_gate[l].astype(BF16), sh_up=sh_w_up[l].astype(BF16), sh_down=sh_w_down[l].astype(BF16),
        ln2_g=row(ln2_g[l]), ln2_b=row(ln2_b[l]),
    )


def _pick(n, pref):
    t = min(n, pref)
    while n % t:
        t //= 2
    return t


def _layer(x, p):
    b, s, D = x.shape
    T = b * s
    x2d = x.reshape(T, D)
    tm = _pick(T, 256)
    z_ret, z_rwkv = _in_proj(x2d, p["w_ret"], p["w_rwkv"], tm)
    cos, sin = _rotary_tables(s, RET_CHUNK)
    ret_out = _retention(z_ret.reshape(b, s, -1), cos, sin, p["logg"], p["ret_gn_g"], p["ret_gn_b"])
    r, v, kk, lw, kd, kka, bonus, g = _rwkv_prep(z_rwkv.reshape(b, s, -1), p, _pick(s, 256))
    y_f, y_b = _rwkv_scan(r, kk, v, lw, kd, kka, _pick(s, 512))
    W = p["rwkv_width"]
    x1, xp, idx_t, gates, cnt = _mix_out(y_f.reshape(T, W), y_b.reshape(T, W), bonus.reshape(T, W),
                                         g.reshape(T, W), ret_out.reshape(T, -1), x2d, p, tm)
    counts = cnt[:, 0].astype(jnp.int32)
    pad_start, blocks_per_expert, first_block, n_used, n_blocks = _segment_layout(counts, T)
    dest = _slot_plan(idx_t, pad_start, _pick(T, 512))
    xs = _sc_dispatch(xp, dest, n_blocks * MOE_BLOCK)
    ys = _moe_ffn(xs, blocks_per_expert, first_block, counts, n_used, p["exp_gate"], p["exp_up"], p["exp_down"])
    yg = _sc_gather(ys, dest).reshape(TOP_K, T, D // 2)
    out = _combine(yg, x1, gates.T, p, _pick(T, 256))
    return out.reshape(b, s, D)


def kernel(x_prompt, x_sample, w_in, ret_gn_g, ret_gn_b, rwkv_mu, rwkv_w0, rwkv_w_up, rwkv_a0, rwkv_a_up,
           rwkv_g_up, rwkv_k_k, rwkv_k_a, rwkv_r_k, rwkv_gn_g, rwkv_gn_b, w_out, ln1_g, ln1_b, router_w,
           router_bias, exp_w_gate, exp_w_up, exp_w_down, sh_w_gate, sh_w_up, sh_w_down, ln2_g, ln2_b):
    weights = (w_in, ret_gn_g, ret_gn_b, rwkv_mu, rwkv_w0, rwkv_w_up, rwkv_a0, rwkv_a_up, rwkv_g_up,
               rwkv_k_k, rwkv_k_a, rwkv_r_k, rwkv_gn_g, rwkv_gn_b, w_out, ln1_g, ln1_b, router_w,
               router_bias, exp_w_gate, exp_w_up, exp_w_down, sh_w_gate, sh_w_up, sh_w_down, ln2_g, ln2_b)
    depth = w_in.shape[0]
    layers = [_layer_params(l, depth, *weights) for l in range(depth)]

    def trunk(x):
        for p in layers:
            x = _layer(x, p)
        return x

    return trunk(x_prompt), trunk(x_sample)
```

```python
import functools
import math

import jax
import jax.numpy as jnp
from jax import lax
from jax.experimental import pallas as pl
from jax.experimental.pallas import tpu as pltpu
from jax.experimental.pallas import tpu_sc as plsc

F32 = jnp.float32
BF16 = jnp.bfloat16

RET_HEADS = 4
RET_CHUNK = 128
ROPE_BASE = 10000.0
TOP_K = 8
N_GROUPS = 8
TOPK_GROUPS = 4
ROUTE_SCALE = 2.5
MOE_BLOCK = 256
LN_EPS = 1e-5
GN_EPS = 1e-5
RWKV_GN_EPS = 64e-5

LANES = 128
SUBLANES = 8
VMEM_LIMIT_BYTES = 56 * 1024 * 1024

SCAN_CHUNK = 64


def _cparams(semantics):
    return pltpu.CompilerParams(dimension_semantics=semantics, vmem_limit_bytes=VMEM_LIMIT_BYTES)


def _dot(a, b):
    return jnp.dot(a.astype(BF16), b.astype(BF16), preferred_element_type=F32)


def _dot_nt(a, b):
    return lax.dot_general(a.astype(BF16), b.astype(BF16), (((1,), (1,)), ((), ())),
                           preferred_element_type=F32)


def _split3(x):
    hi = x.astype(BF16)
    r1 = x - hi.astype(F32)
    mid = r1.astype(BF16)
    lo = (r1 - mid.astype(F32)).astype(BF16)
    return hi, mid, lo


def _dot_exact_rhs(x, w_bf16):
    hi, mid, lo = _split3(x)
    out = jnp.dot(hi, w_bf16, preferred_element_type=F32)
    out += jnp.dot(mid, w_bf16, preferred_element_type=F32)
    out += jnp.dot(lo, w_bf16, preferred_element_type=F32)
    return out


def _dot_exact_lhs(w_bf16, x):
    hi, mid, lo = _split3(x)
    out = jnp.dot(w_bf16, hi, preferred_element_type=F32)
    out += jnp.dot(w_bf16, mid, preferred_element_type=F32)
    out += jnp.dot(w_bf16, lo, preferred_element_type=F32)
    return out


def _sigmoid(x):
    return 1.0 / (1.0 + jnp.exp(-x))


def _silu(x):
    return x * _sigmoid(x)


def _layer_norm(h, g, b):
    mu = jnp.mean(h, axis=-1, keepdims=True)
    d = h - mu
    var = jnp.mean(d * d, axis=-1, keepdims=True)
    return d * lax.rsqrt(var + LN_EPS) * g + b


RET_GROUP = 4


def _ret_body(q_ref, k_ref, v_ref, gt_ref, cos_ref, sin_ref, lg_ref, gg_ref, gb_ref, o_ref,
              qs_ref, sf_ref, sb_ref, acc_ref, *, qscale):
    C = RET_CHUNK
    U = RET_GROUP
    s = q_ref.shape[0]
    n = s // C
    lg = lg_ref[...]
    pos = lax.broadcasted_iota(jnp.int32, (C, C), 0).astype(F32)
    col = lax.broadcasted_iota(jnp.int32, (C, C), 1).astype(F32)
    sc_q_prev = jnp.exp((pos + 1.0) * lg)
    sc_k_fwd = jnp.exp((C - 1.0 - pos) * lg)
    sc_k_bwd = jnp.exp(pos * lg)
    sc_q_next = jnp.exp((C - pos) * lg)
    g_chunk = jnp.exp(float(C) * lg)
    decay = jnp.exp(jnp.abs(pos - col) * lg)
    gg = gg_ref[...]
    gb = gb_ref[...]

    def rows_of(grp):
        return [pl.ds(pl.multiple_of((grp * U + u) * C, C), C) for u in range(U)]

    def rot(x, r):
        x = x.astype(F32)
        return x * cos_ref[r, :] + pltpu.roll(x, C // 2, 1) * sin_ref[r, :]

    def local(grp, carry):
        rows = rows_of(grp)
        q = [rot(q_ref[r, :], r) * qscale for r in rows]
        k = [rot(k_ref[r, :], r) for r in rows]
        vb = [v_ref[r, :].astype(BF16) for r in rows]
        sc = [_dot_nt(q[u], k[u]) * decay for u in range(U)]
        out = [_dot(sc[u], vb[u]) for u in range(U)]
        kf = [_dot((k[u] * sc_k_fwd).T, vb[u]) for u in range(U)]
        kb = [_dot((k[u] * sc_k_bwd).T, vb[u]) for u in range(U)]
        for u in range(U):
            qs_ref[rows[u], :] = q[u]
            acc_ref[rows[u], :] = out[u]
            sf_ref[grp * U + u] = kf[u]
            sb_ref[grp * U + u] = kb[u]
        return carry

    lax.fori_loop(0, n // U, local, 0)

    def fwd(c, S):
        kv = sf_ref[c]
        sf_ref[c] = S
        return S * g_chunk + kv

    lax.fori_loop(0, n, fwd, jnp.zeros((C, C), F32))

    def bwd(i, S):
        c = n - 1 - i
        kv = sb_ref[c]
        sb_ref[c] = S
        return S * g_chunk + kv

    lax.fori_loop(0, n, bwd, jnp.zeros((C, C), F32))

    def cross(grp, carry):
        rows = rows_of(grp)
        q = [qs_ref[r, :] for r in rows]
        y = [acc_ref[rows[u], :] + _dot(jnp.concatenate([q[u] * sc_q_prev, q[u] * sc_q_next], axis=1),
                                        jnp.concatenate([sf_ref[grp * U + u], sb_ref[grp * U + u]], axis=0))
             for u in range(U)]
        for u in range(U):
            mu = jnp.mean(y[u], axis=-1, keepdims=True)
            d = y[u] - mu
            var = jnp.mean(d * d, axis=-1, keepdims=True)
            yn = d * lax.rsqrt(var + GN_EPS) * gg + gb
            o_ref[rows[u], :] = _silu(gt_ref[rows[u], :].astype(F32)) * yn
        return carry

    lax.fori_loop(0, n // U, cross, 0)


def _retention(z_ret, cos, sin, logg, gn_g, gn_b):
    b, s, _ = z_ret.shape
    C = RET_CHUNK
    H = RET_HEADS
    assert C == LANES and s % (C * RET_GROUP) == 0
    n = s // C
    blk = lambda off: pl.BlockSpec((None, s, C), lambda i, h, off=off: (i, 0, off + h))
    return pl.pallas_call(
        functools.partial(_ret_body, qscale=float(C) ** -0.5),
        grid=(b, H),
        in_specs=[blk(0), blk(H), blk(2 * H), blk(3 * H),
                  pl.BlockSpec((s, C), lambda i, h: (0, 0)),
                  pl.BlockSpec((s, C), lambda i, h: (0, 0)),
                  pl.BlockSpec((None, 1, C), lambda i, h: (h, 0, 0)),
                  pl.BlockSpec((1, C), lambda i, h: (0, h)),
                  pl.BlockSpec((1, C), lambda i, h: (0, h))],
        out_specs=pl.BlockSpec((None, s, C), lambda i, h: (i, 0, h)),
        out_shape=jax.ShapeDtypeStruct((b, s, H * C), F32),
        scratch_shapes=[pltpu.VMEM((s, C), F32), pltpu.VMEM((n, C, C), F32), pltpu.VMEM((n, C, C), F32),
                        pltpu.VMEM((s, C), F32)],
        compiler_params=_cparams(("parallel", "parallel")),
        name="retention",
    )(z_ret, z_ret, z_ret, z_ret, cos, sin, logg, gn_g, gn_b)


def _prep_body(x_ref, xp_ref, xn_ref, wr_ref, ww_ref, mu_ref, wup_ref, aup_ref, w0_ref, a0_ref, gup_ref,
               kkp_ref, ka_ref, rk_ref, bd_ref,
               zr_o, r_o, v_o, kk_o, lw_o, kd_o, kka_o, bonus_o, g_o, *, width):
    t = pl.program_id(1)
    nt = pl.num_programs(1)
    W = width
    xb = x_ref[...].astype(BF16)
    ww = ww_ref[...]
    zr_o[...] = jnp.dot(xb, wr_ref[...], preferred_element_type=F32).astype(zr_o.dtype)
    z = jnp.dot(xb, ww, preferred_element_type=F32)
    ts = z.shape[0]
    row = lax.broadcasted_iota(jnp.int32, (ts, 1), 0)
    zp = jnp.dot(xp_ref[...].astype(BF16), ww, preferred_element_type=F32)
    zn = jnp.dot(xn_ref[...].astype(BF16), ww, preferred_element_type=F32)
    prev_row = jnp.where(t > 0, zp[SUBLANES - 1:SUBLANES, :], 0.0)
    next_row = jnp.where(t < nt - 1, zn[0:1, :], 0.0)
    prev = jnp.where(row == 0, prev_row, pltpu.roll(z, 1, 0))
    nxt = jnp.where(row == ts - 1, next_row, pltpu.roll(z, ts - 1, 0))
    zs = z + mu_ref[...] * (0.5 * (prev + nxt) - z)

    r = zs[:, 0:W]
    kx = zs[:, W:2 * W]
    vx = zs[:, 2 * W:3 * W]
    wd = jnp.tanh(zs[:, 3 * W:3 * W + LANES])
    ad = zs[:, 3 * W + LANES:3 * W + 2 * LANES]
    gd = _sigmoid(zs[:, 3 * W + 2 * LANES:3 * W + 3 * LANES])
    bd = bd_ref[...]

    kk = kx * kkp_ref[...]
    ssq = _dot_exact_rhs(kk * kk, bd)
    kk = kk * lax.rsqrt(jnp.maximum(ssq, 1e-24))
    ka = ka_ref[...]
    ksum = None
    for d in range(2):
        pre = w0_ref[d:d + 1, :] + _dot(wd, wup_ref[d])
        lw_o[d] = -math.exp(-0.5) * _sigmoid(pre)
        a = _sigmoid(a0_ref[d:d + 1, :] + _dot(ad, aup_ref[d]))
        kd = kx * (1.0 + (a - 1.0) * ka)
        kd_o[d] = kd.astype(kd_o.dtype)
        kka_o[d] = (kk * a).astype(kka_o.dtype)
        ksum = kd if ksum is None else ksum + kd
    r_o[...] = r.astype(r_o.dtype)
    v_o[...] = vx.astype(v_o.dtype)
    kk_o[...] = kk.astype(kk_o.dtype)
    bonus_o[...] = (_dot_exact_rhs(r * ksum * rk_ref[...], bd) * vx).astype(bonus_o.dtype)
    g_o[...] = _dot(gd, gup_ref[...]).astype(g_o.dtype)


def _proj_prep(x, p, ts):
    b, s, D = x.shape
    W = p["rwkv_width"]
    nr = p["w_ret"].shape[1]
    nt = s // ts
    hb = ts // SUBLANES
    last = s // SUBLANES - 1
    full = lambda a: pl.BlockSpec(a.shape, lambda i, t, nd=a.ndim: (0,) * nd)
    out_tok = pl.BlockSpec((None, ts, W), lambda i, t: (i, t, 0))
    out_dir = pl.BlockSpec((2, None, ts, W), lambda i, t: (0, i, t, 0))
    tok_shape = jax.ShapeDtypeStruct((b, s, W), BF16)
    dir_shape = jax.ShapeDtypeStruct((2, b, s, W), BF16)
    lw_shape = jax.ShapeDtypeStruct((2, b, s, W), F32)
    consts = [p["w_ret"], p["w_rwkv"], p["mu"], p["wup_pad"], p["aup_pad"], p["w0"], p["a0"], p["gup"],
              p["k_k"], p["k_a"], p["r_k"], p["head_ones"]]
    return pl.pallas_call(
        functools.partial(_prep_body, width=W),
        grid=(b, nt),
        in_specs=[pl.BlockSpec((None, ts, D), lambda i, t: (i, t, 0)),
                  pl.BlockSpec((None, SUBLANES, D), lambda i, t: (i, jnp.maximum(t * hb - 1, 0), 0)),
                  pl.BlockSpec((None, SUBLANES, D), lambda i, t: (i, jnp.minimum((t + 1) * hb, last), 0)),
                  ] + [full(a) for a in consts],
        out_specs=[pl.BlockSpec((None, ts, nr), lambda i, t: (i, t, 0)),
                   out_tok, out_tok, out_tok, out_dir, out_dir, out_dir, out_tok, out_tok],
        out_shape=[jax.ShapeDtypeStruct((b, s, nr), BF16),
                   tok_shape, tok_shape, tok_shape, lw_shape, dir_shape, dir_shape, tok_shape, tok_shape],
        compiler_params=_cparams(("parallel", "parallel")),
        name="proj_prep",
    )(x, x, x, *consts)


def _scan_body(rf_ref, kkf_ref, vf_ref, rb_ref, kkb_ref, vb_ref, lwf_ref, kdf_ref, kkaf_ref,
               lwb_ref, kdb_ref, kkab_ref, yf_ref, yb_ref, st_ref, *, n_pairs):
    L = SCAN_CHUNK
    H = 2 * L
    assert H == LANES
    tt = rf_ref.shape[0]
    nch = tt // L

    @pl.when(pl.program_id(1) == 0)
    def _():
        st_ref[...] = jnp.zeros_like(st_ref)

    r_refs, kk_refs, v_refs = (rf_ref, rb_ref), (kkf_ref, kkb_ref), (vf_ref, vb_ref)
    lw_refs, kd_refs, kka_refs = (lwf_ref, lwb_ref), (kdf_ref, kdb_ref), (kkaf_ref, kkab_ref)
    y_refs = (yf_ref, yb_ref)
    ii = lax.broadcasted_iota(jnp.int32, (H, H), 0)
    jj = lax.broadcasted_iota(jnp.int32, (H, H), 1)
    same = (ii < L) == (jj < L)
    strict = (jnp.logical_and(same, ii > jj), jnp.logical_and(same, ii < jj))
    incl = (jnp.logical_and(same, ii >= jj), jnp.logical_and(same, ii <= jj))
    eye = ii == jj
    li = lax.broadcasted_iota(jnp.int32, (L, L), 0)
    lj = lax.broadcasted_iota(jnp.int32, (L, L), 1)
    tri = (jnp.where(li >= lj, 1.0, 0.0).astype(BF16), jnp.where(li <= lj, 1.0, 0.0).astype(BF16))
    head0 = lax.broadcasted_iota(jnp.int32, (L, H), 1) < L

    def stack(x):
        return jnp.concatenate([jnp.where(head0, x, 0.0), jnp.where(head0, 0.0, x)], axis=0)

    chains = [(d, hp) for d in range(2) for hp in range(n_pairs)]
    P = range(len(chains))
    lanes = [slice(hp * H, (hp + 1) * H) for _, hp in chains]
    dirs = [d for d, _ in chains]

    def chunk(j, carry):
        rows = (pl.ds(pl.multiple_of(j * L, L), L), pl.ds(pl.multiple_of((nch - 1 - j) * L, L), L))
        ld = lambda refs, h: refs[dirs[h]][rows[dirs[h]], lanes[h]].astype(F32)
        lw = [ld(lw_refs, h) for h in P]
        cum = [_dot_exact_lhs(tri[dirs[h]], lw[h]) for h in P]
        tot = [jnp.sum(x, axis=0, keepdims=True) for x in lw]
        e_incl = [jnp.exp(a) for a in cum]
        e_excl = [jnp.exp(a - x) for a, x in zip(cum, lw)]
        e_inv = [jnp.exp(-a) for a in cum]
        e_rem = [jnp.exp(t_ - a) for t_, a in zip(tot, cum)]
        kk = [ld(kk_refs, h) for h in P]
        kka = [ld(kka_refs, h) for h in P]
        kd = [ld(kd_refs, h) for h in P]
        Kk = [stack(a * e) for a, e in zip(kk, e_excl)]
        R = [stack(ld(r_refs, h) * e_incl[h]) for h in P]
        B = [stack(a * e) for a, e in zip(kka, e_inv)]
        Kd = [stack(a * e) for a, e in zip(kd, e_inv)]
        Bh = [stack(a * e).T.astype(BF16) for a, e in zip(kka, e_rem)]
        Kh = [stack(a * e).T.astype(BF16) for a, e in zip(kd, e_rem)]
        V = [stack(ld(v_refs, h)).astype(BF16) for h in P]

        gram = [_dot_nt(jnp.concatenate([Kk[h], R[h]], axis=0), jnp.concatenate([B[h], Kd[h]], axis=0))
                for h in P]
        Np = [jnp.where(strict[dirs[h]], gram[h][:H, :H], 0.0) for h in P]
        AD = [_dot(jnp.concatenate([jnp.where(strict[dirs[h]], gram[h][:H, H:], 0.0).astype(BF16),
                                    jnp.where(incl[dirs[h]], gram[h][H:, H:], 0.0).astype(BF16),
                                    Kh[h]], axis=0), V[h]) for h in P]
        CB = [jnp.concatenate([jnp.where(incl[dirs[h]], gram[h][H:, :H], 0.0).astype(BF16), Bh[h]], axis=0)
              for h in P]
        X = [jnp.concatenate([Kk[h], AD[h][:H]], axis=1) for h in P]
        n_fac = L.bit_length() - 1
        for f in range(n_fac):
            Nb = [a.astype(BF16) for a in Np]
            NX = [_dot(Nb[h], X[h]) for h in P]
            if f < n_fac - 1:
                Np = [_dot(Nb[h], Nb[h]) for h in P]
            X = [x - nx if f == 0 else x + nx for x, nx in zip(X, NX)]
        CBX = [_dot(CB[h], X[h]) for h in P]
        QM = []
        for h in P:
            Qh = R[h] - CBX[h][:H, :H]
            M = jnp.where(eye, jnp.exp(tot[h]), 0.0) - CBX[h][H:, :H]
            QM.append(_dot(jnp.concatenate([Qh, M], axis=0), st_ref[h]))
        for h in P:
            Ys = QM[h][:H] + (AD[h][H:2 * H] - CBX[h][:H, H:])
            st_ref[h] = QM[h][H:] + (AD[h][2 * H:] - CBX[h][H:, H:])
            y_refs[dirs[h]][rows[dirs[h]], lanes[h]] = Ys[:L] + Ys[L:]
        return carry

    lax.fori_loop(0, nch, chunk, 0)


def _rwkv_scan(r, kk, v, lw, kd, kka, tt):
    b, s, W = r.shape
    nt = s // tt
    n_pairs = W // LANES
    tok_f = pl.BlockSpec((None, tt, W), lambda i, t: (i, t, 0))
    tok_b = pl.BlockSpec((None, tt, W), lambda i, t: (i, nt - 1 - t, 0))
    dir_f = pl.BlockSpec((None, None, tt, W), lambda i, t: (0, i, t, 0))
    dir_b = pl.BlockSpec((None, None, tt, W), lambda i, t: (1, i, nt - 1 - t, 0))
    out = jax.ShapeDtypeStruct((b, s, W), F32)
    return pl.pallas_call(
        functools.partial(_scan_body, n_pairs=n_pairs),
        grid=(b, nt),
        in_specs=[tok_f, tok_f, tok_f, tok_b, tok_b, tok_b, dir_f, dir_f, dir_f, dir_b, dir_b, dir_b],
        out_specs=[tok_f, tok_b],
        out_shape=[out, out],
        scratch_shapes=[pltpu.VMEM((2 * n_pairs, LANES, LANES), F32)],
        compiler_params=_cparams(("parallel", "arbitrary")),
        name="rwkv_scan",
    )(r, kk, v, r, kk, v, lw, kd, kka, lw, kd, kka)


def _mix_body(yf_ref, yb_ref, bonus_ref, g_ref, ret_ref, x_ref, gng_ref, gnb_ref, avg_ref, wo1_ref, wo2_ref,
              l1g_ref, l1b_ref, rwt_ref, rb_ref, x1_ref, xp_ref, idx_ref, gate_ref, cnt_ref, *, alpha):
    y = yf_ref[...] + yb_ref[...]
    avg = avg_ref[...]
    mu = _dot_exact_rhs(y, avg)
    dl = y - mu
    var = jnp.dot((dl * dl).astype(BF16), avg, preferred_element_type=F32)
    yn = dl * lax.rsqrt(var + RWKV_GN_EPS) * gng_ref[...] + gnb_ref[...]
    rw = (yn + bonus_ref[...]) * g_ref[...]
    m = _dot(ret_ref[...], wo1_ref[...]) + _dot(rw, wo2_ref[...])
    x1 = _layer_norm(alpha * x_ref[...] + m, l1g_ref[...], l1b_ref[...])
    x1_ref[...] = x1
    xp_ref[...] = _pack_halves(x1)

    scores = _sigmoid(_dot_nt(rwt_ref[...], x1))
    E, tm = scores.shape
    GS = E // N_GROUPS
    NEG = -jnp.inf
    biased = scores + rb_ref[...]
    rowi = lax.broadcasted_iota(jnp.int32, (E, tm), 0)
    ri = lax.broadcasted_iota(jnp.int32, (GS, tm), 0)
    gs_rows = []
    for gi in range(N_GROUPS):
        blk = biased[gi * GS:(gi + 1) * GS, :]
        m1 = jnp.max(blk, axis=0, keepdims=True)
        i1 = jnp.min(jnp.where(blk == m1, ri, GS), axis=0, keepdims=True)
        m2 = jnp.max(jnp.where(ri == i1, NEG, blk), axis=0, keepdims=True)
        gs_rows.append(m1 + m2)
    cur = jnp.concatenate(gs_rows, axis=0)
    gidx = lax.broadcasted_iota(jnp.int32, (N_GROUPS, tm), 0)
    top_groups = []
    for _ in range(TOPK_GROUPS):
        mx = jnp.max(cur, axis=0, keepdims=True)
        ix = jnp.min(jnp.where(cur == mx, gidx, N_GROUPS), axis=0, keepdims=True)
        top_groups.append(ix)
        cur = jnp.where(gidx == ix, NEG, cur)
    blocks = []
    for gi in range(N_GROUPS):
        keep = top_groups[0] == gi
        for ix in top_groups[1:]:
            keep = jnp.logical_or(keep, ix == gi)
        blocks.append(jnp.where(keep, biased[gi * GS:(gi + 1) * GS, :], NEG))
    allowed = jnp.concatenate(blocks, axis=0)
    cur = allowed
    idxs, sels = [], []
    for _ in range(TOP_K):
        mx = jnp.max(cur, axis=0, keepdims=True)
        ix = jnp.min(jnp.where(cur == mx, rowi, E), axis=0, keepdims=True)
        hit = rowi == ix
        sels.append(jnp.sum(jnp.where(hit, scores, 0.0), axis=0, keepdims=True))
        idxs.append(ix)
        cur = jnp.where(hit, NEG, cur)
    chosen = jnp.where(jnp.logical_and(allowed > NEG, cur == NEG), 1.0, 0.0)
    sel = jnp.concatenate(sels, axis=0)
    idx_ref[...] = jnp.concatenate(idxs, axis=0)
    gate_ref[...] = sel / jnp.sum(sel, axis=0, keepdims=True) * ROUTE_SCALE

    @pl.when(pl.program_id(0) == 0)
    def _():
        cnt_ref[...] = jnp.zeros_like(cnt_ref)

    cnt_ref[...] += jnp.sum(chosen, axis=1, keepdims=True)


def _mix_out(y_f, y_b, bonus, g, ret_out, x2d, p, tm):
    T, D = x2d.shape
    W = bonus.shape[1]
    Wr = ret_out.shape[1]
    E = p["router_wt"].shape[0]
    full = lambda a: pl.BlockSpec(a.shape, lambda i, nd=a.ndim: (0,) * nd)
    consts = [p["gn_g"], p["gn_b"], p["head_avg"], p["wo_ret"], p["wo_rwkv"], p["ln1_g"], p["ln1_b"],
              p["router_wt"], p["router_b"]]
    return pl.pallas_call(
        functools.partial(_mix_body, alpha=p["alpha"]),
        grid=(T // tm,),
        in_specs=[pl.BlockSpec((tm, W), lambda i: (i, 0)),
                  pl.BlockSpec((tm, W), lambda i: (i, 0)),
                  pl.BlockSpec((tm, W), lambda i: (i, 0)),
                  pl.BlockSpec((tm, W), lambda i: (i, 0)),
                  pl.BlockSpec((tm, Wr), lambda i: (i, 0)),
                  pl.BlockSpec((tm, D), lambda i: (i, 0))] + [full(a) for a in consts],
        out_specs=[pl.BlockSpec((tm, D), lambda i: (i, 0)),
                   pl.BlockSpec((tm, D // 2), lambda i: (i, 0)),
                   pl.BlockSpec((TOP_K, tm), lambda i: (0, i)),
                   pl.BlockSpec((TOP_K, tm), lambda i: (0, i)),
                   pl.BlockSpec((E, LANES), lambda i: (0, 0))],
        out_shape=[jax.ShapeDtypeStruct((T, D), F32),
                   jax.ShapeDtypeStruct((T, D // 2), jnp.int32),
                   jax.ShapeDtypeStruct((TOP_K, T), jnp.int32),
                   jax.ShapeDtypeStruct((TOP_K, T), F32),
                   jax.ShapeDtypeStruct((E, LANES), F32)],
        compiler_params=_cparams(("arbitrary",)),
        name="mix_out",
    )(y_f, y_b, bonus, g, ret_out, x2d, *consts)


def _plan_body(idx_ref, pstart_ref, upper_ref, dest_ref, run_ref):
    @pl.when(pl.program_id(0) == 0)
    def _():
        run_ref[...] = jnp.zeros_like(run_ref)

    E = pstart_ref.shape[0]
    tm = idx_ref.shape[1]
    rowi = lax.broadcasted_iota(jnp.int32, (E, tm), 0)
    hits = [rowi == idx_ref[k:k + 1, :] for k in range(TOP_K)]
    member = jnp.zeros((E, tm), F32)
    for hit in hits:
        member = jnp.where(hit, 1.0, member)
    before = jnp.dot(member.astype(BF16), upper_ref[...], preferred_element_type=F32)
    slot = pstart_ref[...] + run_ref[:, 0:1] + before
    dest_ref[...] = jnp.concatenate(
        [jnp.sum(jnp.where(hit, slot, 0.0), axis=0, keepdims=True) for hit in hits], axis=0).astype(jnp.int32)
    run_ref[...] += jnp.sum(member, axis=1, keepdims=True)


def _slot_plan(idx_t, pad_start, tm):
    T = idx_t.shape[1]
    E = pad_start.shape[0]
    upper = (jnp.arange(tm)[:, None] < jnp.arange(tm)[None, :]).astype(BF16)
    return pl.pallas_call(
        _plan_body,
        grid=(T // tm,),
        in_specs=[pl.BlockSpec((TOP_K, tm), lambda i: (0, i)),
                  pl.BlockSpec((E, 1), lambda i: (0, 0)),
                  pl.BlockSpec((tm, tm), lambda i: (0, 0))],
        out_specs=pl.BlockSpec((TOP_K, tm), lambda i: (0, i)),
        out_shape=jax.ShapeDtypeStruct((TOP_K, T), jnp.int32),
        scratch_shapes=[pltpu.VMEM((E, LANES), F32)],
        compiler_params=_cparams(("arbitrary",)),
        name="slot_plan",
    )(idx_t, pad_start.reshape(E, 1).astype(F32), upper)


_UPPER_HALF = -65536


def _bf16_bits(x):
    return lax.bitcast_convert_type(x.astype(BF16).astype(F32), jnp.int32)


def _pack_halves(x):
    c = x.shape[1] // 2
    return _bf16_bits(x[:, :c]) | (jnp.right_shift(_bf16_bits(x[:, c:]), 16) & 0xFFFF)


def _unpack_halves(u):
    hi = lax.bitcast_convert_type(u & _UPPER_HALF, F32)
    lo = lax.bitcast_convert_type(jnp.left_shift(u, 16), F32)
    return hi, lo


SC_WINDOW = 64


def _sc_dispatch(xp, dest, n_slots):
    T, C = xp.shape
    n_win = T // SC_WINDOW
    mesh = plsc.VectorSubcoreMesh(core_axis_name="core", subcore_axis_name="subcore")

    @pl.kernel(out_type=jax.ShapeDtypeStruct((n_slots, C), xp.dtype), mesh=mesh, scratch_types=[])
    def scatter_rows(x_hbm, i_hbm, o_hbm):
        def body(x_vmem, i_vmem):
            for k in range(TOP_K):
                pltpu.sync_copy(x_vmem, o_hbm.at[i_vmem.at[k]])

        pltpu.emit_pipeline(
            body,
            grid=(n_win,),
            in_specs=[pl.BlockSpec((SC_WINDOW, C), lambda i: (i, 0)),
                      pl.BlockSpec((None, TOP_K, SC_WINDOW), lambda i: (i, 0, 0))],
            out_specs=[],
            core_axis_name=("core", "subcore"),
            dimension_semantics=(pltpu.PARALLEL,),
        )(x_hbm, i_hbm)

    idx = dest.reshape(TOP_K, n_win, SC_WINDOW).transpose(1, 0, 2)
    return scatter_rows(xp, idx)


def _sc_gather(ys, dest):
    n_idx = dest.shape[0] * dest.shape[1]
    C = ys.shape[1]
    mesh = plsc.VectorSubcoreMesh(core_axis_name="core", subcore_axis_name="subcore")

    @pl.kernel(out_type=jax.ShapeDtypeStruct((n_idx, C), ys.dtype), mesh=mesh, scratch_types=[])
    def gather_rows(y_hbm, i_hbm, o_hbm):
        def body(i_vmem, o_vmem):
            pltpu.sync_copy(y_hbm.at[i_vmem.at[0]], o_vmem)

        pltpu.emit_pipeline(
            body,
            grid=(n_idx // SC_WINDOW,),
            in_specs=[pl.BlockSpec((1, SC_WINDOW), lambda i: (i, 0))],
            out_specs=[pl.BlockSpec((SC_WINDOW, C), lambda i: (i, 0))],
            core_axis_name=("core", "subcore"),
            dimension_semantics=(pltpu.PARALLEL,),
        )(i_hbm, o_hbm)

    return gather_rows(ys, dest.reshape(n_idx // SC_WINDOW, SC_WINDOW))


MOE_IN_SLOTS = 4
MOE_OUT_SLOTS = 2


def _moe_body(nb_ref, b0_ref, cnt_ref, nu_ref, xs_hbm, wg_ref, wu_ref, wd_ref, ys_hbm,
              xbuf, obuf, wgb_ref, wub_ref, wdb_ref, in_sem, out_sem):
    G = MOE_BLOCK
    e = pl.program_id(0)
    nb = nb_ref[e]
    b0 = b0_ref[e]
    n_used = nu_ref[0]

    def in_copy(g):
        slot = lax.rem(g, MOE_IN_SLOTS)
        return pltpu.make_async_copy(xs_hbm.at[pl.ds(pl.multiple_of(g * G, G), G), :], xbuf.at[slot],
                                     in_sem.at[slot])

    def out_copy(g):
        slot = lax.rem(g, MOE_OUT_SLOTS)
        return pltpu.make_async_copy(obuf.at[slot], ys_hbm.at[pl.ds(pl.multiple_of(g * G, G), G), :],
                                     out_sem.at[slot])

    @pl.when(e == 0)
    def _():
        for g in range(MOE_IN_SLOTS - 1):
            @pl.when(g < n_used)
            def _():
                in_copy(g).start()

    @pl.when(nb > 0)
    def _():
        wgb_ref[...] = wg_ref[...].astype(BF16)
        wub_ref[...] = wu_ref[...].astype(BF16)
        wdb_ref[...] = wd_ref[...].astype(BF16)
        row = lax.broadcasted_iota(jnp.int32, (G, xbuf.shape[2]), 0)
        c = xbuf.shape[2]

        def step(j, carry):
            g = b0 + j
            in_copy(g).wait()

            @pl.when(g + (MOE_IN_SLOTS - 1) < n_used)
            def _():
                in_copy(g + (MOE_IN_SLOTS - 1)).start()

            @pl.when(g >= MOE_OUT_SLOTS)
            def _():
                out_copy(g - MOE_OUT_SLOTS).wait()

            u = jnp.where(row < cnt_ref[e] - j * G, xbuf[lax.rem(g, MOE_IN_SLOTS)], 0)
            x_hi, x_lo = _unpack_halves(u)
            gate = _dot(x_hi, wgb_ref[:c, :]) + _dot(x_lo, wgb_ref[c:, :])
            up = _dot(x_hi, wub_ref[:c, :]) + _dot(x_lo, wub_ref[c:, :])
            obuf[lax.rem(g, MOE_OUT_SLOTS)] = _pack_halves(_dot(_silu(gate) * up, wdb_ref[...]))
            out_copy(g).start()
            return carry

        lax.fori_loop(0, nb, step, 0)

    @pl.when(e == pl.num_programs(0) - 1)
    def _():
        for back in range(MOE_OUT_SLOTS, 0, -1):
            @pl.when(n_used >= back)
            def _():
                out_copy(n_used - back).wait()


def _moe_ffn(xs, blocks_per_expert, first_block, counts, n_used, w_gate, w_up, w_down):
    P, C = xs.shape
    G = MOE_BLOCK
    E, D, De = w_gate.shape
    wspec = lambda shape: pl.BlockSpec((None,) + shape, lambda e, nb, b0, cnt, nu: (e, 0, 0))
    grid_spec = pltpu.PrefetchScalarGridSpec(
        num_scalar_prefetch=4,
        grid=(E,),
        in_specs=[pl.BlockSpec(memory_space=pl.ANY), wspec((D, De)), wspec((D, De)), wspec((De, D))],
        out_specs=pl.BlockSpec(memory_space=pl.ANY),
        scratch_shapes=[pltpu.VMEM((MOE_IN_SLOTS, G, C), jnp.int32), pltpu.VMEM((MOE_OUT_SLOTS, G, C), jnp.int32),
                        pltpu.VMEM((D, De), BF16), pltpu.VMEM((D, De), BF16), pltpu.VMEM((De, D), BF16),
                        pltpu.SemaphoreType.DMA((MOE_IN_SLOTS,)), pltpu.SemaphoreType.DMA((MOE_OUT_SLOTS,))],
    )
    return pl.pallas_call(
        _moe_body,
        grid_spec=grid_spec,
        out_shape=jax.ShapeDtypeStruct((P, C), jnp.int32),
        compiler_params=_cparams(("arbitrary",)),
        name="moe_ffn",
    )(blocks_per_expert, first_block, counts, n_used, xs, w_gate, w_up, w_down)


def _comb_body(yg_ref, x1_ref, gt_ref, sg_ref, su_ref, sd_ref, l2g_ref, l2b_ref, o_ref, *, alpha):
    x1 = x1_ref[...]
    xb = x1.astype(BF16)
    shared = _dot(_silu(_dot(xb, sg_ref[...])) * _dot(xb, su_ref[...]), sd_ref[...])
    gt = gt_ref[...]
    acc_hi = acc_lo = None
    for k in range(TOP_K):
        hi, lo = _unpack_halves(yg_ref[k])
        gk = gt[:, k:k + 1]
        acc_hi = gk * hi if acc_hi is None else acc_hi + gk * hi
        acc_lo = gk * lo if acc_lo is None else acc_lo + gk * lo
    routed = jnp.concatenate([acc_hi, acc_lo], axis=1)
    o_ref[...] = _layer_norm(alpha * x1 + (routed + shared), l2g_ref[...], l2b_ref[...])


def _combine(yg, x1, gates_t, p, tm):
    T, D = x1.shape
    full = lambda a: pl.BlockSpec(a.shape, lambda i, nd=a.ndim: (0,) * nd)
    consts = [p["sh_gate"], p["sh_up"], p["sh_down"], p["ln2_g"], p["ln2_b"]]
    return pl.pallas_call(
        functools.partial(_comb_body, alpha=p["alpha"]),
        grid=(T // tm,),
        in_specs=[pl.BlockSpec((TOP_K, tm, D // 2), lambda i: (0, i, 0)),
                  pl.BlockSpec((tm, D), lambda i: (i, 0)),
                  pl.BlockSpec((tm, TOP_K), lambda i: (i, 0))] + [full(a) for a in consts],
        out_specs=pl.BlockSpec((tm, D), lambda i: (i, 0)),
        out_shape=jax.ShapeDtypeStruct((T, D), F32),
        compiler_params=_cparams(("parallel",)),
        name="combine",
    )(yg, x1, gates_t, *consts)


def _segment_layout(counts, n_tokens):
    G = MOE_BLOCK
    E = counts.shape[0]
    n_blocks = -(-(n_tokens * TOP_K + E * (G - 1)) // G)
    padded = (counts + G - 1) // G * G
    pad_end = jnp.cumsum(padded)
    pad_start = pad_end - padded
    n_used = pad_end[-1:] // G
    return (pad_start, (padded // G).astype(jnp.int32), (pad_start // G).astype(jnp.int32),
            n_used.astype(jnp.int32), n_blocks)


def _rotary_tables(s, d):
    inv = ROPE_BASE ** (-jnp.arange(0, d, 2, dtype=F32) / d)
    ang = jnp.arange(s, dtype=F32)[:, None] * inv[None, :]
    cos = jnp.cos(ang)
    sin = jnp.sin(ang)
    return jnp.concatenate([cos, cos], axis=-1), jnp.concatenate([-sin, sin], axis=-1)


def _layer_params(l, depth, w_in, ret_gn_g, ret_gn_b, rwkv_mu, rwkv_w0, rwkv_w_up, rwkv_a0, rwkv_a_up,
                  rwkv_g_up, rwkv_k_k, rwkv_k_a, rwkv_r_k, rwkv_gn_g, rwkv_gn_b, w_out, ln1_g, ln1_b,
                  router_w, router_bias, exp_w_gate, exp_w_up, exp_w_down, sh_w_gate, sh_w_up,
                  sh_w_down, ln2_g, ln2_b):
    ret_w = ret_gn_g.shape[-1]
    W = rwkv_gn_g.shape[-1]
    n_heads, hd = rwkv_r_k.shape[-2:]
    rank_w = rwkv_w_up.shape[2]
    rank_a = rwkv_a_up.shape[2]
    assert rank_w * 2 == LANES and rank_a * 2 == LANES and rwkv_g_up.shape[1] == LANES
    assert hd * 2 == LANES and SCAN_CHUNK == hd
    row = lambda a: a.reshape(1, -1).astype(F32)
    zw = jnp.zeros((rank_w, W), F32)
    za = jnp.zeros((rank_a, W), F32)
    head_id = jnp.arange(W) // hd
    same_head = (head_id[:, None] == head_id[None, :])
    wi = w_in[l]
    return dict(
        alpha=float((2 * depth) ** 0.25),
        rwkv_width=W,
        w_ret=wi[:, :4 * ret_w].astype(BF16),
        w_rwkv=wi[:, 4 * ret_w:].astype(BF16),
        ret_gn_g=row(ret_gn_g[l]), ret_gn_b=row(ret_gn_b[l]),
        logg=jnp.broadcast_to(
            jnp.log1p(-jnp.exp2(-5.0 - jnp.arange(RET_HEADS, dtype=F32)))[:, None, None],
            (RET_HEADS, 1, LANES)),
        mu=row(rwkv_mu[l]),
        wup_pad=jnp.stack([jnp.concatenate([rwkv_w_up[l, 0], zw], 0),
                           jnp.concatenate([zw, rwkv_w_up[l, 1]], 0)]).astype(BF16),
        aup_pad=jnp.stack([jnp.concatenate([rwkv_a_up[l, 0], za], 0),
                           jnp.concatenate([za, rwkv_a_up[l, 1]], 0)]).astype(BF16),
        w0=rwkv_w0[l].astype(F32), a0=rwkv_a0[l].astype(F32),
        gup=rwkv_g_up[l].astype(BF16),
        k_k=row(rwkv_k_k[l]), k_a=row(rwkv_k_a[l]), r_k=row(rwkv_r_k[l]),
        head_ones=same_head.astype(BF16),
        head_avg=(same_head.astype(F32) / hd).astype(BF16),
        gn_g=row(rwkv_gn_g[l]), gn_b=row(rwkv_gn_b[l]),
        wo_ret=w_out[l, :ret_w].astype(BF16), wo_rwkv=w_out[l, ret_w:].astype(BF16),
        ln1_g=row(ln1_g[l]), ln1_b=row(ln1_b[l]),
        router_wt=router_w[l].T.astype(BF16), router_b=router_bias[l].reshape(-1, 1).astype(F32),
        exp_gate=exp_w_gate[l], exp_up=exp_w_up[l], exp_down=exp_w_down[l],
        sh_gate=sh_w_gate[l].astype(BF16), sh_up=sh_w_up[l].astype(BF16), sh_down=sh_w_down[l].astype(BF16),
        ln2_g=row(ln2_g[l]), ln2_b=row(ln2_b[l]),
    )


def _pick(n, pref):
    t = min(n, pref)
    while n % t:
        t //= 2
    return t


def _layer(x, p):
    b, s, D = x.shape
    T = b * s
    x2d = x.reshape(T, D)
    tm = _pick(T, 256)
    z_ret, r, v, kk, lw, kd, kka, bonus, g = _proj_prep(x, p, _pick(s, 256))
    cos, sin = _rotary_tables(s, RET_CHUNK)
    ret_out = _retention(z_ret, cos, sin, p["logg"], p["ret_gn_g"], p["ret_gn_b"])
    y_f, y_b = _rwkv_scan(r, kk, v, lw, kd, kka, _pick(s, 512))
    W = p["rwkv_width"]
    x1, xp, idx_t, gates, cnt = _mix_out(y_f.reshape(T, W), y_b.reshape(T, W), bonus.reshape(T, W),
                                         g.reshape(T, W), ret_out.reshape(T, -1), x2d, p, tm)
    counts = cnt[:, 0].astype(jnp.int32)
    pad_start, blocks_per_expert, first_block, n_used, n_blocks = _segment_layout(counts, T)
    dest = _slot_plan(idx_t, pad_start, _pick(T, 512))
    xs = _sc_dispatch(xp, dest, n_blocks * MOE_BLOCK)
    ys = _moe_ffn(xs, blocks_per_expert, first_block, counts, n_used, p["exp_gate"], p["exp_up"], p["exp_down"])
    yg = _sc_gather(ys, dest).reshape(TOP_K, T, D // 2)
    out = _combine(yg, x1, gates.T, p, _pick(T, 256))
    return out.reshape(b, s, D)


def kernel(x_prompt, x_sample, w_in, ret_gn_g, ret_gn_b, rwkv_mu, rwkv_w0, rwkv_w_up, rwkv_a0, rwkv_a_up,
           rwkv_g_up, rwkv_k_k, rwkv_k_a, rwkv_r_k, rwkv_gn_g, rwkv_gn_b, w_out, ln1_g, ln1_b, router_w,
           router_bias, exp_w_gate, exp_w_up, exp_w_down, sh_w_gate, sh_w_up, sh_w_down, ln2_g, ln2_b):
    weights = (w_in, ret_gn_g, ret_gn_b, rwkv_mu, rwkv_w0, rwkv_w_up, rwkv_a0, rwkv_a_up, rwkv_g_up,
               rwkv_k_k, rwkv_k_a, rwkv_r_k, rwkv_gn_g, rwkv_gn_b, w_out, ln1_g, ln1_b, router_w,
               router_bias, exp_w_gate, exp_w_up, exp_w_down, sh_w_gate, sh_w_up, sh_w_down, ln2_g, ln2_b)
    depth = w_in.shape[0]
    layers = [_layer_params(l, depth, *weights) for l in range(depth)]

    def trunk(x):
        for p in layers:
            x = _layer(x, p)
        return x

    return trunk(x_prompt), trunk(x_sample)
```

```python
import functools
import math

import jax
import jax.numpy as jnp
from jax import lax
from jax.experimental import pallas as pl
from jax.experimental.pallas import tpu as pltpu
from jax.experimental.pallas import tpu_sc as plsc

F32 = jnp.float32
BF16 = jnp.bfloat16

RET_HEADS = 4
RET_CHUNK = 128
ROPE_BASE = 10000.0
TOP_K = 8
N_GROUPS = 8
TOPK_GROUPS = 4
ROUTE_SCALE = 2.5
MOE_BLOCK = 256
LN_EPS = 1e-5
GN_EPS = 1e-5
RWKV_GN_EPS = 64e-5

LANES = 128
SUBLANES = 8
VMEM_LIMIT_BYTES = 56 * 1024 * 1024

SCAN_CHUNK = 64


def _cparams(semantics):
    return pltpu.CompilerParams(dimension_semantics=semantics, vmem_limit_bytes=VMEM_LIMIT_BYTES)


def _dot(a, b):
    return jnp.dot(a.astype(BF16), b.astype(BF16), preferred_element_type=F32)


def _dot_nt(a, b):
    return lax.dot_general(a.astype(BF16), b.astype(BF16), (((1,), (1,)), ((), ())),
                           preferred_element_type=F32)


def _dot_exact_rhs(x, w_bf16):
    hi = x.astype(BF16)
    mid = (x - hi.astype(F32)).astype(BF16)
    return (jnp.dot(hi, w_bf16, preferred_element_type=F32)
            + jnp.dot(mid, w_bf16, preferred_element_type=F32))


def _dot_exact_lhs(w_bf16, x):
    hi = x.astype(BF16)
    mid = (x - hi.astype(F32)).astype(BF16)
    return (jnp.dot(w_bf16, hi, preferred_element_type=F32)
            + jnp.dot(w_bf16, mid, preferred_element_type=F32))


def _sigmoid(x):
    return 1.0 / (1.0 + jnp.exp(-x))


def _silu(x):
    return x * _sigmoid(x)


def _layer_norm(h, g, b):
    mu = jnp.mean(h, axis=-1, keepdims=True)
    d = h - mu
    var = jnp.mean(d * d, axis=-1, keepdims=True)
    return d * lax.rsqrt(var + LN_EPS) * g + b


RET_GROUP = 4


def _ret_body(q_ref, k_ref, v_ref, gt_ref, cos_ref, sin_ref, lg_ref, gg_ref, gb_ref, o_ref,
              qs_ref, sf_ref, sb_ref, acc_ref, *, qscale):
    C = RET_CHUNK
    U = RET_GROUP
    s = q_ref.shape[0]
    n = s // C
    lg = lg_ref[...]
    pos = lax.broadcasted_iota(jnp.int32, (C, C), 0).astype(F32)
    col = lax.broadcasted_iota(jnp.int32, (C, C), 1).astype(F32)
    sc_q_prev = jnp.exp((pos + 1.0) * lg)
    sc_k_fwd = jnp.exp((C - 1.0 - pos) * lg)
    sc_k_bwd = jnp.exp(pos * lg)
    sc_q_next = jnp.exp((C - pos) * lg)
    g_chunk = jnp.exp(float(C) * lg)
    decay = jnp.exp(jnp.abs(pos - col) * lg)
    gg = gg_ref[...]
    gb = gb_ref[...]

    def rows_of(grp):
        return [pl.ds(pl.multiple_of((grp * U + u) * C, C), C) for u in range(U)]

    def rot(x, r):
        x = x.astype(F32)
        return x * cos_ref[r, :] + pltpu.roll(x, C // 2, 1) * sin_ref[r, :]

    def local(grp, carry):
        rows = rows_of(grp)
        q = [rot(q_ref[r, :], r) * qscale for r in rows]
        k = [rot(k_ref[r, :], r) for r in rows]
        vb = [v_ref[r, :].astype(BF16) for r in rows]
        sc = [_dot_nt(q[u], k[u]) * decay for u in range(U)]
        out = [_dot(sc[u], vb[u]) for u in range(U)]
        kf = [_dot((k[u] * sc_k_fwd).T, vb[u]) for u in range(U)]
        kb = [_dot((k[u] * sc_k_bwd).T, vb[u]) for u in range(U)]
        for u in range(U):
            qs_ref[rows[u], :] = q[u]
            acc_ref[rows[u], :] = out[u]
            sf_ref[grp * U + u] = kf[u]
            sb_ref[grp * U + u] = kb[u]
        return carry

    lax.fori_loop(0, n // U, local, 0)

    def fwd(c, S):
        kv = sf_ref[c]
        sf_ref[c] = S
        return S * g_chunk + kv

    lax.fori_loop(0, n, fwd, jnp.zeros((C, C), F32))

    def bwd(i, S):
        c = n - 1 - i
        kv = sb_ref[c]
        sb_ref[c] = S
        return S * g_chunk + kv

    lax.fori_loop(0, n, bwd, jnp.zeros((C, C), F32))

    def cross(grp, carry):
        rows = rows_of(grp)
        q = [qs_ref[r, :] for r in rows]
        y = [acc_ref[rows[u], :] + _dot(jnp.concatenate([q[u] * sc_q_prev, q[u] * sc_q_next], axis=1),
                                        jnp.concatenate([sf_ref[grp * U + u], sb_ref[grp * U + u]], axis=0))
             for u in range(U)]
        for u in range(U):
            mu = jnp.mean(y[u], axis=-1, keepdims=True)
            d = y[u] - mu
            var = jnp.mean(d * d, axis=-1, keepdims=True)
            yn = d * lax.rsqrt(var + GN_EPS) * gg + gb
            o_ref[rows[u], :] = _silu(gt_ref[rows[u], :].astype(F32)) * yn
        return carry

    lax.fori_loop(0, n // U, cross, 0)


def _retention(z_ret, cos, sin, logg, gn_g, gn_b):
    b, s, _ = z_ret.shape
    C = RET_CHUNK
    H = RET_HEADS
    assert C == LANES and s % (C * RET_GROUP) == 0
    n = s // C
    blk = lambda off: pl.BlockSpec((None, s, C), lambda i, h, off=off: (i, 0, off + h))
    return pl.pallas_call(
        functools.partial(_ret_body, qscale=float(C) ** -0.5),
        grid=(b, H),
        in_specs=[blk(0), blk(H), blk(2 * H), blk(3 * H),
                  pl.BlockSpec((s, C), lambda i, h: (0, 0)),
                  pl.BlockSpec((s, C), lambda i, h: (0, 0)),
                  pl.BlockSpec((None, 1, C), lambda i, h: (h, 0, 0)),
                  pl.BlockSpec((1, C), lambda i, h: (0, h)),
                  pl.BlockSpec((1, C), lambda i, h: (0, h))],
        out_specs=pl.BlockSpec((None, s, C), lambda i, h: (i, 0, h)),
        out_shape=jax.ShapeDtypeStruct((b, s, H * C), F32),
        scratch_shapes=[pltpu.VMEM((s, C), F32), pltpu.VMEM((n, C, C), F32), pltpu.VMEM((n, C, C), F32),
                        pltpu.VMEM((s, C), F32)],
        compiler_params=_cparams(("parallel", "parallel")),
        name="retention",
    )(z_ret, z_ret, z_ret, z_ret, cos, sin, logg, gn_g, gn_b)


def _prep_body(x_ref, xp_ref, xn_ref, wr_ref, ww_ref, mu_ref, wup_ref, aup_ref, w0_ref, a0_ref, gup_ref,
               kkp_ref, ka_ref, rk_ref, bd_ref,
               zr_o, r_o, v_o, kk_o, lw_o, kd_o, kka_o, bonus_o, g_o, *, width):
    t = pl.program_id(1)
    nt = pl.num_programs(1)
    W = width
    ts = x_ref.shape[0]
    x_all = jnp.concatenate([x_ref[...], xp_ref[...], xn_ref[...]], axis=0).astype(BF16)
    zr_o[...] = jnp.dot(x_all[:ts], wr_ref[...], preferred_element_type=F32).astype(zr_o.dtype)
    z_all = jnp.dot(x_all, ww_ref[...], preferred_element_type=F32)
    z = z_all[:ts]
    row = lax.broadcasted_iota(jnp.int32, (ts, 1), 0)
    prev_row = jnp.where(t > 0, z_all[ts + SUBLANES - 1:ts + SUBLANES, :], 0.0)
    next_row = jnp.where(t < nt - 1, z_all[ts + SUBLANES:ts + SUBLANES + 1, :], 0.0)
    prev = jnp.where(row == 0, prev_row, pltpu.roll(z, 1, 0))
    nxt = jnp.where(row == ts - 1, next_row, pltpu.roll(z, ts - 1, 0))
    zs = z + mu_ref[...] * (0.5 * (prev + nxt) - z)

    r = zs[:, 0:W]
    kx = zs[:, W:2 * W]
    vx = zs[:, 2 * W:3 * W]
    wd = jnp.tanh(zs[:, 3 * W:3 * W + LANES])
    ad = zs[:, 3 * W + LANES:3 * W + 2 * LANES]
    gd = _sigmoid(zs[:, 3 * W + 2 * LANES:3 * W + 3 * LANES])
    bd = bd_ref[...]

    kk = kx * kkp_ref[...]
    ssq = jnp.dot((kk * kk).astype(BF16), bd, preferred_element_type=F32)
    kk = kk * lax.rsqrt(jnp.maximum(ssq, 1e-24))
    ka = ka_ref[...]
    ksum = None
    for d in range(2):
        pre = w0_ref[d:d + 1, :] + _dot(wd, wup_ref[d])
        lw_o[d] = -math.exp(-0.5) * _sigmoid(pre)
        a = _sigmoid(a0_ref[d:d + 1, :] + _dot(ad, aup_ref[d]))
        kd = kx * (1.0 + (a - 1.0) * ka)
        kd_o[d] = kd.astype(kd_o.dtype)
        kka_o[d] = (kk * a).astype(kka_o.dtype)
        ksum = kd if ksum is None else ksum + kd
    r_o[...] = r.astype(r_o.dtype)
    v_o[...] = vx.astype(v_o.dtype)
    kk_o[...] = kk.astype(kk_o.dtype)
    bonus_o[...] = (jnp.dot((r * ksum * rk_ref[...]).astype(BF16), bd, preferred_element_type=F32)
                    * vx).astype(bonus_o.dtype)
    g_o[...] = _dot(gd, gup_ref[...]).astype(g_o.dtype)


def _proj_prep(x, p, ts):
    b, s, D = x.shape
    W = p["rwkv_width"]
    nr = p["w_ret"].shape[1]
    nt = s // ts
    hb = ts // SUBLANES
    last = s // SUBLANES - 1
    full = lambda a: pl.BlockSpec(a.shape, lambda i, t, nd=a.ndim: (0,) * nd)
    out_tok = pl.BlockSpec((None, ts, W), lambda i, t: (i, t, 0))
    out_dir = pl.BlockSpec((2, None, ts, W), lambda i, t: (0, i, t, 0))
    tok_shape = jax.ShapeDtypeStruct((b, s, W), BF16)
    dir_shape = jax.ShapeDtypeStruct((2, b, s, W), BF16)
    lw_shape = jax.ShapeDtypeStruct((2, b, s, W), F32)
    consts = [p["w_ret"], p["w_rwkv"], p["mu"], p["wup_pad"], p["aup_pad"], p["w0"], p["a0"], p["gup"],
              p["k_k"], p["k_a"], p["r_k"], p["head_ones"]]
    return pl.pallas_call(
        functools.partial(_prep_body, width=W),
        grid=(b, nt),
        in_specs=[pl.BlockSpec((None, ts, D), lambda i, t: (i, t, 0)),
                  pl.BlockSpec((None, SUBLANES, D), lambda i, t: (i, jnp.maximum(t * hb - 1, 0), 0)),
                  pl.BlockSpec((None, SUBLANES, D), lambda i, t: (i, jnp.minimum((t + 1) * hb, last), 0)),
                  ] + [full(a) for a in consts],
        out_specs=[pl.BlockSpec((None, ts, nr), lambda i, t: (i, t, 0)),
                   out_tok, out_tok, out_tok, out_dir, out_dir, out_dir, out_tok, out_tok],
        out_shape=[jax.ShapeDtypeStruct((b, s, nr), BF16),
                   tok_shape, tok_shape, tok_shape, lw_shape, dir_shape, dir_shape, tok_shape, tok_shape],
        compiler_params=_cparams(("parallel", "parallel")),
        name="proj_prep",
    )(x, x, x, *consts)


def _scan_body(rf_ref, kkf_ref, vf_ref, rb_ref, kkb_ref, vb_ref, lwf_ref, kdf_ref, kkaf_ref,
               lwb_ref, kdb_ref, kkab_ref, yf_ref, yb_ref, st_ref, *, n_pairs):
    L = SCAN_CHUNK
    H = 2 * L
    assert H == LANES
    tt = rf_ref.shape[0]
    nch = tt // L

    @pl.when(pl.program_id(1) == 0)
    def _():
        st_ref[...] = jnp.zeros_like(st_ref)

    r_refs, kk_refs, v_refs = (rf_ref, rb_ref), (kkf_ref, kkb_ref), (vf_ref, vb_ref)
    lw_refs, kd_refs, kka_refs = (lwf_ref, lwb_ref), (kdf_ref, kdb_ref), (kkaf_ref, kkab_ref)
    y_refs = (yf_ref, yb_ref)
    ii = lax.broadcasted_iota(jnp.int32, (H, H), 0)
    jj = lax.broadcasted_iota(jnp.int32, (H, H), 1)
    same = (ii < L) == (jj < L)
    strict = (jnp.logical_and(same, ii > jj), jnp.logical_and(same, ii < jj))
    incl = (jnp.logical_and(same, ii >= jj), jnp.logical_and(same, ii <= jj))
    eye = ii == jj
    li = lax.broadcasted_iota(jnp.int32, (L, L), 0)
    lj = lax.broadcasted_iota(jnp.int32, (L, L), 1)
    tri = (jnp.where(li >= lj, 1.0, 0.0).astype(BF16), jnp.where(li <= lj, 1.0, 0.0).astype(BF16))
    head0 = lax.broadcasted_iota(jnp.int32, (L, H), 1) < L

    def stack(x):
        return jnp.concatenate([jnp.where(head0, x, 0.0), jnp.where(head0, 0.0, x)], axis=0)

    chains = [(d, hp) for d in range(2) for hp in range(n_pairs)]
    P = range(len(chains))
    lanes = [slice(hp * H, (hp + 1) * H) for _, hp in chains]
    dirs = [d for d, _ in chains]

    def chunk(j, carry):
        rows = (pl.ds(pl.multiple_of(j * L, L), L), pl.ds(pl.multiple_of((nch - 1 - j) * L, L), L))
        ld = lambda refs, h: refs[dirs[h]][rows[dirs[h]], lanes[h]].astype(F32)
        lw = [ld(lw_refs, h) for h in P]
        cum = [_dot_exact_lhs(tri[dirs[h]], lw[h]) for h in P]
        tot = [jnp.sum(x, axis=0, keepdims=True) for x in lw]
        e_incl = [jnp.exp(a) for a in cum]
        e_excl = [jnp.exp(a - x) for a, x in zip(cum, lw)]
        e_inv = [jnp.exp(-a) for a in cum]
        e_rem = [jnp.exp(t_ - a) for t_, a in zip(tot, cum)]
        kk = [ld(kk_refs, h) for h in P]
        kka = [ld(kka_refs, h) for h in P]
        kd = [ld(kd_refs, h) for h in P]
        Kk = [stack(a * e) for a, e in zip(kk, e_excl)]
        R = [stack(ld(r_refs, h) * e_incl[h]) for h in P]
        B = [stack(a * e) for a, e in zip(kka, e_inv)]
        Kd = [stack(a * e) for a, e in zip(kd, e_inv)]
        Bh = [stack(a * e).T.astype(BF16) for a, e in zip(kka, e_rem)]
        Kh = [stack(a * e).T.astype(BF16) for a, e in zip(kd, e_rem)]
        V = [stack(ld(v_refs, h)).astype(BF16) for h in P]

        gram = [_dot_nt(jnp.concatenate([Kk[h], R[h]], axis=0), jnp.concatenate([B[h], Kd[h]], axis=0))
                for h in P]
        Np = [jnp.where(strict[dirs[h]], gram[h][:H, :H], 0.0) for h in P]
        AD = [_dot(jnp.concatenate([jnp.where(strict[dirs[h]], gram[h][:H, H:], 0.0).astype(BF16),
                                    jnp.where(incl[dirs[h]], gram[h][H:, H:], 0.0).astype(BF16),
                                    Kh[h]], axis=0), V[h]) for h in P]
        CB = [jnp.concatenate([jnp.where(incl[dirs[h]], gram[h][H:, :H], 0.0).astype(BF16), Bh[h]], axis=0)
              for h in P]
        X = [jnp.concatenate([Kk[h], AD[h][:H]], axis=1) for h in P]
        n_fac = L.bit_length() - 1
        for f in range(n_fac):
            Nb = [a.astype(BF16) for a in Np]
            NX = [_dot(Nb[h], X[h]) for h in P]
            if f < n_fac - 1:
                Np = [_dot(Nb[h], Nb[h]) for h in P]
            X = [x - nx if f == 0 else x + nx for x, nx in zip(X, NX)]
        CBX = [_dot(CB[h], X[h]) for h in P]
        QM = []
        for h in P:
            Qh = R[h] - CBX[h][:H, :H]
            M = jnp.where(eye, jnp.exp(tot[h]), 0.0) - CBX[h][H:, :H]
            QM.append(_dot(jnp.concatenate([Qh, M], axis=0), st_ref[h]))
        for h in P:
            Ys = QM[h][:H] + (AD[h][H:2 * H] - CBX[h][:H, H:])
            st_ref[h] = QM[h][H:] + (AD[h][2 * H:] - CBX[h][H:, H:])
            y_refs[dirs[h]][rows[dirs[h]], lanes[h]] = Ys[:L] + Ys[L:]
        return carry

    lax.fori_loop(0, nch, chunk, 0)


def _rwkv_scan(r, kk, v, lw, kd, kka, tt):
    b, s, W = r.shape
    nt = s // tt
    n_pairs = W // LANES
    tok_f = pl.BlockSpec((None, tt, W), lambda i, t: (i, t, 0))
    tok_b = pl.BlockSpec((None, tt, W), lambda i, t: (i, nt - 1 - t, 0))
    dir_f = pl.BlockSpec((None, None, tt, W), lambda i, t: (0, i, t, 0))
    dir_b = pl.BlockSpec((None, None, tt, W), lambda i, t: (1, i, nt - 1 - t, 0))
    out = jax.ShapeDtypeStruct((b, s, W), F32)
    return pl.pallas_call(
        functools.partial(_scan_body, n_pairs=n_pairs),
        grid=(b, nt),
        in_specs=[tok_f, tok_f, tok_f, tok_b, tok_b, tok_b, dir_f, dir_f, dir_f, dir_b, dir_b, dir_b],
        out_specs=[tok_f, tok_b],
        out_shape=[out, out],
        scratch_shapes=[pltpu.VMEM((2 * n_pairs, LANES, LANES), F32)],
        compiler_params=_cparams(("parallel", "arbitrary")),
        name="rwkv_scan",
    )(r, kk, v, r, kk, v, lw, kd, kka, lw, kd, kka)


def _mix_body(yf_ref, yb_ref, bonus_ref, g_ref, ret_ref, x_ref, gng_ref, gnb_ref, avg_ref, wo1_ref, wo2_ref,
              l1g_ref, l1b_ref, rwt_ref, rb_ref, x1_ref, xp_ref, idx_ref, gate_ref, cnt_ref, *, alpha):
    y = yf_ref[...] + yb_ref[...]
    avg = avg_ref[...]
    mu = _dot_exact_rhs(y, avg)
    dl = y - mu
    var = jnp.dot((dl * dl).astype(BF16), avg, preferred_element_type=F32)
    yn = dl * lax.rsqrt(var + RWKV_GN_EPS) * gng_ref[...] + gnb_ref[...]
    rw = (yn + bonus_ref[...]) * g_ref[...]
    m = _dot(ret_ref[...], wo1_ref[...]) + _dot(rw, wo2_ref[...])
    x1 = _layer_norm(alpha * x_ref[...] + m, l1g_ref[...], l1b_ref[...])
    x1_ref[...] = x1
    xp_ref[...] = _pack_halves(x1)

    scores = _sigmoid(_dot_nt(rwt_ref[...], x1))
    E, tm = scores.shape
    GS = E // N_GROUPS
    NEG = -jnp.inf
    biased = scores + rb_ref[...]
    rowi = lax.broadcasted_iota(jnp.int32, (E, tm), 0)
    ri = lax.broadcasted_iota(jnp.int32, (GS, tm), 0)
    gs_rows = []
    for gi in range(N_GROUPS):
        blk = biased[gi * GS:(gi + 1) * GS, :]
        m1 = jnp.max(blk, axis=0, keepdims=True)
        i1 = jnp.min(jnp.where(blk == m1, ri, GS), axis=0, keepdims=True)
        m2 = jnp.max(jnp.where(ri == i1, NEG, blk), axis=0, keepdims=True)
        gs_rows.append(m1 + m2)
    cur = jnp.concatenate(gs_rows, axis=0)
    gidx = lax.broadcasted_iota(jnp.int32, (N_GROUPS, tm), 0)
    top_groups = []
    for _ in range(TOPK_GROUPS):
        mx = jnp.max(cur, axis=0, keepdims=True)
        ix = jnp.min(jnp.where(cur == mx, gidx, N_GROUPS), axis=0, keepdims=True)
        top_groups.append(ix)
        cur = jnp.where(gidx == ix, NEG, cur)
    blocks = []
    for gi in range(N_GROUPS):
        keep = top_groups[0] == gi
        for ix in top_groups[1:]:
            keep = jnp.logical_or(keep, ix == gi)
        blocks.append(jnp.where(keep, biased[gi * GS:(gi + 1) * GS, :], NEG))
    allowed = jnp.concatenate(blocks, axis=0)
    cur = allowed
    idxs, sels = [], []
    for _ in range(TOP_K):
        mx = jnp.max(cur, axis=0, keepdims=True)
        ix = jnp.min(jnp.where(cur == mx, rowi, E), axis=0, keepdims=True)
        hit = rowi == ix
        sels.append(jnp.sum(jnp.where(hit, scores, 0.0), axis=0, keepdims=True))
        idxs.append(ix)
        cur = jnp.where(hit, NEG, cur)
    chosen = jnp.where(jnp.logical_and(allowed > NEG, cur == NEG), 1.0, 0.0)
    sel = jnp.concatenate(sels, axis=0)
    idx_ref[...] = jnp.concatenate(idxs, axis=0)
    gate_ref[...] = sel / jnp.sum(sel, axis=0, keepdims=True) * ROUTE_SCALE

    @pl.when(pl.program_id(0) == 0)
    def _():
        cnt_ref[...] = jnp.zeros_like(cnt_ref)

    cnt_ref[...] += jnp.sum(chosen, axis=1, keepdims=True)


def _mix_out(y_f, y_b, bonus, g, ret_out, x2d, p, tm):
    T, D = x2d.shape
    W = bonus.shape[1]
    Wr = ret_out.shape[1]
    E = p["router_wt"].shape[0]
    full = lambda a: pl.BlockSpec(a.shape, lambda i, nd=a.ndim: (0,) * nd)
    consts = [p["gn_g"], p["gn_b"], p["head_avg"], p["wo_ret"], p["wo_rwkv"], p["ln1_g"], p["ln1_b"],
              p["router_wt"], p["router_b"]]
    return pl.pallas_call(
        functools.partial(_mix_body, alpha=p["alpha"]),
        grid=(T // tm,),
        in_specs=[pl.BlockSpec((tm, W), lambda i: (i, 0)),
                  pl.BlockSpec((tm, W), lambda i: (i, 0)),
                  pl.BlockSpec((tm, W), lambda i: (i, 0)),
                  pl.BlockSpec((tm, W), lambda i: (i, 0)),
                  pl.BlockSpec((tm, Wr), lambda i: (i, 0)),
                  pl.BlockSpec((tm, D), lambda i: (i, 0))] + [full(a) for a in consts],
        out_specs=[pl.BlockSpec((tm, D), lambda i: (i, 0)),
                   pl.BlockSpec((tm, D // 2), lambda i: (i, 0)),
                   pl.BlockSpec((TOP_K, tm), lambda i: (0, i)),
                   pl.BlockSpec((TOP_K, tm), lambda i: (0, i)),
                   pl.BlockSpec((E, LANES), lambda i: (0, 0))],
        out_shape=[jax.ShapeDtypeStruct((T, D), F32),
                   jax.ShapeDtypeStruct((T, D // 2), jnp.int32),
                   jax.ShapeDtypeStruct((TOP_K, T), jnp.int32),
                   jax.ShapeDtypeStruct((TOP_K, T), F32),
                   jax.ShapeDtypeStruct((E, LANES), F32)],
        compiler_params=_cparams(("arbitrary",)),
        name="mix_out",
    )(y_f, y_b, bonus, g, ret_out, x2d, *consts)


def _plan_body(idx_ref, pstart_ref, upper_ref, dest_ref, run_ref):
    @pl.when(pl.program_id(0) == 0)
    def _():
        run_ref[...] = jnp.zeros_like(run_ref)

    E = pstart_ref.shape[0]
    tm = idx_ref.shape[1]
    rowi = lax.broadcasted_iota(jnp.int32, (E, tm), 0)
    hits = [rowi == idx_ref[k:k + 1, :] for k in range(TOP_K)]
    member = jnp.zeros((E, tm), F32)
    for hit in hits:
        member = jnp.where(hit, 1.0, member)
    before = jnp.dot(member.astype(BF16), upper_ref[...], preferred_element_type=F32)
    slot = pstart_ref[...] + run_ref[:, 0:1] + before
    dest_ref[...] = jnp.concatenate(
        [jnp.sum(jnp.where(hit, slot, 0.0), axis=0, keepdims=True) for hit in hits], axis=0).astype(jnp.int32)
    run_ref[...] += jnp.sum(member, axis=1, keepdims=True)


def _slot_plan(idx_t, pad_start, tm):
    T = idx_t.shape[1]
    E = pad_start.shape[0]
    upper = (jnp.arange(tm)[:, None] < jnp.arange(tm)[None, :]).astype(BF16)
    return pl.pallas_call(
        _plan_body,
        grid=(T // tm,),
        in_specs=[pl.BlockSpec((TOP_K, tm), lambda i: (0, i)),
                  pl.BlockSpec((E, 1), lambda i: (0, 0)),
                  pl.BlockSpec((tm, tm), lambda i: (0, 0))],
        out_specs=pl.BlockSpec((TOP_K, tm), lambda i: (0, i)),
        out_shape=jax.ShapeDtypeStruct((TOP_K, T), jnp.int32),
        scratch_shapes=[pltpu.VMEM((E, LANES), F32)],
        compiler_params=_cparams(("arbitrary",)),
        name="slot_plan",
    )(idx_t, pad_start.reshape(E, 1).astype(F32), upper)


_UPPER_HALF = -65536


def _bf16_bits(x):
    return lax.bitcast_convert_type(x.astype(BF16).astype(F32), jnp.int32)


def _pack_halves(x):
    c = x.shape[1] // 2
    return _bf16_bits(x[:, :c]) | (jnp.right_shift(_bf16_bits(x[:, c:]), 16) & 0xFFFF)


def _unpack_halves(u):
    hi = lax.bitcast_convert_type(u & _UPPER_HALF, F32)
    lo = lax.bitcast_convert_type(jnp.left_shift(u, 16), F32)
    return hi, lo


SC_WINDOW = 64


def _sc_dispatch(xp, dest, n_slots):
    T, C = xp.shape
    n_win = T // SC_WINDOW
    mesh = plsc.VectorSubcoreMesh(core_axis_name="core", subcore_axis_name="subcore")

    @pl.kernel(out_type=jax.ShapeDtypeStruct((n_slots, C), xp.dtype), mesh=mesh, scratch_types=[])
    def scatter_rows(x_hbm, i_hbm, o_hbm):
        def body(x_vmem, i_vmem):
            for k in range(TOP_K):
                pltpu.sync_copy(x_vmem, o_hbm.at[i_vmem.at[k]])

        pltpu.emit_pipeline(
            body,
            grid=(n_win,),
            in_specs=[pl.BlockSpec((SC_WINDOW, C), lambda i: (i, 0)),
                      pl.BlockSpec((None, TOP_K, SC_WINDOW), lambda i: (i, 0, 0))],
            out_specs=[],
            core_axis_name=("core", "subcore"),
            dimension_semantics=(pltpu.PARALLEL,),
        )(x_hbm, i_hbm)

    idx = dest.reshape(TOP_K, n_win, SC_WINDOW).transpose(1, 0, 2)
    return scatter_rows(xp, idx)


def _sc_gather(ys, dest):
    n_idx = dest.shape[0] * dest.shape[1]
    C = ys.shape[1]
    mesh = plsc.VectorSubcoreMesh(core_axis_name="core", subcore_axis_name="subcore")

    @pl.kernel(out_type=jax.ShapeDtypeStruct((n_idx, C), ys.dtype), mesh=mesh, scratch_types=[])
    def gather_rows(y_hbm, i_hbm, o_hbm):
        def body(i_vmem, o_vmem):
            pltpu.sync_copy(y_hbm.at[i_vmem.at[0]], o_vmem)

        pltpu.emit_pipeline(
            body,
            grid=(n_idx // SC_WINDOW,),
            in_specs=[pl.BlockSpec((1, SC_WINDOW), lambda i: (i, 0))],
            out_specs=[pl.BlockSpec((SC_WINDOW, C), lambda i: (i, 0))],
            core_axis_name=("core", "subcore"),
            dimension_semantics=(pltpu.PARALLEL,),
        )(i_hbm, o_hbm)

    return gather_rows(ys, dest.reshape(n_idx // SC_WINDOW, SC_WINDOW))


MOE_IN_SLOTS = 4
MOE_OUT_SLOTS = 2


def _moe_body(nb_ref, b0_ref, cnt_ref, nu_ref, xs_hbm, wg_ref, wu_ref, wd_ref, ys_hbm,
              xbuf, obuf, wgb_ref, wub_ref, wdb_ref, in_sem, out_sem):
    G = MOE_BLOCK
    e = pl.program_id(0)
    nb = nb_ref[e]
    b0 = b0_ref[e]
    n_used = nu_ref[0]

    def in_copy(g):
        slot = lax.rem(g, MOE_IN_SLOTS)
        return pltpu.make_async_copy(xs_hbm.at[pl.ds(pl.multiple_of(g * G, G), G), :], xbuf.at[slot],
                                     in_sem.at[slot])

    def out_copy(g):
        slot = lax.rem(g, MOE_OUT_SLOTS)
        return pltpu.make_async_copy(obuf.at[slot], ys_hbm.at[pl.ds(pl.multiple_of(g * G, G), G), :],
                                     out_sem.at[slot])

    @pl.when(e == 0)
    def _():
        for g in range(MOE_IN_SLOTS - 1):
            @pl.when(g < n_used)
            def _():
                in_copy(g).start()

    @pl.when(nb > 0)
    def _():
        wgb_ref[...] = wg_ref[...].astype(BF16)
        wub_ref[...] = wu_ref[...].astype(BF16)
        wdb_ref[...] = wd_ref[...].astype(BF16)
        row = lax.broadcasted_iota(jnp.int32, (G, xbuf.shape[2]), 0)
        c = xbuf.shape[2]

        def step(j, carry):
            g = b0 + j
            in_copy(g).wait()

            @pl.when(g + (MOE_IN_SLOTS - 1) < n_used)
            def _():
                in_copy(g + (MOE_IN_SLOTS - 1)).start()

            @pl.when(g >= MOE_OUT_SLOTS)
            def _():
                out_copy(g - MOE_OUT_SLOTS).wait()

            u = jnp.where(row < cnt_ref[e] - j * G, xbuf[lax.rem(g, MOE_IN_SLOTS)], 0)
            x_hi, x_lo = _unpack_halves(u)
            gate = _dot(x_hi, wgb_ref[:c, :]) + _dot(x_lo, wgb_ref[c:, :])
            up = _dot(x_hi, wub_ref[:c, :]) + _dot(x_lo, wub_ref[c:, :])
            obuf[lax.rem(g, MOE_OUT_SLOTS)] = _pack_halves(_dot(_silu(gate) * up, wdb_ref[...]))
            out_copy(g).start()
            return carry

        lax.fori_loop(0, nb, step, 0)

    @pl.when(e == pl.num_programs(0) - 1)
    def _():
        for back in range(MOE_OUT_SLOTS, 0, -1):
            @pl.when(n_used >= back)
            def _():
                out_copy(n_used - back).wait()


def _moe_ffn(xs, blocks_per_expert, first_block, counts, n_used, w_gate, w_up, w_down):
    P, C = xs.shape
    G = MOE_BLOCK
    E, D, De = w_gate.shape
    wspec = lambda shape: pl.BlockSpec((None,) + shape, lambda e, nb, b0, cnt, nu: (e, 0, 0))
    grid_spec = pltpu.PrefetchScalarGridSpec(
        num_scalar_prefetch=4,
        grid=(E,),
        in_specs=[pl.BlockSpec(memory_space=pl.ANY), wspec((D, De)), wspec((D, De)), wspec((De, D))],
        out_specs=pl.BlockSpec(memory_space=pl.ANY),
        scratch_shapes=[pltpu.VMEM((MOE_IN_SLOTS, G, C), jnp.int32), pltpu.VMEM((MOE_OUT_SLOTS, G, C), jnp.int32),
                        pltpu.VMEM((D, De), BF16), pltpu.VMEM((D, De), BF16), pltpu.VMEM((De, D), BF16),
                        pltpu.SemaphoreType.DMA((MOE_IN_SLOTS,)), pltpu.SemaphoreType.DMA((MOE_OUT_SLOTS,))],
    )
    return pl.pallas_call(
        _moe_body,
        grid_spec=grid_spec,
        out_shape=jax.ShapeDtypeStruct((P, C), jnp.int32),
        compiler_params=_cparams(("arbitrary",)),
        name="moe_ffn",
    )(blocks_per_expert, first_block, counts, n_used, xs, w_gate, w_up, w_down)


def _comb_body(yg_ref, x1_ref, gt_ref, sg_ref, su_ref, sd_ref, l2g_ref, l2b_ref, o_ref, *, alpha):
    x1 = x1_ref[...]
    xb = x1.astype(BF16)
    shared = _dot(_silu(_dot(xb, sg_ref[...])) * _dot(xb, su_ref[...]), sd_ref[...])
    gt = gt_ref[...]
    acc_hi = acc_lo = None
    for k in range(TOP_K):
        hi, lo = _unpack_halves(yg_ref[k])
        gk = gt[:, k:k + 1]
        acc_hi = gk * hi if acc_hi is None else acc_hi + gk * hi
        acc_lo = gk * lo if acc_lo is None else acc_lo + gk * lo
    routed = jnp.concatenate([acc_hi, acc_lo], axis=1)
    o_ref[...] = _layer_norm(alpha * x1 + (routed + shared), l2g_ref[...], l2b_ref[...])


def _combine(yg, x1, gates_t, p, tm):
    T, D = x1.shape
    full = lambda a: pl.BlockSpec(a.shape, lambda i, nd=a.ndim: (0,) * nd)
    consts = [p["sh_gate"], p["sh_up"], p["sh_down"], p["ln2_g"], p["ln2_b"]]
    return pl.pallas_call(
        functools.partial(_comb_body, alpha=p["alpha"]),
        grid=(T // tm,),
        in_specs=[pl.BlockSpec((TOP_K, tm, D // 2), lambda i: (0, i, 0)),
                  pl.BlockSpec((tm, D), lambda i: (i, 0)),
                  pl.BlockSpec((tm, TOP_K), lambda i: (i, 0))] + [full(a) for a in consts],
        out_specs=pl.BlockSpec((tm, D), lambda i: (i, 0)),
        out_shape=jax.ShapeDtypeStruct((T, D), F32),
        compiler_params=_cparams(("parallel",)),
        name="combine",
    )(yg, x1, gates_t, *consts)


def _segment_layout(counts, n_tokens):
    G = MOE_BLOCK
    E = counts.shape[0]
    n_blocks = -(-(n_tokens * TOP_K + E * (G - 1)) // G)
    padded = (counts + G - 1) // G * G
    pad_end = jnp.cumsum(padded)
    pad_start = pad_end - padded
    n_used = pad_end[-1:] // G
    return (pad_start, (padded // G).astype(jnp.int32), (pad_start // G).astype(jnp.int32),
            n_used.astype(jnp.int32), n_blocks)


def _rotary_tables(s, d):
    inv = ROPE_BASE ** (-jnp.arange(0, d, 2, dtype=F32) / d)
    ang = jnp.arange(s, dtype=F32)[:, None] * inv[None, :]
    cos = jnp.cos(ang)
    sin = jnp.sin(ang)
    return jnp.concatenate([cos, cos], axis=-1), jnp.concatenate([-sin, sin], axis=-1)


def _layer_params(l, depth, w_in, ret_gn_g, ret_gn_b, rwkv_mu, rwkv_w0, rwkv_w_up, rwkv_a0, rwkv_a_up,
                  rwkv_g_up, rwkv_k_k, rwkv_k_a, rwkv_r_k, rwkv_gn_g, rwkv_gn_b, w_out, ln1_g, ln1_b,
                  router_w, router_bias, exp_w_gate, exp_w_up, exp_w_down, sh_w_gate, sh_w_up,
                  sh_w_down, ln2_g, ln2_b):
    ret_w = ret_gn_g.shape[-1]
    W = rwkv_gn_g.shape[-1]
    n_heads, hd = rwkv_r_k.shape[-2:]
    rank_w = rwkv_w_up.shape[2]
    rank_a = rwkv_a_up.shape[2]
    assert rank_w * 2 == LANES and rank_a * 2 == LANES and rwkv_g_up.shape[1] == LANES
    assert hd * 2 == LANES and SCAN_CHUNK == hd
    row = lambda a: a.reshape(1, -1).astype(F32)
    zw = jnp.zeros((rank_w, W), F32)
    za = jnp.zeros((rank_a, W), F32)
    head_id = jnp.arange(W) // hd
    same_head = (head_id[:, None] == head_id[None, :])
    wi = w_in[l]
    return dict(
        alpha=float((2 * depth) ** 0.25),
        rwkv_width=W,
        w_ret=wi[:, :4 * ret_w].astype(BF16),
        w_rwkv=wi[:, 4 * ret_w:].astype(BF16),
        ret_gn_g=row(ret_gn_g[l]), ret_gn_b=row(ret_gn_b[l]),
        logg=jnp.broadcast_to(
            jnp.log1p(-jnp.exp2(-5.0 - jnp.arange(RET_HEADS, dtype=F32)))[:, None, None],
            (RET_HEADS, 1, LANES)),
        mu=row(rwkv_mu[l]),
        wup_pad=jnp.stack([jnp.concatenate([rwkv_w_up[l, 0], zw], 0),
                           jnp.concatenate([zw, rwkv_w_up[l, 1]], 0)]).astype(BF16),
        aup_pad=jnp.stack([jnp.concatenate([rwkv_a_up[l, 0], za], 0),
                           jnp.concatenate([za, rwkv_a_up[l, 1]], 0)]).astype(BF16),
        w0=rwkv_w0[l].astype(F32), a0=rwkv_a0[l].astype(F32),
        gup=rwkv_g_up[l].astype(BF16),
        k_k=row(rwkv_k_k[l]), k_a=row(rwkv_k_a[l]), r_k=row(rwkv_r_k[l]),
        head_ones=same_head.astype(BF16),
        head_avg=(same_head.astype(F32) / hd).astype(BF16),
        gn_g=row(rwkv_gn_g[l]), gn_b=row(rwkv_gn_b[l]),
        wo_ret=w_out[l, :ret_w].astype(BF16), wo_rwkv=w_out[l, ret_w:].astype(BF16),
        ln1_g=row(ln1_g[l]), ln1_b=row(ln1_b[l]),
        router_wt=router_w[l].T.astype(BF16), router_b=router_bias[l].reshape(-1, 1).astype(F32),
        exp_gate=exp_w_gate[l], exp_up=exp_w_up[l], exp_down=exp_w_down[l],
        sh_gate=sh_w_gate[l].astype(BF16), sh_up=sh_w_up[l].astype(BF16), sh_down=sh_w_down[l].astype(BF16),
        ln2_g=row(ln2_g[l]), ln2_b=row(ln2_b[l]),
    )


def _pick(n, pref):
    t = min(n, pref)
    while n % t:
        t //= 2
    return t


def _layer(x, p):
    b, s, D = x.shape
    T = b * s
    x2d = x.reshape(T, D)
    tm = _pick(T, 256)
    z_ret, r, v, kk, lw, kd, kka, bonus, g = _proj_prep(x, p, _pick(s, 256))
    cos, sin = _rotary_tables(s, RET_CHUNK)
    ret_out = _retention(z_ret, cos, sin, p["logg"], p["ret_gn_g"], p["ret_gn_b"])
    y_f, y_b = _rwkv_scan(r, kk, v, lw, kd, kka, _pick(s, 512))
    W = p["rwkv_width"]
    x1, xp, idx_t, gates, cnt = _mix_out(y_f.reshape(T, W), y_b.reshape(T, W), bonus.reshape(T, W),
                                         g.reshape(T, W), ret_out.reshape(T, -1), x2d, p, tm)
    counts = cnt[:, 0].astype(jnp.int32)
    pad_start, blocks_per_expert, first_block, n_used, n_blocks = _segment_layout(counts, T)
    dest = _slot_plan(idx_t, pad_start, _pick(T, 512))
    xs = _sc_dispatch(xp, dest, n_blocks * MOE_BLOCK)
    ys = _moe_ffn(xs, blocks_per_expert, first_block, counts, n_used, p["exp_gate"], p["exp_up"], p["exp_down"])
    yg = _sc_gather(ys, dest).reshape(TOP_K, T, D // 2)
    out = _combine(yg, x1, gates.T, p, _pick(T, 256))
    return out.reshape(b, s, D)


def kernel(x_prompt, x_sample, w_in, ret_gn_g, ret_gn_b, rwkv_mu, rwkv_w0, rwkv_w_up, rwkv_a0, rwkv_a_up,
           rwkv_g_up, rwkv_k_k, rwkv_k_a, rwkv_r_k, rwkv_gn_g, rwkv_gn_b, w_out, ln1_g, ln1_b, router_w,
           router_bias, exp_w_gate, exp_w_up, exp_w_down, sh_w_gate, sh_w_up, sh_w_down, ln2_g, ln2_b):
    weights = (w_in, ret_gn_g, ret_gn_b, rwkv_mu, rwkv_w0, rwkv_w_up, rwkv_a0, rwkv_a_up, rwkv_g_up,
               rwkv_k_k, rwkv_k_a, rwkv_r_k, rwkv_gn_g, rwkv_gn_b, w_out, ln1_g, ln1_b, router_w,
               router_bias, exp_w_gate, exp_w_up, exp_w_down, sh_w_gate, sh_w_up, sh_w_down, ln2_g, ln2_b)
    depth = w_in.shape[0]
    layers = [_layer_params(l, depth, *weights) for l in range(depth)]

    def trunk(x):
        for p in layers:
            x = _layer(x, p)
        return x

    return trunk(x_prompt), trunk(x_sample)
```

```python
import functools
import math

import jax
import jax.numpy as jnp
from jax import lax
from jax.experimental import pallas as pl
from jax.experimental.pallas import tpu as pltpu
from jax.experimental.pallas import tpu_sc as plsc

F32 = jnp.float32
BF16 = jnp.bfloat16

RET_HEADS = 4
RET_CHUNK = 128
ROPE_BASE = 10000.0
TOP_K = 8
N_GROUPS = 8
TOPK_GROUPS = 4
ROUTE_SCALE = 2.5
MOE_BLOCK = 256
LN_EPS = 1e-5
GN_EPS = 1e-5
RWKV_GN_EPS = 64e-5

LANES = 128
SUBLANES = 8
VMEM_LIMIT_BYTES = 56 * 1024 * 1024

SCAN_CHUNK = 64


def _cparams(semantics):
    return pltpu.CompilerParams(dimension_semantics=semantics, vmem_limit_bytes=VMEM_LIMIT_BYTES)


def _dot(a, b):
    return jnp.dot(a.astype(BF16), b.astype(BF16), preferred_element_type=F32)


def _dot_nt(a, b):
    return lax.dot_general(a.astype(BF16), b.astype(BF16), (((1,), (1,)), ((), ())),
                           preferred_element_type=F32)


def _dot_exact_rhs(x, w_bf16):
    hi = x.astype(BF16)
    mid = (x - hi.astype(F32)).astype(BF16)
    return (jnp.dot(hi, w_bf16, preferred_element_type=F32)
            + jnp.dot(mid, w_bf16, preferred_element_type=F32))


def _sigmoid(x):
    return 1.0 / (1.0 + jnp.exp(-x))


def _silu(x):
    return x * _sigmoid(x)


def _layer_norm(h, g, b):
    mu = jnp.mean(h, axis=-1, keepdims=True)
    d = h - mu
    var = jnp.mean(d * d, axis=-1, keepdims=True)
    return d * lax.rsqrt(var + LN_EPS) * g + b


RET_GROUP = 4


def _ret_body(q_ref, k_ref, v_ref, gt_ref, cos_ref, sin_ref, lg_ref, gg_ref, gb_ref, o_ref,
              qs_ref, sf_ref, sb_ref, acc_ref, *, qscale):
    C = RET_CHUNK
    U = RET_GROUP
    s = q_ref.shape[0]
    n = s // C
    lg = lg_ref[...]
    pos = lax.broadcasted_iota(jnp.int32, (C, C), 0).astype(F32)
    col = lax.broadcasted_iota(jnp.int32, (C, C), 1).astype(F32)
    sc_q_prev = jnp.exp((pos + 1.0) * lg)
    sc_k_fwd = jnp.exp((C - 1.0 - pos) * lg)
    sc_k_bwd = jnp.exp(pos * lg)
    sc_q_next = jnp.exp((C - pos) * lg)
    g_chunk = jnp.exp(float(C) * lg)
    decay = jnp.exp(jnp.abs(pos - col) * lg)
    gg = gg_ref[...]
    gb = gb_ref[...]

    def rows_of(grp):
        return [pl.ds(pl.multiple_of((grp * U + u) * C, C), C) for u in range(U)]

    def rot(x, r):
        x = x.astype(F32)
        return x * cos_ref[r, :] + pltpu.roll(x, C // 2, 1) * sin_ref[r, :]

    def local(grp, carry):
        rows = rows_of(grp)
        q = [rot(q_ref[r, :], r) * qscale for r in rows]
        k = [rot(k_ref[r, :], r) for r in rows]
        vb = [v_ref[r, :].astype(BF16) for r in rows]
        sc = [_dot_nt(q[u], k[u]) * decay for u in range(U)]
        out = [_dot(sc[u], vb[u]) for u in range(U)]
        kf = [_dot((k[u] * sc_k_fwd).T, vb[u]) for u in range(U)]
        kb = [_dot((k[u] * sc_k_bwd).T, vb[u]) for u in range(U)]
        for u in range(U):
            qs_ref[rows[u], :] = q[u]
            acc_ref[rows[u], :] = out[u]
            sf_ref[grp * U + u] = kf[u]
            sb_ref[grp * U + u] = kb[u]
        return carry

    lax.fori_loop(0, n // U, local, 0)

    def fwd(c, S):
        kv = sf_ref[c]
        sf_ref[c] = S
        return S * g_chunk + kv

    lax.fori_loop(0, n, fwd, jnp.zeros((C, C), F32))

    def bwd(i, S):
        c = n - 1 - i
        kv = sb_ref[c]
        sb_ref[c] = S
        return S * g_chunk + kv

    lax.fori_loop(0, n, bwd, jnp.zeros((C, C), F32))

    def cross(grp, carry):
        rows = rows_of(grp)
        q = [qs_ref[r, :] for r in rows]
        y = [acc_ref[rows[u], :] + _dot(jnp.concatenate([q[u] * sc_q_prev, q[u] * sc_q_next], axis=1),
                                        jnp.concatenate([sf_ref[grp * U + u], sb_ref[grp * U + u]], axis=0))
             for u in range(U)]
        for u in range(U):
            mu = jnp.mean(y[u], axis=-1, keepdims=True)
            d = y[u] - mu
            var = jnp.mean(d * d, axis=-1, keepdims=True)
            yn = d * lax.rsqrt(var + GN_EPS) * gg + gb
            o_ref[rows[u], :] = _silu(gt_ref[rows[u], :].astype(F32)) * yn
        return carry

    lax.fori_loop(0, n // U, cross, 0)


def _retention(z_ret, cos, sin, logg, gn_g, gn_b):
    b, s, _ = z_ret.shape
    C = RET_CHUNK
    H = RET_HEADS
    assert C == LANES and s % (C * RET_GROUP) == 0
    n = s // C
    blk = lambda off: pl.BlockSpec((None, s, C), lambda i, h, off=off: (i, 0, off + h))
    return pl.pallas_call(
        functools.partial(_ret_body, qscale=float(C) ** -0.5),
        grid=(b, H),
        in_specs=[blk(0), blk(H), blk(2 * H), blk(3 * H),
                  pl.BlockSpec((s, C), lambda i, h: (0, 0)),
                  pl.BlockSpec((s, C), lambda i, h: (0, 0)),
                  pl.BlockSpec((None, 1, C), lambda i, h: (h, 0, 0)),
                  pl.BlockSpec((1, C), lambda i, h: (0, h)),
                  pl.BlockSpec((1, C), lambda i, h: (0, h))],
        out_specs=pl.BlockSpec((None, s, C), lambda i, h: (i, 0, h)),
        out_shape=jax.ShapeDtypeStruct((b, s, H * C), F32),
        scratch_shapes=[pltpu.VMEM((s, C), F32), pltpu.VMEM((n, C, C), F32), pltpu.VMEM((n, C, C), F32),
                        pltpu.VMEM((s, C), F32)],
        compiler_params=_cparams(("parallel", "parallel")),
        name="retention",
    )(z_ret, z_ret, z_ret, z_ret, cos, sin, logg, gn_g, gn_b)


def _prep_body(x_ref, xp_ref, xn_ref, wr_ref, ww_ref, mu_ref, wup_ref, aup_ref, w0_ref, a0_ref, gup_ref,
               kkp_ref, ka_ref, rk_ref, bd_ref,
               zr_o, r_o, v_o, kk_o, lw_o, kd_o, kka_o, bonus_o, g_o, *, width):
    t = pl.program_id(1)
    nt = pl.num_programs(1)
    W = width
    ts = x_ref.shape[0]
    x_all = jnp.concatenate([x_ref[...], xp_ref[...], xn_ref[...]], axis=0).astype(BF16)
    zr_o[...] = jnp.dot(x_all[:ts], wr_ref[...], preferred_element_type=F32).astype(zr_o.dtype)
    z_all = jnp.dot(x_all, ww_ref[...], preferred_element_type=F32)
    z = z_all[:ts]
    row = lax.broadcasted_iota(jnp.int32, (ts, 1), 0)
    prev_row = jnp.where(t > 0, z_all[ts + SUBLANES - 1:ts + SUBLANES, :], 0.0)
    next_row = jnp.where(t < nt - 1, z_all[ts + SUBLANES:ts + SUBLANES + 1, :], 0.0)
    prev = jnp.where(row == 0, prev_row, pltpu.roll(z, 1, 0))
    nxt = jnp.where(row == ts - 1, next_row, pltpu.roll(z, ts - 1, 0))
    zs = z + mu_ref[...] * (0.5 * (prev + nxt) - z)

    r = zs[:, 0:W]
    kx = zs[:, W:2 * W]
    vx = zs[:, 2 * W:3 * W]
    wd = jnp.tanh(zs[:, 3 * W:3 * W + LANES])
    ad = zs[:, 3 * W + LANES:3 * W + 2 * LANES]
    gd = _sigmoid(zs[:, 3 * W + 2 * LANES:3 * W + 3 * LANES])
    bd = bd_ref[...]

    kk = kx * kkp_ref[...]
    ssq = jnp.dot((kk * kk).astype(BF16), bd, preferred_element_type=F32)
    kk = kk * lax.rsqrt(jnp.maximum(ssq, 1e-24))
    ka = ka_ref[...]
    ksum = None
    for d in range(2):
        pre = w0_ref[d:d + 1, :] + _dot(wd, wup_ref[d])
        lw_o[d] = -math.exp(-0.5) * _sigmoid(pre)
        a = _sigmoid(a0_ref[d:d + 1, :] + _dot(ad, aup_ref[d]))
        kd = kx * (1.0 + (a - 1.0) * ka)
        kd_o[d] = kd.astype(kd_o.dtype)
        kka_o[d] = (kk * a).astype(kka_o.dtype)
        ksum = kd if ksum is None else ksum + kd
    r_o[...] = r.astype(r_o.dtype)
    v_o[...] = vx.astype(v_o.dtype)
    kk_o[...] = kk.astype(kk_o.dtype)
    bonus_o[...] = (jnp.dot((r * ksum * rk_ref[...]).astype(BF16), bd, preferred_element_type=F32)
                    * vx).astype(bonus_o.dtype)
    g_o[...] = _dot(gd, gup_ref[...]).astype(g_o.dtype)


def _proj_prep(x, p, ts):
    b, s, D = x.shape
    W = p["rwkv_width"]
    nr = p["w_ret"].shape[1]
    nt = s // ts
    hb = ts // SUBLANES
    last = s // SUBLANES - 1
    full = lambda a: pl.BlockSpec(a.shape, lambda i, t, nd=a.ndim: (0,) * nd)
    out_tok = pl.BlockSpec((None, ts, W), lambda i, t: (i, t, 0))
    out_dir = pl.BlockSpec((2, None, ts, W), lambda i, t: (0, i, t, 0))
    tok_shape = jax.ShapeDtypeStruct((b, s, W), BF16)
    dir_shape = jax.ShapeDtypeStruct((2, b, s, W), BF16)
    lw_shape = jax.ShapeDtypeStruct((2, b, s, W), F32)
    consts = [p["w_ret"], p["w_rwkv"], p["mu"], p["wup_pad"], p["aup_pad"], p["w0"], p["a0"], p["gup"],
              p["k_k"], p["k_a"], p["r_k"], p["head_ones"]]
    return pl.pallas_call(
        functools.partial(_prep_body, width=W),
        grid=(b, nt),
        in_specs=[pl.BlockSpec((None, ts, D), lambda i, t: (i, t, 0)),
                  pl.BlockSpec((None, SUBLANES, D), lambda i, t: (i, jnp.maximum(t * hb - 1, 0), 0)),
                  pl.BlockSpec((None, SUBLANES, D), lambda i, t: (i, jnp.minimum((t + 1) * hb, last), 0)),
                  ] + [full(a) for a in consts],
        out_specs=[pl.BlockSpec((None, ts, nr), lambda i, t: (i, t, 0)),
                   out_tok, out_tok, out_tok, out_dir, out_dir, out_dir, out_tok, out_tok],
        out_shape=[jax.ShapeDtypeStruct((b, s, nr), BF16),
                   tok_shape, tok_shape, tok_shape, lw_shape, dir_shape, dir_shape, tok_shape, tok_shape],
        compiler_params=_cparams(("parallel", "parallel")),
        name="proj_prep",
    )(x, x, x, *consts)


def _scan_body(rf_ref, kkf_ref, vf_ref, rb_ref, kkb_ref, vb_ref, lwf_ref, kdf_ref, kkaf_ref,
               lwb_ref, kdb_ref, kkab_ref, yf_ref, yb_ref, st_ref, *, n_pairs):
    L = SCAN_CHUNK
    H = 2 * L
    assert H == LANES
    tt = rf_ref.shape[0]
    nch = tt // L

    @pl.when(pl.program_id(1) == 0)
    def _():
        st_ref[...] = jnp.zeros_like(st_ref)

    r_refs, kk_refs, v_refs = (rf_ref, rb_ref), (kkf_ref, kkb_ref), (vf_ref, vb_ref)
    lw_refs, kd_refs, kka_refs = (lwf_ref, lwb_ref), (kdf_ref, kdb_ref), (kkaf_ref, kkab_ref)
    y_refs = (yf_ref, yb_ref)
    ii = lax.broadcasted_iota(jnp.int32, (H, H), 0)
    jj = lax.broadcasted_iota(jnp.int32, (H, H), 1)
    same = (ii < L) == (jj < L)
    strict = (jnp.logical_and(same, ii > jj), jnp.logical_and(same, ii < jj))
    incl = (jnp.logical_and(same, ii >= jj), jnp.logical_and(same, ii <= jj))
    eye = ii == jj
    li = lax.broadcasted_iota(jnp.int32, (L, L), 0)
    lj = lax.broadcasted_iota(jnp.int32, (L, L), 1)
    tri = (jnp.where(li >= lj, 1.0, 0.0).astype(BF16), jnp.where(li <= lj, 1.0, 0.0).astype(BF16))
    head0 = lax.broadcasted_iota(jnp.int32, (L, H), 1) < L

    def stack(x):
        return jnp.concatenate([jnp.where(head0, x, 0.0), jnp.where(head0, 0.0, x)], axis=0)

    chains = [(d, hp) for d in range(2) for hp in range(n_pairs)]
    P = range(len(chains))
    lanes = [slice(hp * H, (hp + 1) * H) for _, hp in chains]
    dirs = [d for d, _ in chains]

    def chunk(j, carry):
        rows = (pl.ds(pl.multiple_of(j * L, L), L), pl.ds(pl.multiple_of((nch - 1 - j) * L, L), L))
        ld = lambda refs, h: refs[dirs[h]][rows[dirs[h]], lanes[h]].astype(F32)
        lw = [ld(lw_refs, h) for h in P]
        lw_hi = [x.astype(BF16) for x in lw]
        cum2 = [jnp.dot(tri[dirs[h]],
                        jnp.concatenate([lw_hi[h], (lw[h] - lw_hi[h].astype(F32)).astype(BF16)], axis=1),
                        preferred_element_type=F32) for h in P]
        cum = [c2[:, :H] + c2[:, H:] for c2 in cum2]
        tot = [jnp.sum(x, axis=0, keepdims=True) for x in lw]
        e_incl = [jnp.exp(a) for a in cum]
        e_excl = [jnp.exp(a - x) for a, x in zip(cum, lw)]
        e_inv = [jnp.exp(-a) for a in cum]
        e_rem = [jnp.exp(t_ - a) for t_, a in zip(tot, cum)]
        kk = [ld(kk_refs, h) for h in P]
        kka = [ld(kka_refs, h) for h in P]
        kd = [ld(kd_refs, h) for h in P]
        Kk = [stack(a * e) for a, e in zip(kk, e_excl)]
        R = [stack(ld(r_refs, h) * e_incl[h]) for h in P]
        B = [stack(a * e) for a, e in zip(kka, e_inv)]
        Kd = [stack(a * e) for a, e in zip(kd, e_inv)]
        Bh = [stack(a * e).T.astype(BF16) for a, e in zip(kka, e_rem)]
        Kh = [stack(a * e).T.astype(BF16) for a, e in zip(kd, e_rem)]
        V = [stack(ld(v_refs, h)).astype(BF16) for h in P]

        gram = [_dot_nt(jnp.concatenate([Kk[h], R[h]], axis=0), jnp.concatenate([B[h], Kd[h]], axis=0))
                for h in P]
        Np = [jnp.where(strict[dirs[h]], gram[h][:H, :H], 0.0) for h in P]
        AD = [_dot(jnp.concatenate([jnp.where(strict[dirs[h]], gram[h][:H, H:], 0.0).astype(BF16),
                                    jnp.where(incl[dirs[h]], gram[h][H:, H:], 0.0).astype(BF16),
                                    Kh[h]], axis=0), V[h]) for h in P]
        CB = [jnp.concatenate([jnp.where(incl[dirs[h]], gram[h][H:, :H], 0.0).astype(BF16), Bh[h]], axis=0)
              for h in P]
        n_fac = L.bit_length() - 1
        Tm = [jnp.where(eye, 1.0, 0.0) - a for a in Np]
        Np = [_dot(a, a) for a in Np]
        for f in range(1, n_fac):
            if f < n_fac - 1:
                both = [_dot(Np[h], jnp.concatenate([Tm[h], Np[h]], axis=1)) for h in P]
                Tm = [t_ + o[:, :H] for t_, o in zip(Tm, both)]
                Np = [o[:, H:] for o in both]
            else:
                Tm = [t_ + _dot(a, t_) for t_, a in zip(Tm, Np)]
        X = [_dot(Tm[h], jnp.concatenate([Kk[h], AD[h][:H]], axis=1)) for h in P]
        CBX = [_dot(CB[h], X[h]) for h in P]
        QM = []
        for h in P:
            Qh = R[h] - CBX[h][:H, :H]
            M = jnp.where(eye, jnp.exp(tot[h]), 0.0) - CBX[h][H:, :H]
            QM.append(_dot(jnp.concatenate([Qh, M], axis=0), st_ref[h]))
        for h in P:
            Ys = QM[h][:H] + (AD[h][H:2 * H] - CBX[h][:H, H:])
            st_ref[h] = QM[h][H:] + (AD[h][2 * H:] - CBX[h][H:, H:])
            y_refs[dirs[h]][rows[dirs[h]], lanes[h]] = Ys[:L] + Ys[L:]
        return carry

    lax.fori_loop(0, nch, chunk, 0)


def _rwkv_scan(r, kk, v, lw, kd, kka, tt):
    b, s, W = r.shape
    nt = s // tt
    n_pairs = W // LANES
    tok_f = pl.BlockSpec((None, tt, W), lambda i, t: (i, t, 0))
    tok_b = pl.BlockSpec((None, tt, W), lambda i, t: (i, nt - 1 - t, 0))
    dir_f = pl.BlockSpec((None, None, tt, W), lambda i, t: (0, i, t, 0))
    dir_b = pl.BlockSpec((None, None, tt, W), lambda i, t: (1, i, nt - 1 - t, 0))
    out = jax.ShapeDtypeStruct((b, s, W), F32)
    return pl.pallas_call(
        functools.partial(_scan_body, n_pairs=n_pairs),
        grid=(b, nt),
        in_specs=[tok_f, tok_f, tok_f, tok_b, tok_b, tok_b, dir_f, dir_f, dir_f, dir_b, dir_b, dir_b],
        out_specs=[tok_f, tok_b],
        out_shape=[out, out],
        scratch_shapes=[pltpu.VMEM((2 * n_pairs, LANES, LANES), F32)],
        compiler_params=_cparams(("parallel", "arbitrary")),
        name="rwkv_scan",
    )(r, kk, v, r, kk, v, lw, kd, kka, lw, kd, kka)


def _mix_body(yf_ref, yb_ref, bonus_ref, g_ref, ret_ref, x_ref, gng_ref, gnb_ref, avg_ref, wo1_ref, wo2_ref,
              l1g_ref, l1b_ref, rwt_ref, rb_ref, x1_ref, xp_ref, idx_ref, gate_ref, cnt_ref, *, alpha):
    y = yf_ref[...] + yb_ref[...]
    avg = avg_ref[...]
    mu = _dot_exact_rhs(y, avg)
    dl = y - mu
    var = jnp.dot((dl * dl).astype(BF16), avg, preferred_element_type=F32)
    yn = dl * lax.rsqrt(var + RWKV_GN_EPS) * gng_ref[...] + gnb_ref[...]
    rw = (yn + bonus_ref[...]) * g_ref[...]
    m = _dot(ret_ref[...], wo1_ref[...]) + _dot(rw, wo2_ref[...])
    x1 = _layer_norm(alpha * x_ref[...] + m, l1g_ref[...], l1b_ref[...])
    x1_ref[...] = x1
    xp_ref[...] = _pack_halves(x1)

    scores = _sigmoid(_dot_nt(rwt_ref[...], x1))
    E, tm = scores.shape
    GS = E // N_GROUPS
    NEG = -jnp.inf
    biased = scores + rb_ref[...]
    rowi = lax.broadcasted_iota(jnp.int32, (E, tm), 0)
    ri = lax.broadcasted_iota(jnp.int32, (GS, tm), 0)
    gs_rows = []
    for gi in range(N_GROUPS):
        blk = biased[gi * GS:(gi + 1) * GS, :]
        m1 = jnp.max(blk, axis=0, keepdims=True)
        i1 = jnp.min(jnp.where(blk == m1, ri, GS), axis=0, keepdims=True)
        m2 = jnp.max(jnp.where(ri == i1, NEG, blk), axis=0, keepdims=True)
        gs_rows.append(m1 + m2)
    cur = jnp.concatenate(gs_rows, axis=0)
    gidx = lax.broadcasted_iota(jnp.int32, (N_GROUPS, tm), 0)
    top_groups = []
    for _ in range(TOPK_GROUPS):
        mx = jnp.max(cur, axis=0, keepdims=True)
        ix = jnp.min(jnp.where(cur == mx, gidx, N_GROUPS), axis=0, keepdims=True)
        top_groups.append(ix)
        cur = jnp.where(gidx == ix, NEG, cur)
    blocks = []
    for gi in range(N_GROUPS):
        keep = top_groups[0] == gi
        for ix in top_groups[1:]:
            keep = jnp.logical_or(keep, ix == gi)
        blocks.append(jnp.where(keep, biased[gi * GS:(gi + 1) * GS, :], NEG))
    allowed = jnp.concatenate(blocks, axis=0)
    cur = allowed
    idxs, sels = [], []
    for _ in range(TOP_K):
        mx = jnp.max(cur, axis=0, keepdims=True)
        ix = jnp.min(jnp.where(cur == mx, rowi, E), axis=0, keepdims=True)
        hit = rowi == ix
        sels.append(jnp.sum(jnp.where(hit, scores, 0.0), axis=0, keepdims=True))
        idxs.append(ix)
        cur = jnp.where(hit, NEG, cur)
    chosen = jnp.where(jnp.logical_and(allowed > NEG, cur == NEG), 1.0, 0.0)
    sel = jnp.concatenate(sels, axis=0)
    idx_ref[...] = jnp.concatenate(idxs, axis=0)
    gate_ref[...] = sel / jnp.sum(sel, axis=0, keepdims=True) * ROUTE_SCALE

    @pl.when(pl.program_id(0) == 0)
    def _():
        cnt_ref[...] = jnp.zeros_like(cnt_ref)

    cnt_ref[...] += jnp.sum(chosen, axis=1, keepdims=True)


def _mix_out(y_f, y_b, bonus, g, ret_out, x2d, p, tm):
    T, D = x2d.shape
    W = bonus.shape[1]
    Wr = ret_out.shape[1]
    E = p["router_wt"].shape[0]
    full = lambda a: pl.BlockSpec(a.shape, lambda i, nd=a.ndim: (0,) * nd)
    consts = [p["gn_g"], p["gn_b"], p["head_avg"], p["wo_ret"], p["wo_rwkv"], p["ln1_g"], p["ln1_b"],
              p["router_wt"], p["router_b"]]
    return pl.pallas_call(
        functools.partial(_mix_body, alpha=p["alpha"]),
        grid=(T // tm,),
        in_specs=[pl.BlockSpec((tm, W), lambda i: (i, 0)),
                  pl.BlockSpec((tm, W), lambda i: (i, 0)),
                  pl.BlockSpec((tm, W), lambda i: (i, 0)),
                  pl.BlockSpec((tm, W), lambda i: (i, 0)),
                  pl.BlockSpec((tm, Wr), lambda i: (i, 0)),
                  pl.BlockSpec((tm, D), lambda i: (i, 0))] + [full(a) for a in consts],
        out_specs=[pl.BlockSpec((tm, D), lambda i: (i, 0)),
                   pl.BlockSpec((tm, D // 2), lambda i: (i, 0)),
                   pl.BlockSpec((TOP_K, tm), lambda i: (0, i)),
                   pl.BlockSpec((TOP_K, tm), lambda i: (0, i)),
                   pl.BlockSpec((E, LANES), lambda i: (0, 0))],
        out_shape=[jax.ShapeDtypeStruct((T, D), F32),
                   jax.ShapeDtypeStruct((T, D // 2), jnp.int32),
                   jax.ShapeDtypeStruct((TOP_K, T), jnp.int32),
                   jax.ShapeDtypeStruct((TOP_K, T), F32),
                   jax.ShapeDtypeStruct((E, LANES), F32)],
        compiler_params=_cparams(("arbitrary",)),
        name="mix_out",
    )(y_f, y_b, bonus, g, ret_out, x2d, *consts)


def _plan_body(idx_ref, pstart_ref, upper_ref, dest_ref, run_ref):
    @pl.when(pl.program_id(0) == 0)
    def _():
        run_ref[...] = jnp.zeros_like(run_ref)

    E = pstart_ref.shape[0]
    tm = idx_ref.shape[1]
    rowi = lax.broadcasted_iota(jnp.int32, (E, tm), 0)
    hits = [rowi == idx_ref[k:k + 1, :] for k in range(TOP_K)]
    member = jnp.zeros((E, tm), F32)
    for hit in hits:
        member = jnp.where(hit, 1.0, member)
    before = jnp.dot(member.astype(BF16), upper_ref[...], preferred_element_type=F32)
    slot = pstart_ref[...] + run_ref[:, 0:1] + before
    dest_ref[...] = jnp.concatenate(
        [jnp.sum(jnp.where(hit, slot, 0.0), axis=0, keepdims=True) for hit in hits], axis=0).astype(jnp.int32)
    run_ref[...] += jnp.sum(member, axis=1, keepdims=True)


def _slot_plan(idx_t, pad_start, tm):
    T = idx_t.shape[1]
    E = pad_start.shape[0]
    upper = (jnp.arange(tm)[:, None] < jnp.arange(tm)[None, :]).astype(BF16)
    return pl.pallas_call(
        _plan_body,
        grid=(T // tm,),
        in_specs=[pl.BlockSpec((TOP_K, tm), lambda i: (0, i)),
                  pl.BlockSpec((E, 1), lambda i: (0, 0)),
                  pl.BlockSpec((tm, tm), lambda i: (0, 0))],
        out_specs=pl.BlockSpec((TOP_K, tm), lambda i: (0, i)),
        out_shape=jax.ShapeDtypeStruct((TOP_K, T), jnp.int32),
        scratch_shapes=[pltpu.VMEM((E, LANES), F32)],
        compiler_params=_cparams(("arbitrary",)),
        name="slot_plan",
    )(idx_t, pad_start.reshape(E, 1).astype(F32), upper)


_UPPER_HALF = -65536


def _bf16_bits(x):
    return lax.bitcast_convert_type(x.astype(BF16).astype(F32), jnp.int32)


def _pack_halves(x):
    c = x.shape[1] // 2
    return _bf16_bits(x[:, :c]) | (jnp.right_shift(_bf16_bits(x[:, c:]), 16) & 0xFFFF)


def _unpack_halves(u):
    hi = lax.bitcast_convert_type(u & _UPPER_HALF, F32)
    lo = lax.bitcast_convert_type(jnp.left_shift(u, 16), F32)
    return hi, lo


SC_WINDOW = 64


def _sc_dispatch(xp, dest, n_slots):
    T, C = xp.shape
    n_win = T // SC_WINDOW
    mesh = plsc.VectorSubcoreMesh(core_axis_name="core", subcore_axis_name="subcore")

    @pl.kernel(out_type=jax.ShapeDtypeStruct((n_slots, C), xp.dtype), mesh=mesh, scratch_types=[])
    def scatter_rows(x_hbm, i_hbm, o_hbm):
        def body(x_vmem, i_vmem):
            for k in range(TOP_K):
                pltpu.sync_copy(x_vmem, o_hbm.at[i_vmem.at[k]])

        pltpu.emit_pipeline(
            body,
            grid=(n_win,),
            in_specs=[pl.BlockSpec((SC_WINDOW, C), lambda i: (i, 0)),
                      pl.BlockSpec((None, TOP_K, SC_WINDOW), lambda i: (i, 0, 0))],
            out_specs=[],
            core_axis_name=("core", "subcore"),
            dimension_semantics=(pltpu.PARALLEL,),
        )(x_hbm, i_hbm)

    idx = dest.reshape(TOP_K, n_win, SC_WINDOW).transpose(1, 0, 2)
    return scatter_rows(xp, idx)


def _sc_gather(ys, dest):
    n_idx = dest.shape[0] * dest.shape[1]
    C = ys.shape[1]
    mesh = plsc.VectorSubcoreMesh(core_axis_name="core", subcore_axis_name="subcore")

    @pl.kernel(out_type=jax.ShapeDtypeStruct((n_idx, C), ys.dtype), mesh=mesh, scratch_types=[])
    def gather_rows(y_hbm, i_hbm, o_hbm):
        def body(i_vmem, o_vmem):
            pltpu.sync_copy(y_hbm.at[i_vmem.at[0]], o_vmem)

        pltpu.emit_pipeline(
            body,
            grid=(n_idx // SC_WINDOW,),
            in_specs=[pl.BlockSpec((1, SC_WINDOW), lambda i: (i, 0))],
            out_specs=[pl.BlockSpec((SC_WINDOW, C), lambda i: (i, 0))],
            core_axis_name=("core", "subcore"),
            dimension_semantics=(pltpu.PARALLEL,),
        )(i_hbm, o_hbm)

    return gather_rows(ys, dest.reshape(n_idx // SC_WINDOW, SC_WINDOW))


MOE_IN_SLOTS = 4
MOE_OUT_SLOTS = 2


def _moe_body(nb_ref, b0_ref, cnt_ref, nu_ref, xs_hbm, wg_ref, wu_ref, wd_ref, ys_hbm,
              xbuf, obuf, wgb_ref, wub_ref, wdb_ref, in_sem, out_sem):
    G = MOE_BLOCK
    e = pl.program_id(0)
    nb = nb_ref[e]
    b0 = b0_ref[e]
    n_used = nu_ref[0]

    def in_copy(g):
        slot = lax.rem(g, MOE_IN_SLOTS)
        return pltpu.make_async_copy(xs_hbm.at[pl.ds(pl.multiple_of(g * G, G), G), :], xbuf.at[slot],
                                     in_sem.at[slot])

    def out_copy(g):
        slot = lax.rem(g, MOE_OUT_SLOTS)
        return pltpu.make_async_copy(obuf.at[slot], ys_hbm.at[pl.ds(pl.multiple_of(g * G, G), G), :],
                                     out_sem.at[slot])

    @pl.when(e == 0)
    def _():
        for g in range(MOE_IN_SLOTS - 1):
            @pl.when(g < n_used)
            def _():
                in_copy(g).start()

    @pl.when(nb > 0)
    def _():
        wgb_ref[...] = wg_ref[...].astype(BF16)
        wub_ref[...] = wu_ref[...].astype(BF16)
        wdb_ref[...] = wd_ref[...].astype(BF16)
        row = lax.broadcasted_iota(jnp.int32, (G, xbuf.shape[2]), 0)
        c = xbuf.shape[2]

        def step(j, carry):
            g = b0 + j
            in_copy(g).wait()

            @pl.when(g + (MOE_IN_SLOTS - 1) < n_used)
            def _():
                in_copy(g + (MOE_IN_SLOTS - 1)).start()

            @pl.when(g >= MOE_OUT_SLOTS)
            def _():
                out_copy(g - MOE_OUT_SLOTS).wait()

            u = jnp.where(row < cnt_ref[e] - j * G, xbuf[lax.rem(g, MOE_IN_SLOTS)], 0)
            x_hi, x_lo = _unpack_halves(u)
            gate = _dot(x_hi, wgb_ref[:c, :]) + _dot(x_lo, wgb_ref[c:, :])
            up = _dot(x_hi, wub_ref[:c, :]) + _dot(x_lo, wub_ref[c:, :])
            obuf[lax.rem(g, MOE_OUT_SLOTS)] = _pack_halves(_dot(_silu(gate) * up, wdb_ref[...]))
            out_copy(g).start()
            return carry

        lax.fori_loop(0, nb, step, 0)

    @pl.when(e == pl.num_programs(0) - 1)
    def _():
        for back in range(MOE_OUT_SLOTS, 0, -1):
            @pl.when(n_used >= back)
            def _():
                out_copy(n_used - back).wait()


def _moe_ffn(xs, blocks_per_expert, first_block, counts, n_used, w_gate, w_up, w_down):
    P, C = xs.shape
    G = MOE_BLOCK
    E, D, De = w_gate.shape
    wspec = lambda shape: pl.BlockSpec((None,) + shape, lambda e, nb, b0, cnt, nu: (e, 0, 0))
    grid_spec = pltpu.PrefetchScalarGridSpec(
        num_scalar_prefetch=4,
        grid=(E,),
        in_specs=[pl.BlockSpec(memory_space=pl.ANY), wspec((D, De)), wspec((D, De)), wspec((De, D))],
        out_specs=pl.BlockSpec(memory_space=pl.ANY),
        scratch_shapes=[pltpu.VMEM((MOE_IN_SLOTS, G, C), jnp.int32), pltpu.VMEM((MOE_OUT_SLOTS, G, C), jnp.int32),
                        pltpu.VMEM((D, De), BF16), pltpu.VMEM((D, De), BF16), pltpu.VMEM((De, D), BF16),
                        pltpu.SemaphoreType.DMA((MOE_IN_SLOTS,)), pltpu.SemaphoreType.DMA((MOE_OUT_SLOTS,))],
    )
    return pl.pallas_call(
        _moe_body,
        grid_spec=grid_spec,
        out_shape=jax.ShapeDtypeStruct((P, C), jnp.int32),
        compiler_params=_cparams(("arbitrary",)),
        name="moe_ffn",
    )(blocks_per_expert, first_block, counts, n_used, xs, w_gate, w_up, w_down)


def _comb_body(yg_ref, x1_ref, gt_ref, sg_ref, su_ref, sd_ref, l2g_ref, l2b_ref, o_ref, *, alpha):
    x1 = x1_ref[...]
    xb = x1.astype(BF16)
    shared = _dot(_silu(_dot(xb, sg_ref[...])) * _dot(xb, su_ref[...]), sd_ref[...])
    gt = gt_ref[...]
    acc_hi = acc_lo = None
    for k in range(TOP_K):
        hi, lo = _unpack_halves(yg_ref[k])
        gk = gt[:, k:k + 1]
        acc_hi = gk * hi if acc_hi is None else acc_hi + gk * hi
        acc_lo = gk * lo if acc_lo is None else acc_lo + gk * lo
    routed = jnp.concatenate([acc_hi, acc_lo], axis=1)
    o_ref[...] = _layer_norm(alpha * x1 + (routed + shared), l2g_ref[...], l2b_ref[...])


def _combine(yg, x1, gates_t, p, tm):
    T, D = x1.shape
    full = lambda a: pl.BlockSpec(a.shape, lambda i, nd=a.ndim: (0,) * nd)
    consts = [p["sh_gate"], p["sh_up"], p["sh_down"], p["ln2_g"], p["ln2_b"]]
    return pl.pallas_call(
        functools.partial(_comb_body, alpha=p["alpha"]),
        grid=(T // tm,),
        in_specs=[pl.BlockSpec((TOP_K, tm, D // 2), lambda i: (0, i, 0)),
                  pl.BlockSpec((tm, D), lambda i: (i, 0)),
                  pl.BlockSpec((tm, TOP_K), lambda i: (i, 0))] + [full(a) for a in consts],
        out_specs=pl.BlockSpec((tm, D), lambda i: (i, 0)),
        out_shape=jax.ShapeDtypeStruct((T, D), F32),
        compiler_params=_cparams(("parallel",)),
        name="combine",
    )(yg, x1, gates_t, *consts)


def _segment_layout(counts, n_tokens):
    G = MOE_BLOCK
    E = counts.shape[0]
    n_blocks = -(-(n_tokens * TOP_K + E * (G - 1)) // G)
    padded = (counts + G - 1) // G * G
    pad_end = jnp.cumsum(padded)
    pad_start = pad_end - padded
    n_used = pad_end[-1:] // G
    return (pad_start, (padded // G).astype(jnp.int32), (pad_start // G).astype(jnp.int32),
            n_used.astype(jnp.int32), n_blocks)


def _rotary_tables(s, d):
    inv = ROPE_BASE ** (-jnp.arange(0, d, 2, dtype=F32) / d)
    ang = jnp.arange(s, dtype=F32)[:, None] * inv[None, :]
    cos = jnp.cos(ang)
    sin = jnp.sin(ang)
    return jnp.concatenate([cos, cos], axis=-1), jnp.concatenate([-sin, sin], axis=-1)


def _layer_params(l, depth, w_in, ret_gn_g, ret_gn_b, rwkv_mu, rwkv_w0, rwkv_w_up, rwkv_a0, rwkv_a_up,
                  rwkv_g_up, rwkv_k_k, rwkv_k_a, rwkv_r_k, rwkv_gn_g, rwkv_gn_b, w_out, ln1_g, ln1_b,
                  router_w, router_bias, exp_w_gate, exp_w_up, exp_w_down, sh_w_gate, sh_w_up,
                  sh_w_down, ln2_g, ln2_b):
    ret_w = ret_gn_g.shape[-1]
    W = rwkv_gn_g.shape[-1]
    n_heads, hd = rwkv_r_k.shape[-2:]
    rank_w = rwkv_w_up.shape[2]
    rank_a = rwkv_a_up.shape[2]
    assert rank_w * 2 == LANES and rank_a * 2 == LANES and rwkv_g_up.shape[1] == LANES
    assert hd * 2 == LANES and SCAN_CHUNK == hd
    row = lambda a: a.reshape(1, -1).astype(F32)
    zw = jnp.zeros((rank_w, W), F32)
    za = jnp.zeros((rank_a, W), F32)
    head_id = jnp.arange(W) // hd
    same_head = (head_id[:, None] == head_id[None, :])
    wi = w_in[l]
    return dict(
        alpha=float((2 * depth) ** 0.25),
        rwkv_width=W,
        w_ret=wi[:, :4 * ret_w].astype(BF16),
        w_rwkv=wi[:, 4 * ret_w:].astype(BF16),
        ret_gn_g=row(ret_gn_g[l]), ret_gn_b=row(ret_gn_b[l]),
        logg=jnp.broadcast_to(
            jnp.log1p(-jnp.exp2(-5.0 - jnp.arange(RET_HEADS, dtype=F32)))[:, None, None],
            (RET_HEADS, 1, LANES)),
        mu=row(rwkv_mu[l]),
        wup_pad=jnp.stack([jnp.concatenate([rwkv_w_up[l, 0], zw], 0),
                           jnp.concatenate([zw, rwkv_w_up[l, 1]], 0)]).astype(BF16),
        aup_pad=jnp.stack([jnp.concatenate([rwkv_a_up[l, 0], za], 0),
                           jnp.concatenate([za, rwkv_a_up[l, 1]], 0)]).astype(BF16),
        w0=rwkv_w0[l].astype(F32), a0=rwkv_a0[l].astype(F32),
        gup=rwkv_g_up[l].astype(BF16),
        k_k=row(rwkv_k_k[l]), k_a=row(rwkv_k_a[l]), r_k=row(rwkv_r_k[l]),
        head_ones=same_head.astype(BF16),
        head_avg=(same_head.astype(F32) / hd).astype(BF16),
        gn_g=row(rwkv_gn_g[l]), gn_b=row(rwkv_gn_b[l]),
        wo_ret=w_out[l, :ret_w].astype(BF16), wo_rwkv=w_out[l, ret_w:].astype(BF16),
        ln1_g=row(ln1_g[l]), ln1_b=row(ln1_b[l]),
        router_wt=router_w[l].T.astype(BF16), router_b=router_bias[l].reshape(-1, 1).astype(F32),
        exp_gate=exp_w_gate[l], exp_up=exp_w_up[l], exp_down=exp_w_down[l],
        sh_gate=sh_w_gate[l].astype(BF16), sh_up=sh_w_up[l].astype(BF16), sh_down=sh_w_down[l].astype(BF16),
        ln2_g=row(ln2_g[l]), ln2_b=row(ln2_b[l]),
    )


def _pick(n, pref):
    t = min(n, pref)
    while n % t:
        t //= 2
    return t


def _layer(x, p):
    b, s, D = x.shape
    T = b * s
    x2d = x.reshape(T, D)
    tm = _pick(T, 256)
    z_ret, r, v, kk, lw, kd, kka, bonus, g = _proj_prep(x, p, _pick(s, 256))
    cos, sin = _rotary_tables(s, RET_CHUNK)
    ret_out = _retention(z_ret, cos, sin, p["logg"], p["ret_gn_g"], p["ret_gn_b"])
    y_f, y_b = _rwkv_scan(r, kk, v, lw, kd, kka, _pick(s, 512))
    W = p["rwkv_width"]
    x1, xp, idx_t, gates, cnt = _mix_out(y_f.reshape(T, W), y_b.reshape(T, W), bonus.reshape(T, W),
                                         g.reshape(T, W), ret_out.reshape(T, -1), x2d, p, tm)
    counts = cnt[:, 0].astype(jnp.int32)
    pad_start, blocks_per_expert, first_block, n_used, n_blocks = _segment_layout(counts, T)
    dest = _slot_plan(idx_t, pad_start, _pick(T, 512))
    xs = _sc_dispatch(xp, dest, n_blocks * MOE_BLOCK)
    ys = _moe_ffn(xs, blocks_per_expert, first_block, counts, n_used, p["exp_gate"], p["exp_up"], p["exp_down"])
    yg = _sc_gather(ys, dest).reshape(TOP_K, T, D // 2)
    out = _combine(yg, x1, gates.T, p, _pick(T, 256))
    return out.reshape(b, s, D)


def kernel(x_prompt, x_sample, w_in, ret_gn_g, ret_gn_b, rwkv_mu, rwkv_w0, rwkv_w_up, rwkv_a0, rwkv_a_up,
           rwkv_g_up, rwkv_k_k, rwkv_k_a, rwkv_r_k, rwkv_gn_g, rwkv_gn_b, w_out, ln1_g, ln1_b, router_w,
           router_bias, exp_w_gate, exp_w_up, exp_w_down, sh_w_gate, sh_w_up, sh_w_down, ln2_g, ln2_b):
    weights = (w_in, ret_gn_g, ret_gn_b, rwkv_mu, rwkv_w0, rwkv_w_up, rwkv_a0, rwkv_a_up, rwkv_g_up,
               rwkv_k_k, rwkv_k_a, rwkv_r_k, rwkv_gn_g, rwkv_gn_b, w_out, ln1_g, ln1_b, router_w,
               router_bias, exp_w_gate, exp_w_up, exp_w_down, sh_w_gate, sh_w_up, sh_w_down, ln2_g, ln2_b)
    depth = w_in.shape[0]
    layers = [_layer_params(l, depth, *weights) for l in range(depth)]

    def trunk(x):
        for p in layers:
            x = _layer(x, p)
        return x

    return trunk(x_prompt), trunk(x_sample)
```

```python
import functools
import math

import jax
import jax.numpy as jnp
from jax import lax
from jax.experimental import pallas as pl
from jax.experimental.pallas import tpu as pltpu
from jax.experimental.pallas import tpu_sc as plsc

F32 = jnp.float32
BF16 = jnp.bfloat16

RET_HEADS = 4
RET_CHUNK = 128
ROPE_BASE = 10000.0
TOP_K = 8
N_GROUPS = 8
TOPK_GROUPS = 4
ROUTE_SCALE = 2.5
MOE_BLOCK = 256
LN_EPS = 1e-5
GN_EPS = 1e-5
RWKV_GN_EPS = 64e-5

LANES = 128
SUBLANES = 8
VMEM_LIMIT_BYTES = 56 * 1024 * 1024

SCAN_CHUNK = 64


def _cparams(semantics):
    return pltpu.CompilerParams(dimension_semantics=semantics, vmem_limit_bytes=VMEM_LIMIT_BYTES)


def _dot(a, b):
    return jnp.dot(a.astype(BF16), b.astype(BF16), preferred_element_type=F32)


def _dot_nt(a, b):
    return lax.dot_general(a.astype(BF16), b.astype(BF16), (((1,), (1,)), ((), ())),
                           preferred_element_type=F32)


def _dot_exact_rhs(x, w_bf16):
    hi = x.astype(BF16)
    mid = (x - hi.astype(F32)).astype(BF16)
    return (jnp.dot(hi, w_bf16, preferred_element_type=F32)
            + jnp.dot(mid, w_bf16, preferred_element_type=F32))


def _sigmoid(x):
    return 1.0 / (1.0 + jnp.exp(-x))


def _silu(x):
    return x * _sigmoid(x)


def _layer_norm(h, g, b):
    mu = jnp.mean(h, axis=-1, keepdims=True)
    d = h - mu
    var = jnp.mean(d * d, axis=-1, keepdims=True)
    return d * lax.rsqrt(var + LN_EPS) * g + b


RET_GROUP = 4


def _ret_body(q_ref, k_ref, v_ref, gt_ref, cos_ref, sin_ref, lg_ref, gg_ref, gb_ref, o_ref,
              qs_ref, sf_ref, sb_ref, acc_ref, *, qscale):
    C = RET_CHUNK
    U = RET_GROUP
    s = q_ref.shape[0]
    n = s // C
    lg = lg_ref[...]
    pos = lax.broadcasted_iota(jnp.int32, (C, C), 0).astype(F32)
    col = lax.broadcasted_iota(jnp.int32, (C, C), 1).astype(F32)
    sc_q_prev = jnp.exp((pos + 1.0) * lg)
    sc_k_fwd = jnp.exp((C - 1.0 - pos) * lg)
    sc_k_bwd = jnp.exp(pos * lg)
    sc_q_next = jnp.exp((C - pos) * lg)
    g_chunk = jnp.exp(float(C) * lg)
    decay = jnp.exp(jnp.abs(pos - col) * lg)
    gg = gg_ref[...]
    gb = gb_ref[...]

    def rows_of(grp):
        return [pl.ds(pl.multiple_of((grp * U + u) * C, C), C) for u in range(U)]

    def rot(x, r):
        x = x.astype(F32)
        return x * cos_ref[r, :] + pltpu.roll(x, C // 2, 1) * sin_ref[r, :]

    def local(grp, carry):
        rows = rows_of(grp)
        q = [rot(q_ref[r, :], r) * qscale for r in rows]
        k = [rot(k_ref[r, :], r) for r in rows]
        vb = [v_ref[r, :].astype(BF16) for r in rows]
        sc = [_dot_nt(q[u], k[u]) * decay for u in range(U)]
        out = [_dot(sc[u], vb[u]) for u in range(U)]
        kf = [_dot((k[u] * sc_k_fwd).T, vb[u]) for u in range(U)]
        kb = [_dot((k[u] * sc_k_bwd).T, vb[u]) for u in range(U)]
        for u in range(U):
            qs_ref[rows[u], :] = q[u]
            acc_ref[rows[u], :] = out[u]
            sf_ref[grp * U + u] = kf[u]
            sb_ref[grp * U + u] = kb[u]
        return carry

    lax.fori_loop(0, n // U, local, 0)

    def fwd(c, S):
        kv = sf_ref[c]
        sf_ref[c] = S
        return S * g_chunk + kv

    lax.fori_loop(0, n, fwd, jnp.zeros((C, C), F32))

    def bwd(i, S):
        c = n - 1 - i
        kv = sb_ref[c]
        sb_ref[c] = S
        return S * g_chunk + kv

    lax.fori_loop(0, n, bwd, jnp.zeros((C, C), F32))

    def cross(grp, carry):
        rows = rows_of(grp)
        q = [qs_ref[r, :] for r in rows]
        y = [acc_ref[rows[u], :] + _dot(jnp.concatenate([q[u] * sc_q_prev, q[u] * sc_q_next], axis=1),
                                        jnp.concatenate([sf_ref[grp * U + u], sb_ref[grp * U + u]], axis=0))
             for u in range(U)]
        for u in range(U):
            mu = jnp.mean(y[u], axis=-1, keepdims=True)
            d = y[u] - mu
            var = jnp.mean(d * d, axis=-1, keepdims=True)
            yn = d * lax.rsqrt(var + GN_EPS) * gg + gb
            o_ref[rows[u], :] = _silu(gt_ref[rows[u], :].astype(F32)) * yn
        return carry

    lax.fori_loop(0, n // U, cross, 0)


def _retention(z_ret, cos, sin, logg, gn_g, gn_b):
    b, s, _ = z_ret.shape
    C = RET_CHUNK
    H = RET_HEADS
    assert C == LANES and s % (C * RET_GROUP) == 0
    n = s // C
    blk = lambda off: pl.BlockSpec((None, s, C), lambda i, h, off=off: (i, 0, off + h))
    return pl.pallas_call(
        functools.partial(_ret_body, qscale=float(C) ** -0.5),
        grid=(b, H),
        in_specs=[blk(0), blk(H), blk(2 * H), blk(3 * H),
                  pl.BlockSpec((s, C), lambda i, h: (0, 0)),
                  pl.BlockSpec((s, C), lambda i, h: (0, 0)),
                  pl.BlockSpec((None, 1, C), lambda i, h: (h, 0, 0)),
                  pl.BlockSpec((1, C), lambda i, h: (0, h)),
                  pl.BlockSpec((1, C), lambda i, h: (0, h))],
        out_specs=pl.BlockSpec((None, s, C), lambda i, h: (i, 0, h)),
        out_shape=jax.ShapeDtypeStruct((b, s, H * C), F32),
        scratch_shapes=[pltpu.VMEM((s, C), F32), pltpu.VMEM((n, C, C), F32), pltpu.VMEM((n, C, C), F32),
                        pltpu.VMEM((s, C), F32)],
        compiler_params=_cparams(("parallel", "parallel")),
        name="retention",
    )(z_ret, z_ret, z_ret, z_ret, cos, sin, logg, gn_g, gn_b)


def _prep_body(x_ref, xp_ref, xn_ref, wr_ref, ww_ref, mu_ref, wup_ref, aup_ref, w0_ref, a0_ref, gup_ref,
               kkp_ref, ka_ref, rk_ref, bd_ref,
               zr_o, r_o, v_o, kk_o, lw_o, kd_o, kka_o, bonus_o, g_o, *, width):
    t = pl.program_id(1)
    nt = pl.num_programs(1)
    W = width
    ts = x_ref.shape[0]
    x_all = jnp.concatenate([x_ref[...], xp_ref[...], xn_ref[...]], axis=0).astype(BF16)
    zr_o[...] = jnp.dot(x_all[:ts], wr_ref[...], preferred_element_type=F32).astype(zr_o.dtype)
    z_all = jnp.dot(x_all, ww_ref[...], preferred_element_type=F32)
    z = z_all[:ts]
    row = lax.broadcasted_iota(jnp.int32, (ts, 1), 0)
    prev_row = jnp.where(t > 0, z_all[ts + SUBLANES - 1:ts + SUBLANES, :], 0.0)
    next_row = jnp.where(t < nt - 1, z_all[ts + SUBLANES:ts + SUBLANES + 1, :], 0.0)
    prev = jnp.where(row == 0, prev_row, pltpu.roll(z, 1, 0))
    nxt = jnp.where(row == ts - 1, next_row, pltpu.roll(z, ts - 1, 0))
    zs = z + mu_ref[...] * (0.5 * (prev + nxt) - z)

    r = zs[:, 0:W]
    kx = zs[:, W:2 * W]
    vx = zs[:, 2 * W:3 * W]
    wd = jnp.tanh(zs[:, 3 * W:3 * W + LANES])
    ad = zs[:, 3 * W + LANES:3 * W + 2 * LANES]
    gd = _sigmoid(zs[:, 3 * W + 2 * LANES:3 * W + 3 * LANES])
    bd = bd_ref[...]

    kk = kx * kkp_ref[...]
    ssq = jnp.dot((kk * kk).astype(BF16), bd, preferred_element_type=F32)
    kk = kk * lax.rsqrt(jnp.maximum(ssq, 1e-24))
    ka = ka_ref[...]
    ksum = None
    for d in range(2):
        pre = w0_ref[d:d + 1, :] + _dot(wd, wup_ref[d])
        lw_o[d] = -math.exp(-0.5) * _sigmoid(pre)
        a = _sigmoid(a0_ref[d:d + 1, :] + _dot(ad, aup_ref[d]))
        kd = kx * (1.0 + (a - 1.0) * ka)
        kd_o[d] = kd.astype(kd_o.dtype)
        kka_o[d] = (kk * a).astype(kka_o.dtype)
        ksum = kd if ksum is None else ksum + kd
    r_o[...] = r.astype(r_o.dtype)
    v_o[...] = vx.astype(v_o.dtype)
    kk_o[...] = kk.astype(kk_o.dtype)
    bonus_o[...] = (jnp.dot((r * ksum * rk_ref[...]).astype(BF16), bd, preferred_element_type=F32)
                    * vx).astype(bonus_o.dtype)
    g_o[...] = _dot(gd, gup_ref[...]).astype(g_o.dtype)


def _proj_prep(x, p, ts):
    b, s, D = x.shape
    W = p["rwkv_width"]
    nr = p["w_ret"].shape[1]
    nt = s // ts
    hb = ts // SUBLANES
    last = s // SUBLANES - 1
    full = lambda a: pl.BlockSpec(a.shape, lambda i, t, nd=a.ndim: (0,) * nd)
    out_tok = pl.BlockSpec((None, ts, W), lambda i, t: (i, t, 0))
    out_dir = pl.BlockSpec((2, None, ts, W), lambda i, t: (0, i, t, 0))
    tok_shape = jax.ShapeDtypeStruct((b, s, W), BF16)
    dir_shape = jax.ShapeDtypeStruct((2, b, s, W), BF16)
    lw_shape = jax.ShapeDtypeStruct((2, b, s, W), F32)
    consts = [p["w_ret"], p["w_rwkv"], p["mu"], p["wup_pad"], p["aup_pad"], p["w0"], p["a0"], p["gup"],
              p["k_k"], p["k_a"], p["r_k"], p["head_ones"]]
    return pl.pallas_call(
        functools.partial(_prep_body, width=W),
        grid=(b, nt),
        in_specs=[pl.BlockSpec((None, ts, D), lambda i, t: (i, t, 0)),
                  pl.BlockSpec((None, SUBLANES, D), lambda i, t: (i, jnp.maximum(t * hb - 1, 0), 0)),
                  pl.BlockSpec((None, SUBLANES, D), lambda i, t: (i, jnp.minimum((t + 1) * hb, last), 0)),
                  ] + [full(a) for a in consts],
        out_specs=[pl.BlockSpec((None, ts, nr), lambda i, t: (i, t, 0)),
                   out_tok, out_tok, out_tok, out_dir, out_dir, out_dir, out_tok, out_tok],
        out_shape=[jax.ShapeDtypeStruct((b, s, nr), BF16),
                   tok_shape, tok_shape, tok_shape, lw_shape, dir_shape, dir_shape, tok_shape, tok_shape],
        compiler_params=_cparams(("parallel", "parallel")),
        name="proj_prep",
    )(x, x, x, *consts)


def _scan_body(rf_ref, kkf_ref, vf_ref, rb_ref, kkb_ref, vb_ref, lwf_ref, kdf_ref, kkaf_ref,
               lwb_ref, kdb_ref, kkab_ref, yf_ref, yb_ref, st_ref, *, n_pairs):
    L = SCAN_CHUNK
    H = 2 * L
    assert H == LANES
    tt = rf_ref.shape[0]
    nch = tt // L

    @pl.when(pl.program_id(1) == 0)
    def _():
        st_ref[...] = jnp.zeros_like(st_ref)

    r_refs, kk_refs, v_refs = (rf_ref, rb_ref), (kkf_ref, kkb_ref), (vf_ref, vb_ref)
    lw_refs, kd_refs, kka_refs = (lwf_ref, lwb_ref), (kdf_ref, kdb_ref), (kkaf_ref, kkab_ref)
    y_refs = (yf_ref, yb_ref)
    ii = lax.broadcasted_iota(jnp.int32, (H, H), 0)
    jj = lax.broadcasted_iota(jnp.int32, (H, H), 1)
    same = (ii < L) == (jj < L)
    strict = (jnp.logical_and(same, ii > jj), jnp.logical_and(same, ii < jj))
    incl = (jnp.logical_and(same, ii >= jj), jnp.logical_and(same, ii <= jj))
    eye = ii == jj
    li = lax.broadcasted_iota(jnp.int32, (L, L), 0)
    lj = lax.broadcasted_iota(jnp.int32, (L, L), 1)
    tri = (jnp.where(li >= lj, 1.0, 0.0).astype(BF16), jnp.where(li <= lj, 1.0, 0.0).astype(BF16))
    head0 = lax.broadcasted_iota(jnp.int32, (L, H), 1) < L

    def stack(x):
        return jnp.concatenate([jnp.where(head0, x, 0.0), jnp.where(head0, 0.0, x)], axis=0)

    chains = [(d, hp) for d in range(2) for hp in range(n_pairs)]
    P = range(len(chains))
    lanes = [slice(hp * H, (hp + 1) * H) for _, hp in chains]
    dirs = [d for d, _ in chains]

    def chunk(j, carry):
        rows = (pl.ds(pl.multiple_of(j * L, L), L), pl.ds(pl.multiple_of((nch - 1 - j) * L, L), L))
        ld = lambda refs, h: refs[dirs[h]][rows[dirs[h]], lanes[h]].astype(F32)
        lw = [ld(lw_refs, h) for h in P]
        lw_hi = [x.astype(BF16) for x in lw]
        cum2 = [jnp.dot(tri[dirs[h]],
                        jnp.concatenate([lw_hi[h], (lw[h] - lw_hi[h].astype(F32)).astype(BF16)], axis=1),
                        preferred_element_type=F32) for h in P]
        cum = [c2[:, :H] + c2[:, H:] for c2 in cum2]
        tot = [jnp.sum(x, axis=0, keepdims=True) for x in lw]
        e_incl = [jnp.exp(a) for a in cum]
        e_excl = [jnp.exp(a - x) for a, x in zip(cum, lw)]
        e_inv = [jnp.exp(-a) for a in cum]
        e_rem = [jnp.exp(t_ - a) for t_, a in zip(tot, cum)]
        kk = [ld(kk_refs, h) for h in P]
        kka = [ld(kka_refs, h) for h in P]
        kd = [ld(kd_refs, h) for h in P]
        Kk = [stack(a * e) for a, e in zip(kk, e_excl)]
        R = [stack(ld(r_refs, h) * e_incl[h]) for h in P]
        B = [stack(a * e) for a, e in zip(kka, e_inv)]
        Kd = [stack(a * e) for a, e in zip(kd, e_inv)]
        Bh = [stack(a * e).T.astype(BF16) for a, e in zip(kka, e_rem)]
        Kh = [stack(a * e).T.astype(BF16) for a, e in zip(kd, e_rem)]
        V = [stack(ld(v_refs, h)).astype(BF16) for h in P]

        gram = [_dot_nt(jnp.concatenate([Kk[h], R[h]], axis=0), jnp.concatenate([B[h], Kd[h]], axis=0))
                for h in P]
        Np = [jnp.where(strict[dirs[h]], gram[h][:H, :H], 0.0) for h in P]
        AV = [_dot(jnp.where(strict[dirs[h]], gram[h][:H, H:], 0.0), V[h]) for h in P]
        DC = [jnp.concatenate([jnp.where(incl[dirs[h]], gram[h][H:, H:], 0.0).astype(BF16),
                               jnp.where(incl[dirs[h]], gram[h][H:, :H], 0.0).astype(BF16)], axis=1) for h in P]
        KB = [jnp.concatenate([Kh[h], Bh[h]], axis=1) for h in P]
        n_fac = L.bit_length() - 1
        Tm = [jnp.where(eye, 1.0, 0.0) - a for a in Np]
        Np = [_dot(a, a) for a in Np]
        for f in range(1, n_fac):
            if f < n_fac - 1:
                both = [_dot(Np[h], jnp.concatenate([Tm[h], Np[h]], axis=1)) for h in P]
                Tm = [t_ + o[:, :H] for t_, o in zip(Tm, both)]
                Np = [o[:, H:] for o in both]
            else:
                Tm = [t_ + _dot(a, t_) for t_, a in zip(Tm, Np)]
        Wm = [_dot(Tm[h], jnp.concatenate([Kk[h], AV[h]], axis=1)) for h in P]
        WR = [_dot(jnp.concatenate([Wm[h][:, :H], R[h]], axis=0), st_ref[h]) for h in P]
        VU = [jnp.concatenate([V[h], (-(WR[h][:H] + Wm[h][:, H:])).astype(BF16)], axis=0) for h in P]
        Ys = [WR[h][H:] + _dot(DC[h], VU[h]) for h in P]
        Sn = [_dot(KB[h], VU[h]) for h in P]
        for h in P:
            pl_col = jnp.sum(jnp.where(eye, jnp.exp(tot[h]), 0.0), axis=1, keepdims=True)
            st_ref[h] = pl_col * st_ref[h] + Sn[h]
            y_refs[dirs[h]][rows[dirs[h]], lanes[h]] = Ys[h][:L] + Ys[h][L:]
        return carry

    lax.fori_loop(0, nch, chunk, 0)


def _rwkv_scan(r, kk, v, lw, kd, kka, tt):
    b, s, W = r.shape
    nt = s // tt
    n_pairs = W // LANES
    tok_f = pl.BlockSpec((None, tt, W), lambda i, t: (i, t, 0))
    tok_b = pl.BlockSpec((None, tt, W), lambda i, t: (i, nt - 1 - t, 0))
    dir_f = pl.BlockSpec((None, None, tt, W), lambda i, t: (0, i, t, 0))
    dir_b = pl.BlockSpec((None, None, tt, W), lambda i, t: (1, i, nt - 1 - t, 0))
    out = jax.ShapeDtypeStruct((b, s, W), F32)
    return pl.pallas_call(
        functools.partial(_scan_body, n_pairs=n_pairs),
        grid=(b, nt),
        in_specs=[tok_f, tok_f, tok_f, tok_b, tok_b, tok_b, dir_f, dir_f, dir_f, dir_b, dir_b, dir_b],
        out_specs=[tok_f, tok_b],
        out_shape=[out, out],
        scratch_shapes=[pltpu.VMEM((2 * n_pairs, LANES, LANES), F32)],
        compiler_params=_cparams(("parallel", "arbitrary")),
        name="rwkv_scan",
    )(r, kk, v, r, kk, v, lw, kd, kka, lw, kd, kka)


def _mix_body(yf_ref, yb_ref, bonus_ref, g_ref, ret_ref, x_ref, gng_ref, gnb_ref, avg_ref, wo1_ref, wo2_ref,
              l1g_ref, l1b_ref, rwt_ref, rb_ref, x1_ref, xp_ref, idx_ref, gate_ref, cnt_ref, *, alpha):
    y = yf_ref[...] + yb_ref[...]
    avg = avg_ref[...]
    mu = _dot_exact_rhs(y, avg)
    dl = y - mu
    var = jnp.dot((dl * dl).astype(BF16), avg, preferred_element_type=F32)
    yn = dl * lax.rsqrt(var + RWKV_GN_EPS) * gng_ref[...] + gnb_ref[...]
    rw = (yn + bonus_ref[...]) * g_ref[...]
    m = _dot(ret_ref[...], wo1_ref[...]) + _dot(rw, wo2_ref[...])
    x1 = _layer_norm(alpha * x_ref[...] + m, l1g_ref[...], l1b_ref[...])
    x1_ref[...] = x1
    xp_ref[...] = _pack_halves(x1)

    scores = _sigmoid(_dot_nt(rwt_ref[...], x1))
    E, tm = scores.shape
    GS = E // N_GROUPS
    NEG = -jnp.inf
    biased = scores + rb_ref[...]
    rowi = lax.broadcasted_iota(jnp.int32, (E, tm), 0)
    ri = lax.broadcasted_iota(jnp.int32, (GS, tm), 0)
    gs_rows = []
    for gi in range(N_GROUPS):
        blk = biased[gi * GS:(gi + 1) * GS, :]
        m1 = jnp.max(blk, axis=0, keepdims=True)
        i1 = jnp.min(jnp.where(blk == m1, ri, GS), axis=0, keepdims=True)
        m2 = jnp.max(jnp.where(ri == i1, NEG, blk), axis=0, keepdims=True)
        gs_rows.append(m1 + m2)
    cur = jnp.concatenate(gs_rows, axis=0)
    gidx = lax.broadcasted_iota(jnp.int32, (N_GROUPS, tm), 0)
    top_groups = []
    for _ in range(TOPK_GROUPS):
        mx = jnp.max(cur, axis=0, keepdims=True)
        ix = jnp.min(jnp.where(cur == mx, gidx, N_GROUPS), axis=0, keepdims=True)
        top_groups.append(ix)
        cur = jnp.where(gidx == ix, NEG, cur)
    blocks = []
    for gi in range(N_GROUPS):
        keep = top_groups[0] == gi
        for ix in top_groups[1:]:
            keep = jnp.logical_or(keep, ix == gi)
        blocks.append(jnp.where(keep, biased[gi * GS:(gi + 1) * GS, :], NEG))
    allowed = jnp.concatenate(blocks, axis=0)
    cur = allowed
    idxs, sels = [], []
    for _ in range(TOP_K):
        mx = jnp.max(cur, axis=0, keepdims=True)
        ix = jnp.min(jnp.where(cur == mx, rowi, E), axis=0, keepdims=True)
        hit = rowi == ix
        sels.append(jnp.sum(jnp.where(hit, scores, 0.0), axis=0, keepdims=True))
        idxs.append(ix)
        cur = jnp.where(hit, NEG, cur)
    chosen = jnp.where(jnp.logical_and(allowed > NEG, cur == NEG), 1.0, 0.0)
    sel = jnp.concatenate(sels, axis=0)
    idx_ref[...] = jnp.concatenate(idxs, axis=0)
    gate_ref[...] = sel / jnp.sum(sel, axis=0, keepdims=True) * ROUTE_SCALE

    @pl.when(pl.program_id(0) == 0)
    def _():
        cnt_ref[...] = jnp.zeros_like(cnt_ref)

    cnt_ref[...] += jnp.sum(chosen, axis=1, keepdims=True)


def _mix_out(y_f, y_b, bonus, g, ret_out, x2d, p, tm):
    T, D = x2d.shape
    W = bonus.shape[1]
    Wr = ret_out.shape[1]
    E = p["router_wt"].shape[0]
    full = lambda a: pl.BlockSpec(a.shape, lambda i, nd=a.ndim: (0,) * nd)
    consts = [p["gn_g"], p["gn_b"], p["head_avg"], p["wo_ret"], p["wo_rwkv"], p["ln1_g"], p["ln1_b"],
              p["router_wt"], p["router_b"]]
    return pl.pallas_call(
        functools.partial(_mix_body, alpha=p["alpha"]),
        grid=(T // tm,),
        in_specs=[pl.BlockSpec((tm, W), lambda i: (i, 0)),
                  pl.BlockSpec((tm, W), lambda i: (i, 0)),
                  pl.BlockSpec((tm, W), lambda i: (i, 0)),
                  pl.BlockSpec((tm, W), lambda i: (i, 0)),
                  pl.BlockSpec((tm, Wr), lambda i: (i, 0)),
                  pl.BlockSpec((tm, D), lambda i: (i, 0))] + [full(a) for a in consts],
        out_specs=[pl.BlockSpec((tm, D), lambda i: (i, 0)),
                   pl.BlockSpec((tm, D // 2), lambda i: (i, 0)),
                   pl.BlockSpec((TOP_K, tm), lambda i: (0, i)),
                   pl.BlockSpec((TOP_K, tm), lambda i: (0, i)),
                   pl.BlockSpec((E, LANES), lambda i: (0, 0))],
        out_shape=[jax.ShapeDtypeStruct((T, D), F32),
                   jax.ShapeDtypeStruct((T, D // 2), jnp.int32),
                   jax.ShapeDtypeStruct((TOP_K, T), jnp.int32),
                   jax.ShapeDtypeStruct((TOP_K, T), F32),
                   jax.ShapeDtypeStruct((E, LANES), F32)],
        compiler_params=_cparams(("arbitrary",)),
        name="mix_out",
    )(y_f, y_b, bonus, g, ret_out, x2d, *consts)


def _plan_body(idx_ref, pstart_ref, upper_ref, dest_ref, run_ref):
    @pl.when(pl.program_id(0) == 0)
    def _():
        run_ref[...] = jnp.zeros_like(run_ref)

    E = pstart_ref.shape[0]
    tm = idx_ref.shape[1]
    rowi = lax.broadcasted_iota(jnp.int32, (E, tm), 0)
    hits = [rowi == idx_ref[k:k + 1, :] for k in range(TOP_K)]
    member = jnp.zeros((E, tm), F32)
    for hit in hits:
        member = jnp.where(hit, 1.0, member)
    before = jnp.dot(member.astype(BF16), upper_ref[...], preferred_element_type=F32)
    slot = pstart_ref[...] + run_ref[:, 0:1] + before
    dest_ref[...] = jnp.concatenate(
        [jnp.sum(jnp.where(hit, slot, 0.0), axis=0, keepdims=True) for hit in hits], axis=0).astype(jnp.int32)
    run_ref[...] += jnp.sum(member, axis=1, keepdims=True)


def _slot_plan(idx_t, pad_start, tm):
    T = idx_t.shape[1]
    E = pad_start.shape[0]
    upper = (jnp.arange(tm)[:, None] < jnp.arange(tm)[None, :]).astype(BF16)
    return pl.pallas_call(
        _plan_body,
        grid=(T // tm,),
        in_specs=[pl.BlockSpec((TOP_K, tm), lambda i: (0, i)),
                  pl.BlockSpec((E, 1), lambda i: (0, 0)),
                  pl.BlockSpec((tm, tm), lambda i: (0, 0))],
        out_specs=pl.BlockSpec((TOP_K, tm), lambda i: (0, i)),
        out_shape=jax.ShapeDtypeStruct((TOP_K, T), jnp.int32),
        scratch_shapes=[pltpu.VMEM((E, LANES), F32)],
        compiler_params=_cparams(("arbitrary",)),
        name="slot_plan",
    )(idx_t, pad_start.reshape(E, 1).astype(F32), upper)


_UPPER_HALF = -65536


def _bf16_bits(x):
    return lax.bitcast_convert_type(x.astype(BF16).astype(F32), jnp.int32)


def _pack_halves(x):
    c = x.shape[1] // 2
    return _bf16_bits(x[:, :c]) | (jnp.right_shift(_bf16_bits(x[:, c:]), 16) & 0xFFFF)


def _unpack_halves(u):
    hi = lax.bitcast_convert_type(u & _UPPER_HALF, F32)
    lo = lax.bitcast_convert_type(jnp.left_shift(u, 16), F32)
    return hi, lo


SC_WINDOW = 64


def _sc_dispatch(xp, dest, n_slots):
    T, C = xp.shape
    n_win = T // SC_WINDOW
    mesh = plsc.VectorSubcoreMesh(core_axis_name="core", subcore_axis_name="subcore")

    @pl.kernel(out_type=jax.ShapeDtypeStruct((n_slots, C), xp.dtype), mesh=mesh, scratch_types=[])
    def scatter_rows(x_hbm, i_hbm, o_hbm):
        def body(x_vmem, i_vmem):
            for k in range(TOP_K):
                pltpu.sync_copy(x_vmem, o_hbm.at[i_vmem.at[k]])

        pltpu.emit_pipeline(
            body,
            grid=(n_win,),
            in_specs=[pl.BlockSpec((SC_WINDOW, C), lambda i: (i, 0)),
                      pl.BlockSpec((None, TOP_K, SC_WINDOW), lambda i: (i, 0, 0))],
            out_specs=[],
            core_axis_name=("core", "subcore"),
            dimension_semantics=(pltpu.PARALLEL,),
        )(x_hbm, i_hbm)

    idx = dest.reshape(TOP_K, n_win, SC_WINDOW).transpose(1, 0, 2)
    return scatter_rows(xp, idx)


def _sc_gather(ys, dest):
    n_idx = dest.shape[0] * dest.shape[1]
    C = ys.shape[1]
    mesh = plsc.VectorSubcoreMesh(core_axis_name="core", subcore_axis_name="subcore")

    @pl.kernel(out_type=jax.ShapeDtypeStruct((n_idx, C), ys.dtype), mesh=mesh, scratch_types=[])
    def gather_rows(y_hbm, i_hbm, o_hbm):
        def body(i_vmem, o_vmem):
            pltpu.sync_copy(y_hbm.at[i_vmem.at[0]], o_vmem)

        pltpu.emit_pipeline(
            body,
            grid=(n_idx // SC_WINDOW,),
            in_specs=[pl.BlockSpec((1, SC_WINDOW), lambda i: (i, 0))],
            out_specs=[pl.BlockSpec((SC_WINDOW, C), lambda i: (i, 0))],
            core_axis_name=("core", "subcore"),
            dimension_semantics=(pltpu.PARALLEL,),
        )(i_hbm, o_hbm)

    return gather_rows(ys, dest.reshape(n_idx // SC_WINDOW, SC_WINDOW))


MOE_IN_SLOTS = 4
MOE_OUT_SLOTS = 2


def _moe_body(nb_ref, b0_ref, cnt_ref, nu_ref, xs_hbm, wg_ref, wu_ref, wd_ref, ys_hbm,
              xbuf, obuf, wgb_ref, wub_ref, wdb_ref, in_sem, out_sem):
    G = MOE_BLOCK
    e = pl.program_id(0)
    nb = nb_ref[e]
    b0 = b0_ref[e]
    n_used = nu_ref[0]

    def in_copy(g):
        slot = lax.rem(g, MOE_IN_SLOTS)
        return pltpu.make_async_copy(xs_hbm.at[pl.ds(pl.multiple_of(g * G, G), G), :], xbuf.at[slot],
                                     in_sem.at[slot])

    def out_copy(g):
        slot = lax.rem(g, MOE_OUT_SLOTS)
        return pltpu.make_async_copy(obuf.at[slot], ys_hbm.at[pl.ds(pl.multiple_of(g * G, G), G), :],
                                     out_sem.at[slot])

    @pl.when(e == 0)
    def _():
        for g in range(MOE_IN_SLOTS - 1):
            @pl.when(g < n_used)
            def _():
                in_copy(g).start()

    @pl.when(nb > 0)
    def _():
        wgb_ref[...] = wg_ref[...].astype(BF16)
        wub_ref[...] = wu_ref[...].astype(BF16)
        wdb_ref[...] = wd_ref[...].astype(BF16)
        row = lax.broadcasted_iota(jnp.int32, (G, xbuf.shape[2]), 0)
        c = xbuf.shape[2]

        def step(j, carry):
            g = b0 + j
            in_copy(g).wait()

            @pl.when(g + (MOE_IN_SLOTS - 1) < n_used)
            def _():
                in_copy(g + (MOE_IN_SLOTS - 1)).start()

            @pl.when(g >= MOE_OUT_SLOTS)
            def _():
                out_copy(g - MOE_OUT_SLOTS).wait()

            u = jnp.where(row < cnt_ref[e] - j * G, xbuf[lax.rem(g, MOE_IN_SLOTS)], 0)
            x_hi, x_lo = _unpack_halves(u)
            gate = _dot(x_hi, wgb_ref[:c, :]) + _dot(x_lo, wgb_ref[c:, :])
            up = _dot(x_hi, wub_ref[:c, :]) + _dot(x_lo, wub_ref[c:, :])
            obuf[lax.rem(g, MOE_OUT_SLOTS)] = _pack_halves(_dot(_silu(gate) * up, wdb_ref[...]))
            out_copy(g).start()
            return carry

        lax.fori_loop(0, nb, step, 0)

    @pl.when(e == pl.num_programs(0) - 1)
    def _():
        for back in range(MOE_OUT_SLOTS, 0, -1):
            @pl.when(n_used >= back)
            def _():
                out_copy(n_used - back).wait()


def _moe_ffn(xs, blocks_per_expert, first_block, counts, n_used, w_gate, w_up, w_down):
    P, C = xs.shape
    G = MOE_BLOCK
    E, D, De = w_gate.shape
    wspec = lambda shape: pl.BlockSpec((None,) + shape, lambda e, nb, b0, cnt, nu: (e, 0, 0))
    grid_spec = pltpu.PrefetchScalarGridSpec(
        num_scalar_prefetch=4,
        grid=(E,),
        in_specs=[pl.BlockSpec(memory_space=pl.ANY), wspec((D, De)), wspec((D, De)), wspec((De, D))],
        out_specs=pl.BlockSpec(memory_space=pl.ANY),
        scratch_shapes=[pltpu.VMEM((MOE_IN_SLOTS, G, C), jnp.int32), pltpu.VMEM((MOE_OUT_SLOTS, G, C), jnp.int32),
                        pltpu.VMEM((D, De), BF16), pltpu.VMEM((D, De), BF16), pltpu.VMEM((De, D), BF16),
                        pltpu.SemaphoreType.DMA((MOE_IN_SLOTS,)), pltpu.SemaphoreType.DMA((MOE_OUT_SLOTS,))],
    )
    return pl.pallas_call(
        _moe_body,
        grid_spec=grid_spec,
        out_shape=jax.ShapeDtypeStruct((P, C), jnp.int32),
        compiler_params=_cparams(("arbitrary",)),
        name="moe_ffn",
    )(blocks_per_expert, first_block, counts, n_used, xs, w_gate, w_up, w_down)


def _comb_body(yg_ref, x1_ref, gt_ref, sg_ref, su_ref, sd_ref, l2g_ref, l2b_ref, o_ref, *, alpha):
    x1 = x1_ref[...]
    xb = x1.astype(BF16)
    shared = _dot(_silu(_dot(xb, sg_ref[...])) * _dot(xb, su_ref[...]), sd_ref[...])
    gt = gt_ref[...]
    acc_hi = acc_lo = None
    for k in range(TOP_K):
        hi, lo = _unpack_halves(yg_ref[k])
        gk = gt[:, k:k + 1]
        acc_hi = gk * hi if acc_hi is None else acc_hi + gk * hi
        acc_lo = gk * lo if acc_lo is None else acc_lo + gk * lo
    routed = jnp.concatenate([acc_hi, acc_lo], axis=1)
    o_ref[...] = _layer_norm(alpha * x1 + (routed + shared), l2g_ref[...], l2b_ref[...])


def _combine(yg, x1, gates_t, p, tm):
    T, D = x1.shape
    full = lambda a: pl.BlockSpec(a.shape, lambda i, nd=a.ndim: (0,) * nd)
    consts = [p["sh_gate"], p["sh_up"], p["sh_down"], p["ln2_g"], p["ln2_b"]]
    return pl.pallas_call(
        functools.partial(_comb_body, alpha=p["alpha"]),
        grid=(T // tm,),
        in_specs=[pl.BlockSpec((TOP_K, tm, D // 2), lambda i: (0, i, 0)),
                  pl.BlockSpec((tm, D), lambda i: (i, 0)),
                  pl.BlockSpec((tm, TOP_K), lambda i: (i, 0))] + [full(a) for a in consts],
        out_specs=pl.BlockSpec((tm, D), lambda i: (i, 0)),
        out_shape=jax.ShapeDtypeStruct((T, D), F32),
        compiler_params=_cparams(("parallel",)),
        name="combine",
    )(yg, x1, gates_t, *consts)


def _segment_layout(counts, n_tokens):
    G = MOE_BLOCK
    E = counts.shape[0]
    n_blocks = -(-(n_tokens * TOP_K + E * (G - 1)) // G)
    padded = (counts + G - 1) // G * G
    pad_end = jnp.cumsum(padded)
    pad_start = pad_end - padded
    n_used = pad_end[-1:] // G
    return (pad_start, (padded // G).astype(jnp.int32), (pad_start // G).astype(jnp.int32),
            n_used.astype(jnp.int32), n_blocks)


def _rotary_tables(s, d):
    inv = ROPE_BASE ** (-jnp.arange(0, d, 2, dtype=F32) / d)
    ang = jnp.arange(s, dtype=F32)[:, None] * inv[None, :]
    cos = jnp.cos(ang)
    sin = jnp.sin(ang)
    return jnp.concatenate([cos, cos], axis=-1), jnp.concatenate([-sin, sin], axis=-1)


def _layer_params(l, depth, w_in, ret_gn_g, ret_gn_b, rwkv_mu, rwkv_w0, rwkv_w_up, rwkv_a0, rwkv_a_up,
                  rwkv_g_up, rwkv_k_k, rwkv_k_a, rwkv_r_k, rwkv_gn_g, rwkv_gn_b, w_out, ln1_g, ln1_b,
                  router_w, router_bias, exp_w_gate, exp_w_up, exp_w_down, sh_w_gate, sh_w_up,
                  sh_w_down, ln2_g, ln2_b):
    ret_w = ret_gn_g.shape[-1]
    W = rwkv_gn_g.shape[-1]
    n_heads, hd = rwkv_r_k.shape[-2:]
    rank_w = rwkv_w_up.shape[2]
    rank_a = rwkv_a_up.shape[2]
    assert rank_w * 2 == LANES and rank_a * 2 == LANES and rwkv_g_up.shape[1] == LANES
    assert hd * 2 == LANES and SCAN_CHUNK == hd
    row = lambda a: a.reshape(1, -1).astype(F32)
    zw = jnp.zeros((rank_w, W), F32)
    za = jnp.zeros((rank_a, W), F32)
    head_id = jnp.arange(W) // hd
    same_head = (head_id[:, None] == head_id[None, :])
    wi = w_in[l]
    return dict(
        alpha=float((2 * depth) ** 0.25),
        rwkv_width=W,
        w_ret=wi[:, :4 * ret_w].astype(BF16),
        w_rwkv=wi[:, 4 * ret_w:].astype(BF16),
        ret_gn_g=row(ret_gn_g[l]), ret_gn_b=row(ret_gn_b[l]),
        logg=jnp.broadcast_to(
            jnp.log1p(-jnp.exp2(-5.0 - jnp.arange(RET_HEADS, dtype=F32)))[:, None, None],
            (RET_HEADS, 1, LANES)),
        mu=row(rwkv_mu[l]),
        wup_pad=jnp.stack([jnp.concatenate([rwkv_w_up[l, 0], zw], 0),
                           jnp.concatenate([zw, rwkv_w_up[l, 1]], 0)]).astype(BF16),
        aup_pad=jnp.stack([jnp.concatenate([rwkv_a_up[l, 0], za], 0),
                           jnp.concatenate([za, rwkv_a_up[l, 1]], 0)]).astype(BF16),
        w0=rwkv_w0[l].astype(F32), a0=rwkv_a0[l].astype(F32),
        gup=rwkv_g_up[l].astype(BF16),
        k_k=row(rwkv_k_k[l]), k_a=row(rwkv_k_a[l]), r_k=row(rwkv_r_k[l]),
        head_ones=same_head.astype(BF16),
        head_avg=(same_head.astype(F32) / hd).astype(BF16),
        gn_g=row(rwkv_gn_g[l]), gn_b=row(rwkv_gn_b[l]),
        wo_ret=w_out[l, :ret_w].astype(BF16), wo_rwkv=w_out[l, ret_w:].astype(BF16),
        ln1_g=row(ln1_g[l]), ln1_b=row(ln1_b[l]),
        router_wt=router_w[l].T.astype(BF16), router_b=router_bias[l].reshape(-1, 1).astype(F32),
        exp_gate=exp_w_gate[l], exp_up=exp_w_up[l], exp_down=exp_w_down[l],
        sh_gate=sh_w_gate[l].astype(BF16), sh_up=sh_w_up[l].astype(BF16), sh_down=sh_w_down[l].astype(BF16),
        ln2_g=row(ln2_g[l]), ln2_b=row(ln2_b[l]),
    )


def _pick(n, pref):
    t = min(n, pref)
    while n % t:
        t //= 2
    return t


def _layer(x, p):
    b, s, D = x.shape
    T = b * s
    x2d = x.reshape(T, D)
    tm = _pick(T, 256)
    z_ret, r, v, kk, lw, kd, kka, bonus, g = _proj_prep(x, p, _pick(s, 256))
    cos, sin = _rotary_tables(s, RET_CHUNK)
    ret_out = _retention(z_ret, cos, sin, p["logg"], p["ret_gn_g"], p["ret_gn_b"])
    y_f, y_b = _rwkv_scan(r, kk, v, lw, kd, kka, _pick(s, 512))
    W = p["rwkv_width"]
    x1, xp, idx_t, gates, cnt = _mix_out(y_f.reshape(T, W), y_b.reshape(T, W), bonus.reshape(T, W),
                                         g.reshape(T, W), ret_out.reshape(T, -1), x2d, p, tm)
    counts = cnt[:, 0].astype(jnp.int32)
    pad_start, blocks_per_expert, first_block, n_used, n_blocks = _segment_layout(counts, T)
    dest = _slot_plan(idx_t, pad_start, _pick(T, 512))
    xs = _sc_dispatch(xp, dest, n_blocks * MOE_BLOCK)
    ys = _moe_ffn(xs, blocks_per_expert, first_block, counts, n_used, p["exp_gate"], p["exp_up"], p["exp_down"])
    yg = _sc_gather(ys, dest).reshape(TOP_K, T, D // 2)
    out = _combine(yg, x1, gates.T, p, _pick(T, 256))
    return out.reshape(b, s, D)


def kernel(x_prompt, x_sample, w_in, ret_gn_g, ret_gn_b, rwkv_mu, rwkv_w0, rwkv_w_up, rwkv_a0, rwkv_a_up,
           rwkv_g_up, rwkv_k_k, rwkv_k_a, rwkv_r_k, rwkv_gn_g, rwkv_gn_b, w_out, ln1_g, ln1_b, router_w,
           router_bias, exp_w_gate, exp_w_up, exp_w_down, sh_w_gate, sh_w_up, sh_w_down, ln2_g, ln2_b):
    weights = (w_in, ret_gn_g, ret_gn_b, rwkv_mu, rwkv_w0, rwkv_w_up, rwkv_a0, rwkv_a_up, rwkv_g_up,
               rwkv_k_k, rwkv_k_a, rwkv_r_k, rwkv_gn_g, rwkv_gn_b, w_out, ln1_g, ln1_b, router_w,
               router_bias, exp_w_gate, exp_w_up, exp_w_down, sh_w_gate, sh_w_up, sh_w_down, ln2_g, ln2_b)
    depth = w_in.shape[0]
    layers = [_layer_params(l, depth, *weights) for l in range(depth)]

    def trunk(x):
        for p in layers:
            x = _layer(x, p)
        return x

    return trunk(x_prompt), trunk(x_sample)
```

```python
import functools
import math

import jax
import jax.numpy as jnp
from jax import lax
from jax.experimental import pallas as pl
from jax.experimental.pallas import tpu as pltpu
from jax.experimental.pallas import tpu_sc as plsc

F32 = jnp.float32
BF16 = jnp.bfloat16

RET_HEADS = 4
RET_CHUNK = 128
ROPE_BASE = 10000.0
TOP_K = 8
N_GROUPS = 8
TOPK_GROUPS = 4
ROUTE_SCALE = 2.5
MOE_BLOCK = 256
LN_EPS = 1e-5
GN_EPS = 1e-5
RWKV_GN_EPS = 64e-5

LANES = 128
SUBLANES = 8
VMEM_LIMIT_BYTES = 56 * 1024 * 1024

SCAN_CHUNK = 64


def _cparams(semantics):
    return pltpu.CompilerParams(dimension_semantics=semantics, vmem_limit_bytes=VMEM_LIMIT_BYTES)


def _dot(a, b):
    return jnp.dot(a.astype(BF16), b.astype(BF16), preferred_element_type=F32)


def _dot_nt(a, b):
    return lax.dot_general(a.astype(BF16), b.astype(BF16), (((1,), (1,)), ((), ())),
                           preferred_element_type=F32)


def _dot_exact_rhs(x, w_bf16):
    hi = x.astype(BF16)
    mid = (x - hi.astype(F32)).astype(BF16)
    return (jnp.dot(hi, w_bf16, preferred_element_type=F32)
            + jnp.dot(mid, w_bf16, preferred_element_type=F32))


def _sigmoid(x):
    return 1.0 / (1.0 + jnp.exp(-x))


def _silu(x):
    return x * _sigmoid(x)


def _layer_norm(h, g, b):
    mu = jnp.mean(h, axis=-1, keepdims=True)
    d = h - mu
    var = jnp.mean(d * d, axis=-1, keepdims=True)
    return d * lax.rsqrt(var + LN_EPS) * g + b


RET_GROUP = 4


def _ret_body(q_ref, k_ref, v_ref, gt_ref, cos_ref, sin_ref, lg_ref, gg_ref, gb_ref, o_ref,
              qs_ref, sf_ref, sb_ref, acc_ref, *, qscale):
    C = RET_CHUNK
    U = RET_GROUP
    s = q_ref.shape[0]
    n = s // C
    lg = lg_ref[...]
    pos = lax.broadcasted_iota(jnp.int32, (C, C), 0).astype(F32)
    col = lax.broadcasted_iota(jnp.int32, (C, C), 1).astype(F32)
    sc_q_prev = jnp.exp((pos + 1.0) * lg)
    sc_k_fwd = jnp.exp((C - 1.0 - pos) * lg)
    sc_k_bwd = jnp.exp(pos * lg)
    sc_q_next = jnp.exp((C - pos) * lg)
    g_chunk = jnp.exp(float(C) * lg)
    decay = jnp.exp(jnp.abs(pos - col) * lg)
    gg = gg_ref[...]
    gb = gb_ref[...]

    def rows_of(grp):
        return [pl.ds(pl.multiple_of((grp * U + u) * C, C), C) for u in range(U)]

    def rot(x, r):
        x = x.astype(F32)
        return x * cos_ref[r, :] + pltpu.roll(x, C // 2, 1) * sin_ref[r, :]

    def local(grp, carry):
        rows = rows_of(grp)
        q = [rot(q_ref[r, :], r) * qscale for r in rows]
        k = [rot(k_ref[r, :], r) for r in rows]
        vb = [v_ref[r, :].astype(BF16) for r in rows]
        sc = [_dot_nt(q[u], k[u]) * decay for u in range(U)]
        out = [_dot(sc[u], vb[u]) for u in range(U)]
        kf = [_dot((k[u] * sc_k_fwd).T, vb[u]) for u in range(U)]
        kb = [_dot((k[u] * sc_k_bwd).T, vb[u]) for u in range(U)]
        for u in range(U):
            qs_ref[rows[u], :] = q[u]
            acc_ref[rows[u], :] = out[u]
            sf_ref[grp * U + u] = kf[u]
            sb_ref[grp * U + u] = kb[u]
        return carry

    lax.fori_loop(0, n // U, local, 0)

    def fwd(c, S):
        kv = sf_ref[c]
        sf_ref[c] = S
        return S * g_chunk + kv

    lax.fori_loop(0, n, fwd, jnp.zeros((C, C), F32))

    def bwd(i, S):
        c = n - 1 - i
        kv = sb_ref[c]
        sb_ref[c] = S
        return S * g_chunk + kv

    lax.fori_loop(0, n, bwd, jnp.zeros((C, C), F32))

    def cross(grp, carry):
        rows = rows_of(grp)
        q = [qs_ref[r, :] for r in rows]
        y = [acc_ref[rows[u], :] + _dot(jnp.concatenate([q[u] * sc_q_prev, q[u] * sc_q_next], axis=1),
                                        jnp.concatenate([sf_ref[grp * U + u], sb_ref[grp * U + u]], axis=0))
             for u in range(U)]
        for u in range(U):
            mu = jnp.mean(y[u], axis=-1, keepdims=True)
            d = y[u] - mu
            var = jnp.mean(d * d, axis=-1, keepdims=True)
            yn = d * lax.rsqrt(var + GN_EPS) * gg + gb
            o_ref[rows[u], :] = _silu(gt_ref[rows[u], :].astype(F32)) * yn
        return carry

    lax.fori_loop(0, n // U, cross, 0)


def _retention(z_ret, cos, sin, logg, gn_g, gn_b):
    b, s, _ = z_ret.shape
    C = RET_CHUNK
    H = RET_HEADS
    assert C == LANES and s % (C * RET_GROUP) == 0
    n = s // C
    blk = lambda off: pl.BlockSpec((None, s, C), lambda i, h, off=off: (i, 0, off + h))
    return pl.pallas_call(
        functools.partial(_ret_body, qscale=float(C) ** -0.5),
        grid=(b, H),
        in_specs=[blk(0), blk(H), blk(2 * H), blk(3 * H),
                  pl.BlockSpec((s, C), lambda i, h: (0, 0)),
                  pl.BlockSpec((s, C), lambda i, h: (0, 0)),
                  pl.BlockSpec((None, 1, C), lambda i, h: (h, 0, 0)),
                  pl.BlockSpec((1, C), lambda i, h: (0, h)),
                  pl.BlockSpec((1, C), lambda i, h: (0, h))],
        out_specs=pl.BlockSpec((None, s, C), lambda i, h: (i, 0, h)),
        out_shape=jax.ShapeDtypeStruct((b, s, H * C), F32),
        scratch_shapes=[pltpu.VMEM((s, C), F32), pltpu.VMEM((n, C, C), F32), pltpu.VMEM((n, C, C), F32),
                        pltpu.VMEM((s, C), F32)],
        compiler_params=_cparams(("parallel", "parallel")),
        name="retention",
    )(z_ret, z_ret, z_ret, z_ret, cos, sin, logg, gn_g, gn_b)


def _prep_body(x_ref, xp_ref, xn_ref, wr_ref, ww_ref, mu_ref, wup_ref, aup_ref, w0_ref, a0_ref, gup_ref,
               kkp_ref, ka_ref, rk_ref, bd_ref,
               zr_o, r_o, v_o, kk_o, lw_o, kd_o, kka_o, bonus_o, g_o, *, width):
    t = pl.program_id(1)
    nt = pl.num_programs(1)
    W = width
    ts = x_ref.shape[0]
    x_all = jnp.concatenate([x_ref[...], xp_ref[...], xn_ref[...]], axis=0).astype(BF16)
    zr_o[...] = jnp.dot(x_all[:ts], wr_ref[...], preferred_element_type=F32).astype(zr_o.dtype)
    z_all = jnp.dot(x_all, ww_ref[...], preferred_element_type=F32)
    z = z_all[:ts]
    row = lax.broadcasted_iota(jnp.int32, (ts, 1), 0)
    prev_row = jnp.where(t > 0, z_all[ts + SUBLANES - 1:ts + SUBLANES, :], 0.0)
    next_row = jnp.where(t < nt - 1, z_all[ts + SUBLANES:ts + SUBLANES + 1, :], 0.0)
    prev = jnp.where(row == 0, prev_row, pltpu.roll(z, 1, 0))
    nxt = jnp.where(row == ts - 1, next_row, pltpu.roll(z, ts - 1, 0))
    zs = z + mu_ref[...] * (0.5 * (prev + nxt) - z)

    r = zs[:, 0:W]
    kx = zs[:, W:2 * W]
    vx = zs[:, 2 * W:3 * W]
    wd = jnp.tanh(zs[:, 3 * W:3 * W + LANES])
    ad = zs[:, 3 * W + LANES:3 * W + 2 * LANES]
    gd = _sigmoid(zs[:, 3 * W + 2 * LANES:3 * W + 3 * LANES])
    bd = bd_ref[...]

    kk = kx * kkp_ref[...]
    ssq = jnp.dot((kk * kk).astype(BF16), bd, preferred_element_type=F32)
    kk = kk * lax.rsqrt(jnp.maximum(ssq, 1e-24))
    ka = ka_ref[...]
    ksum = None
    for d in range(2):
        pre = w0_ref[d:d + 1, :] + _dot(wd, wup_ref[d])
        lw_o[d] = -math.exp(-0.5) * _sigmoid(pre)
        a = _sigmoid(a0_ref[d:d + 1, :] + _dot(ad, aup_ref[d]))
        kd = kx * (1.0 + (a - 1.0) * ka)
        kd_o[d] = kd.astype(kd_o.dtype)
        kka_o[d] = (kk * a).astype(kka_o.dtype)
        ksum = kd if ksum is None else ksum + kd
    r_o[...] = r.astype(r_o.dtype)
    v_o[...] = vx.astype(v_o.dtype)
    kk_o[...] = kk.astype(kk_o.dtype)
    bonus_o[...] = (jnp.dot((r * ksum * rk_ref[...]).astype(BF16), bd, preferred_element_type=F32)
                    * vx).astype(bonus_o.dtype)
    g_o[...] = _dot(gd, gup_ref[...]).astype(g_o.dtype)


def _proj_prep(x, p, ts):
    b, s, D = x.shape
    W = p["rwkv_width"]
    nr = p["w_ret"].shape[1]
    nt = s // ts
    hb = ts // SUBLANES
    last = s // SUBLANES - 1
    full = lambda a: pl.BlockSpec(a.shape, lambda i, t, nd=a.ndim: (0,) * nd)
    out_tok = pl.BlockSpec((None, ts, W), lambda i, t: (i, t, 0))
    out_dir = pl.BlockSpec((2, None, ts, W), lambda i, t: (0, i, t, 0))
    tok_shape = jax.ShapeDtypeStruct((b, s, W), BF16)
    dir_shape = jax.ShapeDtypeStruct((2, b, s, W), BF16)
    lw_shape = jax.ShapeDtypeStruct((2, b, s, W), F32)
    consts = [p["w_ret"], p["w_rwkv"], p["mu"], p["wup_pad"], p["aup_pad"], p["w0"], p["a0"], p["gup"],
              p["k_k"], p["k_a"], p["r_k"], p["head_ones"]]
    return pl.pallas_call(
        functools.partial(_prep_body, width=W),
        grid=(b, nt),
        in_specs=[pl.BlockSpec((None, ts, D), lambda i, t: (i, t, 0)),
                  pl.BlockSpec((None, SUBLANES, D), lambda i, t: (i, jnp.maximum(t * hb - 1, 0), 0)),
                  pl.BlockSpec((None, SUBLANES, D), lambda i, t: (i, jnp.minimum((t + 1) * hb, last), 0)),
                  ] + [full(a) for a in consts],
        out_specs=[pl.BlockSpec((None, ts, nr), lambda i, t: (i, t, 0)),
                   out_tok, out_tok, out_tok, out_dir, out_dir, out_dir, out_tok, out_tok],
        out_shape=[jax.ShapeDtypeStruct((b, s, nr), BF16),
                   tok_shape, tok_shape, tok_shape, lw_shape, dir_shape, dir_shape, tok_shape, tok_shape],
        compiler_params=_cparams(("parallel", "parallel")),
        name="proj_prep",
    )(x, x, x, *consts)


def _scan_body(rf_ref, kkf_ref, vf_ref, rb_ref, kkb_ref, vb_ref, lwf_ref, kdf_ref, kkaf_ref,
               lwb_ref, kdb_ref, kkab_ref, yf_ref, yb_ref, st_ref, *, n_pairs):
    L = SCAN_CHUNK
    H = 2 * L
    assert H == LANES
    tt = rf_ref.shape[0]
    nch = tt // L

    @pl.when(pl.program_id(1) == 0)
    def _():
        st_ref[...] = jnp.zeros_like(st_ref)

    r_refs, kk_refs, v_refs = (rf_ref, rb_ref), (kkf_ref, kkb_ref), (vf_ref, vb_ref)
    lw_refs, kd_refs, kka_refs = (lwf_ref, lwb_ref), (kdf_ref, kdb_ref), (kkaf_ref, kkab_ref)
    y_refs = (yf_ref, yb_ref)
    ii = lax.broadcasted_iota(jnp.int32, (H, H), 0)
    jj = lax.broadcasted_iota(jnp.int32, (H, H), 1)
    same = (ii < L) == (jj < L)
    strict = (jnp.logical_and(same, ii > jj), jnp.logical_and(same, ii < jj))
    incl = (jnp.logical_and(same, ii >= jj), jnp.logical_and(same, ii <= jj))
    eye = ii == jj
    li = lax.broadcasted_iota(jnp.int32, (L, L), 0)
    lj = lax.broadcasted_iota(jnp.int32, (L, L), 1)
    tri = (jnp.where(li >= lj, 1.0, 0.0).astype(BF16), jnp.where(li <= lj, 1.0, 0.0).astype(BF16))
    head0 = lax.broadcasted_iota(jnp.int32, (L, H), 1) < L

    def stack(x):
        return jnp.concatenate([jnp.where(head0, x, 0.0), jnp.where(head0, 0.0, x)], axis=0)

    chains = [(d, hp) for d in range(2) for hp in range(n_pairs)]
    P = range(len(chains))
    lanes = [slice(hp * H, (hp + 1) * H) for _, hp in chains]
    dirs = [d for d, _ in chains]

    def chunk(j, carry):
        rows = (pl.ds(pl.multiple_of(j * L, L), L), pl.ds(pl.multiple_of((nch - 1 - j) * L, L), L))
        ld = lambda refs, h: refs[dirs[h]][rows[dirs[h]], lanes[h]].astype(F32)
        lw = [ld(lw_refs, h) for h in P]
        lw_hi = [x.astype(BF16) for x in lw]
        cum2 = [jnp.dot(tri[dirs[h]],
                        jnp.concatenate([lw_hi[h], (lw[h] - lw_hi[h].astype(F32)).astype(BF16)], axis=1),
                        preferred_element_type=F32) for h in P]
        cum = [c2[:, :H] + c2[:, H:] for c2 in cum2]
        tot = [jnp.sum(x, axis=0, keepdims=True) for x in lw]
        e_incl = [jnp.exp(a) for a in cum]
        e_excl = [jnp.exp(a - x) for a, x in zip(cum, lw)]
        e_inv = [jnp.exp(-a) for a in cum]
        e_rem = [jnp.exp(t_ - a) for t_, a in zip(tot, cum)]
        kk = [ld(kk_refs, h) for h in P]
        kka = [ld(kka_refs, h) for h in P]
        kd = [ld(kd_refs, h) for h in P]
        Kk = [stack(a * e) for a, e in zip(kk, e_excl)]
        R = [stack(ld(r_refs, h) * e_incl[h]) for h in P]
        B = [stack(a * e) for a, e in zip(kka, e_inv)]
        Kd = [stack(a * e) for a, e in zip(kd, e_inv)]
        Bh = [stack(a * e).T.astype(BF16) for a, e in zip(kka, e_rem)]
        Kh = [stack(a * e).T.astype(BF16) for a, e in zip(kd, e_rem)]
        V = [stack(ld(v_refs, h)).astype(BF16) for h in P]

        gram = [_dot_nt(jnp.concatenate([Kk[h], R[h]], axis=0), jnp.concatenate([B[h], Kd[h]], axis=0))
                for h in P]
        Np = [jnp.where(strict[dirs[h]], gram[h][:H, :H], 0.0) for h in P]
        AV = [_dot(jnp.where(strict[dirs[h]], gram[h][:H, H:], 0.0), V[h]) for h in P]
        DC = [jnp.concatenate([jnp.where(incl[dirs[h]], gram[h][H:, H:], 0.0).astype(BF16),
                               jnp.where(incl[dirs[h]], gram[h][H:, :H], 0.0).astype(BF16)], axis=1) for h in P]
        KB = [jnp.concatenate([Kh[h], Bh[h]], axis=1) for h in P]
        n_fac = L.bit_length() - 1
        Tm = [jnp.where(eye, 1.0, 0.0) - a for a in Np]
        Np = [_dot(a, a) for a in Np]
        for f in range(1, n_fac):
            if f < n_fac - 1:
                both = [_dot(Np[h], jnp.concatenate([Tm[h], Np[h]], axis=1)) for h in P]
                Tm = [t_ + o[:, :H] for t_, o in zip(Tm, both)]
                Np = [o[:, H:] for o in both]
            else:
                Tm = [t_ + _dot(a, t_) for t_, a in zip(Tm, Np)]
        Wm = [_dot(Tm[h], jnp.concatenate([Kk[h], AV[h]], axis=1)) for h in P]
        WR = [_dot(jnp.concatenate([Wm[h][:, :H], R[h]], axis=0), st_ref[h]) for h in P]
        VU = [jnp.concatenate([V[h], (-(WR[h][:H] + Wm[h][:, H:])).astype(BF16)], axis=0) for h in P]
        Ys = [WR[h][H:] + _dot(DC[h], VU[h]) for h in P]
        Sn = [_dot(KB[h], VU[h]) for h in P]
        for h in P:
            pl_col = jnp.sum(jnp.where(eye, jnp.exp(tot[h]), 0.0), axis=1, keepdims=True)
            st_ref[h] = pl_col * st_ref[h] + Sn[h]
            y_refs[dirs[h]][rows[dirs[h]], lanes[h]] = Ys[h][:L] + Ys[h][L:]
        return carry

    lax.fori_loop(0, nch, chunk, 0)


def _rwkv_scan(r, kk, v, lw, kd, kka, tt):
    b, s, W = r.shape
    nt = s // tt
    n_pairs = W // LANES
    tok_f = pl.BlockSpec((None, tt, W), lambda i, t: (i, t, 0))
    tok_b = pl.BlockSpec((None, tt, W), lambda i, t: (i, nt - 1 - t, 0))
    dir_f = pl.BlockSpec((None, None, tt, W), lambda i, t: (0, i, t, 0))
    dir_b = pl.BlockSpec((None, None, tt, W), lambda i, t: (1, i, nt - 1 - t, 0))
    out = jax.ShapeDtypeStruct((b, s, W), F32)
    return pl.pallas_call(
        functools.partial(_scan_body, n_pairs=n_pairs),
        grid=(b, nt),
        in_specs=[tok_f, tok_f, tok_f, tok_b, tok_b, tok_b, dir_f, dir_f, dir_f, dir_b, dir_b, dir_b],
        out_specs=[tok_f, tok_b],
        out_shape=[out, out],
        scratch_shapes=[pltpu.VMEM((2 * n_pairs, LANES, LANES), F32)],
        compiler_params=_cparams(("parallel", "arbitrary")),
        name="rwkv_scan",
    )(r, kk, v, r, kk, v, lw, kd, kka, lw, kd, kka)


def _mix_body(yf_ref, yb_ref, bonus_ref, g_ref, ret_ref, x_ref, gng_ref, gnb_ref, avg_ref, wo1_ref, wo2_ref,
              l1g_ref, l1b_ref, rwt_ref, rb_ref, x1_ref, xp_ref, idx_ref, gate_ref, cnt_ref, *, alpha):
    y = yf_ref[...] + yb_ref[...]
    avg = avg_ref[...]
    mu = _dot_exact_rhs(y, avg)
    dl = y - mu
    var = jnp.dot((dl * dl).astype(BF16), avg, preferred_element_type=F32)
    yn = dl * lax.rsqrt(var + RWKV_GN_EPS) * gng_ref[...] + gnb_ref[...]
    rw = (yn + bonus_ref[...]) * g_ref[...]
    m = _dot(ret_ref[...], wo1_ref[...]) + _dot(rw, wo2_ref[...])
    x1 = _layer_norm(alpha * x_ref[...] + m, l1g_ref[...], l1b_ref[...])
    x1_ref[...] = x1
    xp_ref[...] = _pack_halves(x1)

    scores = _sigmoid(_dot_nt(rwt_ref[...], x1))
    E, tm = scores.shape
    GS = E // N_GROUPS
    NEG = -jnp.inf
    biased = scores + rb_ref[...]
    rowi = lax.broadcasted_iota(jnp.int32, (E, tm), 0)
    ri = lax.broadcasted_iota(jnp.int32, (GS, tm), 0)
    gs_rows = []
    for gi in range(N_GROUPS):
        blk = biased[gi * GS:(gi + 1) * GS, :]
        m1 = jnp.max(blk, axis=0, keepdims=True)
        i1 = jnp.min(jnp.where(blk == m1, ri, GS), axis=0, keepdims=True)
        m2 = jnp.max(jnp.where(ri == i1, NEG, blk), axis=0, keepdims=True)
        gs_rows.append(m1 + m2)
    cur = jnp.concatenate(gs_rows, axis=0)
    gidx = lax.broadcasted_iota(jnp.int32, (N_GROUPS, tm), 0)
    top_groups = []
    for _ in range(TOPK_GROUPS):
        mx = jnp.max(cur, axis=0, keepdims=True)
        ix = jnp.min(jnp.where(cur == mx, gidx, N_GROUPS), axis=0, keepdims=True)
        top_groups.append(ix)
        cur = jnp.where(gidx == ix, NEG, cur)
    blocks = []
    for gi in range(N_GROUPS):
        keep = top_groups[0] == gi
        for ix in top_groups[1:]:
            keep = jnp.logical_or(keep, ix == gi)
        blocks.append(jnp.where(keep, biased[gi * GS:(gi + 1) * GS, :], NEG))
    allowed = jnp.concatenate(blocks, axis=0)
    cur = allowed
    idxs, sels = [], []
    for _ in range(TOP_K):
        mx = jnp.max(cur, axis=0, keepdims=True)
        ix = jnp.min(jnp.where(cur == mx, rowi, E), axis=0, keepdims=True)
        hit = rowi == ix
        sels.append(jnp.sum(jnp.where(hit, scores, 0.0), axis=0, keepdims=True))
        idxs.append(ix)
        cur = jnp.where(hit, NEG, cur)
    chosen = jnp.where(jnp.logical_and(allowed > NEG, cur == NEG), 1.0, 0.0)
    sel = jnp.concatenate(sels, axis=0)
    idx_ref[...] = jnp.concatenate(idxs, axis=0)
    gate_ref[...] = sel / jnp.sum(sel, axis=0, keepdims=True) * ROUTE_SCALE

    @pl.when(pl.program_id(0) == 0)
    def _():
        cnt_ref[...] = jnp.zeros_like(cnt_ref)

    cnt_ref[...] += jnp.sum(chosen, axis=1, keepdims=True)


def _mix_out(y_f, y_b, bonus, g, ret_out, x2d, p, tm):
    T, D = x2d.shape
    W = bonus.shape[1]
    Wr = ret_out.shape[1]
    E = p["router_wt"].shape[0]
    full = lambda a: pl.BlockSpec(a.shape, lambda i, nd=a.ndim: (0,) * nd)
    consts = [p["gn_g"], p["gn_b"], p["head_avg"], p["wo_ret"], p["wo_rwkv"], p["ln1_g"], p["ln1_b"],
              p["router_wt"], p["router_b"]]
    return pl.pallas_call(
        functools.partial(_mix_body, alpha=p["alpha"]),
        grid=(T // tm,),
        in_specs=[pl.BlockSpec((tm, W), lambda i: (i, 0)),
                  pl.BlockSpec((tm, W), lambda i: (i, 0)),
                  pl.BlockSpec((tm, W), lambda i: (i, 0)),
                  pl.BlockSpec((tm, W), lambda i: (i, 0)),
                  pl.BlockSpec((tm, Wr), lambda i: (i, 0)),
                  pl.BlockSpec((tm, D), lambda i: (i, 0))] + [full(a) for a in consts],
        out_specs=[pl.BlockSpec((tm, D), lambda i: (i, 0)),
                   pl.BlockSpec((tm, D // 2), lambda i: (i, 0)),
                   pl.BlockSpec((TOP_K, tm), lambda i: (0, i)),
                   pl.BlockSpec((TOP_K, tm), lambda i: (0, i)),
                   pl.BlockSpec((E, LANES), lambda i: (0, 0))],
        out_shape=[jax.ShapeDtypeStruct((T, D), F32),
                   jax.ShapeDtypeStruct((T, D // 2), jnp.int32),
                   jax.ShapeDtypeStruct((TOP_K, T), jnp.int32),
                   jax.ShapeDtypeStruct((TOP_K, T), F32),
                   jax.ShapeDtypeStruct((E, LANES), F32)],
        compiler_params=_cparams(("arbitrary",)),
        name="mix_out",
    )(y_f, y_b, bonus, g, ret_out, x2d, *consts)


def _plan_body(idx_ref, pstart_ref, upper_ref, dest_ref, run_ref):
    @pl.when(pl.program_id(0) == 0)
    def _():
        run_ref[...] = jnp.zeros_like(run_ref)

    E = pstart_ref.shape[0]
    tm = idx_ref.shape[1]
    rowi = lax.broadcasted_iota(jnp.int32, (E, tm), 0)
    hits = [rowi == idx_ref[k:k + 1, :] for k in range(TOP_K)]
    member = jnp.zeros((E, tm), F32)
    for hit in hits:
        member = jnp.where(hit, 1.0, member)
    before = jnp.dot(member.astype(BF16), upper_ref[...], preferred_element_type=F32)
    slot = pstart_ref[...] + run_ref[:, 0:1] + before
    dest_ref[...] = jnp.concatenate(
        [jnp.sum(jnp.where(hit, slot, 0.0), axis=0, keepdims=True) for hit in hits], axis=0).astype(jnp.int32)
    run_ref[...] += jnp.sum(member, axis=1, keepdims=True)


def _slot_plan(idx_t, pad_start, tm):
    T = idx_t.shape[1]
    E = pad_start.shape[0]
    upper = (jnp.arange(tm)[:, None] < jnp.arange(tm)[None, :]).astype(BF16)
    return pl.pallas_call(
        _plan_body,
        grid=(T // tm,),
        in_specs=[pl.BlockSpec((TOP_K, tm), lambda i: (0, i)),
                  pl.BlockSpec((E, 1), lambda i: (0, 0)),
                  pl.BlockSpec((tm, tm), lambda i: (0, 0))],
        out_specs=pl.BlockSpec((TOP_K, tm), lambda i: (0, i)),
        out_shape=jax.ShapeDtypeStruct((TOP_K, T), jnp.int32),
        scratch_shapes=[pltpu.VMEM((E, LANES), F32)],
        compiler_params=_cparams(("arbitrary",)),
        name="slot_plan",
    )(idx_t, pad_start.reshape(E, 1).astype(F32), upper)


_UPPER_HALF = -65536


def _bf16_bits(x):
    return lax.bitcast_convert_type(x.astype(BF16).astype(F32), jnp.int32)


def _pack_halves(x):
    c = x.shape[1] // 2
    return _bf16_bits(x[:, :c]) | (jnp.right_shift(_bf16_bits(x[:, c:]), 16) & 0xFFFF)


def _unpack_halves(u):
    hi = lax.bitcast_convert_type(u & _UPPER_HALF, F32)
    lo = lax.bitcast_convert_type(jnp.left_shift(u, 16), F32)
    return hi, lo


SC_WINDOW = 64


def _sc_dispatch(xp, dest, n_slots):
    T, C = xp.shape
    n_win = T // SC_WINDOW
    mesh = plsc.VectorSubcoreMesh(core_axis_name="core", subcore_axis_name="subcore")

    @pl.kernel(out_type=jax.ShapeDtypeStruct((n_slots, C), xp.dtype), mesh=mesh, scratch_types=[])
    def scatter_rows(x_hbm, i_hbm, o_hbm):
        def body(x_vmem, i_vmem):
            for k in range(TOP_K):
                pltpu.sync_copy(x_vmem, o_hbm.at[i_vmem.at[k]])

        pltpu.emit_pipeline(
            body,
            grid=(n_win,),
            in_specs=[pl.BlockSpec((SC_WINDOW, C), lambda i: (i, 0)),
                      pl.BlockSpec((None, TOP_K, SC_WINDOW), lambda i: (i, 0, 0))],
            out_specs=[],
            core_axis_name=("core", "subcore"),
            dimension_semantics=(pltpu.PARALLEL,),
        )(x_hbm, i_hbm)

    idx = dest.reshape(TOP_K, n_win, SC_WINDOW).transpose(1, 0, 2)
    return scatter_rows(xp, idx)


def _sc_gather(ys, dest):
    n_idx = dest.shape[0] * dest.shape[1]
    C = ys.shape[1]
    mesh = plsc.VectorSubcoreMesh(core_axis_name="core", subcore_axis_name="subcore")

    @pl.kernel(out_type=jax.ShapeDtypeStruct((n_idx, C), ys.dtype), mesh=mesh, scratch_types=[])
    def gather_rows(y_hbm, i_hbm, o_hbm):
        def body(i_vmem, o_vmem):
            pltpu.sync_copy(y_hbm.at[i_vmem.at[0]], o_vmem)

        pltpu.emit_pipeline(
            body,
            grid=(n_idx // SC_WINDOW,),
            in_specs=[pl.BlockSpec((1, SC_WINDOW), lambda i: (i, 0))],
            out_specs=[pl.BlockSpec((SC_WINDOW, C), lambda i: (i, 0))],
            core_axis_name=("core", "subcore"),
            dimension_semantics=(pltpu.PARALLEL,),
        )(i_hbm, o_hbm)

    return gather_rows(ys, dest.reshape(n_idx // SC_WINDOW, SC_WINDOW))


MOE_IN_SLOTS = 4
MOE_OUT_SLOTS = 2
MOE_EXPERTS_PER_STEP = 4


def _moe_body(nb_ref, b0_ref, cnt_ref, nu_ref, xs_hbm, wg_ref, wu_ref, wd_ref, ys_hbm,
              xbuf, obuf, wgb_ref, wub_ref, wdb_ref, in_sem, out_sem):
    G = MOE_BLOCK
    i = pl.program_id(0)
    n_used = nu_ref[0]

    def in_copy(g):
        slot = lax.rem(g, MOE_IN_SLOTS)
        return pltpu.make_async_copy(xs_hbm.at[pl.ds(pl.multiple_of(g * G, G), G), :], xbuf.at[slot],
                                     in_sem.at[slot])

    def out_copy(g):
        slot = lax.rem(g, MOE_OUT_SLOTS)
        return pltpu.make_async_copy(obuf.at[slot], ys_hbm.at[pl.ds(pl.multiple_of(g * G, G), G), :],
                                     out_sem.at[slot])

    @pl.when(i == 0)
    def _():
        for g in range(MOE_IN_SLOTS - 1):
            @pl.when(g < n_used)
            def _():
                in_copy(g).start()

    row = lax.broadcasted_iota(jnp.int32, (G, xbuf.shape[2]), 0)
    c = xbuf.shape[2]

    def run_expert(q):
        e = i * MOE_EXPERTS_PER_STEP + q
        nb = nb_ref[e]
        b0 = b0_ref[e]

        @pl.when(nb > 0)
        def _():
            wgb_ref[...] = wg_ref[q].astype(BF16)
            wub_ref[...] = wu_ref[q].astype(BF16)
            wdb_ref[...] = wd_ref[q].astype(BF16)

            def step(j, carry):
                g = b0 + j
                in_copy(g).wait()

                @pl.when(g + (MOE_IN_SLOTS - 1) < n_used)
                def _():
                    in_copy(g + (MOE_IN_SLOTS - 1)).start()

                @pl.when(g >= MOE_OUT_SLOTS)
                def _():
                    out_copy(g - MOE_OUT_SLOTS).wait()

                u = jnp.where(row < cnt_ref[e] - j * G, xbuf[lax.rem(g, MOE_IN_SLOTS)], 0)
                x_hi, x_lo = _unpack_halves(u)
                gate = _dot(x_hi, wgb_ref[:c, :]) + _dot(x_lo, wgb_ref[c:, :])
                up = _dot(x_hi, wub_ref[:c, :]) + _dot(x_lo, wub_ref[c:, :])
                obuf[lax.rem(g, MOE_OUT_SLOTS)] = _pack_halves(_dot(_silu(gate) * up, wdb_ref[...]))
                out_copy(g).start()
                return carry

            lax.fori_loop(0, nb, step, 0)

    for q in range(MOE_EXPERTS_PER_STEP):
        run_expert(q)

    @pl.when(i == pl.num_programs(0) - 1)
    def _():
        for back in range(MOE_OUT_SLOTS, 0, -1):
            @pl.when(n_used >= back)
            def _():
                out_copy(n_used - back).wait()


def _moe_ffn(xs, blocks_per_expert, first_block, counts, n_used, w_gate, w_up, w_down):
    P, C = xs.shape
    G = MOE_BLOCK
    E, D, De = w_gate.shape
    XP = MOE_EXPERTS_PER_STEP
    assert E % XP == 0
    wspec = lambda shape: pl.BlockSpec((XP,) + shape, lambda i, nb, b0, cnt, nu: (i, 0, 0))
    grid_spec = pltpu.PrefetchScalarGridSpec(
        num_scalar_prefetch=4,
        grid=(E // XP,),
        in_specs=[pl.BlockSpec(memory_space=pl.ANY), wspec((D, De)), wspec((D, De)), wspec((De, D))],
        out_specs=pl.BlockSpec(memory_space=pl.ANY),
        scratch_shapes=[pltpu.VMEM((MOE_IN_SLOTS, G, C), jnp.int32), pltpu.VMEM((MOE_OUT_SLOTS, G, C), jnp.int32),
                        pltpu.VMEM((D, De), BF16), pltpu.VMEM((D, De), BF16), pltpu.VMEM((De, D), BF16),
                        pltpu.SemaphoreType.DMA((MOE_IN_SLOTS,)), pltpu.SemaphoreType.DMA((MOE_OUT_SLOTS,))],
    )
    return pl.pallas_call(
        _moe_body,
        grid_spec=grid_spec,
        out_shape=jax.ShapeDtypeStruct((P, C), jnp.int32),
        compiler_params=_cparams(("arbitrary",)),
        name="moe_ffn",
    )(blocks_per_expert, first_block, counts, n_used, xs, w_gate, w_up, w_down)


def _comb_body(yg_ref, x1_ref, gt_ref, sg_ref, su_ref, sd_ref, l2g_ref, l2b_ref, o_ref, *, alpha):
    x1 = x1_ref[...]
    xb = x1.astype(BF16)
    shared = _dot(_silu(_dot(xb, sg_ref[...])) * _dot(xb, su_ref[...]), sd_ref[...])
    gt = gt_ref[...]
    acc_hi = acc_lo = None
    for k in range(TOP_K):
        hi, lo = _unpack_halves(yg_ref[k])
        gk = gt[:, k:k + 1]
        acc_hi = gk * hi if acc_hi is None else acc_hi + gk * hi
        acc_lo = gk * lo if acc_lo is None else acc_lo + gk * lo
    routed = jnp.concatenate([acc_hi, acc_lo], axis=1)
    o_ref[...] = _layer_norm(alpha * x1 + (routed + shared), l2g_ref[...], l2b_ref[...])


def _combine(yg, x1, gates_t, p, tm):
    T, D = x1.shape
    full = lambda a: pl.BlockSpec(a.shape, lambda i, nd=a.ndim: (0,) * nd)
    consts = [p["sh_gate"], p["sh_up"], p["sh_down"], p["ln2_g"], p["ln2_b"]]
    return pl.pallas_call(
        functools.partial(_comb_body, alpha=p["alpha"]),
        grid=(T // tm,),
        in_specs=[pl.BlockSpec((TOP_K, tm, D // 2), lambda i: (0, i, 0)),
                  pl.BlockSpec((tm, D), lambda i: (i, 0)),
                  pl.BlockSpec((tm, TOP_K), lambda i: (i, 0))] + [full(a) for a in consts],
        out_specs=pl.BlockSpec((tm, D), lambda i: (i, 0)),
        out_shape=jax.ShapeDtypeStruct((T, D), F32),
        compiler_params=_cparams(("parallel",)),
        name="combine",
    )(yg, x1, gates_t, *consts)


def _segment_layout(counts, n_tokens):
    G = MOE_BLOCK
    E = counts.shape[0]
    n_blocks = -(-(n_tokens * TOP_K + E * (G - 1)) // G)
    padded = (counts + G - 1) // G * G
    pad_end = jnp.cumsum(padded)
    pad_start = pad_end - padded
    n_used = pad_end[-1:] // G
    return (pad_start, (padded // G).astype(jnp.int32), (pad_start // G).astype(jnp.int32),
            n_used.astype(jnp.int32), n_blocks)


def _rotary_tables(s, d):
    inv = ROPE_BASE ** (-jnp.arange(0, d, 2, dtype=F32) / d)
    ang = jnp.arange(s, dtype=F32)[:, None] * inv[None, :]
    cos = jnp.cos(ang)
    sin = jnp.sin(ang)
    return jnp.concatenate([cos, cos], axis=-1), jnp.concatenate([-sin, sin], axis=-1)


def _layer_params(l, depth, w_in, ret_gn_g, ret_gn_b, rwkv_mu, rwkv_w0, rwkv_w_up, rwkv_a0, rwkv_a_up,
                  rwkv_g_up, rwkv_k_k, rwkv_k_a, rwkv_r_k, rwkv_gn_g, rwkv_gn_b, w_out, ln1_g, ln1_b,
                  router_w, router_bias, exp_w_gate, exp_w_up, exp_w_down, sh_w_gate, sh_w_up,
                  sh_w_down, ln2_g, ln2_b):
    ret_w = ret_gn_g.shape[-1]
    W = rwkv_gn_g.shape[-1]
    n_heads, hd = rwkv_r_k.shape[-2:]
    rank_w = rwkv_w_up.shape[2]
    rank_a = rwkv_a_up.shape[2]
    assert rank_w * 2 == LANES and rank_a * 2 == LANES and rwkv_g_up.shape[1] == LANES
    assert hd * 2 == LANES and SCAN_CHUNK == hd
    row = lambda a: a.reshape(1, -1).astype(F32)
    zw = jnp.zeros((rank_w, W), F32)
    za = jnp.zeros((rank_a, W), F32)
    head_id = jnp.arange(W) // hd
    same_head = (head_id[:, None] == head_id[None, :])
    wi = w_in[l]
    return dict(
        alpha=float((2 * depth) ** 0.25),
        rwkv_width=W,
        w_ret=wi[:, :4 * ret_w].astype(BF16),
        w_rwkv=wi[:, 4 * ret_w:].astype(BF16),
        ret_gn_g=row(ret_gn_g[l]), ret_gn_b=row(ret_gn_b[l]),
        logg=jnp.broadcast_to(
            jnp.log1p(-jnp.exp2(-5.0 - jnp.arange(RET_HEADS, dtype=F32)))[:, None, None],
            (RET_HEADS, 1, LANES)),
        mu=row(rwkv_mu[l]),
        wup_pad=jnp.stack([jnp.concatenate([rwkv_w_up[l, 0], zw], 0),
                           jnp.concatenate([zw, rwkv_w_up[l, 1]], 0)]).astype(BF16),
        aup_pad=jnp.stack([jnp.concatenate([rwkv_a_up[l, 0], za], 0),
                           jnp.concatenate([za, rwkv_a_up[l, 1]], 0)]).astype(BF16),
        w0=rwkv_w0[l].astype(F32), a0=rwkv_a0[l].astype(F32),
        gup=rwkv_g_up[l].astype(BF16),
        k_k=row(rwkv_k_k[l]), k_a=row(rwkv_k_a[l]), r_k=row(rwkv_r_k[l]),
        head_ones=same_head.astype(BF16),
        head_avg=(same_head.astype(F32) / hd).astype(BF16),
        gn_g=row(rwkv_gn_g[l]), gn_b=row(rwkv_gn_b[l]),
        wo_ret=w_out[l, :ret_w].astype(BF16), wo_rwkv=w_out[l, ret_w:].astype(BF16),
        ln1_g=row(ln1_g[l]), ln1_b=row(ln1_b[l]),
        router_wt=router_w[l].T.astype(BF16), router_b=router_bias[l].reshape(-1, 1).astype(F32),
        exp_gate=exp_w_gate[l], exp_up=exp_w_up[l], exp_down=exp_w_down[l],
        sh_gate=sh_w_gate[l].astype(BF16), sh_up=sh_w_up[l].astype(BF16), sh_down=sh_w_down[l].astype(BF16),
        ln2_g=row(ln2_g[l]), ln2_b=row(ln2_b[l]),
    )


def _pick(n, pref):
    t = min(n, pref)
    while n % t:
        t //= 2
    return t


def _layer(x, p):
    b, s, D = x.shape
    T = b * s
    x2d = x.reshape(T, D)
    tm = _pick(T, 256)
    z_ret, r, v, kk, lw, kd, kka, bonus, g = _proj_prep(x, p, _pick(s, 256))
    cos, sin = _rotary_tables(s, RET_CHUNK)
    ret_out = _retention(z_ret, cos, sin, p["logg"], p["ret_gn_g"], p["ret_gn_b"])
    y_f, y_b = _rwkv_scan(r, kk, v, lw, kd, kka, _pick(s, 512))
    W = p["rwkv_width"]
    x1, xp, idx_t, gates, cnt = _mix_out(y_f.reshape(T, W), y_b.reshape(T, W), bonus.reshape(T, W),
                                         g.reshape(T, W), ret_out.reshape(T, -1), x2d, p, tm)
    counts = cnt[:, 0].astype(jnp.int32)
    pad_start, blocks_per_expert, first_block, n_used, n_blocks = _segment_layout(counts, T)
    dest = _slot_plan(idx_t, pad_start, _pick(T, 512))
    xs = _sc_dispatch(xp, dest, n_blocks * MOE_BLOCK)
    ys = _moe_ffn(xs, blocks_per_expert, first_block, counts, n_used, p["exp_gate"], p["exp_up"], p["exp_down"])
    yg = _sc_gather(ys, dest).reshape(TOP_K, T, D // 2)
    out = _combine(yg, x1, gates.T, p, _pick(T, 256))
    return out.reshape(b, s, D)


def kernel(x_prompt, x_sample, w_in, ret_gn_g, ret_gn_b, rwkv_mu, rwkv_w0, rwkv_w_up, rwkv_a0, rwkv_a_up,
           rwkv_g_up, rwkv_k_k, rwkv_k_a, rwkv_r_k, rwkv_gn_g, rwkv_gn_b, w_out, ln1_g, ln1_b, router_w,
           router_bias, exp_w_gate, exp_w_up, exp_w_down, sh_w_gate, sh_w_up, sh_w_down, ln2_g, ln2_b):
    weights = (w_in, ret_gn_g, ret_gn_b, rwkv_mu, rwkv_w0, rwkv_w_up, rwkv_a0, rwkv_a_up, rwkv_g_up,
               rwkv_k_k, rwkv_k_a, rwkv_r_k, rwkv_gn_g, rwkv_gn_b, w_out, ln1_g, ln1_b, router_w,
               router_bias, exp_w_gate, exp_w_up, exp_w_down, sh_w_gate, sh_w_up, sh_w_down, ln2_g, ln2_b)
    depth = w_in.shape[0]
    layers = [_layer_params(l, depth, *weights) for l in range(depth)]

    def trunk(x):
        for p in layers:
            x = _layer(x, p)
        return x

    return trunk(x_prompt), trunk(x_sample)
```

```python
import functools
import math

import jax
import jax.numpy as jnp
from jax import lax
from jax.experimental import pallas as pl
from jax.experimental.pallas import tpu as pltpu
from jax.experimental.pallas import tpu_sc as plsc

F32 = jnp.float32
BF16 = jnp.bfloat16

RET_HEADS = 4
RET_CHUNK = 128
ROPE_BASE = 10000.0
TOP_K = 8
N_GROUPS = 8
TOPK_GROUPS = 4
ROUTE_SCALE = 2.5
MOE_BLOCK = 256
LN_EPS = 1e-5
GN_EPS = 1e-5
RWKV_GN_EPS = 64e-5

LANES = 128
SUBLANES = 8
VMEM_LIMIT_BYTES = 56 * 1024 * 1024

SCAN_CHUNK = 64


def _cparams(semantics):
    return pltpu.CompilerParams(dimension_semantics=semantics, vmem_limit_bytes=VMEM_LIMIT_BYTES)


def _dot(a, b):
    return jnp.dot(a.astype(BF16), b.astype(BF16), preferred_element_type=F32)


def _dot_nt(a, b):
    return lax.dot_general(a.astype(BF16), b.astype(BF16), (((1,), (1,)), ((), ())),
                           preferred_element_type=F32)


def _dot_exact_rhs(x, w_bf16):
    hi = x.astype(BF16)
    mid = (x - hi.astype(F32)).astype(BF16)
    return (jnp.dot(hi, w_bf16, preferred_element_type=F32)
            + jnp.dot(mid, w_bf16, preferred_element_type=F32))


def _sigmoid(x):
    return 1.0 / (1.0 + jnp.exp(-x))


def _silu(x):
    return x * _sigmoid(x)


def _layer_norm(h, g, b):
    mu = jnp.mean(h, axis=-1, keepdims=True)
    d = h - mu
    var = jnp.mean(d * d, axis=-1, keepdims=True)
    return d * lax.rsqrt(var + LN_EPS) * g + b


RET_GROUP = 8


def _ret_body(q_ref, k_ref, v_ref, gt_ref, cos_ref, sin_ref, lg_ref, gg_ref, gb_ref, o_ref,
              qs_ref, sf_ref, sb_ref, acc_ref, *, qscale):
    C = RET_CHUNK
    U = RET_GROUP
    s = q_ref.shape[0]
    n = s // C
    lg = lg_ref[...]
    pos = lax.broadcasted_iota(jnp.int32, (C, C), 0).astype(F32)
    col = lax.broadcasted_iota(jnp.int32, (C, C), 1).astype(F32)
    sc_q_prev = jnp.exp((pos + 1.0) * lg)
    sc_k_fwd = jnp.exp((C - 1.0 - pos) * lg)
    sc_k_bwd = jnp.exp(pos * lg)
    sc_q_next = jnp.exp((C - pos) * lg)
    g_chunk = jnp.exp(float(C) * lg)
    decay = jnp.exp(jnp.abs(pos - col) * lg)
    gg = gg_ref[...]
    gb = gb_ref[...]

    def rows_of(grp):
        return [pl.ds(pl.multiple_of((grp * U + u) * C, C), C) for u in range(U)]

    def rot(x, r):
        x = x.astype(F32)
        return x * cos_ref[r, :] + pltpu.roll(x, C // 2, 1) * sin_ref[r, :]

    def local(grp, carry):
        rows = rows_of(grp)
        q = [rot(q_ref[r, :], r) * qscale for r in rows]
        k = [rot(k_ref[r, :], r) for r in rows]
        vb = [v_ref[r, :].astype(BF16) for r in rows]
        sc = [_dot_nt(q[u], k[u]) * decay for u in range(U)]
        out = [_dot(sc[u], vb[u]) for u in range(U)]
        kf = [_dot((k[u] * sc_k_fwd).T, vb[u]) for u in range(U)]
        kb = [_dot((k[u] * sc_k_bwd).T, vb[u]) for u in range(U)]
        for u in range(U):
            qs_ref[rows[u], :] = q[u]
            acc_ref[rows[u], :] = out[u]
            sf_ref[grp * U + u] = kf[u]
            sb_ref[grp * U + u] = kb[u]
        return carry

    lax.fori_loop(0, n // U, local, 0)

    def fwd(c, S):
        kv = sf_ref[c]
        sf_ref[c] = S
        return S * g_chunk + kv

    lax.fori_loop(0, n, fwd, jnp.zeros((C, C), F32))

    def bwd(i, S):
        c = n - 1 - i
        kv = sb_ref[c]
        sb_ref[c] = S
        return S * g_chunk + kv

    lax.fori_loop(0, n, bwd, jnp.zeros((C, C), F32))

    def cross(grp, carry):
        rows = rows_of(grp)
        q = [qs_ref[r, :] for r in rows]
        y = [acc_ref[rows[u], :] + _dot(jnp.concatenate([q[u] * sc_q_prev, q[u] * sc_q_next], axis=1),
                                        jnp.concatenate([sf_ref[grp * U + u], sb_ref[grp * U + u]], axis=0))
             for u in range(U)]
        for u in range(U):
            mu = jnp.mean(y[u], axis=-1, keepdims=True)
            d = y[u] - mu
            var = jnp.mean(d * d, axis=-1, keepdims=True)
            yn = d * lax.rsqrt(var + GN_EPS) * gg + gb
            o_ref[rows[u], :] = _silu(gt_ref[rows[u], :].astype(F32)) * yn
        return carry

    lax.fori_loop(0, n // U, cross, 0)


def _retention(z_ret, cos, sin, logg, gn_g, gn_b):
    b, s, _ = z_ret.shape
    C = RET_CHUNK
    H = RET_HEADS
    assert C == LANES and s % (C * RET_GROUP) == 0
    n = s // C
    blk = lambda off: pl.BlockSpec((None, s, C), lambda i, h, off=off: (i, 0, off + h))
    return pl.pallas_call(
        functools.partial(_ret_body, qscale=float(C) ** -0.5),
        grid=(b, H),
        in_specs=[blk(0), blk(H), blk(2 * H), blk(3 * H),
                  pl.BlockSpec((s, C), lambda i, h: (0, 0)),
                  pl.BlockSpec((s, C), lambda i, h: (0, 0)),
                  pl.BlockSpec((None, 1, C), lambda i, h: (h, 0, 0)),
                  pl.BlockSpec((1, C), lambda i, h: (0, h)),
                  pl.BlockSpec((1, C), lambda i, h: (0, h))],
        out_specs=pl.BlockSpec((None, s, C), lambda i, h: (i, 0, h)),
        out_shape=jax.ShapeDtypeStruct((b, s, H * C), F32),
        scratch_shapes=[pltpu.VMEM((s, C), F32), pltpu.VMEM((n, C, C), F32), pltpu.VMEM((n, C, C), F32),
                        pltpu.VMEM((s, C), F32)],
        compiler_params=_cparams(("parallel", "parallel")),
        name="retention",
    )(z_ret, z_ret, z_ret, z_ret, cos, sin, logg, gn_g, gn_b)


def _prep_body(x_ref, xp_ref, xn_ref, wr_ref, ww_ref, mu_ref, wup_ref, aup_ref, w0_ref, a0_ref, gup_ref,
               kkp_ref, ka_ref, rk_ref, bd_ref,
               zr_o, r_o, v_o, kk_o, lw_o, kd_o, kka_o, bonus_o, g_o, *, width):
    t = pl.program_id(1)
    nt = pl.num_programs(1)
    W = width
    ts = x_ref.shape[0]
    x_all = jnp.concatenate([x_ref[...], xp_ref[...], xn_ref[...]], axis=0).astype(BF16)
    zr_o[...] = jnp.dot(x_all[:ts], wr_ref[...], preferred_element_type=F32).astype(zr_o.dtype)
    z_all = jnp.dot(x_all, ww_ref[...], preferred_element_type=F32)
    z = z_all[:ts]
    row = lax.broadcasted_iota(jnp.int32, (ts, 1), 0)
    prev_row = jnp.where(t > 0, z_all[ts + SUBLANES - 1:ts + SUBLANES, :], 0.0)
    next_row = jnp.where(t < nt - 1, z_all[ts + SUBLANES:ts + SUBLANES + 1, :], 0.0)
    prev = jnp.where(row == 0, prev_row, pltpu.roll(z, 1, 0))
    nxt = jnp.where(row == ts - 1, next_row, pltpu.roll(z, ts - 1, 0))
    zs = z + mu_ref[...] * (0.5 * (prev + nxt) - z)

    r = zs[:, 0:W]
    kx = zs[:, W:2 * W]
    vx = zs[:, 2 * W:3 * W]
    wd = jnp.tanh(zs[:, 3 * W:3 * W + LANES])
    ad = zs[:, 3 * W + LANES:3 * W + 2 * LANES]
    gd = _sigmoid(zs[:, 3 * W + 2 * LANES:3 * W + 3 * LANES])
    bd = bd_ref[...]

    kk = kx * kkp_ref[...]
    ssq = jnp.dot((kk * kk).astype(BF16), bd, preferred_element_type=F32)
    kk = kk * lax.rsqrt(jnp.maximum(ssq, 1e-24))
    ka = ka_ref[...]
    ksum = None
    for d in range(2):
        pre = w0_ref[d:d + 1, :] + _dot(wd, wup_ref[d])
        lw_o[d] = -math.exp(-0.5) * _sigmoid(pre)
        a = _sigmoid(a0_ref[d:d + 1, :] + _dot(ad, aup_ref[d]))
        kd = kx * (1.0 + (a - 1.0) * ka)
        kd_o[d] = kd.astype(kd_o.dtype)
        kka_o[d] = (kk * a).astype(kka_o.dtype)
        ksum = kd if ksum is None else ksum + kd
    r_o[...] = r.astype(r_o.dtype)
    v_o[...] = vx.astype(v_o.dtype)
    kk_o[...] = kk.astype(kk_o.dtype)
    bonus_o[...] = (jnp.dot((r * ksum * rk_ref[...]).astype(BF16), bd, preferred_element_type=F32)
                    * vx).astype(bonus_o.dtype)
    g_o[...] = _dot(gd, gup_ref[...]).astype(g_o.dtype)


def _proj_prep(x, p, ts):
    b, s, D = x.shape
    W = p["rwkv_width"]
    nr = p["w_ret"].shape[1]
    nt = s // ts
    hb = ts // SUBLANES
    last = s // SUBLANES - 1
    full = lambda a: pl.BlockSpec(a.shape, lambda i, t, nd=a.ndim: (0,) * nd)
    out_tok = pl.BlockSpec((None, ts, W), lambda i, t: (i, t, 0))
    out_dir = pl.BlockSpec((2, None, ts, W), lambda i, t: (0, i, t, 0))
    tok_shape = jax.ShapeDtypeStruct((b, s, W), BF16)
    dir_shape = jax.ShapeDtypeStruct((2, b, s, W), BF16)
    lw_shape = jax.ShapeDtypeStruct((2, b, s, W), F32)
    consts = [p["w_ret"], p["w_rwkv"], p["mu"], p["wup_pad"], p["aup_pad"], p["w0"], p["a0"], p["gup"],
              p["k_k"], p["k_a"], p["r_k"], p["head_ones"]]
    return pl.pallas_call(
        functools.partial(_prep_body, width=W),
        grid=(b, nt),
        in_specs=[pl.BlockSpec((None, ts, D), lambda i, t: (i, t, 0)),
                  pl.BlockSpec((None, SUBLANES, D), lambda i, t: (i, jnp.maximum(t * hb - 1, 0), 0)),
                  pl.BlockSpec((None, SUBLANES, D), lambda i, t: (i, jnp.minimum((t + 1) * hb, last), 0)),
                  ] + [full(a) for a in consts],
        out_specs=[pl.BlockSpec((None, ts, nr), lambda i, t: (i, t, 0)),
                   out_tok, out_tok, out_tok, out_dir, out_dir, out_dir, out_tok, out_tok],
        out_shape=[jax.ShapeDtypeStruct((b, s, nr), BF16),
                   tok_shape, tok_shape, tok_shape, lw_shape, dir_shape, dir_shape, tok_shape, tok_shape],
        compiler_params=_cparams(("parallel", "parallel")),
        name="proj_prep",
    )(x, x, x, *consts)


def _scan_body(rf_ref, kkf_ref, vf_ref, rb_ref, kkb_ref, vb_ref, lwf_ref, kdf_ref, kkaf_ref,
               lwb_ref, kdb_ref, kkab_ref, yf_ref, yb_ref, st_ref, *, n_pairs):
    L = SCAN_CHUNK
    H = 2 * L
    assert H == LANES
    tt = rf_ref.shape[0]
    nch = tt // L

    @pl.when(pl.program_id(1) == 0)
    def _():
        st_ref[...] = jnp.zeros_like(st_ref)

    r_refs, kk_refs, v_refs = (rf_ref, rb_ref), (kkf_ref, kkb_ref), (vf_ref, vb_ref)
    lw_refs, kd_refs, kka_refs = (lwf_ref, lwb_ref), (kdf_ref, kdb_ref), (kkaf_ref, kkab_ref)
    y_refs = (yf_ref, yb_ref)
    ii = lax.broadcasted_iota(jnp.int32, (H, H), 0)
    jj = lax.broadcasted_iota(jnp.int32, (H, H), 1)
    same = (ii < L) == (jj < L)
    strict = (jnp.logical_and(same, ii > jj), jnp.logical_and(same, ii < jj))
    incl = (jnp.logical_and(same, ii >= jj), jnp.logical_and(same, ii <= jj))
    eye = ii == jj
    li = lax.broadcasted_iota(jnp.int32, (L, L), 0)
    lj = lax.broadcasted_iota(jnp.int32, (L, L), 1)
    tri = (jnp.where(li >= lj, 1.0, 0.0).astype(BF16), jnp.where(li <= lj, 1.0, 0.0).astype(BF16))
    head0 = lax.broadcasted_iota(jnp.int32, (L, H), 1) < L

    def stack(x):
        return jnp.concatenate([jnp.where(head0, x, 0.0), jnp.where(head0, 0.0, x)], axis=0)

    chains = [(d, hp) for d in range(2) for hp in range(n_pairs)]
    P = range(len(chains))
    lanes = [slice(hp * H, (hp + 1) * H) for _, hp in chains]
    dirs = [d for d, _ in chains]

    def chunk(j, carry):
        rows = (pl.ds(pl.multiple_of(j * L, L), L), pl.ds(pl.multiple_of((nch - 1 - j) * L, L), L))
        ld = lambda refs, h: refs[dirs[h]][rows[dirs[h]], lanes[h]].astype(F32)
        lw = [ld(lw_refs, h) for h in P]
        lw_hi = [x.astype(BF16) for x in lw]
        cum2 = [jnp.dot(tri[dirs[h]],
                        jnp.concatenate([lw_hi[h], (lw[h] - lw_hi[h].astype(F32)).astype(BF16)], axis=1),
                        preferred_element_type=F32) for h in P]
        cum = [c2[:, :H] + c2[:, H:] for c2 in cum2]
        tot = [jnp.sum(x, axis=0, keepdims=True) for x in lw]
        e_incl = [jnp.exp(a) for a in cum]
        e_excl = [jnp.exp(a - x) for a, x in zip(cum, lw)]
        e_inv = [jnp.exp(-a) for a in cum]
        e_rem = [jnp.exp(t_ - a) for t_, a in zip(tot, cum)]
        kk = [ld(kk_refs, h) for h in P]
        kka = [ld(kka_refs, h) for h in P]
        kd = [ld(kd_refs, h) for h in P]
        Kk = [stack(a * e) for a, e in zip(kk, e_excl)]
        R = [stack(ld(r_refs, h) * e_incl[h]) for h in P]
        B = [stack(a * e) for a, e in zip(kka, e_inv)]
        Kd = [stack(a * e) for a, e in zip(kd, e_inv)]
        Bh = [stack(a * e).T.astype(BF16) for a, e in zip(kka, e_rem)]
        Kh = [stack(a * e).T.astype(BF16) for a, e in zip(kd, e_rem)]
        V = [stack(ld(v_refs, h)).astype(BF16) for h in P]

        gram = [_dot_nt(jnp.concatenate([Kk[h], R[h]], axis=0), jnp.concatenate([B[h], Kd[h]], axis=0))
                for h in P]
        Np = [jnp.where(strict[dirs[h]], gram[h][:H, :H], 0.0) for h in P]
        AV = [_dot(jnp.where(strict[dirs[h]], gram[h][:H, H:], 0.0), V[h]) for h in P]
        DC = [jnp.concatenate([jnp.where(incl[dirs[h]], gram[h][H:, H:], 0.0).astype(BF16),
                               jnp.where(incl[dirs[h]], gram[h][H:, :H], 0.0).astype(BF16)], axis=1) for h in P]
        KB = [jnp.concatenate([Kh[h], Bh[h]], axis=1) for h in P]
        n_fac = L.bit_length() - 1
        Tm = [jnp.where(eye, 1.0, 0.0) - a for a in Np]
        Np = [_dot(a, a) for a in Np]
        for f in range(1, n_fac):
            if f < n_fac - 1:
                both = [_dot(Np[h], jnp.concatenate([Tm[h], Np[h]], axis=1)) for h in P]
                Tm = [t_ + o[:, :H] for t_, o in zip(Tm, both)]
                Np = [o[:, H:] for o in both]
            else:
                Tm = [t_ + _dot(a, t_) for t_, a in zip(Tm, Np)]
        Wm = [_dot(Tm[h], jnp.concatenate([Kk[h], AV[h]], axis=1)) for h in P]
        WR = [_dot(jnp.concatenate([Wm[h][:, :H], R[h]], axis=0), st_ref[h]) for h in P]
        VU = [jnp.concatenate([V[h], (-(WR[h][:H] + Wm[h][:, H:])).astype(BF16)], axis=0) for h in P]
        Ys = [WR[h][H:] + _dot(DC[h], VU[h]) for h in P]
        Sn = [_dot(KB[h], VU[h]) for h in P]
        for h in P:
            pl_col = jnp.sum(jnp.where(eye, jnp.exp(tot[h]), 0.0), axis=1, keepdims=True)
            st_ref[h] = pl_col * st_ref[h] + Sn[h]
            y_refs[dirs[h]][rows[dirs[h]], lanes[h]] = Ys[h][:L] + Ys[h][L:]
        return carry

    lax.fori_loop(0, nch, chunk, 0)


def _rwkv_scan(r, kk, v, lw, kd, kka, tt):
    b, s, W = r.shape
    nt = s // tt
    n_pairs = W // LANES
    tok_f = pl.BlockSpec((None, tt, W), lambda i, t: (i, t, 0))
    tok_b = pl.BlockSpec((None, tt, W), lambda i, t: (i, nt - 1 - t, 0))
    dir_f = pl.BlockSpec((None, None, tt, W), lambda i, t: (0, i, t, 0))
    dir_b = pl.BlockSpec((None, None, tt, W), lambda i, t: (1, i, nt - 1 - t, 0))
    out = jax.ShapeDtypeStruct((b, s, W), F32)
    return pl.pallas_call(
        functools.partial(_scan_body, n_pairs=n_pairs),
        grid=(b, nt),
        in_specs=[tok_f, tok_f, tok_f, tok_b, tok_b, tok_b, dir_f, dir_f, dir_f, dir_b, dir_b, dir_b],
        out_specs=[tok_f, tok_b],
        out_shape=[out, out],
        scratch_shapes=[pltpu.VMEM((2 * n_pairs, LANES, LANES), F32)],
        compiler_params=_cparams(("parallel", "arbitrary")),
        name="rwkv_scan",
    )(r, kk, v, r, kk, v, lw, kd, kka, lw, kd, kka)


def _mix_body(yf_ref, yb_ref, bonus_ref, g_ref, ret_ref, x_ref, gng_ref, gnb_ref, avg_ref, wo1_ref, wo2_ref,
              l1g_ref, l1b_ref, rwt_ref, rb_ref, x1_ref, xp_ref, idx_ref, gate_ref, cnt_ref, *, alpha):
    y = yf_ref[...] + yb_ref[...]
    avg = avg_ref[...]
    mu = _dot_exact_rhs(y, avg)
    dl = y - mu
    var = jnp.dot((dl * dl).astype(BF16), avg, preferred_element_type=F32)
    yn = dl * lax.rsqrt(var + RWKV_GN_EPS) * gng_ref[...] + gnb_ref[...]
    rw = (yn + bonus_ref[...]) * g_ref[...]
    m = _dot(ret_ref[...], wo1_ref[...]) + _dot(rw, wo2_ref[...])
    x1 = _layer_norm(alpha * x_ref[...] + m, l1g_ref[...], l1b_ref[...])
    x1_ref[...] = x1
    xp_ref[...] = _pack_halves(x1)

    scores = _sigmoid(_dot_nt(rwt_ref[...], x1))
    E, tm = scores.shape
    GS = E // N_GROUPS
    NEG = -jnp.inf
    biased = scores + rb_ref[...]
    rowi = lax.broadcasted_iota(jnp.int32, (E, tm), 0)
    ri = lax.broadcasted_iota(jnp.int32, (GS, tm), 0)
    gs_rows = []
    for gi in range(N_GROUPS):
        blk = biased[gi * GS:(gi + 1) * GS, :]
        m1 = jnp.max(blk, axis=0, keepdims=True)
        i1 = jnp.min(jnp.where(blk == m1, ri, GS), axis=0, keepdims=True)
        m2 = jnp.max(jnp.where(ri == i1, NEG, blk), axis=0, keepdims=True)
        gs_rows.append(m1 + m2)
    cur = jnp.concatenate(gs_rows, axis=0)
    gidx = lax.broadcasted_iota(jnp.int32, (N_GROUPS, tm), 0)
    top_groups = []
    for _ in range(TOPK_GROUPS):
        mx = jnp.max(cur, axis=0, keepdims=True)
        ix = jnp.min(jnp.where(cur == mx, gidx, N_GROUPS), axis=0, keepdims=True)
        top_groups.append(ix)
        cur = jnp.where(gidx == ix, NEG, cur)
    blocks = []
    for gi in range(N_GROUPS):
        keep = top_groups[0] == gi
        for ix in top_groups[1:]:
            keep = jnp.logical_or(keep, ix == gi)
        blocks.append(jnp.where(keep, biased[gi * GS:(gi + 1) * GS, :], NEG))
    allowed = jnp.concatenate(blocks, axis=0)
    cur = allowed
    idxs, sels = [], []
    for _ in range(TOP_K):
        mx = jnp.max(cur, axis=0, keepdims=True)
        ix = jnp.min(jnp.where(cur == mx, rowi, E), axis=0, keepdims=True)
        hit = rowi == ix
        sels.append(jnp.sum(jnp.where(hit, scores, 0.0), axis=0, keepdims=True))
        idxs.append(ix)
        cur = jnp.where(hit, NEG, cur)
    chosen = jnp.where(jnp.logical_and(allowed > NEG, cur == NEG), 1.0, 0.0)
    sel = jnp.concatenate(sels, axis=0)
    idx_ref[...] = jnp.concatenate(idxs, axis=0)
    gate_ref[...] = sel / jnp.sum(sel, axis=0, keepdims=True) * ROUTE_SCALE

    @pl.when(pl.program_id(0) == 0)
    def _():
        cnt_ref[...] = jnp.zeros_like(cnt_ref)

    cnt_ref[...] += jnp.sum(chosen, axis=1, keepdims=True)


def _mix_out(y_f, y_b, bonus, g, ret_out, x2d, p, tm):
    T, D = x2d.shape
    W = bonus.shape[1]
    Wr = ret_out.shape[1]
    E = p["router_wt"].shape[0]
    full = lambda a: pl.BlockSpec(a.shape, lambda i, nd=a.ndim: (0,) * nd)
    consts = [p["gn_g"], p["gn_b"], p["head_avg"], p["wo_ret"], p["wo_rwkv"], p["ln1_g"], p["ln1_b"],
              p["router_wt"], p["router_b"]]
    return pl.pallas_call(
        functools.partial(_mix_body, alpha=p["alpha"]),
        grid=(T // tm,),
        in_specs=[pl.BlockSpec((tm, W), lambda i: (i, 0)),
                  pl.BlockSpec((tm, W), lambda i: (i, 0)),
                  pl.BlockSpec((tm, W), lambda i: (i, 0)),
                  pl.BlockSpec((tm, W), lambda i: (i, 0)),
                  pl.BlockSpec((tm, Wr), lambda i: (i, 0)),
                  pl.BlockSpec((tm, D), lambda i: (i, 0))] + [full(a) for a in consts],
        out_specs=[pl.BlockSpec((tm, D), lambda i: (i, 0)),
                   pl.BlockSpec((tm, D // 2), lambda i: (i, 0)),
                   pl.BlockSpec((TOP_K, tm), lambda i: (0, i)),
                   pl.BlockSpec((TOP_K, tm), lambda i: (0, i)),
                   pl.BlockSpec((E, LANES), lambda i: (0, 0))],
        out_shape=[jax.ShapeDtypeStruct((T, D), F32),
                   jax.ShapeDtypeStruct((T, D // 2), jnp.int32),
                   jax.ShapeDtypeStruct((TOP_K, T), jnp.int32),
                   jax.ShapeDtypeStruct((TOP_K, T), F32),
                   jax.ShapeDtypeStruct((E, LANES), F32)],
        compiler_params=_cparams(("arbitrary",)),
        name="mix_out",
    )(y_f, y_b, bonus, g, ret_out, x2d, *consts)


def _plan_body(idx_ref, pstart_ref, upper_ref, dest_ref, run_ref):
    @pl.when(pl.program_id(0) == 0)
    def _():
        run_ref[...] = jnp.zeros_like(run_ref)

    E = pstart_ref.shape[0]
    tm = idx_ref.shape[1]
    rowi = lax.broadcasted_iota(jnp.int32, (E, tm), 0)
    hits = [rowi == idx_ref[k:k + 1, :] for k in range(TOP_K)]
    member = jnp.zeros((E, tm), F32)
    for hit in hits:
        member = jnp.where(hit, 1.0, member)
    before = jnp.dot(member.astype(BF16), upper_ref[...], preferred_element_type=F32)
    slot = pstart_ref[...] + run_ref[:, 0:1] + before
    dest_ref[...] = jnp.concatenate(
        [jnp.sum(jnp.where(hit, slot, 0.0), axis=0, keepdims=True) for hit in hits], axis=0).astype(jnp.int32)
    run_ref[...] += jnp.sum(member, axis=1, keepdims=True)


def _slot_plan(idx_t, pad_start, tm):
    T = idx_t.shape[1]
    E = pad_start.shape[0]
    upper = (jnp.arange(tm)[:, None] < jnp.arange(tm)[None, :]).astype(BF16)
    return pl.pallas_call(
        _plan_body,
        grid=(T // tm,),
        in_specs=[pl.BlockSpec((TOP_K, tm), lambda i: (0, i)),
                  pl.BlockSpec((E, 1), lambda i: (0, 0)),
                  pl.BlockSpec((tm, tm), lambda i: (0, 0))],
        out_specs=pl.BlockSpec((TOP_K, tm), lambda i: (0, i)),
        out_shape=jax.ShapeDtypeStruct((TOP_K, T), jnp.int32),
        scratch_shapes=[pltpu.VMEM((E, LANES), F32)],
        compiler_params=_cparams(("arbitrary",)),
        name="slot_plan",
    )(idx_t, pad_start.reshape(E, 1).astype(F32), upper)


_UPPER_HALF = -65536


def _bf16_bits(x):
    return lax.bitcast_convert_type(x.astype(BF16).astype(F32), jnp.int32)


def _pack_halves(x):
    c = x.shape[1] // 2
    return _bf16_bits(x[:, :c]) | (jnp.right_shift(_bf16_bits(x[:, c:]), 16) & 0xFFFF)


def _unpack_halves(u):
    hi = lax.bitcast_convert_type(u & _UPPER_HALF, F32)
    lo = lax.bitcast_convert_type(jnp.left_shift(u, 16), F32)
    return hi, lo


SC_WINDOW = 64


def _sc_dispatch(xp, dest, n_slots):
    T, C = xp.shape
    n_win = T // SC_WINDOW
    mesh = plsc.VectorSubcoreMesh(core_axis_name="core", subcore_axis_name="subcore")

    @pl.kernel(out_type=jax.ShapeDtypeStruct((n_slots, C), xp.dtype), mesh=mesh, scratch_types=[])
    def scatter_rows(x_hbm, i_hbm, o_hbm):
        def body(x_vmem, i_vmem):
            for k in range(TOP_K):
                pltpu.sync_copy(x_vmem, o_hbm.at[i_vmem.at[k]])

        pltpu.emit_pipeline(
            body,
            grid=(n_win,),
            in_specs=[pl.BlockSpec((SC_WINDOW, C), lambda i: (i, 0)),
                      pl.BlockSpec((None, TOP_K, SC_WINDOW), lambda i: (i, 0, 0))],
            out_specs=[],
            core_axis_name=("core", "subcore"),
            dimension_semantics=(pltpu.PARALLEL,),
        )(x_hbm, i_hbm)

    idx = dest.reshape(TOP_K, n_win, SC_WINDOW).transpose(1, 0, 2)
    return scatter_rows(xp, idx)


def _sc_gather(ys, dest):
    n_idx = dest.shape[0] * dest.shape[1]
    C = ys.shape[1]
    mesh = plsc.VectorSubcoreMesh(core_axis_name="core", subcore_axis_name="subcore")

    @pl.kernel(out_type=jax.ShapeDtypeStruct((n_idx, C), ys.dtype), mesh=mesh, scratch_types=[])
    def gather_rows(y_hbm, i_hbm, o_hbm):
        def body(i_vmem, o_vmem):
            pltpu.sync_copy(y_hbm.at[i_vmem.at[0]], o_vmem)

        pltpu.emit_pipeline(
            body,
            grid=(n_idx // SC_WINDOW,),
            in_specs=[pl.BlockSpec((1, SC_WINDOW), lambda i: (i, 0))],
            out_specs=[pl.BlockSpec((SC_WINDOW, C), lambda i: (i, 0))],
            core_axis_name=("core", "subcore"),
            dimension_semantics=(pltpu.PARALLEL,),
        )(i_hbm, o_hbm)

    return gather_rows(ys, dest.reshape(n_idx // SC_WINDOW, SC_WINDOW))


MOE_IN_SLOTS = 4
MOE_OUT_SLOTS = 2
MOE_EXPERTS_PER_STEP = 1


def _moe_body(nb_ref, b0_ref, cnt_ref, nu_ref, xs_hbm, wg_ref, wu_ref, wd_ref, ys_hbm,
              xbuf, obuf, wgb_ref, wub_ref, wdb_ref, in_sem, out_sem):
    G = MOE_BLOCK
    i = pl.program_id(0)
    n_used = nu_ref[0]

    def in_copy(g):
        slot = lax.rem(g, MOE_IN_SLOTS)
        return pltpu.make_async_copy(xs_hbm.at[pl.ds(pl.multiple_of(g * G, G), G), :], xbuf.at[slot],
                                     in_sem.at[slot])

    def out_copy(g):
        slot = lax.rem(g, MOE_OUT_SLOTS)
        return pltpu.make_async_copy(obuf.at[slot], ys_hbm.at[pl.ds(pl.multiple_of(g * G, G), G), :],
                                     out_sem.at[slot])

    @pl.when(i == 0)
    def _():
        for g in range(MOE_IN_SLOTS - 1):
            @pl.when(g < n_used)
            def _():
                in_copy(g).start()

    row = lax.broadcasted_iota(jnp.int32, (G, xbuf.shape[2]), 0)
    c = xbuf.shape[2]

    def run_expert(q):
        e = i * MOE_EXPERTS_PER_STEP + q
        nb = nb_ref[e]
        b0 = b0_ref[e]

        @pl.when(nb > 0)
        def _():
            wgb_ref[...] = wg_ref[q].astype(BF16)
            wub_ref[...] = wu_ref[q].astype(BF16)
            wdb_ref[...] = wd_ref[q].astype(BF16)

            def step(j, carry):
                g = b0 + j
                in_copy(g).wait()

                @pl.when(g + (MOE_IN_SLOTS - 1) < n_used)
                def _():
                    in_copy(g + (MOE_IN_SLOTS - 1)).start()

                @pl.when(g >= MOE_OUT_SLOTS)
                def _():
                    out_copy(g - MOE_OUT_SLOTS).wait()

                u = jnp.where(row < cnt_ref[e] - j * G, xbuf[lax.rem(g, MOE_IN_SLOTS)], 0)
                x_hi, x_lo = _unpack_halves(u)
                gate = _dot(x_hi, wgb_ref[:c, :]) + _dot(x_lo, wgb_ref[c:, :])
                up = _dot(x_hi, wub_ref[:c, :]) + _dot(x_lo, wub_ref[c:, :])
                obuf[lax.rem(g, MOE_OUT_SLOTS)] = _pack_halves(_dot(_silu(gate) * up, wdb_ref[...]))
                out_copy(g).start()
                return carry

            lax.fori_loop(0, nb, step, 0)

    for q in range(MOE_EXPERTS_PER_STEP):
        run_expert(q)

    @pl.when(i == pl.num_programs(0) - 1)
    def _():
        for back in range(MOE_OUT_SLOTS, 0, -1):
            @pl.when(n_used >= back)
            def _():
                out_copy(n_used - back).wait()


def _moe_ffn(xs, blocks_per_expert, first_block, counts, n_used, w_gate, w_up, w_down):
    P, C = xs.shape
    G = MOE_BLOCK
    E, D, De = w_gate.shape
    XP = MOE_EXPERTS_PER_STEP
    assert E % XP == 0
    wspec = lambda shape: pl.BlockSpec((XP,) + shape, lambda i, nb, b0, cnt, nu: (i, 0, 0))
    grid_spec = pltpu.PrefetchScalarGridSpec(
        num_scalar_prefetch=4,
        grid=(E // XP,),
        in_specs=[pl.BlockSpec(memory_space=pl.ANY), wspec((D, De)), wspec((D, De)), wspec((De, D))],
        out_specs=pl.BlockSpec(memory_space=pl.ANY),
        scratch_shapes=[pltpu.VMEM((MOE_IN_SLOTS, G, C), jnp.int32), pltpu.VMEM((MOE_OUT_SLOTS, G, C), jnp.int32),
                        pltpu.VMEM((D, De), BF16), pltpu.VMEM((D, De), BF16), pltpu.VMEM((De, D), BF16),
                        pltpu.SemaphoreType.DMA((MOE_IN_SLOTS,)), pltpu.SemaphoreType.DMA((MOE_OUT_SLOTS,))],
    )
    return pl.pallas_call(
        _moe_body,
        grid_spec=grid_spec,
        out_shape=jax.ShapeDtypeStruct((P, C), jnp.int32),
        compiler_params=_cparams(("arbitrary",)),
        name="moe_ffn",
    )(blocks_per_expert, first_block, counts, n_used, xs, w_gate, w_up, w_down)


def _comb_body(yg_ref, x1_ref, gt_ref, sg_ref, su_ref, sd_ref, l2g_ref, l2b_ref, o_ref, *, alpha):
    x1 = x1_ref[...]
    xb = x1.astype(BF16)
    shared = _dot(_silu(_dot(xb, sg_ref[...])) * _dot(xb, su_ref[...]), sd_ref[...])
    gt = gt_ref[...]
    acc_hi = acc_lo = None
    for k in range(TOP_K):
        hi, lo = _unpack_halves(yg_ref[k])
        gk = gt[:, k:k + 1]
        acc_hi = gk * hi if acc_hi is None else acc_hi + gk * hi
        acc_lo = gk * lo if acc_lo is None else acc_lo + gk * lo
    routed = jnp.concatenate([acc_hi, acc_lo], axis=1)
    o_ref[...] = _layer_norm(alpha * x1 + (routed + shared), l2g_ref[...], l2b_ref[...])


def _combine(yg, x1, gates_t, p, tm):
    T, D = x1.shape
    full = lambda a: pl.BlockSpec(a.shape, lambda i, nd=a.ndim: (0,) * nd)
    consts = [p["sh_gate"], p["sh_up"], p["sh_down"], p["ln2_g"], p["ln2_b"]]
    return pl.pallas_call(
        functools.partial(_comb_body, alpha=p["alpha"]),
        grid=(T // tm,),
        in_specs=[pl.BlockSpec((TOP_K, tm, D // 2), lambda i: (0, i, 0)),
                  pl.BlockSpec((tm, D), lambda i: (i, 0)),
                  pl.BlockSpec((tm, TOP_K), lambda i: (i, 0))] + [full(a) for a in consts],
        out_specs=pl.BlockSpec((tm, D), lambda i: (i, 0)),
        out_shape=jax.ShapeDtypeStruct((T, D), F32),
        compiler_params=_cparams(("parallel",)),
        name="combine",
    )(yg, x1, gates_t, *consts)


def _segment_layout(counts, n_tokens):
    G = MOE_BLOCK
    E = counts.shape[0]
    n_blocks = -(-(n_tokens * TOP_K + E * (G - 1)) // G)
    padded = (counts + G - 1) // G * G
    pad_end = jnp.cumsum(padded)
    pad_start = pad_end - padded
    n_used = pad_end[-1:] // G
    return (pad_start, (padded // G).astype(jnp.int32), (pad_start // G).astype(jnp.int32),
            n_used.astype(jnp.int32), n_blocks)


def _rotary_tables(s, d):
    inv = ROPE_BASE ** (-jnp.arange(0, d, 2, dtype=F32) / d)
    ang = jnp.arange(s, dtype=F32)[:, None] * inv[None, :]
    cos = jnp.cos(ang)
    sin = jnp.sin(ang)
    return jnp.concatenate([cos, cos], axis=-1), jnp.concatenate([-sin, sin], axis=-1)


def _layer_params(l, depth, w_in, ret_gn_g, ret_gn_b, rwkv_mu, rwkv_w0, rwkv_w_up, rwkv_a0, rwkv_a_up,
                  rwkv_g_up, rwkv_k_k, rwkv_k_a, rwkv_r_k, rwkv_gn_g, rwkv_gn_b, w_out, ln1_g, ln1_b,
                  router_w, router_bias, exp_w_gate, exp_w_up, exp_w_down, sh_w_gate, sh_w_up,
                  sh_w_down, ln2_g, ln2_b):
    ret_w = ret_gn_g.shape[-1]
    W = rwkv_gn_g.shape[-1]
    n_heads, hd = rwkv_r_k.shape[-2:]
    rank_w = rwkv_w_up.shape[2]
    rank_a = rwkv_a_up.shape[2]
    assert rank_w * 2 == LANES and rank_a * 2 == LANES and rwkv_g_up.shape[1] == LANES
    assert hd * 2 == LANES and SCAN_CHUNK == hd
    row = lambda a: a.reshape(1, -1).astype(F32)
    zw = jnp.zeros((rank_w, W), F32)
    za = jnp.zeros((rank_a, W), F32)
    head_id = jnp.arange(W) // hd
    same_head = (head_id[:, None] == head_id[None, :])
    wi = w_in[l]
    return dict(
        alpha=float((2 * depth) ** 0.25),
        rwkv_width=W,
        w_ret=wi[:, :4 * ret_w].astype(BF16),
        w_rwkv=wi[:, 4 * ret_w:].astype(BF16),
        ret_gn_g=row(ret_gn_g[l]), ret_gn_b=row(ret_gn_b[l]),
        logg=jnp.broadcast_to(
            jnp.log1p(-jnp.exp2(-5.0 - jnp.arange(RET_HEADS, dtype=F32)))[:, None, None],
            (RET_HEADS, 1, LANES)),
        mu=row(rwkv_mu[l]),
        wup_pad=jnp.stack([jnp.concatenate([rwkv_w_up[l, 0], zw], 0),
                           jnp.concatenate([zw, rwkv_w_up[l, 1]], 0)]).astype(BF16),
        aup_pad=jnp.stack([jnp.concatenate([rwkv_a_up[l, 0], za], 0),
                           jnp.concatenate([za, rwkv_a_up[l, 1]], 0)]).astype(BF16),
        w0=rwkv_w0[l].astype(F32), a0=rwkv_a0[l].astype(F32),
        gup=rwkv_g_up[l].astype(BF16),
        k_k=row(rwkv_k_k[l]), k_a=row(rwkv_k_a[l]), r_k=row(rwkv_r_k[l]),
        head_ones=same_head.astype(BF16),
        head_avg=(same_head.astype(F32) / hd).astype(BF16),
        gn_g=row(rwkv_gn_g[l]), gn_b=row(rwkv_gn_b[l]),
        wo_ret=w_out[l, :ret_w].astype(BF16), wo_rwkv=w_out[l, ret_w:].astype(BF16),
        ln1_g=row(ln1_g[l]), ln1_b=row(ln1_b[l]),
        router_wt=router_w[l].T.astype(BF16), router_b=router_bias[l].reshape(-1, 1).astype(F32),
        exp_gate=exp_w_gate[l], exp_up=exp_w_up[l], exp_down=exp_w_down[l],
        sh_gate=sh_w_gate[l].astype(BF16), sh_up=sh_w_up[l].astype(BF16), sh_down=sh_w_down[l].astype(BF16),
        ln2_g=row(ln2_g[l]), ln2_b=row(ln2_b[l]),
    )


def _pick(n, pref):
    t = min(n, pref)
    while n % t:
        t //= 2
    return t


def _layer(x, p):
    b, s, D = x.shape
    T = b * s
    x2d = x.reshape(T, D)
    tm = _pick(T, 256)
    z_ret, r, v, kk, lw, kd, kka, bonus, g = _proj_prep(x, p, _pick(s, 256))
    cos, sin = _rotary_tables(s, RET_CHUNK)
    ret_out = _retention(z_ret, cos, sin, p["logg"], p["ret_gn_g"], p["ret_gn_b"])
    y_f, y_b = _rwkv_scan(r, kk, v, lw, kd, kka, _pick(s, 512))
    W = p["rwkv_width"]
    x1, xp, idx_t, gates, cnt = _mix_out(y_f.reshape(T, W), y_b.reshape(T, W), bonus.reshape(T, W),
                                         g.reshape(T, W), ret_out.reshape(T, -1), x2d, p, tm)
    counts = cnt[:, 0].astype(jnp.int32)
    pad_start, blocks_per_expert, first_block, n_used, n_blocks = _segment_layout(counts, T)
    dest = _slot_plan(idx_t, pad_start, _pick(T, 512))
    xs = _sc_dispatch(xp, dest, n_blocks * MOE_BLOCK)
    ys = _moe_ffn(xs, blocks_per_expert, first_block, counts, n_used, p["exp_gate"], p["exp_up"], p["exp_down"])
    yg = _sc_gather(ys, dest).reshape(TOP_K, T, D // 2)
    out = _combine(yg, x1, gates.T, p, _pick(T, 256))
    return out.reshape(b, s, D)


def kernel(x_prompt, x_sample, w_in, ret_gn_g, ret_gn_b, rwkv_mu, rwkv_w0, rwkv_w_up, rwkv_a0, rwkv_a_up,
           rwkv_g_up, rwkv_k_k, rwkv_k_a, rwkv_r_k, rwkv_gn_g, rwkv_gn_b, w_out, ln1_g, ln1_b, router_w,
           router_bias, exp_w_gate, exp_w_up, exp_w_down, sh_w_gate, sh_w_up, sh_w_down, ln2_g, ln2_b):
    weights = (w_in, ret_gn_g, ret_gn_b, rwkv_mu, rwkv_w0, rwkv_w_up, rwkv_a0, rwkv_a_up, rwkv_g_up,
               rwkv_k_k, rwkv_k_a, rwkv_r_k, rwkv_gn_g, rwkv_gn_b, w_out, ln1_g, ln1_b, router_w,
               router_bias, exp_w_gate, exp_w_up, exp_w_down, sh_w_gate, sh_w_up, sh_w_down, ln2_g, ln2_b)
    depth = w_in.shape[0]
    layers = [_layer_params(l, depth, *weights) for l in range(depth)]

    def trunk(x):
        for p in layers:
            x = _layer(x, p)
        return x

    return trunk(x_prompt), trunk(x_sample)
```

```python
import functools
import math

import jax
import jax.numpy as jnp
from jax import lax
from jax.experimental import pallas as pl
from jax.experimental.pallas import tpu as pltpu
from jax.experimental.pallas import tpu_sc as plsc

F32 = jnp.float32
BF16 = jnp.bfloat16

RET_HEADS = 4
RET_CHUNK = 128
ROPE_BASE = 10000.0
TOP_K = 8
N_GROUPS = 8
TOPK_GROUPS = 4
ROUTE_SCALE = 2.5
MOE_BLOCK = 256
LN_EPS = 1e-5
GN_EPS = 1e-5
RWKV_GN_EPS = 64e-5

LANES = 128
SUBLANES = 8
VMEM_LIMIT_BYTES = 56 * 1024 * 1024

SCAN_CHUNK = 64


def _cparams(semantics):
    return pltpu.CompilerParams(dimension_semantics=semantics, vmem_limit_bytes=VMEM_LIMIT_BYTES)


def _dot(a, b):
    return jnp.dot(a.astype(BF16), b.astype(BF16), preferred_element_type=F32)


def _dot_nt(a, b):
    return lax.dot_general(a.astype(BF16), b.astype(BF16), (((1,), (1,)), ((), ())),
                           preferred_element_type=F32)


def _dot_exact_rhs(x, w_bf16):
    hi = x.astype(BF16)
    mid = (x - hi.astype(F32)).astype(BF16)
    return (jnp.dot(hi, w_bf16, preferred_element_type=F32)
            + jnp.dot(mid, w_bf16, preferred_element_type=F32))


def _sigmoid(x):
    return 1.0 / (1.0 + jnp.exp(-x))


def _silu(x):
    return x * _sigmoid(x)


def _layer_norm(h, g, b):
    mu = jnp.mean(h, axis=-1, keepdims=True)
    d = h - mu
    var = jnp.mean(d * d, axis=-1, keepdims=True)
    return d * lax.rsqrt(var + LN_EPS) * g + b


RET_GROUP = 16


def _ret_body(q_ref, k_ref, v_ref, gt_ref, cos_ref, sin_ref, lg_ref, gg_ref, gb_ref, o_ref,
              qs_ref, sf_ref, sb_ref, acc_ref, *, qscale):
    C = RET_CHUNK
    U = RET_GROUP
    s = q_ref.shape[0]
    n = s // C
    lg = lg_ref[...]
    pos = lax.broadcasted_iota(jnp.int32, (C, C), 0).astype(F32)
    col = lax.broadcasted_iota(jnp.int32, (C, C), 1).astype(F32)
    sc_q_prev = jnp.exp((pos + 1.0) * lg)
    sc_k_fwd = jnp.exp((C - 1.0 - pos) * lg)
    sc_k_bwd = jnp.exp(pos * lg)
    sc_q_next = jnp.exp((C - pos) * lg)
    g_chunk = jnp.exp(float(C) * lg)
    decay = jnp.exp(jnp.abs(pos - col) * lg)
    gg = gg_ref[...]
    gb = gb_ref[...]

    def rows_of(grp):
        return [pl.ds(pl.multiple_of((grp * U + u) * C, C), C) for u in range(U)]

    def rot(x, r):
        x = x.astype(F32)
        return x * cos_ref[r, :] + pltpu.roll(x, C // 2, 1) * sin_ref[r, :]

    def local(grp, carry):
        rows = rows_of(grp)
        q = [rot(q_ref[r, :], r) * qscale for r in rows]
        k = [rot(k_ref[r, :], r) for r in rows]
        vb = [v_ref[r, :].astype(BF16) for r in rows]
        sc = [_dot_nt(q[u], k[u]) * decay for u in range(U)]
        out = [_dot(sc[u], vb[u]) for u in range(U)]
        kf = [_dot((k[u] * sc_k_fwd).T, vb[u]) for u in range(U)]
        kb = [_dot((k[u] * sc_k_bwd).T, vb[u]) for u in range(U)]
        for u in range(U):
            qs_ref[rows[u], :] = q[u]
            acc_ref[rows[u], :] = out[u]
            sf_ref[grp * U + u] = kf[u]
            sb_ref[grp * U + u] = kb[u]
        return carry

    lax.fori_loop(0, n // U, local, 0)

    def fwd(c, S):
        kv = sf_ref[c]
        sf_ref[c] = S
        return S * g_chunk + kv

    lax.fori_loop(0, n, fwd, jnp.zeros((C, C), F32))

    def bwd(i, S):
        c = n - 1 - i
        kv = sb_ref[c]
        sb_ref[c] = S
        return S * g_chunk + kv

    lax.fori_loop(0, n, bwd, jnp.zeros((C, C), F32))

    def cross(grp, carry):
        rows = rows_of(grp)
        q = [qs_ref[r, :] for r in rows]
        y = [acc_ref[rows[u], :] + _dot(jnp.concatenate([q[u] * sc_q_prev, q[u] * sc_q_next], axis=1),
                                        jnp.concatenate([sf_ref[grp * U + u], sb_ref[grp * U + u]], axis=0))
             for u in range(U)]
        for u in range(U):
            mu = jnp.mean(y[u], axis=-1, keepdims=True)
            d = y[u] - mu
            var = jnp.mean(d * d, axis=-1, keepdims=True)
            yn = d * lax.rsqrt(var + GN_EPS) * gg + gb
            o_ref[rows[u], :] = _silu(gt_ref[rows[u], :].astype(F32)) * yn
        return carry

    lax.fori_loop(0, n // U, cross, 0)


def _retention(z_ret, cos, sin, logg, gn_g, gn_b):
    b, s, _ = z_ret.shape
    C = RET_CHUNK
    H = RET_HEADS
    assert C == LANES and s % (C * RET_GROUP) == 0
    n = s // C
    blk = lambda off: pl.BlockSpec((None, s, C), lambda i, h, off=off: (i, 0, off + h))
    return pl.pallas_call(
        functools.partial(_ret_body, qscale=float(C) ** -0.5),
        grid=(b, H),
        in_specs=[blk(0), blk(H), blk(2 * H), blk(3 * H),
                  pl.BlockSpec((s, C), lambda i, h: (0, 0)),
                  pl.BlockSpec((s, C), lambda i, h: (0, 0)),
                  pl.BlockSpec((None, 1, C), lambda i, h: (h, 0, 0)),
                  pl.BlockSpec((1, C), lambda i, h: (0, h)),
                  pl.BlockSpec((1, C), lambda i, h: (0, h))],
        out_specs=pl.BlockSpec((None, s, C), lambda i, h: (i, 0, h)),
        out_shape=jax.ShapeDtypeStruct((b, s, H * C), F32),
        scratch_shapes=[pltpu.VMEM((s, C), F32), pltpu.VMEM((n, C, C), F32), pltpu.VMEM((n, C, C), F32),
                        pltpu.VMEM((s, C), F32)],
        compiler_params=_cparams(("parallel", "parallel")),
        name="retention",
    )(z_ret, z_ret, z_ret, z_ret, cos, sin, logg, gn_g, gn_b)


def _prep_body(x_ref, xp_ref, xn_ref, wr_ref, ww_ref, mu_ref, wup_ref, aup_ref, w0_ref, a0_ref, gup_ref,
               kkp_ref, ka_ref, rk_ref, bd_ref,
               zr_o, r_o, v_o, kk_o, lw_o, kd_o, kka_o, bonus_o, g_o, *, width):
    t = pl.program_id(1)
    nt = pl.num_programs(1)
    W = width
    ts = x_ref.shape[0]
    x_all = jnp.concatenate([x_ref[...], xp_ref[...], xn_ref[...]], axis=0).astype(BF16)
    zr_o[...] = jnp.dot(x_all[:ts], wr_ref[...], preferred_element_type=F32).astype(zr_o.dtype)
    z_all = jnp.dot(x_all, ww_ref[...], preferred_element_type=F32)
    z = z_all[:ts]
    row = lax.broadcasted_iota(jnp.int32, (ts, 1), 0)
    prev_row = jnp.where(t > 0, z_all[ts + SUBLANES - 1:ts + SUBLANES, :], 0.0)
    next_row = jnp.where(t < nt - 1, z_all[ts + SUBLANES:ts + SUBLANES + 1, :], 0.0)
    prev = jnp.where(row == 0, prev_row, pltpu.roll(z, 1, 0))
    nxt = jnp.where(row == ts - 1, next_row, pltpu.roll(z, ts - 1, 0))
    zs = z + mu_ref[...] * (0.5 * (prev + nxt) - z)

    r = zs[:, 0:W]
    kx = zs[:, W:2 * W]
    vx = zs[:, 2 * W:3 * W]
    wd = jnp.tanh(zs[:, 3 * W:3 * W + LANES])
    ad = zs[:, 3 * W + LANES:3 * W + 2 * LANES]
    gd = _sigmoid(zs[:, 3 * W + 2 * LANES:3 * W + 3 * LANES])
    bd = bd_ref[...]

    kk = kx * kkp_ref[...]
    ssq = jnp.dot((kk * kk).astype(BF16), bd, preferred_element_type=F32)
    kk = kk * lax.rsqrt(jnp.maximum(ssq, 1e-24))
    ka = ka_ref[...]
    ksum = None
    for d in range(2):
        pre = w0_ref[d:d + 1, :] + _dot(wd, wup_ref[d])
        lw_o[d] = -math.exp(-0.5) * _sigmoid(pre)
        a = _sigmoid(a0_ref[d:d + 1, :] + _dot(ad, aup_ref[d]))
        kd = kx * (1.0 + (a - 1.0) * ka)
        kd_o[d] = kd.astype(kd_o.dtype)
        kka_o[d] = (kk * a).astype(kka_o.dtype)
        ksum = kd if ksum is None else ksum + kd
    r_o[...] = r.astype(r_o.dtype)
    v_o[...] = vx.astype(v_o.dtype)
    kk_o[...] = kk.astype(kk_o.dtype)
    bonus_o[...] = (jnp.dot((r * ksum * rk_ref[...]).astype(BF16), bd, preferred_element_type=F32)
                    * vx).astype(bonus_o.dtype)
    g_o[...] = _dot(gd, gup_ref[...]).astype(g_o.dtype)


def _proj_prep(x, p, ts):
    b, s, D = x.shape
    W = p["rwkv_width"]
    nr = p["w_ret"].shape[1]
    nt = s // ts
    hb = ts // SUBLANES
    last = s // SUBLANES - 1
    full = lambda a: pl.BlockSpec(a.shape, lambda i, t, nd=a.ndim: (0,) * nd)
    out_tok = pl.BlockSpec((None, ts, W), lambda i, t: (i, t, 0))
    out_dir = pl.BlockSpec((2, None, ts, W), lambda i, t: (0, i, t, 0))
    tok_shape = jax.ShapeDtypeStruct((b, s, W), BF16)
    dir_shape = jax.ShapeDtypeStruct((2, b, s, W), BF16)
    lw_shape = jax.ShapeDtypeStruct((2, b, s, W), F32)
    consts = [p["w_ret"], p["w_rwkv"], p["mu"], p["wup_pad"], p["aup_pad"], p["w0"], p["a0"], p["gup"],
              p["k_k"], p["k_a"], p["r_k"], p["head_ones"]]
    return pl.pallas_call(
        functools.partial(_prep_body, width=W),
        grid=(b, nt),
        in_specs=[pl.BlockSpec((None, ts, D), lambda i, t: (i, t, 0)),
                  pl.BlockSpec((None, SUBLANES, D), lambda i, t: (i, jnp.maximum(t * hb - 1, 0), 0)),
                  pl.BlockSpec((None, SUBLANES, D), lambda i, t: (i, jnp.minimum((t + 1) * hb, last), 0)),
                  ] + [full(a) for a in consts],
        out_specs=[pl.BlockSpec((None, ts, nr), lambda i, t: (i, t, 0)),
                   out_tok, out_tok, out_tok, out_dir, out_dir, out_dir, out_tok, out_tok],
        out_shape=[jax.ShapeDtypeStruct((b, s, nr), BF16),
                   tok_shape, tok_shape, tok_shape, lw_shape, dir_shape, dir_shape, tok_shape, tok_shape],
        compiler_params=_cparams(("parallel", "parallel")),
        name="proj_prep",
    )(x, x, x, *consts)


def _scan_body(rf_ref, kkf_ref, vf_ref, rb_ref, kkb_ref, vb_ref, lwf_ref, kdf_ref, kkaf_ref,
               lwb_ref, kdb_ref, kkab_ref, yf_ref, yb_ref, st_ref, *, n_pairs):
    L = SCAN_CHUNK
    H = 2 * L
    assert H == LANES
    tt = rf_ref.shape[0]
    nch = tt // L

    @pl.when(pl.program_id(1) == 0)
    def _():
        st_ref[...] = jnp.zeros_like(st_ref)

    r_refs, kk_refs, v_refs = (rf_ref, rb_ref), (kkf_ref, kkb_ref), (vf_ref, vb_ref)
    lw_refs, kd_refs, kka_refs = (lwf_ref, lwb_ref), (kdf_ref, kdb_ref), (kkaf_ref, kkab_ref)
    y_refs = (yf_ref, yb_ref)
    ii = lax.broadcasted_iota(jnp.int32, (H, H), 0)
    jj = lax.broadcasted_iota(jnp.int32, (H, H), 1)
    same = (ii < L) == (jj < L)
    strict = (jnp.logical_and(same, ii > jj), jnp.logical_and(same, ii < jj))
    incl = (jnp.logical_and(same, ii >= jj), jnp.logical_and(same, ii <= jj))
    eye = ii == jj
    li = lax.broadcasted_iota(jnp.int32, (L, L), 0)
    lj = lax.broadcasted_iota(jnp.int32, (L, L), 1)
    tri = (jnp.where(li >= lj, 1.0, 0.0).astype(BF16), jnp.where(li <= lj, 1.0, 0.0).astype(BF16))
    head0 = lax.broadcasted_iota(jnp.int32, (L, H), 1) < L

    def stack(x):
        return jnp.concatenate([jnp.where(head0, x, 0.0), jnp.where(head0, 0.0, x)], axis=0)

    chains = [(d, hp) for d in range(2) for hp in range(n_pairs)]
    P = range(len(chains))
    lanes = [slice(hp * H, (hp + 1) * H) for _, hp in chains]
    dirs = [d for d, _ in chains]

    def chunk(j, carry):
        rows = (pl.ds(pl.multiple_of(j * L, L), L), pl.ds(pl.multiple_of((nch - 1 - j) * L, L), L))
        ld = lambda refs, h: refs[dirs[h]][rows[dirs[h]], lanes[h]].astype(F32)
        lw = [ld(lw_refs, h) for h in P]
        lw_hi = [x.astype(BF16) for x in lw]
        cum2 = [jnp.dot(tri[dirs[h]],
                        jnp.concatenate([lw_hi[h], (lw[h] - lw_hi[h].astype(F32)).astype(BF16)], axis=1),
                        preferred_element_type=F32) for h in P]
        cum = [c2[:, :H] + c2[:, H:] for c2 in cum2]
        tot = [jnp.sum(x, axis=0, keepdims=True) for x in lw]
        e_incl = [jnp.exp(a) for a in cum]
        e_excl = [jnp.exp(a - x) for a, x in zip(cum, lw)]
        e_inv = [jnp.exp(-a) for a in cum]
        e_rem = [jnp.exp(t_ - a) for t_, a in zip(tot, cum)]
        kk = [ld(kk_refs, h) for h in P]
        kka = [ld(kka_refs, h) for h in P]
        kd = [ld(kd_refs, h) for h in P]
        Kk = [stack(a * e) for a, e in zip(kk, e_excl)]
        R = [stack(ld(r_refs, h) * e_incl[h]) for h in P]
        B = [stack(a * e) for a, e in zip(kka, e_inv)]
        Kd = [stack(a * e) for a, e in zip(kd, e_inv)]
        Bh = [stack(a * e).T.astype(BF16) for a, e in zip(kka, e_rem)]
        Kh = [stack(a * e).T.astype(BF16) for a, e in zip(kd, e_rem)]
        V = [stack(ld(v_refs, h)).astype(BF16) for h in P]

        gram = [_dot_nt(jnp.concatenate([Kk[h], R[h]], axis=0), jnp.concatenate([B[h], Kd[h]], axis=0))
                for h in P]
        Np = [jnp.where(strict[dirs[h]], gram[h][:H, :H], 0.0) for h in P]
        AV = [_dot(jnp.where(strict[dirs[h]], gram[h][:H, H:], 0.0), V[h]) for h in P]
        DC = [jnp.concatenate([jnp.where(incl[dirs[h]], gram[h][H:, H:], 0.0).astype(BF16),
                               jnp.where(incl[dirs[h]], gram[h][H:, :H], 0.0).astype(BF16)], axis=1) for h in P]
        KB = [jnp.concatenate([Kh[h], Bh[h]], axis=1) for h in P]
        n_fac = L.bit_length() - 1
        Tm = [jnp.where(eye, 1.0, 0.0) - a for a in Np]
        Np = [_dot(a, a) for a in Np]
        for f in range(1, n_fac):
            if f < n_fac - 1:
                both = [_dot(Np[h], jnp.concatenate([Tm[h], Np[h]], axis=1)) for h in P]
                Tm = [t_ + o[:, :H] for t_, o in zip(Tm, both)]
                Np = [o[:, H:] for o in both]
            else:
                Tm = [t_ + _dot(a, t_) for t_, a in zip(Tm, Np)]
        Wm = [_dot(Tm[h], jnp.concatenate([Kk[h], AV[h]], axis=1)) for h in P]
        WR = [_dot(jnp.concatenate([Wm[h][:, :H], R[h]], axis=0), st_ref[h]) for h in P]
        VU = [jnp.concatenate([V[h], (-(WR[h][:H] + Wm[h][:, H:])).astype(BF16)], axis=0) for h in P]
        Ys = [WR[h][H:] + _dot(DC[h], VU[h]) for h in P]
        Sn = [_dot(KB[h], VU[h]) for h in P]
        for h in P:
            pl_col = jnp.sum(jnp.where(eye, jnp.exp(tot[h]), 0.0), axis=1, keepdims=True)
            st_ref[h] = pl_col * st_ref[h] + Sn[h]
            y_refs[dirs[h]][rows[dirs[h]], lanes[h]] = Ys[h][:L] + Ys[h][L:]
        return carry

    lax.fori_loop(0, nch, chunk, 0)


def _rwkv_scan(r, kk, v, lw, kd, kka, tt):
    b, s, W = r.shape
    nt = s // tt
    n_pairs = W // LANES
    tok_f = pl.BlockSpec((None, tt, W), lambda i, t: (i, t, 0))
    tok_b = pl.BlockSpec((None, tt, W), lambda i, t: (i, nt - 1 - t, 0))
    dir_f = pl.BlockSpec((None, None, tt, W), lambda i, t: (0, i, t, 0))
    dir_b = pl.BlockSpec((None, None, tt, W), lambda i, t: (1, i, nt - 1 - t, 0))
    out = jax.ShapeDtypeStruct((b, s, W), F32)
    return pl.pallas_call(
        functools.partial(_scan_body, n_pairs=n_pairs),
        grid=(b, nt),
        in_specs=[tok_f, tok_f, tok_f, tok_b, tok_b, tok_b, dir_f, dir_f, dir_f, dir_b, dir_b, dir_b],
        out_specs=[tok_f, tok_b],
        out_shape=[out, out],
        scratch_shapes=[pltpu.VMEM((2 * n_pairs, LANES, LANES), F32)],
        compiler_params=_cparams(("parallel", "arbitrary")),
        name="rwkv_scan",
    )(r, kk, v, r, kk, v, lw, kd, kka, lw, kd, kka)


def _mix_body(yf_ref, yb_ref, bonus_ref, g_ref, ret_ref, x_ref, gng_ref, gnb_ref, avg_ref, wo1_ref, wo2_ref,
              l1g_ref, l1b_ref, rwt_ref, rb_ref, x1_ref, xp_ref, idx_ref, gate_ref, cnt_ref, *, alpha):
    y = yf_ref[...] + yb_ref[...]
    avg = avg_ref[...]
    mu = _dot_exact_rhs(y, avg)
    dl = y - mu
    var = jnp.dot((dl * dl).astype(BF16), avg, preferred_element_type=F32)
    yn = dl * lax.rsqrt(var + RWKV_GN_EPS) * gng_ref[...] + gnb_ref[...]
    rw = (yn + bonus_ref[...]) * g_ref[...]
    m = _dot(ret_ref[...], wo1_ref[...]) + _dot(rw, wo2_ref[...])
    x1 = _layer_norm(alpha * x_ref[...] + m, l1g_ref[...], l1b_ref[...])
    x1_ref[...] = x1
    xp_ref[...] = _pack_halves(x1)

    scores = _sigmoid(_dot_nt(rwt_ref[...], x1))
    E, tm = scores.shape
    GS = E // N_GROUPS
    NEG = -jnp.inf
    biased = scores + rb_ref[...]
    rowi = lax.broadcasted_iota(jnp.int32, (E, tm), 0)
    ri = lax.broadcasted_iota(jnp.int32, (GS, tm), 0)
    gs_rows = []
    for gi in range(N_GROUPS):
        blk = biased[gi * GS:(gi + 1) * GS, :]
        m1 = jnp.max(blk, axis=0, keepdims=True)
        i1 = jnp.min(jnp.where(blk == m1, ri, GS), axis=0, keepdims=True)
        m2 = jnp.max(jnp.where(ri == i1, NEG, blk), axis=0, keepdims=True)
        gs_rows.append(m1 + m2)
    cur = jnp.concatenate(gs_rows, axis=0)
    gidx = lax.broadcasted_iota(jnp.int32, (N_GROUPS, tm), 0)
    top_groups = []
    for _ in range(TOPK_GROUPS):
        mx = jnp.max(cur, axis=0, keepdims=True)
        ix = jnp.min(jnp.where(cur == mx, gidx, N_GROUPS), axis=0, keepdims=True)
        top_groups.append(ix)
        cur = jnp.where(gidx == ix, NEG, cur)
    blocks = []
    for gi in range(N_GROUPS):
        keep = top_groups[0] == gi
        for ix in top_groups[1:]:
            keep = jnp.logical_or(keep, ix == gi)
        blocks.append(jnp.where(keep, biased[gi * GS:(gi + 1) * GS, :], NEG))
    allowed = jnp.concatenate(blocks, axis=0)
    cur = allowed
    idxs, sels = [], []
    for _ in range(TOP_K):
        mx = jnp.max(cur, axis=0, keepdims=True)
        ix = jnp.min(jnp.where(cur == mx, rowi, E), axis=0, keepdims=True)
        hit = rowi == ix
        sels.append(jnp.sum(jnp.where(hit, scores, 0.0), axis=0, keepdims=True))
        idxs.append(ix)
        cur = jnp.where(hit, NEG, cur)
    chosen = jnp.where(jnp.logical_and(allowed > NEG, cur == NEG), 1.0, 0.0)
    sel = jnp.concatenate(sels, axis=0)
    idx_ref[...] = jnp.concatenate(idxs, axis=0)
    gate_ref[...] = sel / jnp.sum(sel, axis=0, keepdims=True) * ROUTE_SCALE

    @pl.when(pl.program_id(0) == 0)
    def _():
        cnt_ref[...] = jnp.zeros_like(cnt_ref)

    cnt_ref[...] += jnp.sum(chosen, axis=1, keepdims=True)


def _mix_out(y_f, y_b, bonus, g, ret_out, x2d, p, tm):
    T, D = x2d.shape
    W = bonus.shape[1]
    Wr = ret_out.shape[1]
    E = p["router_wt"].shape[0]
    full = lambda a: pl.BlockSpec(a.shape, lambda i, nd=a.ndim: (0,) * nd)
    consts = [p["gn_g"], p["gn_b"], p["head_avg"], p["wo_ret"], p["wo_rwkv"], p["ln1_g"], p["ln1_b"],
              p["router_wt"], p["router_b"]]
    return pl.pallas_call(
        functools.partial(_mix_body, alpha=p["alpha"]),
        grid=(T // tm,),
        in_specs=[pl.BlockSpec((tm, W), lambda i: (i, 0)),
                  pl.BlockSpec((tm, W), lambda i: (i, 0)),
                  pl.BlockSpec((tm, W), lambda i: (i, 0)),
                  pl.BlockSpec((tm, W), lambda i: (i, 0)),
                  pl.BlockSpec((tm, Wr), lambda i: (i, 0)),
                  pl.BlockSpec((tm, D), lambda i: (i, 0))] + [full(a) for a in consts],
        out_specs=[pl.BlockSpec((tm, D), lambda i: (i, 0)),
                   pl.BlockSpec((tm, D // 2), lambda i: (i, 0)),
                   pl.BlockSpec((TOP_K, tm), lambda i: (0, i)),
                   pl.BlockSpec((TOP_K, tm), lambda i: (0, i)),
                   pl.BlockSpec((E, LANES), lambda i: (0, 0))],
        out_shape=[jax.ShapeDtypeStruct((T, D), F32),
                   jax.ShapeDtypeStruct((T, D // 2), jnp.int32),
                   jax.ShapeDtypeStruct((TOP_K, T), jnp.int32),
                   jax.ShapeDtypeStruct((TOP_K, T), F32),
                   jax.ShapeDtypeStruct((E, LANES), F32)],
        compiler_params=_cparams(("arbitrary",)),
        name="mix_out",
    )(y_f, y_b, bonus, g, ret_out, x2d, *consts)


def _plan_body(idx_ref, pstart_ref, upper_ref, dest_ref, run_ref):
    @pl.when(pl.program_id(0) == 0)
    def _():
        run_ref[...] = jnp.zeros_like(run_ref)

    E = pstart_ref.shape[0]
    tm = idx_ref.shape[1]
    rowi = lax.broadcasted_iota(jnp.int32, (E, tm), 0)
    hits = [rowi == idx_ref[k:k + 1, :] for k in range(TOP_K)]
    member = jnp.zeros((E, tm), F32)
    for hit in hits:
        member = jnp.where(hit, 1.0, member)
    before = jnp.dot(member.astype(BF16), upper_ref[...], preferred_element_type=F32)
    slot = pstart_ref[...] + run_ref[:, 0:1] + before
    dest_ref[...] = jnp.concatenate(
        [jnp.sum(jnp.where(hit, slot, 0.0), axis=0, keepdims=True) for hit in hits], axis=0).astype(jnp.int32)
    run_ref[...] += jnp.sum(member, axis=1, keepdims=True)


def _slot_plan(idx_t, pad_start, tm):
    T = idx_t.shape[1]
    E = pad_start.shape[0]
    upper = (jnp.arange(tm)[:, None] < jnp.arange(tm)[None, :]).astype(BF16)
    return pl.pallas_call(
        _plan_body,
        grid=(T // tm,),
        in_specs=[pl.BlockSpec((TOP_K, tm), lambda i: (0, i)),
                  pl.BlockSpec((E, 1), lambda i: (0, 0)),
                  pl.BlockSpec((tm, tm), lambda i: (0, 0))],
        out_specs=pl.BlockSpec((TOP_K, tm), lambda i: (0, i)),
        out_shape=jax.ShapeDtypeStruct((TOP_K, T), jnp.int32),
        scratch_shapes=[pltpu.VMEM((E, LANES), F32)],
        compiler_params=_cparams(("arbitrary",)),
        name="slot_plan",
    )(idx_t, pad_start.reshape(E, 1).astype(F32), upper)


_UPPER_HALF = -65536


def _bf16_bits(x):
    return lax.bitcast_convert_type(x.astype(BF16).astype(F32), jnp.int32)


def _pack_halves(x):
    c = x.shape[1] // 2
    return _bf16_bits(x[:, :c]) | (jnp.right_shift(_bf16_bits(x[:, c:]), 16) & 0xFFFF)


def _unpack_halves(u):
    hi = lax.bitcast_convert_type(u & _UPPER_HALF, F32)
    lo = lax.bitcast_convert_type(jnp.left_shift(u, 16), F32)
    return hi, lo


SC_WINDOW = 64


def _sc_dispatch(xp, dest, n_slots):
    T, C = xp.shape
    n_win = T // SC_WINDOW
    mesh = plsc.VectorSubcoreMesh(core_axis_name="core", subcore_axis_name="subcore")

    @pl.kernel(out_type=jax.ShapeDtypeStruct((n_slots, C), xp.dtype), mesh=mesh, scratch_types=[])
    def scatter_rows(x_hbm, i_hbm, o_hbm):
        def body(x_vmem, i_vmem):
            for k in range(TOP_K):
                pltpu.sync_copy(x_vmem, o_hbm.at[i_vmem.at[k]])

        pltpu.emit_pipeline(
            body,
            grid=(n_win,),
            in_specs=[pl.BlockSpec((SC_WINDOW, C), lambda i: (i, 0)),
                      pl.BlockSpec((None, TOP_K, SC_WINDOW), lambda i: (i, 0, 0))],
            out_specs=[],
            core_axis_name=("core", "subcore"),
            dimension_semantics=(pltpu.PARALLEL,),
        )(x_hbm, i_hbm)

    idx = dest.reshape(TOP_K, n_win, SC_WINDOW).transpose(1, 0, 2)
    return scatter_rows(xp, idx)


def _sc_gather(ys, dest):
    n_idx = dest.shape[0] * dest.shape[1]
    C = ys.shape[1]
    mesh = plsc.VectorSubcoreMesh(core_axis_name="core", subcore_axis_name="subcore")

    @pl.kernel(out_type=jax.ShapeDtypeStruct((n_idx, C), ys.dtype), mesh=mesh, scratch_types=[])
    def gather_rows(y_hbm, i_hbm, o_hbm):
        def body(i_vmem, o_vmem):
            pltpu.sync_copy(y_hbm.at[i_vmem.at[0]], o_vmem)

        pltpu.emit_pipeline(
            body,
            grid=(n_idx // SC_WINDOW,),
            in_specs=[pl.BlockSpec((1, SC_WINDOW), lambda i: (i, 0))],
            out_specs=[pl.BlockSpec((SC_WINDOW, C), lambda i: (i, 0))],
            core_axis_name=("core", "subcore"),
            dimension_semantics=(pltpu.PARALLEL,),
        )(i_hbm, o_hbm)

    return gather_rows(ys, dest.reshape(n_idx // SC_WINDOW, SC_WINDOW))


MOE_IN_SLOTS = 6
MOE_OUT_SLOTS = 3
MOE_EXPERTS_PER_STEP = 1


def _moe_body(nb_ref, b0_ref, cnt_ref, nu_ref, xs_hbm, wg_ref, wu_ref, wd_ref, ys_hbm,
              xbuf, obuf, wgb_ref, wub_ref, wdb_ref, in_sem, out_sem):
    G = MOE_BLOCK
    i = pl.program_id(0)
    n_used = nu_ref[0]

    def in_copy(g):
        slot = lax.rem(g, MOE_IN_SLOTS)
        return pltpu.make_async_copy(xs_hbm.at[pl.ds(pl.multiple_of(g * G, G), G), :], xbuf.at[slot],
                                     in_sem.at[slot])

    def out_copy(g):
        slot = lax.rem(g, MOE_OUT_SLOTS)
        return pltpu.make_async_copy(obuf.at[slot], ys_hbm.at[pl.ds(pl.multiple_of(g * G, G), G), :],
                                     out_sem.at[slot])

    @pl.when(i == 0)
    def _():
        for g in range(MOE_IN_SLOTS - 1):
            @pl.when(g < n_used)
            def _():
                in_copy(g).start()

    row = lax.broadcasted_iota(jnp.int32, (G, xbuf.shape[2]), 0)
    c = xbuf.shape[2]

    def run_expert(q):
        e = i * MOE_EXPERTS_PER_STEP + q
        nb = nb_ref[e]
        b0 = b0_ref[e]

        @pl.when(nb > 0)
        def _():
            wgb_ref[...] = wg_ref[q].astype(BF16)
            wub_ref[...] = wu_ref[q].astype(BF16)
            wdb_ref[...] = wd_ref[q].astype(BF16)

            def step(j, carry):
                g = b0 + j
                in_copy(g).wait()

                @pl.when(g + (MOE_IN_SLOTS - 1) < n_used)
                def _():
                    in_copy(g + (MOE_IN_SLOTS - 1)).start()

                @pl.when(g >= MOE_OUT_SLOTS)
                def _():
                    out_copy(g - MOE_OUT_SLOTS).wait()

                u = jnp.where(row < cnt_ref[e] - j * G, xbuf[lax.rem(g, MOE_IN_SLOTS)], 0)
                x_hi, x_lo = _unpack_halves(u)
                gate = _dot(x_hi, wgb_ref[:c, :]) + _dot(x_lo, wgb_ref[c:, :])
                up = _dot(x_hi, wub_ref[:c, :]) + _dot(x_lo, wub_ref[c:, :])
                obuf[lax.rem(g, MOE_OUT_SLOTS)] = _pack_halves(_dot(_silu(gate) * up, wdb_ref[...]))
                out_copy(g).start()
                return carry

            lax.fori_loop(0, nb, step, 0)

    for q in range(MOE_EXPERTS_PER_STEP):
        run_expert(q)

    @pl.when(i == pl.num_programs(0) - 1)
    def _():
        for back in range(MOE_OUT_SLOTS, 0, -1):
            @pl.when(n_used >= back)
            def _():
                out_copy(n_used - back).wait()


def _moe_ffn(xs, blocks_per_expert, first_block, counts, n_used, w_gate, w_up, w_down):
    P, C = xs.shape
    G = MOE_BLOCK
    E, D, De = w_gate.shape
    XP = MOE_EXPERTS_PER_STEP
    assert E % XP == 0
    wspec = lambda shape: pl.BlockSpec((XP,) + shape, lambda i, nb, b0, cnt, nu: (i, 0, 0))
    grid_spec = pltpu.PrefetchScalarGridSpec(
        num_scalar_prefetch=4,
        grid=(E // XP,),
        in_specs=[pl.BlockSpec(memory_space=pl.ANY), wspec((D, De)), wspec((D, De)), wspec((De, D))],
        out_specs=pl.BlockSpec(memory_space=pl.ANY),
        scratch_shapes=[pltpu.VMEM((MOE_IN_SLOTS, G, C), jnp.int32), pltpu.VMEM((MOE_OUT_SLOTS, G, C), jnp.int32),
                        pltpu.VMEM((D, De), BF16), pltpu.VMEM((D, De), BF16), pltpu.VMEM((De, D), BF16),
                        pltpu.SemaphoreType.DMA((MOE_IN_SLOTS,)), pltpu.SemaphoreType.DMA((MOE_OUT_SLOTS,))],
    )
    return pl.pallas_call(
        _moe_body,
        grid_spec=grid_spec,
        out_shape=jax.ShapeDtypeStruct((P, C), jnp.int32),
        compiler_params=_cparams(("arbitrary",)),
        name="moe_ffn",
    )(blocks_per_expert, first_block, counts, n_used, xs, w_gate, w_up, w_down)


def _comb_body(yg_ref, x1_ref, gt_ref, sg_ref, su_ref, sd_ref, l2g_ref, l2b_ref, o_ref, *, alpha):
    x1 = x1_ref[...]
    xb = x1.astype(BF16)
    shared = _dot(_silu(_dot(xb, sg_ref[...])) * _dot(xb, su_ref[...]), sd_ref[...])
    gt = gt_ref[...]
    acc_hi = acc_lo = None
    for k in range(TOP_K):
        hi, lo = _unpack_halves(yg_ref[k])
        gk = gt[:, k:k + 1]
        acc_hi = gk * hi if acc_hi is None else acc_hi + gk * hi
        acc_lo = gk * lo if acc_lo is None else acc_lo + gk * lo
    routed = jnp.concatenate([acc_hi, acc_lo], axis=1)
    o_ref[...] = _layer_norm(alpha * x1 + (routed + shared), l2g_ref[...], l2b_ref[...])


def _combine(yg, x1, gates_t, p, tm):
    T, D = x1.shape
    full = lambda a: pl.BlockSpec(a.shape, lambda i, nd=a.ndim: (0,) * nd)
    consts = [p["sh_gate"], p["sh_up"], p["sh_down"], p["ln2_g"], p["ln2_b"]]
    return pl.pallas_call(
        functools.partial(_comb_body, alpha=p["alpha"]),
        grid=(T // tm,),
        in_specs=[pl.BlockSpec((TOP_K, tm, D // 2), lambda i: (0, i, 0)),
                  pl.BlockSpec((tm, D), lambda i: (i, 0)),
                  pl.BlockSpec((tm, TOP_K), lambda i: (i, 0))] + [full(a) for a in consts],
        out_specs=pl.BlockSpec((tm, D), lambda i: (i, 0)),
        out_shape=jax.ShapeDtypeStruct((T, D), F32),
        compiler_params=_cparams(("parallel",)),
        name="combine",
    )(yg, x1, gates_t, *consts)


def _segment_layout(counts, n_tokens):
    G = MOE_BLOCK
    E = counts.shape[0]
    n_blocks = -(-(n_tokens * TOP_K + E * (G - 1)) // G)
    padded = (counts + G - 1) // G * G
    pad_end = jnp.cumsum(padded)
    pad_start = pad_end - padded
    n_used = pad_end[-1:] // G
    return (pad_start, (padded // G).astype(jnp.int32), (pad_start // G).astype(jnp.int32),
            n_used.astype(jnp.int32), n_blocks)


def _rotary_tables(s, d):
    inv = ROPE_BASE ** (-jnp.arange(0, d, 2, dtype=F32) / d)
    ang = jnp.arange(s, dtype=F32)[:, None] * inv[None, :]
    cos = jnp.cos(ang)
    sin = jnp.sin(ang)
    return jnp.concatenate([cos, cos], axis=-1), jnp.concatenate([-sin, sin], axis=-1)


def _layer_params(l, depth, w_in, ret_gn_g, ret_gn_b, rwkv_mu, rwkv_w0, rwkv_w_up, rwkv_a0, rwkv_a_up,
                  rwkv_g_up, rwkv_k_k, rwkv_k_a, rwkv_r_k, rwkv_gn_g, rwkv_gn_b, w_out, ln1_g, ln1_b,
                  router_w, router_bias, exp_w_gate, exp_w_up, exp_w_down, sh_w_gate, sh_w_up,
                  sh_w_down, ln2_g, ln2_b):
    ret_w = ret_gn_g.shape[-1]
    W = rwkv_gn_g.shape[-1]
    n_heads, hd = rwkv_r_k.shape[-2:]
    rank_w = rwkv_w_up.shape[2]
    rank_a = rwkv_a_up.shape[2]
    assert rank_w * 2 == LANES and rank_a * 2 == LANES and rwkv_g_up.shape[1] == LANES
    assert hd * 2 == LANES and SCAN_CHUNK == hd
    row = lambda a: a.reshape(1, -1).astype(F32)
    zw = jnp.zeros((rank_w, W), F32)
    za = jnp.zeros((rank_a, W), F32)
    head_id = jnp.arange(W) // hd
    same_head = (head_id[:, None] == head_id[None, :])
    wi = w_in[l]
    return dict(
        alpha=float((2 * depth) ** 0.25),
        rwkv_width=W,
        w_ret=wi[:, :4 * ret_w].astype(BF16),
        w_rwkv=wi[:, 4 * ret_w:].astype(BF16),
        ret_gn_g=row(ret_gn_g[l]), ret_gn_b=row(ret_gn_b[l]),
        logg=jnp.broadcast_to(
            jnp.log1p(-jnp.exp2(-5.0 - jnp.arange(RET_HEADS, dtype=F32)))[:, None, None],
            (RET_HEADS, 1, LANES)),
        mu=row(rwkv_mu[l]),
        wup_pad=jnp.stack([jnp.concatenate([rwkv_w_up[l, 0], zw], 0),
                           jnp.concatenate([zw, rwkv_w_up[l, 1]], 0)]).astype(BF16),
        aup_pad=jnp.stack([jnp.concatenate([rwkv_a_up[l, 0], za], 0),
                           jnp.concatenate([za, rwkv_a_up[l, 1]], 0)]).astype(BF16),
        w0=rwkv_w0[l].astype(F32), a0=rwkv_a0[l].astype(F32),
        gup=rwkv_g_up[l].astype(BF16),
        k_k=row(rwkv_k_k[l]), k_a=row(rwkv_k_a[l]), r_k=row(rwkv_r_k[l]),
        head_ones=same_head.astype(BF16),
        head_avg=(same_head.astype(F32) / hd).astype(BF16),
        gn_g=row(rwkv_gn_g[l]), gn_b=row(rwkv_gn_b[l]),
        wo_ret=w_out[l, :ret_w].astype(BF16), wo_rwkv=w_out[l, ret_w:].astype(BF16),
        ln1_g=row(ln1_g[l]), ln1_b=row(ln1_b[l]),
        router_wt=router_w[l].T.astype(BF16), router_b=router_bias[l].reshape(-1, 1).astype(F32),
        exp_gate=exp_w_gate[l], exp_up=exp_w_up[l], exp_down=exp_w_down[l],
        sh_gate=sh_w_gate[l].astype(BF16), sh_up=sh_w_up[l].astype(BF16), sh_down=sh_w_down[l].astype(BF16),
        ln2_g=row(ln2_g[l]), ln2_b=row(ln2_b[l]),
    )


def _pick(n, pref):
    t = min(n, pref)
    while n % t:
        t //= 2
    return t


def _layer(x, p):
    b, s, D = x.shape
    T = b * s
    x2d = x.reshape(T, D)
    tm = _pick(T, 256)
    z_ret, r, v, kk, lw, kd, kka, bonus, g = _proj_prep(x, p, _pick(s, 256))
    cos, sin = _rotary_tables(s, RET_CHUNK)
    ret_out = _retention(z_ret, cos, sin, p["logg"], p["ret_gn_g"], p["ret_gn_b"])
    y_f, y_b = _rwkv_scan(r, kk, v, lw, kd, kka, _pick(s, 512))
    W = p["rwkv_width"]
    x1, xp, idx_t, gates, cnt = _mix_out(y_f.reshape(T, W), y_b.reshape(T, W), bonus.reshape(T, W),
                                         g.reshape(T, W), ret_out.reshape(T, -1), x2d, p, tm)
    counts = cnt[:, 0].astype(jnp.int32)
    pad_start, blocks_per_expert, first_block, n_used, n_blocks = _segment_layout(counts, T)
    dest = _slot_plan(idx_t, pad_start, _pick(T, 512))
    xs = _sc_dispatch(xp, dest, n_blocks * MOE_BLOCK)
    ys = _moe_ffn(xs, blocks_per_expert, first_block, counts, n_used, p["exp_gate"], p["exp_up"], p["exp_down"])
    yg = _sc_gather(ys, dest).reshape(TOP_K, T, D // 2)
    out = _combine(yg, x1, gates.T, p, _pick(T, 256))
    return out.reshape(b, s, D)


def kernel(x_prompt, x_sample, w_in, ret_gn_g, ret_gn_b, rwkv_mu, rwkv_w0, rwkv_w_up, rwkv_a0, rwkv_a_up,
           rwkv_g_up, rwkv_k_k, rwkv_k_a, rwkv_r_k, rwkv_gn_g, rwkv_gn_b, w_out, ln1_g, ln1_b, router_w,
           router_bias, exp_w_gate, exp_w_up, exp_w_down, sh_w_gate, sh_w_up, sh_w_down, ln2_g, ln2_b):
    weights = (w_in, ret_gn_g, ret_gn_b, rwkv_mu, rwkv_w0, rwkv_w_up, rwkv_a0, rwkv_a_up, rwkv_g_up,
               rwkv_k_k, rwkv_k_a, rwkv_r_k, rwkv_gn_g, rwkv_gn_b, w_out, ln1_g, ln1_b, router_w,
               router_bias, exp_w_gate, exp_w_up, exp_w_down, sh_w_gate, sh_w_up, sh_w_down, ln2_g, ln2_b)
    depth = w_in.shape[0]
    layers = [_layer_params(l, depth, *weights) for l in range(depth)]

    def trunk(x):
        for p in layers:
            x = _layer(x, p)
        return x

    return trunk(x_prompt), trunk(x_sample)
```

```python
import functools
import math

import jax
import jax.numpy as jnp
from jax import lax
from jax.experimental import pallas as pl
from jax.experimental.pallas import tpu as pltpu
from jax.experimental.pallas import tpu_sc as plsc

F32 = jnp.float32
BF16 = jnp.bfloat16

RET_HEADS = 4
RET_CHUNK = 128
ROPE_BASE = 10000.0
TOP_K = 8
N_GROUPS = 8
TOPK_GROUPS = 4
ROUTE_SCALE = 2.5
MOE_BLOCK = 256
LN_EPS = 1e-5
GN_EPS = 1e-5
RWKV_GN_EPS = 64e-5

LANES = 128
SUBLANES = 8
VMEM_LIMIT_BYTES = 56 * 1024 * 1024

SCAN_CHUNK = 64


def _cparams(semantics):
    return pltpu.CompilerParams(dimension_semantics=semantics, vmem_limit_bytes=VMEM_LIMIT_BYTES)


def _dot(a, b):
    return jnp.dot(a.astype(BF16), b.astype(BF16), preferred_element_type=F32)


def _dot_nt(a, b):
    return lax.dot_general(a.astype(BF16), b.astype(BF16), (((1,), (1,)), ((), ())),
                           preferred_element_type=F32)


def _dot_exact_rhs(x, w_bf16):
    hi = x.astype(BF16)
    mid = (x - hi.astype(F32)).astype(BF16)
    return (jnp.dot(hi, w_bf16, preferred_element_type=F32)
            + jnp.dot(mid, w_bf16, preferred_element_type=F32))


def _sigmoid(x):
    return 1.0 / (1.0 + jnp.exp(-x))


def _silu(x):
    return x * _sigmoid(x)


def _layer_norm(h, g, b):
    mu = jnp.mean(h, axis=-1, keepdims=True)
    d = h - mu
    var = jnp.mean(d * d, axis=-1, keepdims=True)
    return d * lax.rsqrt(var + LN_EPS) * g + b


RET_GROUP = 16


def _ret_body(q_ref, k_ref, v_ref, gt_ref, cos_ref, sin_ref, lg_ref, gg_ref, gb_ref, o_ref,
              qs_ref, sf_ref, sb_ref, acc_ref, *, qscale):
    C = RET_CHUNK
    U = RET_GROUP
    s = q_ref.shape[0]
    n = s // C
    lg = lg_ref[...]
    pos = lax.broadcasted_iota(jnp.int32, (C, C), 0).astype(F32)
    col = lax.broadcasted_iota(jnp.int32, (C, C), 1).astype(F32)
    sc_q_prev = jnp.exp((pos + 1.0) * lg)
    sc_k_fwd = jnp.exp((C - 1.0 - pos) * lg)
    sc_k_bwd = jnp.exp(pos * lg)
    sc_q_next = jnp.exp((C - pos) * lg)
    g_chunk = jnp.exp(float(C) * lg)
    decay = jnp.exp(jnp.abs(pos - col) * lg)
    gg = gg_ref[...]
    gb = gb_ref[...]

    def rows_of(grp):
        return [pl.ds(pl.multiple_of((grp * U + u) * C, C), C) for u in range(U)]

    def rot(x, r):
        x = x.astype(F32)
        return x * cos_ref[r, :] + pltpu.roll(x, C // 2, 1) * sin_ref[r, :]

    def local(grp, carry):
        rows = rows_of(grp)
        q = [rot(q_ref[r, :], r) * qscale for r in rows]
        k = [rot(k_ref[r, :], r) for r in rows]
        vb = [v_ref[r, :].astype(BF16) for r in rows]
        sc = [_dot_nt(q[u], k[u]) * decay for u in range(U)]
        out = [_dot(sc[u], vb[u]) for u in range(U)]
        kf = [_dot((k[u] * sc_k_fwd).T, vb[u]) for u in range(U)]
        kb = [_dot((k[u] * sc_k_bwd).T, vb[u]) for u in range(U)]
        for u in range(U):
            qs_ref[rows[u], :] = q[u]
            acc_ref[rows[u], :] = out[u]
            sf_ref[grp * U + u] = kf[u]
            sb_ref[grp * U + u] = kb[u]
        return carry

    lax.fori_loop(0, n // U, local, 0)

    def fwd(c, S):
        kv = sf_ref[c]
        sf_ref[c] = S
        return S * g_chunk + kv

    lax.fori_loop(0, n, fwd, jnp.zeros((C, C), F32))

    def bwd(i, S):
        c = n - 1 - i
        kv = sb_ref[c]
        sb_ref[c] = S
        return S * g_chunk + kv

    lax.fori_loop(0, n, bwd, jnp.zeros((C, C), F32))

    def cross(grp, carry):
        rows = rows_of(grp)
        q = [qs_ref[r, :] for r in rows]
        y = [acc_ref[rows[u], :] + _dot(jnp.concatenate([q[u] * sc_q_prev, q[u] * sc_q_next], axis=1),
                                        jnp.concatenate([sf_ref[grp * U + u], sb_ref[grp * U + u]], axis=0))
             for u in range(U)]
        for u in range(U):
            mu = jnp.mean(y[u], axis=-1, keepdims=True)
            d = y[u] - mu
            var = jnp.mean(d * d, axis=-1, keepdims=True)
            yn = d * lax.rsqrt(var + GN_EPS) * gg + gb
            o_ref[rows[u], :] = _silu(gt_ref[rows[u], :].astype(F32)) * yn
        return carry

    lax.fori_loop(0, n // U, cross, 0)


def _retention(z_ret, cos, sin, logg, gn_g, gn_b):
    b, s, _ = z_ret.shape
    C = RET_CHUNK
    H = RET_HEADS
    assert C == LANES and s % (C * RET_GROUP) == 0
    n = s // C
    blk = lambda off: pl.BlockSpec((None, s, C), lambda i, h, off=off: (i, 0, off + h))
    return pl.pallas_call(
        functools.partial(_ret_body, qscale=float(C) ** -0.5),
        grid=(b, H),
        in_specs=[blk(0), blk(H), blk(2 * H), blk(3 * H),
                  pl.BlockSpec((s, C), lambda i, h: (0, 0)),
                  pl.BlockSpec((s, C), lambda i, h: (0, 0)),
                  pl.BlockSpec((None, 1, C), lambda i, h: (h, 0, 0)),
                  pl.BlockSpec((1, C), lambda i, h: (0, h)),
                  pl.BlockSpec((1, C), lambda i, h: (0, h))],
        out_specs=pl.BlockSpec((None, s, C), lambda i, h: (i, 0, h)),
        out_shape=jax.ShapeDtypeStruct((b, s, H * C), F32),
        scratch_shapes=[pltpu.VMEM((s, C), F32), pltpu.VMEM((n, C, C), F32), pltpu.VMEM((n, C, C), F32),
                        pltpu.VMEM((s, C), F32)],
        compiler_params=_cparams(("parallel", "parallel")),
        name="retention",
    )(z_ret, z_ret, z_ret, z_ret, cos, sin, logg, gn_g, gn_b)


def _prep_body(x_ref, xp_ref, xn_ref, wr_ref, ww_ref, mu_ref, wup_ref, aup_ref, w0_ref, a0_ref, gup_ref,
               kkp_ref, ka_ref, rk_ref, bd_ref,
               zr_o, r_o, v_o, kk_o, lw_o, kd_o, kka_o, bonus_o, g_o, *, width):
    t = pl.program_id(1)
    nt = pl.num_programs(1)
    W = width
    ts = x_ref.shape[0]
    x_all = jnp.concatenate([x_ref[...], xp_ref[...], xn_ref[...]], axis=0).astype(BF16)
    zr_o[...] = jnp.dot(x_all[:ts], wr_ref[...], preferred_element_type=F32).astype(zr_o.dtype)
    z_all = jnp.dot(x_all, ww_ref[...], preferred_element_type=F32)
    z = z_all[:ts]
    row = lax.broadcasted_iota(jnp.int32, (ts, 1), 0)
    prev_row = jnp.where(t > 0, z_all[ts + SUBLANES - 1:ts + SUBLANES, :], 0.0)
    next_row = jnp.where(t < nt - 1, z_all[ts + SUBLANES:ts + SUBLANES + 1, :], 0.0)
    prev = jnp.where(row == 0, prev_row, pltpu.roll(z, 1, 0))
    nxt = jnp.where(row == ts - 1, next_row, pltpu.roll(z, ts - 1, 0))
    zs = z + mu_ref[...] * (0.5 * (prev + nxt) - z)

    r = zs[:, 0:W]
    kx = zs[:, W:2 * W]
    vx = zs[:, 2 * W:3 * W]
    wd = jnp.tanh(zs[:, 3 * W:3 * W + LANES])
    ad = zs[:, 3 * W + LANES:3 * W + 2 * LANES]
    gd = _sigmoid(zs[:, 3 * W + 2 * LANES:3 * W + 3 * LANES])
    bd = bd_ref[...]

    kk = kx * kkp_ref[...]
    ssq = jnp.dot((kk * kk).astype(BF16), bd, preferred_element_type=F32)
    kk = kk * lax.rsqrt(jnp.maximum(ssq, 1e-24))
    ka = ka_ref[...]
    ksum = None
    for d in range(2):
        pre = w0_ref[d:d + 1, :] + _dot(wd, wup_ref[d])
        lw_o[d] = -math.exp(-0.5) * _sigmoid(pre)
        a = _sigmoid(a0_ref[d:d + 1, :] + _dot(ad, aup_ref[d]))
        kd = kx * (1.0 + (a - 1.0) * ka)
        kd_o[d] = kd.astype(kd_o.dtype)
        kka_o[d] = (kk * a).astype(kka_o.dtype)
        ksum = kd if ksum is None else ksum + kd
    r_o[...] = r.astype(r_o.dtype)
    v_o[...] = vx.astype(v_o.dtype)
    kk_o[...] = kk.astype(kk_o.dtype)
    bonus_o[...] = (jnp.dot((r * ksum * rk_ref[...]).astype(BF16), bd, preferred_element_type=F32)
                    * vx).astype(bonus_o.dtype)
    g_o[...] = _dot(gd, gup_ref[...]).astype(g_o.dtype)


def _proj_prep(x, p, ts):
    b, s, D = x.shape
    W = p["rwkv_width"]
    nr = p["w_ret"].shape[1]
    nt = s // ts
    hb = ts // SUBLANES
    last = s // SUBLANES - 1
    full = lambda a: pl.BlockSpec(a.shape, lambda i, t, nd=a.ndim: (0,) * nd)
    out_tok = pl.BlockSpec((None, ts, W), lambda i, t: (i, t, 0))
    out_dir = pl.BlockSpec((2, None, ts, W), lambda i, t: (0, i, t, 0))
    tok_shape = jax.ShapeDtypeStruct((b, s, W), BF16)
    dir_shape = jax.ShapeDtypeStruct((2, b, s, W), BF16)
    lw_shape = jax.ShapeDtypeStruct((2, b, s, W), F32)
    consts = [p["w_ret"], p["w_rwkv"], p["mu"], p["wup_pad"], p["aup_pad"], p["w0"], p["a0"], p["gup"],
              p["k_k"], p["k_a"], p["r_k"], p["head_ones"]]
    return pl.pallas_call(
        functools.partial(_prep_body, width=W),
        grid=(b, nt),
        in_specs=[pl.BlockSpec((None, ts, D), lambda i, t: (i, t, 0)),
                  pl.BlockSpec((None, SUBLANES, D), lambda i, t: (i, jnp.maximum(t * hb - 1, 0), 0)),
                  pl.BlockSpec((None, SUBLANES, D), lambda i, t: (i, jnp.minimum((t + 1) * hb, last), 0)),
                  ] + [full(a) for a in consts],
        out_specs=[pl.BlockSpec((None, ts, nr), lambda i, t: (i, t, 0)),
                   out_tok, out_tok, out_tok, out_dir, out_dir, out_dir, out_tok, out_tok],
        out_shape=[jax.ShapeDtypeStruct((b, s, nr), BF16),
                   tok_shape, tok_shape, tok_shape, lw_shape, dir_shape, dir_shape, tok_shape, tok_shape],
        compiler_params=_cparams(("parallel", "parallel")),
        name="proj_prep",
    )(x, x, x, *consts)


def _scan_body(rf_ref, kkf_ref, vf_ref, rb_ref, kkb_ref, vb_ref, lwf_ref, kdf_ref, kkaf_ref,
               lwb_ref, kdb_ref, kkab_ref, yf_ref, yb_ref, st_ref, *, n_pairs):
    L = SCAN_CHUNK
    H = 2 * L
    assert H == LANES
    tt = rf_ref.shape[0]
    nch = tt // L

    @pl.when(pl.program_id(1) == 0)
    def _():
        st_ref[...] = jnp.zeros_like(st_ref)

    r_refs, kk_refs, v_refs = (rf_ref, rb_ref), (kkf_ref, kkb_ref), (vf_ref, vb_ref)
    lw_refs, kd_refs, kka_refs = (lwf_ref, lwb_ref), (kdf_ref, kdb_ref), (kkaf_ref, kkab_ref)
    y_refs = (yf_ref, yb_ref)
    ii = lax.broadcasted_iota(jnp.int32, (H, H), 0)
    jj = lax.broadcasted_iota(jnp.int32, (H, H), 1)
    same = (ii < L) == (jj < L)
    strict = (jnp.logical_and(same, ii > jj), jnp.logical_and(same, ii < jj))
    incl = (jnp.logical_and(same, ii >= jj), jnp.logical_and(same, ii <= jj))
    eye = ii == jj
    li = lax.broadcasted_iota(jnp.int32, (L, L), 0)
    lj = lax.broadcasted_iota(jnp.int32, (L, L), 1)
    tri = (jnp.where(li >= lj, 1.0, 0.0).astype(BF16), jnp.where(li <= lj, 1.0, 0.0).astype(BF16))
    head0 = lax.broadcasted_iota(jnp.int32, (L, H), 1) < L

    def stack(x):
        return jnp.concatenate([jnp.where(head0, x, 0.0), jnp.where(head0, 0.0, x)], axis=0)

    chains = [(d, hp) for d in range(2) for hp in range(n_pairs)]
    P = range(len(chains))
    lanes = [slice(hp * H, (hp + 1) * H) for _, hp in chains]
    dirs = [d for d, _ in chains]

    def chunk(j, carry):
        rows = (pl.ds(pl.multiple_of(j * L, L), L), pl.ds(pl.multiple_of((nch - 1 - j) * L, L), L))
        ld = lambda refs, h: refs[dirs[h]][rows[dirs[h]], lanes[h]].astype(F32)
        lw = [ld(lw_refs, h) for h in P]
        lw_hi = [x.astype(BF16) for x in lw]
        cum2 = [jnp.dot(tri[dirs[h]],
                        jnp.concatenate([lw_hi[h], (lw[h] - lw_hi[h].astype(F32)).astype(BF16)], axis=1),
                        preferred_element_type=F32) for h in P]
        cum = [c2[:, :H] + c2[:, H:] for c2 in cum2]
        tot = [jnp.sum(x, axis=0, keepdims=True) for x in lw]
        e_incl = [jnp.exp(a) for a in cum]
        e_excl = [jnp.exp(a - x) for a, x in zip(cum, lw)]
        e_inv = [jnp.exp(-a) for a in cum]
        e_rem = [jnp.exp(t_ - a) for t_, a in zip(tot, cum)]
        kk = [ld(kk_refs, h) for h in P]
        kka = [ld(kka_refs, h) for h in P]
        kd = [ld(kd_refs, h) for h in P]
        Kk = [stack(a * e) for a, e in zip(kk, e_excl)]
        R = [stack(ld(r_refs, h) * e_incl[h]) for h in P]
        B = [stack(a * e) for a, e in zip(kka, e_inv)]
        Kd = [stack(a * e) for a, e in zip(kd, e_inv)]
        Bh = [stack(a * e).T.astype(BF16) for a, e in zip(kka, e_rem)]
        Kh = [stack(a * e).T.astype(BF16) for a, e in zip(kd, e_rem)]
        V = [stack(ld(v_refs, h)).astype(BF16) for h in P]

        gram = [_dot_nt(jnp.concatenate([Kk[h], R[h]], axis=0), jnp.concatenate([B[h], Kd[h]], axis=0))
                for h in P]
        Np = [jnp.where(strict[dirs[h]], gram[h][:H, :H], 0.0) for h in P]
        AV = [_dot(jnp.where(strict[dirs[h]], gram[h][:H, H:], 0.0), V[h]) for h in P]
        DC = [jnp.concatenate([jnp.where(incl[dirs[h]], gram[h][H:, H:], 0.0).astype(BF16),
                               jnp.where(incl[dirs[h]], gram[h][H:, :H], 0.0).astype(BF16)], axis=1) for h in P]
        KB = [jnp.concatenate([Kh[h], Bh[h]], axis=1) for h in P]
        n_fac = L.bit_length() - 1
        Tm = [jnp.where(eye, 1.0, 0.0) - a for a in Np]
        Np = [_dot(a, a) for a in Np]
        for f in range(1, n_fac):
            if f < n_fac - 1:
                both = [_dot(Np[h], jnp.concatenate([Tm[h], Np[h]], axis=1)) for h in P]
                Tm = [t_ + o[:, :H] for t_, o in zip(Tm, both)]
                Np = [o[:, H:] for o in both]
            else:
                Tm = [t_ + _dot(a, t_) for t_, a in zip(Tm, Np)]
        Wm = [_dot(Tm[h], jnp.concatenate([Kk[h], AV[h]], axis=1)) for h in P]
        WR = [_dot(jnp.concatenate([Wm[h][:, :H], R[h]], axis=0), st_ref[h]) for h in P]
        VU = [jnp.concatenate([V[h], (-(WR[h][:H] + Wm[h][:, H:])).astype(BF16)], axis=0) for h in P]
        Ys = [WR[h][H:] + _dot(DC[h], VU[h]) for h in P]
        Sn = [_dot(KB[h], VU[h]) for h in P]
        for h in P:
            pl_col = jnp.sum(jnp.where(eye, jnp.exp(tot[h]), 0.0), axis=1, keepdims=True)
            st_ref[h] = pl_col * st_ref[h] + Sn[h]
            y_refs[dirs[h]][rows[dirs[h]], lanes[h]] = Ys[h][:L] + Ys[h][L:]
        return carry

    lax.fori_loop(0, nch, chunk, 0)


def _rwkv_scan(r, kk, v, lw, kd, kka, tt):
    b, s, W = r.shape
    nt = s // tt
    n_pairs = W // LANES
    tok_f = pl.BlockSpec((None, tt, W), lambda i, t: (i, t, 0))
    tok_b = pl.BlockSpec((None, tt, W), lambda i, t: (i, nt - 1 - t, 0))
    dir_f = pl.BlockSpec((None, None, tt, W), lambda i, t: (0, i, t, 0))
    dir_b = pl.BlockSpec((None, None, tt, W), lambda i, t: (1, i, nt - 1 - t, 0))
    out = jax.ShapeDtypeStruct((b, s, W), F32)
    return pl.pallas_call(
        functools.partial(_scan_body, n_pairs=n_pairs),
        grid=(b, nt),
        in_specs=[tok_f, tok_f, tok_f, tok_b, tok_b, tok_b, dir_f, dir_f, dir_f, dir_b, dir_b, dir_b],
        out_specs=[tok_f, tok_b],
        out_shape=[out, out],
        scratch_shapes=[pltpu.VMEM((2 * n_pairs, LANES, LANES), F32)],
        compiler_params=_cparams(("parallel", "arbitrary")),
        name="rwkv_scan",
    )(r, kk, v, r, kk, v, lw, kd, kka, lw, kd, kka)


def _mix_body(yf_ref, yb_ref, bonus_ref, g_ref, ret_ref, x_ref, gng_ref, gnb_ref, avg_ref, wo1_ref, wo2_ref,
              l1g_ref, l1b_ref, rwt_ref, rb_ref, x1_ref, xp_ref, idx_ref, gate_ref, cnt_ref, *, alpha):
    y = yf_ref[...] + yb_ref[...]
    avg = avg_ref[...]
    mu = _dot_exact_rhs(y, avg)
    dl = y - mu
    var = jnp.dot((dl * dl).astype(BF16), avg, preferred_element_type=F32)
    yn = dl * lax.rsqrt(var + RWKV_GN_EPS) * gng_ref[...] + gnb_ref[...]
    rw = (yn + bonus_ref[...]) * g_ref[...]
    m = _dot(ret_ref[...], wo1_ref[...]) + _dot(rw, wo2_ref[...])
    x1 = _layer_norm(alpha * x_ref[...] + m, l1g_ref[...], l1b_ref[...])
    x1_ref[...] = x1
    xp_ref[...] = _pack_halves(x1)

    scores = _sigmoid(_dot_nt(rwt_ref[...], x1))
    E, tm = scores.shape
    GS = E // N_GROUPS
    NEG = -jnp.inf
    biased = scores + rb_ref[...]
    rowi = lax.broadcasted_iota(jnp.int32, (E, tm), 0)
    ri = lax.broadcasted_iota(jnp.int32, (GS, tm), 0)
    gs_rows = []
    for gi in range(N_GROUPS):
        blk = biased[gi * GS:(gi + 1) * GS, :]
        m1 = jnp.max(blk, axis=0, keepdims=True)
        i1 = jnp.min(jnp.where(blk == m1, ri, GS), axis=0, keepdims=True)
        m2 = jnp.max(jnp.where(ri == i1, NEG, blk), axis=0, keepdims=True)
        gs_rows.append(m1 + m2)
    cur = jnp.concatenate(gs_rows, axis=0)
    gidx = lax.broadcasted_iota(jnp.int32, (N_GROUPS, tm), 0)
    top_groups = []
    for _ in range(TOPK_GROUPS):
        mx = jnp.max(cur, axis=0, keepdims=True)
        ix = jnp.min(jnp.where(cur == mx, gidx, N_GROUPS), axis=0, keepdims=True)
        top_groups.append(ix)
        cur = jnp.where(gidx == ix, NEG, cur)
    blocks = []
    for gi in range(N_GROUPS):
        keep = top_groups[0] == gi
        for ix in top_groups[1:]:
            keep = jnp.logical_or(keep, ix == gi)
        blocks.append(jnp.where(keep, biased[gi * GS:(gi + 1) * GS, :], NEG))
    allowed = jnp.concatenate(blocks, axis=0)
    cur = allowed
    idxs, sels = [], []
    for _ in range(TOP_K):
        mx = jnp.max(cur, axis=0, keepdims=True)
        ix = jnp.min(jnp.where(cur == mx, rowi, E), axis=0, keepdims=True)
        hit = rowi == ix
        sels.append(jnp.sum(jnp.where(hit, scores, 0.0), axis=0, keepdims=True))
        idxs.append(ix)
        cur = jnp.where(hit, NEG, cur)
    chosen = jnp.where(jnp.logical_and(allowed > NEG, cur == NEG), 1.0, 0.0)
    sel = jnp.concatenate(sels, axis=0)
    idx_ref[...] = jnp.concatenate(idxs, axis=0)
    gate_ref[...] = sel / jnp.sum(sel, axis=0, keepdims=True) * ROUTE_SCALE

    @pl.when(pl.program_id(0) == 0)
    def _():
        cnt_ref[...] = jnp.zeros_like(cnt_ref)

    cnt_ref[...] += jnp.sum(chosen, axis=1, keepdims=True)


def _mix_out(y_f, y_b, bonus, g, ret_out, x2d, p, tm):
    T, D = x2d.shape
    W = bonus.shape[1]
    Wr = ret_out.shape[1]
    E = p["router_wt"].shape[0]
    full = lambda a: pl.BlockSpec(a.shape, lambda i, nd=a.ndim: (0,) * nd)
    consts = [p["gn_g"], p["gn_b"], p["head_avg"], p["wo_ret"], p["wo_rwkv"], p["ln1_g"], p["ln1_b"],
              p["router_wt"], p["router_b"]]
    return pl.pallas_call(
        functools.partial(_mix_body, alpha=p["alpha"]),
        grid=(T // tm,),
        in_specs=[pl.BlockSpec((tm, W), lambda i: (i, 0)),
                  pl.BlockSpec((tm, W), lambda i: (i, 0)),
                  pl.BlockSpec((tm, W), lambda i: (i, 0)),
                  pl.BlockSpec((tm, W), lambda i: (i, 0)),
                  pl.BlockSpec((tm, Wr), lambda i: (i, 0)),
                  pl.BlockSpec((tm, D), lambda i: (i, 0))] + [full(a) for a in consts],
        out_specs=[pl.BlockSpec((tm, D), lambda i: (i, 0)),
                   pl.BlockSpec((tm, D // 2), lambda i: (i, 0)),
                   pl.BlockSpec((TOP_K, tm), lambda i: (0, i)),
                   pl.BlockSpec((TOP_K, tm), lambda i: (0, i)),
                   pl.BlockSpec((E, LANES), lambda i: (0, 0))],
        out_shape=[jax.ShapeDtypeStruct((T, D), F32),
                   jax.ShapeDtypeStruct((T, D // 2), jnp.int32),
                   jax.ShapeDtypeStruct((TOP_K, T), jnp.int32),
                   jax.ShapeDtypeStruct((TOP_K, T), F32),
                   jax.ShapeDtypeStruct((E, LANES), F32)],
        compiler_params=_cparams(("arbitrary",)),
        name="mix_out",
    )(y_f, y_b, bonus, g, ret_out, x2d, *consts)


def _plan_body(idx_ref, pstart_ref, upper_ref, dest_ref, run_ref):
    @pl.when(pl.program_id(0) == 0)
    def _():
        run_ref[...] = jnp.zeros_like(run_ref)

    E = pstart_ref.shape[0]
    tm = idx_ref.shape[1]
    rowi = lax.broadcasted_iota(jnp.int32, (E, tm), 0)
    hits = [rowi == idx_ref[k:k + 1, :] for k in range(TOP_K)]
    member = jnp.zeros((E, tm), F32)
    for hit in hits:
        member = jnp.where(hit, 1.0, member)
    before = jnp.dot(member.astype(BF16), upper_ref[...], preferred_element_type=F32)
    slot = pstart_ref[...] + run_ref[:, 0:1] + before
    dest_ref[...] = jnp.concatenate(
        [jnp.sum(jnp.where(hit, slot, 0.0), axis=0, keepdims=True) for hit in hits], axis=0).astype(jnp.int32)
    run_ref[...] += jnp.sum(member, axis=1, keepdims=True)


def _slot_plan(idx_t, pad_start, tm):
    T = idx_t.shape[1]
    E = pad_start.shape[0]
    upper = (jnp.arange(tm)[:, None] < jnp.arange(tm)[None, :]).astype(BF16)
    return pl.pallas_call(
        _plan_body,
        grid=(T // tm,),
        in_specs=[pl.BlockSpec((TOP_K, tm), lambda i: (0, i)),
                  pl.BlockSpec((E, 1), lambda i: (0, 0)),
                  pl.BlockSpec((tm, tm), lambda i: (0, 0))],
        out_specs=pl.BlockSpec((TOP_K, tm), lambda i: (0, i)),
        out_shape=jax.ShapeDtypeStruct((TOP_K, T), jnp.int32),
        scratch_shapes=[pltpu.VMEM((E, LANES), F32)],
        compiler_params=_cparams(("arbitrary",)),
        name="slot_plan",
    )(idx_t, pad_start.reshape(E, 1).astype(F32), upper)


_UPPER_HALF = -65536


def _bf16_bits(x):
    return lax.bitcast_convert_type(x.astype(BF16).astype(F32), jnp.int32)


def _pack_halves(x):
    c = x.shape[1] // 2
    return _bf16_bits(x[:, :c]) | (jnp.right_shift(_bf16_bits(x[:, c:]), 16) & 0xFFFF)


def _unpack_halves(u):
    hi = lax.bitcast_convert_type(u & _UPPER_HALF, F32)
    lo = lax.bitcast_convert_type(jnp.left_shift(u, 16), F32)
    return hi, lo


SC_WINDOW = 64


def _sc_dispatch(xp, dest, n_slots):
    T, C = xp.shape
    n_win = T // SC_WINDOW
    mesh = plsc.VectorSubcoreMesh(core_axis_name="core", subcore_axis_name="subcore")

    @pl.kernel(out_type=jax.ShapeDtypeStruct((n_slots, C), xp.dtype), mesh=mesh, scratch_types=[])
    def scatter_rows(x_hbm, i_hbm, o_hbm):
        def body(x_vmem, i_vmem):
            for k in range(TOP_K):
                pltpu.sync_copy(x_vmem, o_hbm.at[i_vmem.at[k]])

        pltpu.emit_pipeline(
            body,
            grid=(n_win,),
            in_specs=[pl.BlockSpec((SC_WINDOW, C), lambda i: (i, 0)),
                      pl.BlockSpec((None, TOP_K, SC_WINDOW), lambda i: (i, 0, 0))],
            out_specs=[],
            core_axis_name=("core", "subcore"),
            dimension_semantics=(pltpu.PARALLEL,),
        )(x_hbm, i_hbm)

    idx = dest.reshape(TOP_K, n_win, SC_WINDOW).transpose(1, 0, 2)
    return scatter_rows(xp, idx)


def _sc_gather(ys, dest):
    n_idx = dest.shape[0] * dest.shape[1]
    C = ys.shape[1]
    mesh = plsc.VectorSubcoreMesh(core_axis_name="core", subcore_axis_name="subcore")

    @pl.kernel(out_type=jax.ShapeDtypeStruct((n_idx, C), ys.dtype), mesh=mesh, scratch_types=[])
    def gather_rows(y_hbm, i_hbm, o_hbm):
        def body(i_vmem, o_vmem):
            pltpu.sync_copy(y_hbm.at[i_vmem.at[0]], o_vmem)

        pltpu.emit_pipeline(
            body,
            grid=(n_idx // SC_WINDOW,),
            in_specs=[pl.BlockSpec((1, SC_WINDOW), lambda i: (i, 0))],
            out_specs=[pl.BlockSpec((SC_WINDOW, C), lambda i: (i, 0))],
            core_axis_name=("core", "subcore"),
            dimension_semantics=(pltpu.PARALLEL,),
        )(i_hbm, o_hbm)

    return gather_rows(ys, dest.reshape(n_idx // SC_WINDOW, SC_WINDOW))


MOE_IN_SLOTS = 8
MOE_OUT_SLOTS = 4
MOE_EXPERTS_PER_STEP = 1


def _moe_body(nb_ref, b0_ref, cnt_ref, nu_ref, xs_hbm, wg_ref, wu_ref, wd_ref, ys_hbm,
              xbuf, obuf, wgb_ref, wub_ref, wdb_ref, in_sem, out_sem):
    G = MOE_BLOCK
    i = pl.program_id(0)
    n_used = nu_ref[0]

    def in_copy(g):
        slot = lax.rem(g, MOE_IN_SLOTS)
        return pltpu.make_async_copy(xs_hbm.at[pl.ds(pl.multiple_of(g * G, G), G), :], xbuf.at[slot],
                                     in_sem.at[slot])

    def out_copy(g):
        slot = lax.rem(g, MOE_OUT_SLOTS)
        return pltpu.make_async_copy(obuf.at[slot], ys_hbm.at[pl.ds(pl.multiple_of(g * G, G), G), :],
                                     out_sem.at[slot])

    @pl.when(i == 0)
    def _():
        for g in range(MOE_IN_SLOTS - 1):
            @pl.when(g < n_used)
            def _():
                in_copy(g).start()

    row = lax.broadcasted_iota(jnp.int32, (G, xbuf.shape[2]), 0)
    c = xbuf.shape[2]

    def run_expert(q):
        e = i * MOE_EXPERTS_PER_STEP + q
        nb = nb_ref[e]
        b0 = b0_ref[e]

        @pl.when(nb > 0)
        def _():
            wgb_ref[...] = wg_ref[q].astype(BF16)
            wub_ref[...] = wu_ref[q].astype(BF16)
            wdb_ref[...] = wd_ref[q].astype(BF16)

            def step(j, carry):
                g = b0 + j
                in_copy(g).wait()

                @pl.when(g + (MOE_IN_SLOTS - 1) < n_used)
                def _():
                    in_copy(g + (MOE_IN_SLOTS - 1)).start()

                @pl.when(g >= MOE_OUT_SLOTS)
                def _():
                    out_copy(g - MOE_OUT_SLOTS).wait()

                u = jnp.where(row < cnt_ref[e] - j * G, xbuf[lax.rem(g, MOE_IN_SLOTS)], 0)
                x_hi, x_lo = _unpack_halves(u)
                gate = _dot(x_hi, wgb_ref[:c, :]) + _dot(x_lo, wgb_ref[c:, :])
                up = _dot(x_hi, wub_ref[:c, :]) + _dot(x_lo, wub_ref[c:, :])
                obuf[lax.rem(g, MOE_OUT_SLOTS)] = _pack_halves(_dot(_silu(gate) * up, wdb_ref[...]))
                out_copy(g).start()
                return carry

            lax.fori_loop(0, nb, step, 0)

    for q in range(MOE_EXPERTS_PER_STEP):
        run_expert(q)

    @pl.when(i == pl.num_programs(0) - 1)
    def _():
        for back in range(MOE_OUT_SLOTS, 0, -1):
            @pl.when(n_used >= back)
            def _():
                out_copy(n_used - back).wait()


def _moe_ffn(xs, blocks_per_expert, first_block, counts, n_used, w_gate, w_up, w_down):
    P, C = xs.shape
    G = MOE_BLOCK
    E, D, De = w_gate.shape
    XP = MOE_EXPERTS_PER_STEP
    assert E % XP == 0
    wspec = lambda shape: pl.BlockSpec((XP,) + shape, lambda i, nb, b0, cnt, nu: (i, 0, 0))
    grid_spec = pltpu.PrefetchScalarGridSpec(
        num_scalar_prefetch=4,
        grid=(E // XP,),
        in_specs=[pl.BlockSpec(memory_space=pl.ANY), wspec((D, De)), wspec((D, De)), wspec((De, D))],
        out_specs=pl.BlockSpec(memory_space=pl.ANY),
        scratch_shapes=[pltpu.VMEM((MOE_IN_SLOTS, G, C), jnp.int32), pltpu.VMEM((MOE_OUT_SLOTS, G, C), jnp.int32),
                        pltpu.VMEM((D, De), BF16), pltpu.VMEM((D, De), BF16), pltpu.VMEM((De, D), BF16),
                        pltpu.SemaphoreType.DMA((MOE_IN_SLOTS,)), pltpu.SemaphoreType.DMA((MOE_OUT_SLOTS,))],
    )
    return pl.pallas_call(
        _moe_body,
        grid_spec=grid_spec,
        out_shape=jax.ShapeDtypeStruct((P, C), jnp.int32),
        compiler_params=_cparams(("arbitrary",)),
        name="moe_ffn",
    )(blocks_per_expert, first_block, counts, n_used, xs, w_gate, w_up, w_down)


def _comb_body(yg_ref, x1_ref, gt_ref, sg_ref, su_ref, sd_ref, l2g_ref, l2b_ref, o_ref, *, alpha):
    x1 = x1_ref[...]
    xb = x1.astype(BF16)
    shared = _dot(_silu(_dot(xb, sg_ref[...])) * _dot(xb, su_ref[...]), sd_ref[...])
    gt = gt_ref[...]
    acc_hi = acc_lo = None
    for k in range(TOP_K):
        hi, lo = _unpack_halves(yg_ref[k])
        gk = gt[:, k:k + 1]
        acc_hi = gk * hi if acc_hi is None else acc_hi + gk * hi
        acc_lo = gk * lo if acc_lo is None else acc_lo + gk * lo
    routed = jnp.concatenate([acc_hi, acc_lo], axis=1)
    o_ref[...] = _layer_norm(alpha * x1 + (routed + shared), l2g_ref[...], l2b_ref[...])


def _combine(yg, x1, gates_t, p, tm):
    T, D = x1.shape
    full = lambda a: pl.BlockSpec(a.shape, lambda i, nd=a.ndim: (0,) * nd)
    consts = [p["sh_gate"], p["sh_up"], p["sh_down"], p["ln2_g"], p["ln2_b"]]
    return pl.pallas_call(
        functools.partial(_comb_body, alpha=p["alpha"]),
        grid=(T // tm,),
        in_specs=[pl.BlockSpec((TOP_K, tm, D // 2), lambda i: (0, i, 0)),
                  pl.BlockSpec((tm, D), lambda i: (i, 0)),
                  pl.BlockSpec((tm, TOP_K), lambda i: (i, 0))] + [full(a) for a in consts],
        out_specs=pl.BlockSpec((tm, D), lambda i: (i, 0)),
        out_shape=jax.ShapeDtypeStruct((T, D), F32),
        compiler_params=_cparams(("parallel",)),
        name="combine",
    )(yg, x1, gates_t, *consts)


def _segment_layout(counts, n_tokens):
    G = MOE_BLOCK
    E = counts.shape[0]
    n_blocks = -(-(n_tokens * TOP_K + E * (G - 1)) // G)
    padded = (counts + G - 1) // G * G
    pad_end = jnp.cumsum(padded)
    pad_start = pad_end - padded
    n_used = pad_end[-1:] // G
    return (pad_start, (padded // G).astype(jnp.int32), (pad_start // G).astype(jnp.int32),
            n_used.astype(jnp.int32), n_blocks)


def _rotary_tables(s, d):
    inv = ROPE_BASE ** (-jnp.arange(0, d, 2, dtype=F32) / d)
    ang = jnp.arange(s, dtype=F32)[:, None] * inv[None, :]
    cos = jnp.cos(ang)
    sin = jnp.sin(ang)
    return jnp.concatenate([cos, cos], axis=-1), jnp.concatenate([-sin, sin], axis=-1)


def _layer_params(l, depth, w_in, ret_gn_g, ret_gn_b, rwkv_mu, rwkv_w0, rwkv_w_up, rwkv_a0, rwkv_a_up,
                  rwkv_g_up, rwkv_k_k, rwkv_k_a, rwkv_r_k, rwkv_gn_g, rwkv_gn_b, w_out, ln1_g, ln1_b,
                  router_w, router_bias, exp_w_gate, exp_w_up, exp_w_down, sh_w_gate, sh_w_up,
                  sh_w_down, ln2_g, ln2_b):
    ret_w = ret_gn_g.shape[-1]
    W = rwkv_gn_g.shape[-1]
    n_heads, hd = rwkv_r_k.shape[-2:]
    rank_w = rwkv_w_up.shape[2]
    rank_a = rwkv_a_up.shape[2]
    assert rank_w * 2 == LANES and rank_a * 2 == LANES and rwkv_g_up.shape[1] == LANES
    assert hd * 2 == LANES and SCAN_CHUNK == hd
    row = lambda a: a.reshape(1, -1).astype(F32)
    zw = jnp.zeros((rank_w, W), F32)
    za = jnp.zeros((rank_a, W), F32)
    head_id = jnp.arange(W) // hd
    same_head = (head_id[:, None] == head_id[None, :])
    wi = w_in[l]
    return dict(
        alpha=float((2 * depth) ** 0.25),
        rwkv_width=W,
        w_ret=wi[:, :4 * ret_w].astype(BF16),
        w_rwkv=wi[:, 4 * ret_w:].astype(BF16),
        ret_gn_g=row(ret_gn_g[l]), ret_gn_b=row(ret_gn_b[l]),
        logg=jnp.broadcast_to(
            jnp.log1p(-jnp.exp2(-5.0 - jnp.arange(RET_HEADS, dtype=F32)))[:, None, None],
            (RET_HEADS, 1, LANES)),
        mu=row(rwkv_mu[l]),
        wup_pad=jnp.stack([jnp.concatenate([rwkv_w_up[l, 0], zw], 0),
                           jnp.concatenate([zw, rwkv_w_up[l, 1]], 0)]).astype(BF16),
        aup_pad=jnp.stack([jnp.concatenate([rwkv_a_up[l, 0], za], 0),
                           jnp.concatenate([za, rwkv_a_up[l, 1]], 0)]).astype(BF16),
        w0=rwkv_w0[l].astype(F32), a0=rwkv_a0[l].astype(F32),
        gup=rwkv_g_up[l].astype(BF16),
        k_k=row(rwkv_k_k[l]), k_a=row(rwkv_k_a[l]), r_k=row(rwkv_r_k[l]),
        head_ones=same_head.astype(BF16),
        head_avg=(same_head.astype(F32) / hd).astype(BF16),
        gn_g=row(rwkv_gn_g[l]), gn_b=row(rwkv_gn_b[l]),
        wo_ret=w_out[l, :ret_w].astype(BF16), wo_rwkv=w_out[l, ret_w:].astype(BF16),
        ln1_g=row(ln1_g[l]), ln1_b=row(ln1_b[l]),
        router_wt=router_w[l].T.astype(BF16), router_b=router_bias[l].reshape(-1, 1).astype(F32),
        exp_gate=exp_w_gate[l], exp_up=exp_w_up[l], exp_down=exp_w_down[l],
        sh_gate=sh_w_gate[l].astype(BF16), sh_up=sh_w_up[l].astype(BF16), sh_down=sh_w_down[l].astype(BF16),
        ln2_g=row(ln2_g[l]), ln2_b=row(ln2_b[l]),
    )


def _pick(n, pref):
    t = min(n, pref)
    while n % t:
        t //= 2
    return t


def _layer(x, p):
    b, s, D = x.shape
    T = b * s
    x2d = x.reshape(T, D)
    tm = _pick(T, 256)
    z_ret, r, v, kk, lw, kd, kka, bonus, g = _proj_prep(x, p, _pick(s, 256))
    cos, sin = _rotary_tables(s, RET_CHUNK)
    ret_out = _retention(z_ret, cos, sin, p["logg"], p["ret_gn_g"], p["ret_gn_b"])
    y_f, y_b = _rwkv_scan(r, kk, v, lw, kd, kka, _pick(s, 512))
    W = p["rwkv_width"]
    x1, xp, idx_t, gates, cnt = _mix_out(y_f.reshape(T, W), y_b.reshape(T, W), bonus.reshape(T, W),
                                         g.reshape(T, W), ret_out.reshape(T, -1), x2d, p, tm)
    counts = cnt[:, 0].astype(jnp.int32)
    pad_start, blocks_per_expert, first_block, n_used, n_blocks = _segment_layout(counts, T)
    dest = _slot_plan(idx_t, pad_start, _pick(T, 512))
    xs = _sc_dispatch(xp, dest, n_blocks * MOE_BLOCK)
    ys = _moe_ffn(xs, blocks_per_expert, first_block, counts, n_used, p["exp_gate"], p["exp_up"], p["exp_down"])
    yg = _sc_gather(ys, dest).reshape(TOP_K, T, D // 2)
    out = _combine(yg, x1, gates.T, p, _pick(T, 256))
    return out.reshape(b, s, D)


def kernel(x_prompt, x_sample, w_in, ret_gn_g, ret_gn_b, rwkv_mu, rwkv_w0, rwkv_w_up, rwkv_a0, rwkv_a_up,
           rwkv_g_up, rwkv_k_k, rwkv_k_a, rwkv_r_k, rwkv_gn_g, rwkv_gn_b, w_out, ln1_g, ln1_b, router_w,
           router_bias, exp_w_gate, exp_w_up, exp_w_down, sh_w_gate, sh_w_up, sh_w_down, ln2_g, ln2_b):
    weights = (w_in, ret_gn_g, ret_gn_b, rwkv_mu, rwkv_w0, rwkv_w_up, rwkv_a0, rwkv_a_up, rwkv_g_up,
               rwkv_k_k, rwkv_k_a, rwkv_r_k, rwkv_gn_g, rwkv_gn_b, w_out, ln1_g, ln1_b, router_w,
               router_bias, exp_w_gate, exp_w_up, exp_w_down, sh_w_gate, sh_w_up, sh_w_down, ln2_g, ln2_b)
    depth = w_in.shape[0]
    layers = [_layer_params(l, depth, *weights) for l in range(depth)]

    def trunk(x):
        for p in layers:
            x = _layer(x, p)
        return x

    return trunk(x_prompt), trunk(x_sample)
```
